```python
import jax, jax.numpy as jnp
from jax import lax
import numpy as np

D_MODEL = 1024
BATCH = 4
SEQ = 4096
DEPTH = 4

HEAD_DIM = 64
N_HEADS_TOTAL = D_MODEL // HEAD_DIM
N_HEADS_FOX = N_HEADS_TOTAL // 2
N_HEADS_SB = N_HEADS_TOTAL - N_HEADS_FOX
N_HEADS_SWA = N_HEADS_TOTAL
N_KV_SWA = max(1, N_HEADS_SWA // 8)
WINDOW = 128
Q_BLOCK = 128
ROPE_THETA = 500000.0
ROPE_DIM = HEAD_DIM // 4
D_FF = ((8 * D_MODEL // 3 + 127) // 128) * 128
N_EXPERTS = 8
TOP_K = 2
D_FF_EXPERT = 7 * D_MODEL // 2
D_PLE = 256
LN_EPS = 1e-5
N_EVEN = (DEPTH + 1) // 2
N_ODD = DEPTH // 2
DEEPNORM_ALPHA = (2.0 * DEPTH) ** 0.25
DEEPNORM_BETA = (8.0 * DEPTH) ** -0.25
FOX_W = N_HEADS_FOX * HEAD_DIM
SB_W = N_HEADS_SB * HEAD_DIM
AB_IN = 3 * FOX_W + N_HEADS_FOX + 3 * SB_W
AB_SPLITS = [FOX_W, 2 * FOX_W, 3 * FOX_W, 3 * FOX_W + N_HEADS_FOX,
             3 * FOX_W + N_HEADS_FOX + SB_W, 3 * FOX_W + N_HEADS_FOX + 2 * SB_W]
SWA_IN = (N_HEADS_SWA + 2 * N_KV_SWA) * HEAD_DIM
SWA_SPLITS = [N_HEADS_SWA * HEAD_DIM, (N_HEADS_SWA + N_KV_SWA) * HEAD_DIM]

kernel_name = 'fox_stickbreak_swa_sink_moe_deepnorm_ple'

F32 = jnp.float32


def layer_norm(x, g, b):
    xf = x.astype(F32)
    mu = jnp.mean(xf, axis=-1, keepdims=True)
    xc = xf - mu
    var = jnp.mean(xc * xc, axis=-1, keepdims=True)
    return (xc * lax.rsqrt(var + LN_EPS) * g.astype(F32) + b.astype(F32)).astype(x.dtype)


def swiglu(x, w_gate_up, w_down):
    g, u = jnp.split(x @ w_gate_up, 2, axis=-1)
    return (jax.nn.silu(g) * u) @ w_down


def partial_rope(x, pos):
    half = ROPE_DIM // 2
    inv = ROPE_THETA ** (-jnp.arange(half, dtype=F32) * 2.0 / ROPE_DIM)
    ang = pos.astype(F32)[:, None] * inv[None, :]
    cos = jnp.cos(ang)[None, :, None, :]
    sin = jnp.sin(ang)[None, :, None, :]
    xr = x[..., :ROPE_DIM].astype(F32)
    x1, x2 = xr[..., :half], xr[..., half:]
    rot = jnp.concatenate([x1 * cos - x2 * sin, x2 * cos + x1 * sin], axis=-1).astype(x.dtype)
    return jnp.concatenate([rot, x[..., ROPE_DIM:]], axis=-1)


def fox_attention(q, k, v, log_f):
    B, S, H, dh = q.shape
    scale = dh ** -0.5
    c = jnp.cumsum(log_f, axis=1).transpose(0, 2, 1)
    k_pos = jnp.arange(S)

    def block(i):
        start = i * Q_BLOCK
        qb = lax.dynamic_slice_in_dim(q, start, Q_BLOCK, axis=1)
        cb = lax.dynamic_slice_in_dim(c, start, Q_BLOCK, axis=2)
        s = jnp.einsum('bqhd,bkhd->bhqk', qb, k, preferred_element_type=F32) * scale
        s = s + cb[..., :, None] - c[..., None, :]
        q_pos = start + jnp.arange(Q_BLOCK)
        mask = k_pos[None, :] <= q_pos[:, None]
        w = jax.nn.softmax(jnp.where(mask, s, -jnp.inf), axis=-1)
        return jnp.einsum('bhqk,bkhd->bqhd', w.astype(v.dtype), v)

    out = lax.map(block, jnp.arange(S // Q_BLOCK))
    return out.transpose(1, 0, 2, 3, 4).reshape(B, S, H, dh)


def stick_breaking_attention(q, k, v):
    B, S, H, dh = q.shape
    scale = dh ** -0.5
    k_pos = jnp.arange(S)

    def block(i):
        start = i * Q_BLOCK
        qb = lax.dynamic_slice_in_dim(q, start, Q_BLOCK, axis=1)
        z = jnp.einsum('bqhd,bkhd->bhqk', qb, k, preferred_element_type=F32) * scale
        q_pos = start + jnp.arange(Q_BLOCK)
        mask = k_pos[None, :] < q_pos[:, None]
        log_1mb = jnp.where(mask, jax.nn.log_sigmoid(-z), 0.0)
        later = lax.cumsum(log_1mb, axis=3, reverse=True) - log_1mb
        a = jnp.where(mask, jnp.exp(jax.nn.log_sigmoid(z) + later), 0.0)
        return jnp.einsum('bhqk,bkhd->bqhd', a.astype(v.dtype), v)

    out = lax.map(block, jnp.arange(S // Q_BLOCK))
    return out.transpose(1, 0, 2, 3, 4).reshape(B, S, H, dh)


def sliding_window_gqa_sinks(q, k, v, sinks):
    B, S, HQ, dh = q.shape
    HKV = k.shape[2]
    G = HQ // HKV
    W = WINDOW
    n = S // W
    scale = dh ** -0.5
    qb = q.reshape(B, n, W, HKV, G, dh)

    def with_prev(t):
        t = t.reshape(B, n, W, HKV, dh)
        prev = jnp.concatenate([jnp.zeros_like(t[:, :1]), t[:, :-1]], axis=1)
        return jnp.concatenate([prev, t], axis=2)

    kk, vv = with_prev(k), with_prev(v)
    s = jnp.einsum('bnqhgd,bnkhd->bnhgqk', qb, kk, preferred_element_type=F32) * scale
    diff = jnp.arange(W)[:, None] + W - jnp.arange(2 * W)[None, :]
    band = (diff >= 0) & (diff < W)
    key_abs = jnp.arange(n)[:, None] * W + jnp.arange(2 * W)[None, :] - W
    mask = band[None] & (key_abs >= 0)[:, None, :]
    s = jnp.where(mask[None, :, None, None], s, -jnp.inf)
    sink = sinks.astype(F32).reshape(HKV, G)[None, None, :, :, None]
    mx = jnp.maximum(jnp.max(s, axis=-1), sink)
    e = jnp.exp(s - mx[..., None])
    w = e / (jnp.sum(e, axis=-1) + jnp.exp(sink - mx))[..., None]
    out = jnp.einsum('bnhgqk,bnkhd->bnqhgd', w.astype(v.dtype), vv)
    return out.reshape(B, S, HQ, dh)


def fox_sb_mixer(x, w_in, b_f, w_out):
    B, S, _ = x.shape
    h = x @ w_in
    qa, ka, va, fa, qs, ks, vs = jnp.split(h, AB_SPLITS, axis=-1)
    hd = lambda t, H: t.reshape(B, S, H, HEAD_DIM)
    log_f = jax.nn.log_sigmoid((fa + b_f).astype(F32))
    oa = fox_attention(hd(qa, N_HEADS_FOX), hd(ka, N_HEADS_FOX), hd(va, N_HEADS_FOX), log_f)
    ob = stick_breaking_attention(hd(qs, N_HEADS_SB), hd(ks, N_HEADS_SB), hd(vs, N_HEADS_SB))
    o = jnp.concatenate([oa.reshape(B, S, FOX_W), ob.reshape(B, S, SB_W)], axis=-1)
    return o @ w_out


def swa_mixer(x, w_qkv, sinks, w_out, pos):
    B, S, _ = x.shape
    q, k, v = jnp.split(x @ w_qkv, SWA_SPLITS, axis=-1)
    q = partial_rope(q.reshape(B, S, N_HEADS_SWA, HEAD_DIM), pos)
    k = partial_rope(k.reshape(B, S, N_KV_SWA, HEAD_DIM), pos)
    v = v.reshape(B, S, N_KV_SWA, HEAD_DIM)
    o = sliding_window_gqa_sinks(q, k, v, sinks)
    return o.reshape(B, S, N_HEADS_SWA * HEAD_DIM) @ w_out


def moe_swiglu(x, router_w, router_b, w_gate_up, w_down):
    B, S, D = x.shape
    xf = x.reshape(-1, D)
    logits = jnp.einsum('td,de->te', xf, router_w, preferred_element_type=F32) + router_b.astype(F32)
    top_v, top_i = lax.top_k(logits, TOP_K)
    gates = jax.nn.softmax(top_v, axis=-1)
    combine = jnp.sum(gates[..., None] * jax.nn.one_hot(top_i, N_EXPERTS, dtype=F32), axis=1)
    y = jnp.zeros(xf.shape, F32)
    for e in range(N_EXPERTS):
        y = y + combine[:, e:e + 1] * swiglu(xf, w_gate_up[e], w_down[e]).astype(F32)
    return y.astype(x.dtype).reshape(B, S, D)


def setup_inputs(seed: int = 0) -> dict:
    key = jax.random.key(seed)
    ks = jax.random.split(key, 20)
    nrm = lambda k, shape, s: jax.random.normal(k, shape, F32) * s
    return {
        'x': nrm(ks[0], (BATCH, SEQ, D_MODEL), 1.0),
        'p': nrm(ks[1], (DEPTH, BATCH, SEQ, D_PLE), 1.0),
        'ln_mix_g': 1.0 + nrm(ks[2], (DEPTH, D_MODEL), 0.02),
        'ln_mix_b': nrm(ks[3], (DEPTH, D_MODEL), 0.02),
        'ln_ffn_g': 1.0 + nrm(ks[4], (DEPTH, D_MODEL), 0.02),
        'ln_ffn_b': nrm(ks[5], (DEPTH, D_MODEL), 0.02),
        'ab_w_in': nrm(ks[6], (N_EVEN, D_MODEL, AB_IN), D_MODEL ** -0.5),
        'ab_b_f': jax.random.uniform(ks[7], (N_EVEN, N_HEADS_FOX), F32, 1.0, 5.0),
        'ab_w_out': nrm(ks[8], (N_EVEN, FOX_W + SB_W, D_MODEL), (FOX_W + SB_W) ** -0.5 * DEEPNORM_BETA),
        'c_w_qkv': nrm(ks[9], (N_ODD, D_MODEL, SWA_IN), D_MODEL ** -0.5),
        'c_sinks': nrm(ks[10], (N_ODD, N_HEADS_SWA), 1.0),
        'c_w_out': nrm(ks[11], (N_ODD, N_HEADS_SWA * HEAD_DIM, D_MODEL), (N_HEADS_SWA * HEAD_DIM) ** -0.5 * DEEPNORM_BETA),
        'ffn_w_gate_up': nrm(ks[12], (N_EVEN, D_MODEL, 2 * D_FF), D_MODEL ** -0.5),
        'ffn_w_down': nrm(ks[13], (N_EVEN, D_FF, D_MODEL), D_FF ** -0.5 * DEEPNORM_BETA),
        'router_w': nrm(ks[14], (N_ODD, D_MODEL, N_EXPERTS), D_MODEL ** -0.5),
        'router_b': nrm(ks[15], (N_ODD, N_EXPERTS), 0.01),
        'moe_w_gate_up': nrm(ks[16], (N_ODD, N_EXPERTS, D_MODEL, 2 * D_FF_EXPERT), D_MODEL ** -0.5),
        'moe_w_down': nrm(ks[17], (N_ODD, N_EXPERTS, D_FF_EXPERT, D_MODEL), D_FF_EXPERT ** -0.5 * DEEPNORM_BETA),
        'ple_w_gate': nrm(ks[18], (DEPTH, D_MODEL, D_MODEL), D_MODEL ** -0.5),
        'ple_w_proj': nrm(ks[19], (DEPTH, D_PLE, D_MODEL), D_PLE ** -0.5 * 0.5),
    }


def reference(x, p, ln_mix_g, ln_mix_b, ln_ffn_g, ln_ffn_b, ab_w_in, ab_b_f, ab_w_out,
              c_w_qkv, c_sinks, c_w_out, ffn_w_gate_up, ffn_w_down, router_w, router_b,
              moe_w_gate_up, moe_w_down, ple_w_gate, ple_w_proj):
    pos = jnp.arange(x.shape[1])
    for i in range(DEPTH):
        j = i // 2
        if i % 2 == 0:
            h = fox_sb_mixer(x, ab_w_in[j], ab_b_f[j], ab_w_out[j])
        else:
            h = swa_mixer(x, c_w_qkv[j], c_sinks[j], c_w_out[j], pos)
        x = layer_norm(DEEPNORM_ALPHA * x + h, ln_mix_g[i], ln_mix_b[i])
        if i % 2 == 0:
            h = swiglu(x, ffn_w_gate_up[j], ffn_w_down[j])
        else:
            h = moe_swiglu(x, router_w[j], router_b[j], moe_w_gate_up[j], moe_w_down[j])
        x = layer_norm(DEEPNORM_ALPHA * x + h, ln_ffn_g[i], ln_ffn_b[i])
        x = x + jax.nn.sigmoid(x @ ple_w_gate[i]) * (p[i] @ ple_w_proj[i])
    return x
```

```python
import functools
import math

import jax
import jax.numpy as jnp
from jax import lax
from jax.experimental import pallas as pl
from jax.experimental.pallas import tpu as pltpu

F32 = jnp.float32
BF16 = jnp.bfloat16

D_MODEL = 1024
HEAD_DIM = 64
LANES = 128
N_HEADS_FOX = 8
N_HEADS_SB = 8
N_HEADS_SWA = 16
N_KV_SWA = 2
WINDOW = 128
ROPE_THETA = 500000.0
ROPE_DIM = HEAD_DIM // 4
N_EXPERTS = 8
D_PLE = 256
LN_EPS = 1e-5
DEPTH = 4
DEEPNORM_ALPHA = (2.0 * DEPTH) ** 0.25
ATTN_SCALE = HEAD_DIM ** -0.5
FOX_W = N_HEADS_FOX * HEAD_DIM
SB_W = N_HEADS_SB * HEAD_DIM
VMEM_LIMIT = 56 * 1024 * 1024

NEG_INF = float("-inf")


def _cparams(*sem):
    return pltpu.CompilerParams(dimension_semantics=sem, vmem_limit_bytes=VMEM_LIMIT)


def _dot(a, b):
    return jnp.dot(a, b, preferred_element_type=F32)


def _dot_nt(a, b):
    return lax.dot_general(a, b, (((1,), (1,)), ((), ())), preferred_element_type=F32)


def _log_sigmoid(x):
    return jnp.minimum(x, 0.0) - jnp.log1p(jnp.exp(-jnp.abs(x)))


def _head_masks(shape):
    lane = lax.broadcasted_iota(jnp.int32, shape, len(shape) - 1)
    return lane < HEAD_DIM, lane >= HEAD_DIM


def _ab_proj_kernel(x_ref, w_ref, wf_ref, bf_ref, h_ref, lf_ref):
    xb = x_ref[...].astype(BF16)
    h_ref[...] = _dot(xb, w_ref[...]).astype(BF16)
    lf_ref[...] = _log_sigmoid(_dot(xb, wf_ref[...]) + bf_ref[...])


def _ab_proj(x2d, w, wf, bf, tm=512):
    t = x2d.shape[0]
    n = w.shape[1]
    return pl.pallas_call(
        _ab_proj_kernel,
        grid=(t // tm,),
        in_specs=[
            pl.BlockSpec((tm, D_MODEL), lambda i: (i, 0)),
            pl.BlockSpec((D_MODEL, n), lambda i: (0, 0)),
            pl.BlockSpec((D_MODEL, LANES), lambda i: (0, 0)),
            pl.BlockSpec((1, LANES), lambda i: (0, 0)),
        ],
        out_specs=[
            pl.BlockSpec((tm, n), lambda i: (i, 0)),
            pl.BlockSpec((tm, LANES), lambda i: (i, 0)),
        ],
        out_shape=[jax.ShapeDtypeStruct((t, n), BF16), jax.ShapeDtypeStruct((t, LANES), F32)],
        compiler_params=_cparams("parallel"),
        name="ab_proj",
    )(x2d, w, wf, bf)


def _cumsum_kernel(x_ref, o_ref):
    x = x_ref[0]
    s = x.shape[1]
    lane = lax.broadcasted_iota(jnp.int32, x.shape, 1) & (LANES - 1)
    sh = 1
    while sh < LANES:
        x = x + jnp.where(lane >= sh, pltpu.roll(x, sh, axis=1), 0.0)
        sh *= 2
    carry = jnp.zeros((x.shape[0], 1), F32)
    for c in range(s // LANES):
        blk = x[:, c * LANES:(c + 1) * LANES] + carry
        o_ref[0, :, c * LANES:(c + 1) * LANES] = blk
        carry = blk[:, LANES - 1:LANES]


def _cumsum_seq(lf_t):
    b, h, s = lf_t.shape
    return pl.pallas_call(
        _cumsum_kernel,
        grid=(b,),
        in_specs=[pl.BlockSpec((1, h, s), lambda i: (i, 0, 0))],
        out_specs=pl.BlockSpec((1, h, s), lambda i: (i, 0, 0)),
        out_shape=jax.ShapeDtypeStruct((b, h, s), F32),
        compiler_params=_cparams("parallel"),
        name="fox_cumsum",
    )(lf_t)


def _fox_kernel(q_ref, k_ref, v_ref, ccol_ref, crow_ref, o_ref, m_ref, l_ref, acc_ref, *, tq, tk):
    qi = pl.program_id(2)
    q0 = qi * tq
    q = q_ref[0]
    masks = _head_masks(q.shape)
    qm = [jnp.where(mk, q, 0.0) * ATTN_SCALE for mk in masks]
    qm = [t.astype(BF16) for t in qm]
    ccol = ccol_ref[0, 0]
    ci = [ccol[:, r:r + 1] for r in range(2)]

    m_ref[...] = jnp.full(m_ref.shape, NEG_INF, F32)
    l_ref[...] = jnp.zeros(l_ref.shape, F32)
    acc_ref[...] = jnp.zeros(acc_ref.shape, F32)

    def step(k0, masked):
        kt = k_ref[0, pl.ds(k0, tk), :]
        vt = v_ref[0, pl.ds(k0, tk), :]
        if masked:
            row = lax.broadcasted_iota(jnp.int32, (tq, tk), 0)
            col = lax.broadcasted_iota(jnp.int32, (tq, tk), 1)
            keep = col <= row
        for r in range(2):
            cj = crow_ref[0, 0, r:r + 1, pl.ds(k0, tk)]
            s = _dot_nt(qm[r], kt) + ci[r] - cj
            if masked:
                s = jnp.where(keep, s, NEG_INF)
            m_prev = m_ref[r]
            m_new = jnp.maximum(m_prev, jnp.max(s, axis=1, keepdims=True))
            alpha = jnp.exp(m_prev - m_new)
            p = jnp.exp(s - m_new)
            l_ref[r] = alpha * l_ref[r] + jnp.sum(p, axis=1, keepdims=True)
            acc_ref[r] = alpha * acc_ref[r] + _dot(p.astype(BF16), vt)
            m_ref[r] = m_new

    def body(j, carry):
        step(pl.multiple_of(j * tk, tk), False)
        return carry

    lax.fori_loop(0, qi, body, 0)
    step(pl.multiple_of(q0, tk), True)

    o = jnp.zeros((tq, LANES), F32)
    for r in range(2):
        o = o + jnp.where(masks[r], acc_ref[r] / l_ref[r], 0.0)
    o_ref[0] = o.astype(BF16)


def _fox_attention(h3, ccol, crow, tq=512):
    b, s, _ = h3.shape
    nhp = N_HEADS_FOX // 2
    kern = functools.partial(_fox_kernel, tq=tq, tk=tq)
    return pl.pallas_call(
        kern,
        grid=(b, nhp, s // tq),
        in_specs=[
            pl.BlockSpec((1, tq, LANES), lambda bi, hp, qi: (bi, qi, hp)),
            pl.BlockSpec((1, s, LANES), lambda bi, hp, qi: (bi, 0, nhp + hp)),
            pl.BlockSpec((1, s, LANES), lambda bi, hp, qi: (bi, 0, 2 * nhp + hp)),
            pl.BlockSpec((1, 1, tq, 2), lambda bi, hp, qi: (bi, hp, qi, 0)),
            pl.BlockSpec((1, 1, 2, s), lambda bi, hp, qi: (bi, hp, 0, 0)),
        ],
        out_specs=pl.BlockSpec((1, tq, LANES), lambda bi, hp, qi: (bi, qi, hp)),
        out_shape=jax.ShapeDtypeStruct((b, s, FOX_W), BF16),
        scratch_shapes=[
            pltpu.VMEM((2, tq, 1), F32),
            pltpu.VMEM((2, tq, 1), F32),
            pltpu.VMEM((2, tq, LANES), F32),
        ],
        compiler_params=_cparams("parallel", "parallel", "arbitrary"),
        name="fox_attn",
    )(h3, h3, h3, ccol, crow)


def _sb_kernel(q_ref, k_ref, v_ref, o_ref, run_ref, acc_ref, *, tq, cw):
    qi = pl.program_id(2)
    q0 = qi * tq
    q = q_ref[0]
    masks = _head_masks(q.shape)
    qm = [(jnp.where(mk, q, 0.0) * ATTN_SCALE).astype(BF16) for mk in masks]

    jj = lax.broadcasted_iota(jnp.int32, (cw, cw), 0)
    ss = lax.broadcasted_iota(jnp.int32, (cw, cw), 1)
    tri = jnp.where(jj > ss, 1.0, 0.0).astype(BF16)

    run_ref[...] = jnp.zeros(run_ref.shape, F32)
    acc_ref[...] = jnp.zeros(acc_ref.shape, F32)

    def step(k0, masked):
        kt = k_ref[0, pl.ds(k0, cw), :]
        vt = v_ref[0, pl.ds(k0, cw), :]
        if masked:
            row = q0 + lax.broadcasted_iota(jnp.int32, (tq, cw), 0)
            col = k0 + lax.broadcasted_iota(jnp.int32, (tq, cw), 1)
            keep = col < row
        for r in range(2):
            z = _dot_nt(qm[r], kt)
            l1m = -(jnp.maximum(z, 0.0) + jnp.log1p(jnp.exp(-jnp.abs(z))))
            if masked:
                l1m = jnp.where(keep, l1m, 0.0)
            hi = l1m.astype(BF16)
            lo = (l1m - hi.astype(F32)).astype(BF16)
            later = _dot(hi, tri) + _dot(lo, tri)
            run = run_ref[r]
            a = jnp.exp(z + l1m + later + run)
            if masked:
                a = jnp.where(keep, a, 0.0)
            acc_ref[r] = acc_ref[r] + _dot(a.astype(BF16), vt)
            run_ref[r] = run + jnp.sum(l1m, axis=1, keepdims=True)

    nd = tq // cw
    for d in range(nd - 1, -1, -1):
        step(pl.multiple_of(q0 + d * cw, cw), True)

    def body(j, carry):
        step(pl.multiple_of(q0 - (j + 1) * cw, cw), False)
        return carry

    lax.fori_loop(0, qi * nd, body, 0)

    o = jnp.zeros((tq, LANES), F32)
    for r in range(2):
        o = o + jnp.where(masks[r], acc_ref[r], 0.0)
    o_ref[0] = o.astype(BF16)


def _sb_attention(h3, tq=512, cw=256):
    b, s, _ = h3.shape
    nhp = N_HEADS_SB // 2
    base = 3 * (N_HEADS_FOX // 2)
    kern = functools.partial(_sb_kernel, tq=tq, cw=cw)
    return pl.pallas_call(
        kern,
        grid=(b, nhp, s // tq),
        in_specs=[
            pl.BlockSpec((1, tq, LANES), lambda bi, hp, qi: (bi, qi, base + hp)),
            pl.BlockSpec((1, s, LANES), lambda bi, hp, qi: (bi, 0, base + nhp + hp)),
            pl.BlockSpec((1, s, LANES), lambda bi, hp, qi: (bi, 0, base + 2 * nhp + hp)),
        ],
        out_specs=pl.BlockSpec((1, tq, LANES), lambda bi, hp, qi: (bi, qi, hp)),
        out_shape=jax.ShapeDtypeStruct((b, s, SB_W), BF16),
        scratch_shapes=[
            pltpu.VMEM((2, tq, 1), F32),
            pltpu.VMEM((2, tq, LANES), F32),
        ],
        compiler_params=_cparams("parallel", "parallel", "arbitrary"),
        name="sb_attn",
    )(h3, h3, h3)


def _swa_proj_kernel(x_ref, w_ref, cos_ref, sa_ref, sb_ref, q_ref, kv_ref, *, n_rope_blocks, n_q_blocks):
    xb = x_ref[...].astype(BF16)
    y = _dot(xb, w_ref[...])
    cos = cos_ref[...]
    sin_a = sa_ref[...]
    sin_b = sb_ref[...]
    n_blocks = y.shape[1] // LANES
    for c in range(n_blocks):
        blk = y[:, c * LANES:(c + 1) * LANES]
        if c < n_rope_blocks:
            half = ROPE_DIM // 2
            blk = (blk * cos + pltpu.roll(blk, half, axis=1) * sin_a
                   + pltpu.roll(blk, LANES - half, axis=1) * sin_b)
        blk = blk.astype(BF16)
        if c < n_q_blocks:
            q_ref[:, c * LANES:(c + 1) * LANES] = blk
        else:
            kv_ref[:, (c - n_q_blocks) * LANES:(c - n_q_blocks + 1) * LANES] = blk


def _rope_tables(s):
    half = ROPE_DIM // 2
    inv = ROPE_THETA ** (-jnp.arange(half, dtype=F32) * 2.0 / ROPE_DIM)
    ang = jnp.arange(s, dtype=F32)[:, None] * inv[None, :]
    cos, sin = jnp.cos(ang), jnp.sin(ang)
    d = jnp.arange(LANES) % HEAD_DIM
    idx = d % half
    cos_t = jnp.where(d[None, :] < ROPE_DIM, cos[:, idx], 1.0)
    sin_a = jnp.where((d[None, :] >= half) & (d[None, :] < ROPE_DIM), sin[:, idx], 0.0)
    sin_b = jnp.where(d[None, :] < half, -sin[:, idx], 0.0)
    return cos_t.astype(F32), sin_a.astype(F32), sin_b.astype(F32)


def _swa_proj(x2d, w, tables, s, tm=512):
    t = x2d.shape[0]
    n = w.shape[1]
    nq = N_HEADS_SWA * HEAD_DIM
    nkv = n - nq
    spt = s // tm
    kern = functools.partial(_swa_proj_kernel, n_rope_blocks=(nq + nkv // 2) // LANES, n_q_blocks=nq // LANES)
    tab_spec = pl.BlockSpec((tm, LANES), lambda i: (i % spt, 0))
    return pl.pallas_call(
        kern,
        grid=(t // tm,),
        in_specs=[
            pl.BlockSpec((tm, D_MODEL), lambda i: (i, 0)),
            pl.BlockSpec((D_MODEL, n), lambda i: (0, 0)),
            tab_spec, tab_spec, tab_spec,
        ],
        out_specs=[
            pl.BlockSpec((tm, nq), lambda i: (i, 0)),
            pl.BlockSpec((tm, nkv), lambda i: (i, 0)),
        ],
        out_shape=[jax.ShapeDtypeStruct((t, nq), BF16), jax.ShapeDtypeStruct((t, nkv), BF16)],
        compiler_params=_cparams("parallel"),
        name="swa_proj",
    )(x2d, w, *tables)


def _swa_kernel(sink_ref, q_ref, kv_ref, o_ref, *, tq):
    qi = pl.program_id(1)
    q0 = qi * tq
    kw = tq + WINDOW
    kstart = pl.multiple_of(jnp.maximum(q0 - WINDOW, 0), WINDOW)
    kv = kv_ref[0, pl.ds(kstart, kw), :]
    row = q0 + lax.broadcasted_iota(jnp.int32, (tq, kw), 0)
    col = kstart + lax.broadcasted_iota(jnp.int32, (tq, kw), 1)
    diff = row - col
    keep = (diff >= 0) & (diff < WINDOW)
    masks = _head_masks((tq, LANES))
    group = N_HEADS_SWA // N_KV_SWA
    for c in range(N_HEADS_SWA // 2):
        g = (2 * c) // group
        kt = kv[:, g * LANES:(g + 1) * LANES]
        vt = kv[:, (N_KV_SWA + g) * LANES:(N_KV_SWA + g + 1) * LANES]
        qblk = q_ref[0, :, c * LANES:(c + 1) * LANES]
        o = jnp.zeros((tq, LANES), F32)
        for r in range(2):
            qm = (jnp.where(masks[r], qblk, 0.0) * ATTN_SCALE).astype(BF16)
            s = jnp.where(keep, _dot_nt(qm, kt), NEG_INF)
            sink = sink_ref[2 * c + r]
            mx = jnp.maximum(jnp.max(s, axis=1, keepdims=True), sink)
            e = jnp.exp(s - mx)
            den = jnp.sum(e, axis=1, keepdims=True) + jnp.exp(sink - mx)
            pv = _dot(e.astype(BF16), vt)
            o = o + jnp.where(masks[r], pv / den, 0.0)
        o_ref[0, :, c * LANES:(c + 1) * LANES] = o.astype(BF16)


def _swa_attention(q3, kv3, sinks, tq=128):
    b, s, nq = q3.shape
    nkv = kv3.shape[2]
    kern = functools.partial(_swa_kernel, tq=tq)
    return pl.pallas_call(
        kern,
        grid=(b, s // tq),
        in_specs=[
            pl.BlockSpec(memory_space=pltpu.SMEM),
            pl.BlockSpec((1, tq, nq), lambda bi, qi: (bi, qi, 0)),
            pl.BlockSpec((1, s, nkv), lambda bi, qi: (bi, 0, 0)),
        ],
        out_specs=pl.BlockSpec((1, tq, nq), lambda bi, qi: (bi, qi, 0)),
        out_shape=jax.ShapeDtypeStruct((b, s, nq), BF16),
        compiler_params=_cparams("parallel", "arbitrary"),
        name="swa_attn",
    )(sinks, q3, kv3)


def _layer_norm(z, g, b):
    mu = jnp.mean(z, axis=-1, keepdims=True)
    zc = z - mu
    var = jnp.mean(zc * zc, axis=-1, keepdims=True)
    return zc * lax.rsqrt(var + LN_EPS) * g + b


def _outln_kernel(*refs, n_in):
    o_refs = refs[:n_in]
    w_refs = refs[n_in:2 * n_in]
    x_ref, g_ref, b_ref, y_ref, yb_ref = refs[2 * n_in:]
    h = _dot(o_refs[0][...], w_refs[0][...])
    for i in range(1, n_in):
        h = h + _dot(o_refs[i][...], w_refs[i][...])
    y = _layer_norm(DEEPNORM_ALPHA * x_ref[...] + h, g_ref[...], b_ref[...])
    y_ref[...] = y
    yb_ref[...] = y.astype(BF16)


def _out_proj_ln(os_, ws, x2d, g, b, tm=512):
    t = x2d.shape[0]
    n_in = len(os_)
    in_specs = [pl.BlockSpec((tm, o.shape[1]), lambda i: (i, 0)) for o in os_]
    in_specs += [pl.BlockSpec(w.shape, lambda i: (0, 0)) for w in ws]
    in_specs += [
        pl.BlockSpec((tm, D_MODEL), lambda i: (i, 0)),
        pl.BlockSpec((1, D_MODEL), lambda i: (0, 0)),
        pl.BlockSpec((1, D_MODEL), lambda i: (0, 0)),
    ]
    return pl.pallas_call(
        functools.partial(_outln_kernel, n_in=n_in),
        grid=(t // tm,),
        in_specs=in_specs,
        out_specs=[pl.BlockSpec((tm, D_MODEL), lambda i: (i, 0))] * 2,
        out_shape=[jax.ShapeDtypeStruct((t, D_MODEL), F32), jax.ShapeDtypeStruct((t, D_MODEL), BF16)],
        compiler_params=_cparams("parallel"),
        name="out_proj_ln",
    )(*os_, *ws, x2d, g, b)


def _router_kernel(x_ref, w_ref, b_ref, c_ref):
    x = x_ref[...]
    w = w_ref[...]
    xh = x.astype(BF16)
    xl = (x - xh.astype(F32)).astype(BF16)
    wh = w.astype(BF16)
    wl = (w - wh.astype(F32)).astype(BF16)
    logits = _dot(xh, wh) + (_dot(xl, wh) + _dot(xh, wl)) + b_ref[...]
    lane = lax.broadcasted_iota(jnp.int32, logits.shape, 1)
    logits = jnp.where(lane < N_EXPERTS, logits, NEG_INF)
    m1 = jnp.max(logits, axis=1, keepdims=True)
    i1 = jnp.min(jnp.where(logits == m1, lane, LANES), axis=1, keepdims=True)
    rest = jnp.where(lane == i1, NEG_INF, logits)
    m2 = jnp.max(rest, axis=1, keepdims=True)
    i2 = jnp.min(jnp.where(rest == m2, lane, LANES), axis=1, keepdims=True)
    e2 = jnp.exp(m2 - m1)
    den = 1.0 + e2
    c_ref[...] = jnp.where(lane == i1, 1.0 / den, 0.0) + jnp.where(lane == i2, e2 / den, 0.0)


def _router(x2d, w_pad, b_pad, tm=512):
    t = x2d.shape[0]
    return pl.pallas_call(
        _router_kernel,
        grid=(t // tm,),
        in_specs=[
            pl.BlockSpec((tm, D_MODEL), lambda i: (i, 0)),
            pl.BlockSpec((D_MODEL, LANES), lambda i: (0, 0)),
            pl.BlockSpec((1, LANES), lambda i: (0, 0)),
        ],
        out_specs=pl.BlockSpec((tm, LANES), lambda i: (i, 0)),
        out_shape=jax.ShapeDtypeStruct((t, LANES), F32),
        compiler_params=_cparams("parallel"),
        name="router",
    )(x2d, w_pad, b_pad)


def _ffn_kernel(eid_ref, x_ref, wg_ref, wu_ref, wd_ref, *rest, scaled):
    if scaled:
        c_ref, o_ref = rest
    else:
        (o_ref,) = rest
    e = pl.program_id(1)
    c = pl.program_id(2)
    x = x_ref[...]
    gate = _dot(x, wg_ref[0])
    up = _dot(x, wu_ref[0])
    h = gate * jax.nn.sigmoid(gate) * up
    if scaled:
        comb = c_ref[...]
        lane = lax.broadcasted_iota(jnp.int32, comb.shape, 1)
        h = h * jnp.sum(jnp.where(lane == eid_ref[e], comb, 0.0), axis=1, keepdims=True)
    contrib = _dot(h.astype(BF16), wd_ref[0])
    first = (e == 0) & (c == 0)

    @pl.when(first)
    def _():
        o_ref[...] = contrib

    @pl.when(jnp.logical_not(first))
    def _():
        o_ref[...] += contrib


def _ffn(xb, w_gu, w_d, eids, combine=None, tm=512, fc=512):
    r = xb.shape[0]
    f = w_d.shape[1]
    ncf = f // fc
    ne = eids.shape[0]
    scaled = combine is not None
    in_specs = [
        pl.BlockSpec((tm, D_MODEL), lambda i, e, c, eid: (i, 0)),
        pl.BlockSpec((1, D_MODEL, fc), lambda i, e, c, eid: (eid[e], 0, c)),
        pl.BlockSpec((1, D_MODEL, fc), lambda i, e, c, eid: (eid[e], 0, ncf + c)),
        pl.BlockSpec((1, fc, D_MODEL), lambda i, e, c, eid: (eid[e], c, 0)),
    ]
    args = [xb, w_gu, w_gu, w_d]
    if scaled:
        in_specs.append(pl.BlockSpec((tm, LANES), lambda i, e, c, eid: (i, 0)))
        args.append(combine)
    return pl.pallas_call(
        functools.partial(_ffn_kernel, scaled=scaled),
        grid_spec=pltpu.PrefetchScalarGridSpec(
            num_scalar_prefetch=1,
            grid=(r // tm, ne, ncf),
            in_specs=in_specs,
            out_specs=pl.BlockSpec((tm, D_MODEL), lambda i, e, c, eid: (i, 0)),
        ),
        out_shape=jax.ShapeDtypeStruct((r, D_MODEL), F32),
        compiler_params=_cparams("parallel", "arbitrary", "arbitrary"),
        name="ffn_moe" if scaled else "ffn_dense",
    )(eids, *args)


def _lnple_kernel(x_ref, h_ref, g_ref, b_ref, p_ref, wg_ref, wp_ref, o_ref):
    y = _layer_norm(DEEPNORM_ALPHA * x_ref[...] + h_ref[...], g_ref[...], b_ref[...])
    gate = jax.nn.sigmoid(_dot(y.astype(BF16), wg_ref[...]))
    proj = _dot(p_ref[...].astype(BF16), wp_ref[...])
    o_ref[...] = y + gate * proj


def _ln_ple(x2d, h2d, g, b, p2d, wg, wp, tm=512):
    t = x2d.shape[0]
    row = lambda n: pl.BlockSpec((tm, n), lambda i: (i, 0))
    full = lambda a: pl.BlockSpec(a.shape, lambda i: (0, 0))
    return pl.pallas_call(
        _lnple_kernel,
        grid=(t // tm,),
        in_specs=[row(D_MODEL), row(D_MODEL), full(g), full(b), row(D_PLE), full(wg), full(wp)],
        out_specs=row(D_MODEL),
        out_shape=jax.ShapeDtypeStruct((t, D_MODEL), F32),
        compiler_params=_cparams("parallel"),
        name="ln_ple",
    )(x2d, h2d, g, b, p2d, wg, wp)


def _even_mixer(x2d, b, s, w_in, b_f, w_out):
    nf, ns = FOX_W, SB_W
    o_f = 3 * nf
    w_main = jnp.concatenate([w_in[:, :o_f], w_in[:, o_f + N_HEADS_FOX:]], axis=1).astype(BF16)
    w_f = jnp.pad(w_in[:, o_f:o_f + N_HEADS_FOX], ((0, 0), (0, LANES - N_HEADS_FOX))).astype(BF16)
    bias_f = jnp.pad(b_f, (0, LANES - N_HEADS_FOX)).reshape(1, LANES)
    h2d, lf = _ab_proj(x2d, w_main, w_f, bias_f)
    lf_t = lf.reshape(b, s, LANES)[:, :, :N_HEADS_FOX].transpose(0, 2, 1)
    c = _cumsum_seq(lf_t)
    nhp = N_HEADS_FOX // 2
    crow = c.reshape(b, nhp, 2, s)
    ccol = crow.transpose(0, 1, 3, 2)
    h3 = h2d.reshape(b, s, -1)
    oa = _fox_attention(h3, ccol, crow)
    ob = _sb_attention(h3)
    w_out_b = w_out.astype(BF16)
    return [oa.reshape(b * s, nf), ob.reshape(b * s, ns)], [w_out_b[:nf], w_out_b[nf:]]


def _odd_mixer(x2d, b, s, w_qkv, sinks, w_out, tables):
    nq = N_HEADS_SWA * HEAD_DIM
    wq = w_qkv[:, :nq]
    wk = [w_qkv[:, nq + i * HEAD_DIM: nq + (i + 1) * HEAD_DIM] for i in range(N_KV_SWA)]
    wv = [w_qkv[:, nq + (N_KV_SWA + i) * HEAD_DIM: nq + (N_KV_SWA + i + 1) * HEAD_DIM] for i in range(N_KV_SWA)]
    dup = lambda ws: [w for w in ws for _ in range(2)]
    w = jnp.concatenate([wq] + dup(wk) + dup(wv), axis=1).astype(BF16)
    q2d, kv2d = _swa_proj(x2d, w, tables, s)
    o = _swa_attention(q2d.reshape(b, s, nq), kv2d.reshape(b, s, -1), sinks)
    return [o.reshape(b * s, nq)], [w_out.astype(BF16)]


def kernel(x, p, ln_mix_g, ln_mix_b, ln_ffn_g, ln_ffn_b, ab_w_in, ab_b_f, ab_w_out, c_w_qkv, c_sinks, c_w_out,
           ffn_w_gate_up, ffn_w_down, router_w, router_b, moe_w_gate_up, moe_w_down, ple_w_gate, ple_w_proj):
    b, s, d = x.shape
    t = b * s
    x2d = x.reshape(t, d)
    tables = _rope_tables(s)
    row = lambda v: v.reshape(1, -1)
    for i in range(DEPTH):
        j = i // 2
        if i % 2 == 0:
            os_, ws = _even_mixer(x2d, b, s, ab_w_in[j], ab_b_f[j], ab_w_out[j])
        else:
            os_, ws = _odd_mixer(x2d, b, s, c_w_qkv[j], c_sinks[j], c_w_out[j], tables)
        x2d, xb = _out_proj_ln(os_, ws, x2d, row(ln_mix_g[i]), row(ln_mix_b[i]))
        if i % 2 == 0:
            h = _ffn(xb, ffn_w_gate_up[j][None].astype(BF16), ffn_w_down[j][None].astype(BF16),
                     jnp.zeros((1,), jnp.int32), tm=512, fc=1408)
        else:
            rw = jnp.pad(router_w[j], ((0, 0), (0, LANES - N_EXPERTS)))
            rb = jnp.pad(router_b[j], (0, LANES - N_EXPERTS)).reshape(1, LANES)
            combine = _router(x2d, rw, rb)
            h = _ffn(xb, moe_w_gate_up[j].astype(BF16), moe_w_down[j].astype(BF16),
                     jnp.arange(N_EXPERTS, dtype=jnp.int32), combine=combine, tm=1024, fc=512)
        x2d = _ln_ple(x2d, h, row(ln_ffn_g[i]), row(ln_ffn_b[i]), p[i].reshape(t, D_PLE),
                      ple_w_gate[i].astype(BF16), ple_w_proj[i].astype(BF16))
    return x2d.reshape(b, s, d)
```

```python
import functools
import math

import jax
import jax.numpy as jnp
from jax import lax
from jax.experimental import pallas as pl
from jax.experimental.pallas import tpu as pltpu

F32 = jnp.float32
BF16 = jnp.bfloat16

D_MODEL = 1024
HEAD_DIM = 64
LANES = 128
N_HEADS_FOX = 8
N_HEADS_SB = 8
N_HEADS_SWA = 16
N_KV_SWA = 2
WINDOW = 128
ROPE_THETA = 500000.0
ROPE_DIM = HEAD_DIM // 4
N_EXPERTS = 8
D_PLE = 256
LN_EPS = 1e-5
DEPTH = 4
DEEPNORM_ALPHA = (2.0 * DEPTH) ** 0.25
ATTN_SCALE = HEAD_DIM ** -0.5
FOX_W = N_HEADS_FOX * HEAD_DIM
SB_W = N_HEADS_SB * HEAD_DIM
VMEM_LIMIT = 56 * 1024 * 1024

NEG_INF = float("-inf")
LOG2E = math.log2(math.e)


def _cparams(*sem):
    return pltpu.CompilerParams(dimension_semantics=sem, vmem_limit_bytes=VMEM_LIMIT)


def _dot(a, b):
    return jnp.dot(a, b, preferred_element_type=F32)


def _dot_nt(a, b):
    return lax.dot_general(a, b, (((1,), (1,)), ((), ())), preferred_element_type=F32)


def _log_sigmoid(x):
    return jnp.minimum(x, 0.0) - jnp.log1p(jnp.exp(-jnp.abs(x)))


def _head_masks(shape):
    lane = lax.broadcasted_iota(jnp.int32, shape, len(shape) - 1)
    return lane < HEAD_DIM, lane >= HEAD_DIM


def _ab_proj_kernel(x_ref, wk_ref, wt_ref, wf_ref, bf_ref, k_ref, ht_ref, lf_ref):
    xb = x_ref[...].astype(BF16)
    k_ref[...] = _dot(xb, wk_ref[...]).astype(BF16)
    ht_ref[0] = _dot_nt(wt_ref[...], xb).astype(BF16)
    lf_ref[...] = _log_sigmoid(_dot(xb, wf_ref[...]) + bf_ref[...])


def _ab_proj(x2d, wk, wt, wf, bf, b, s, tm=512):
    t = x2d.shape[0]
    nk = wk.shape[1]
    nt = wt.shape[0]
    spt = s // tm
    return pl.pallas_call(
        _ab_proj_kernel,
        grid=(t // tm,),
        in_specs=[
            pl.BlockSpec((tm, D_MODEL), lambda i: (i, 0)),
            pl.BlockSpec((D_MODEL, nk), lambda i: (0, 0)),
            pl.BlockSpec((nt, D_MODEL), lambda i: (0, 0)),
            pl.BlockSpec((D_MODEL, LANES), lambda i: (0, 0)),
            pl.BlockSpec((1, LANES), lambda i: (0, 0)),
        ],
        out_specs=[
            pl.BlockSpec((tm, nk), lambda i: (i, 0)),
            pl.BlockSpec((1, nt, tm), lambda i: (i // spt, 0, i % spt)),
            pl.BlockSpec((tm, LANES), lambda i: (i, 0)),
        ],
        out_shape=[
            jax.ShapeDtypeStruct((t, nk), BF16),
            jax.ShapeDtypeStruct((b, nt, s), BF16),
            jax.ShapeDtypeStruct((t, LANES), F32),
        ],
        compiler_params=_cparams("parallel"),
        name="ab_proj",
    )(x2d, wk, wt, wf, bf)


def _split3(v):
    hi = v.astype(BF16).astype(F32)
    r1 = v - hi
    mid = r1.astype(BF16).astype(F32)
    return hi, mid, r1 - mid


def _cumsum_kernel(x_ref, c_ref, aug_ref, st_ref):
    x = x_ref[0]
    s = x.shape[1]
    lane = lax.broadcasted_iota(jnp.int32, x.shape, 1) & (LANES - 1)
    sh = 1
    while sh < LANES:
        x = x + jnp.where(lane >= sh, pltpu.roll(x, sh, axis=1), 0.0)
        sh *= 2
    carry = jnp.zeros((x.shape[0], 1), F32)
    for c in range(s // LANES):
        blk = x[:, c * LANES:(c + 1) * LANES] + carry
        c_ref[0, :, c * LANES:(c + 1) * LANES] = blk
        carry = blk[:, LANES - 1:LANES]

    ones = jnp.ones((3, s), F32)
    st_ref[...] = jnp.zeros(st_ref.shape, F32)
    for p in range(x.shape[0] // 2):
        for which, base in ((0, HEAD_DIM), (1, 0)):
            negc = -c_ref[0, 2 * p + which:2 * p + which + 1, :]
            hi, mid, lo = _split3(negc)
            st_ref[base:base + 1, :] = hi
            st_ref[base + 1:base + 2, :] = mid
            st_ref[base + 2:base + 3, :] = lo
            st_ref[base + 3:base + 6, :] = ones
        for c in range(s // LANES):
            blk = st_ref[:, c * LANES:(c + 1) * LANES]
            aug_ref[0, p, c * LANES:(c + 1) * LANES, :] = blk.T.astype(BF16)


def _cumsum_seq(lf_t):
    b, h, s = lf_t.shape
    return pl.pallas_call(
        _cumsum_kernel,
        grid=(b,),
        in_specs=[pl.BlockSpec((1, h, s), lambda i: (i, 0, 0))],
        out_specs=[
            pl.BlockSpec((1, h, s), lambda i: (i, 0, 0)),
            pl.BlockSpec((1, h // 2, s, LANES), lambda i: (i, 0, 0, 0)),
        ],
        out_shape=[
            jax.ShapeDtypeStruct((b, h, s), F32),
            jax.ShapeDtypeStruct((b, h // 2, s, LANES), BF16),
        ],
        scratch_shapes=[pltpu.VMEM((LANES, s), F32)],
        compiler_params=_cparams("parallel"),
        name="fox_cumsum",
    )(lf_t)


def _fox_kernel(qt_ref, k_ref, aug_ref, vt_ref, c_ref, o_ref, m_ref, acc_ref, sa_ref, sb_ref, *, tq, tk):
    hp = pl.program_id(1)
    qi = pl.program_id(2)
    q0 = qi * tq
    qt = qt_ref[0].astype(F32)
    sub = lax.broadcasted_iota(jnp.int32, (LANES, tq), 0)
    lane = lax.broadcasted_iota(jnp.int32, (tk, LANES), 1)
    vsub = lax.broadcasted_iota(jnp.int32, (LANES, tk), 0)
    own_rows = [sub < HEAD_DIM, sub >= HEAD_DIM]
    own_lanes = [lane < HEAD_DIM, lane >= HEAD_DIM]
    own_vrows = [vsub < HEAD_DIM, vsub >= HEAD_DIM]

    qaug = []
    for r in range(2):
        base = HEAD_DIM * (1 - r)
        ci = c_ref[0, pl.ds(2 * hp + r, 1), :]
        hi, mid, lo = _split3(ci)
        t = jnp.where(own_rows[r], qt, 0.0)
        t = jnp.where((sub >= base) & (sub < base + 3), 1.0, t)
        t = jnp.where(sub == base + 3, hi, t)
        t = jnp.where(sub == base + 4, mid, t)
        t = jnp.where(sub == base + 5, lo, t)
        qaug.append(t.astype(BF16))

    m_ref[...] = jnp.full(m_ref.shape, NEG_INF, F32)
    acc_ref[...] = jnp.zeros(acc_ref.shape, F32)

    def scores(k0, s_ref):
        kt = k_ref[0, pl.ds(k0, tk), :]
        ca = aug_ref[0, 0, pl.ds(k0, tk), :]
        for r in range(2):
            ka = jnp.where(own_lanes[r], kt, ca)
            s_ref[r] = _dot(ka, qaug[r])

    def softmax_pv(k0, s_ref, masked):
        vt = vt_ref[0, :, pl.ds(k0, tk)]
        if masked:
            key = k0 + lax.broadcasted_iota(jnp.int32, (tk, tq), 0)
            qry = q0 + lax.broadcasted_iota(jnp.int32, (tk, tq), 1)
            keep = key <= qry
        for r in range(2):
            st = s_ref[r]
            if masked:
                st = jnp.where(keep, st, NEG_INF)
            m_prev = m_ref[r]
            m_new = jnp.maximum(m_prev, jnp.max(st, axis=0, keepdims=True))
            alpha = jnp.exp(m_prev - m_new)
            pt = jnp.exp(st - m_new).astype(BF16)
            va = jnp.where(own_vrows[r], vt, 1.0)
            acc_ref[r] = alpha * acc_ref[r] + _dot(va, pt)
            m_ref[r] = m_new

    tile = lambda j: pl.multiple_of(j * tk, tk)
    scores(tile(0), sa_ref)

    def body(i, carry):
        scores(tile(2 * i + 1), sb_ref)
        softmax_pv(tile(2 * i), sa_ref, False)
        scores(tile(2 * i + 2), sa_ref)
        softmax_pv(tile(2 * i + 1), sb_ref, False)
        return carry

    lax.fori_loop(0, qi, body, 0)
    scores(tile(2 * qi + 1), sb_ref)
    softmax_pv(tile(2 * qi), sa_ref, True)
    softmax_pv(tile(2 * qi + 1), sb_ref, True)

    ot = jnp.zeros((LANES, tq), F32)
    for r in range(2):
        a = acc_ref[r]
        den = a[HEAD_DIM * (1 - r):HEAD_DIM * (1 - r) + 1, :]
        ot = ot + jnp.where(own_rows[r], a / den, 0.0)
    o_ref[0] = ot.T.astype(BF16)


def _fox_attention(k3, ht, caug, crow, tq=512):
    b, s, _ = k3.shape
    nhp = N_HEADS_FOX // 2
    tk = tq // 2
    kern = functools.partial(_fox_kernel, tq=tq, tk=tk)
    return pl.pallas_call(
        kern,
        grid=(b, nhp, s // tq),
        in_specs=[
            pl.BlockSpec((1, LANES, tq), lambda bi, hp, qi: (bi, hp, qi)),
            pl.BlockSpec((1, s, LANES), lambda bi, hp, qi: (bi, 0, hp)),
            pl.BlockSpec((1, 1, s, LANES), lambda bi, hp, qi: (bi, hp, 0, 0)),
            pl.BlockSpec((1, LANES, s), lambda bi, hp, qi: (bi, nhp + hp, 0)),
            pl.BlockSpec((1, N_HEADS_FOX, tq), lambda bi, hp, qi: (bi, 0, qi)),
        ],
        out_specs=pl.BlockSpec((1, tq, LANES), lambda bi, hp, qi: (bi, qi, hp)),
        out_shape=jax.ShapeDtypeStruct((b, s, FOX_W), BF16),
        scratch_shapes=[
            pltpu.VMEM((2, 1, tq), F32),
            pltpu.VMEM((2, LANES, tq), F32),
            pltpu.VMEM((2, tk, tq), F32),
            pltpu.VMEM((2, tk, tq), F32),
        ],
        compiler_params=_cparams("parallel", "parallel", "arbitrary"),
        name="fox_attn",
    )(ht, k3, caug, ht, crow)


def _sb_kernel(qt_ref, k_ref, vt_ref, o_ref, run_ref, acc_ref, za_ref, zb_ref, *, tq, cw):
    qi = pl.program_id(2)
    q0 = qi * tq
    qt = qt_ref[0]
    sub = lax.broadcasted_iota(jnp.int32, (LANES, tq), 0)
    own_rows = [sub < HEAD_DIM, sub >= HEAD_DIM]
    qm = [jnp.where(mk, qt, 0.0) for mk in own_rows]

    ss = lax.broadcasted_iota(jnp.int32, (cw, cw), 0)
    jj = lax.broadcasted_iota(jnp.int32, (cw, cw), 1)
    tri = jnp.where(jj > ss, 1.0, 0.0).astype(BF16)

    run_ref[...] = jnp.zeros(run_ref.shape, F32)
    acc_ref[...] = jnp.zeros(acc_ref.shape, F32)

    def start_of(c):
        return pl.multiple_of(jnp.maximum(q0 + tq - (c + 1) * cw, 0), cw)

    def logits(c, z_ref):
        kt = k_ref[0, pl.ds(start_of(c), cw), :]
        for r in range(2):
            z_ref[r] = _dot(kt, qm[r])

    def weights_pv(c, z_ref, masked):
        k0 = start_of(c)
        vt = vt_ref[0, :, pl.ds(k0, cw)]
        if masked:
            key = k0 + lax.broadcasted_iota(jnp.int32, (cw, tq), 0)
            qry = q0 + lax.broadcasted_iota(jnp.int32, (cw, tq), 1)
            keep = key < qry
        for r in range(2):
            z = z_ref[r]
            e = jnp.exp2(jnp.abs(z) * -LOG2E)
            sp = jnp.maximum(z, 0.0) + jnp.log(1.0 + e)
            if masked:
                sp = jnp.where(keep, sp, 0.0)
            hi = sp.astype(BF16)
            lo = (sp - hi.astype(F32)).astype(BF16)
            later = _dot(tri, hi) + _dot(tri, lo)
            run = run_ref[r]
            a = jnp.exp((z - sp) - (later + run))
            if masked:
                a = jnp.where(keep, a, 0.0)
            acc_ref[r] = acc_ref[r] + _dot(vt, a.astype(BF16))
            run_ref[r] = run + jnp.sum(sp, axis=0, keepdims=True)

    logits(0, za_ref)
    logits(1, zb_ref)
    weights_pv(0, za_ref, True)
    logits(2, za_ref)
    weights_pv(1, zb_ref, True)

    def body(i, carry):
        logits(2 * i + 1, zb_ref)
        weights_pv(2 * i, za_ref, False)
        logits(2 * i + 2, za_ref)
        weights_pv(2 * i + 1, zb_ref, False)
        return carry

    lax.fori_loop(1, qi + 1, body, 0)

    ot = jnp.where(own_rows[0], acc_ref[0], acc_ref[1])
    o_ref[0] = ot.T.astype(BF16)


def _sb_attention(k3, ht, tq=512):
    b, s, _ = k3.shape
    cw = tq // 2
    nhp = N_HEADS_SB // 2
    kern = functools.partial(_sb_kernel, tq=tq, cw=cw)
    return pl.pallas_call(
        kern,
        grid=(b, nhp, s // tq),
        in_specs=[
            pl.BlockSpec((1, LANES, tq), lambda bi, hp, qi: (bi, 2 * nhp + hp, qi)),
            pl.BlockSpec((1, s, LANES), lambda bi, hp, qi: (bi, 0, nhp + hp)),
            pl.BlockSpec((1, LANES, s), lambda bi, hp, qi: (bi, 3 * nhp + hp, 0)),
        ],
        out_specs=pl.BlockSpec((1, tq, LANES), lambda bi, hp, qi: (bi, qi, hp)),
        out_shape=jax.ShapeDtypeStruct((b, s, SB_W), BF16),
        scratch_shapes=[
            pltpu.VMEM((2, 1, tq), F32),
            pltpu.VMEM((2, LANES, tq), F32),
            pltpu.VMEM((2, cw, tq), F32),
            pltpu.VMEM((2, cw, tq), F32),
        ],
        compiler_params=_cparams("parallel", "parallel", "arbitrary"),
        name="sb_attn",
    )(ht, k3, ht)


def _swa_proj_kernel(x_ref, w_ref, cos_ref, sa_ref, sb_ref, q_ref, kv_ref, *, n_rope_blocks, n_q_blocks):
    xb = x_ref[...].astype(BF16)
    y = _dot(xb, w_ref[...])
    cos = cos_ref[...]
    sin_a = sa_ref[...]
    sin_b = sb_ref[...]
    n_blocks = y.shape[1] // LANES
    for c in range(n_blocks):
        blk = y[:, c * LANES:(c + 1) * LANES]
        if c < n_rope_blocks:
            half = ROPE_DIM // 2
            blk = (blk * cos + pltpu.roll(blk, half, axis=1) * sin_a
                   + pltpu.roll(blk, LANES - half, axis=1) * sin_b)
        blk = blk.astype(BF16)
        if c < n_q_blocks:
            q_ref[:, c * LANES:(c + 1) * LANES] = blk
        else:
            kv_ref[:, (c - n_q_blocks) * LANES:(c - n_q_blocks + 1) * LANES] = blk


def _rope_tables(s):
    half = ROPE_DIM // 2
    inv = ROPE_THETA ** (-jnp.arange(half, dtype=F32) * 2.0 / ROPE_DIM)
    ang = jnp.arange(s, dtype=F32)[:, None] * inv[None, :]
    cos, sin = jnp.cos(ang), jnp.sin(ang)
    d = jnp.arange(LANES) % HEAD_DIM
    idx = d % half
    cos_t = jnp.where(d[None, :] < ROPE_DIM, cos[:, idx], 1.0)
    sin_a = jnp.where((d[None, :] >= half) & (d[None, :] < ROPE_DIM), sin[:, idx], 0.0)
    sin_b = jnp.where(d[None, :] < half, -sin[:, idx], 0.0)
    return cos_t.astype(F32), sin_a.astype(F32), sin_b.astype(F32)


def _swa_proj(x2d, w, tables, s, tm=512):
    t = x2d.shape[0]
    n = w.shape[1]
    nq = N_HEADS_SWA * HEAD_DIM
    nkv = n - nq
    spt = s // tm
    kern = functools.partial(_swa_proj_kernel, n_rope_blocks=(nq + nkv // 2) // LANES, n_q_blocks=nq // LANES)
    tab_spec = pl.BlockSpec((tm, LANES), lambda i: (i % spt, 0))
    return pl.pallas_call(
        kern,
        grid=(t // tm,),
        in_specs=[
            pl.BlockSpec((tm, D_MODEL), lambda i: (i, 0)),
            pl.BlockSpec((D_MODEL, n), lambda i: (0, 0)),
            tab_spec, tab_spec, tab_spec,
        ],
        out_specs=[
            pl.BlockSpec((tm, nq), lambda i: (i, 0)),
            pl.BlockSpec((tm, nkv), lambda i: (i, 0)),
        ],
        out_shape=[jax.ShapeDtypeStruct((t, nq), BF16), jax.ShapeDtypeStruct((t, nkv), BF16)],
        compiler_params=_cparams("parallel"),
        name="swa_proj",
    )(x2d, w, *tables)


def _swa_kernel(sink_ref, q_ref, kv_ref, o_ref, *, tq):
    qi = pl.program_id(1)
    q0 = qi * tq
    kw = tq + WINDOW
    kstart = pl.multiple_of(jnp.maximum(q0 - WINDOW, 0), WINDOW)
    kv = kv_ref[0, pl.ds(kstart, kw), :]
    row = q0 + lax.broadcasted_iota(jnp.int32, (tq, kw), 0)
    col = kstart + lax.broadcasted_iota(jnp.int32, (tq, kw), 1)
    diff = row - col
    keep = (diff >= 0) & (diff < WINDOW)
    masks = _head_masks((tq, LANES))
    group = N_HEADS_SWA // N_KV_SWA
    for c in range(N_HEADS_SWA // 2):
        g = (2 * c) // group
        kt = kv[:, g * LANES:(g + 1) * LANES]
        vt = kv[:, (N_KV_SWA + g) * LANES:(N_KV_SWA + g + 1) * LANES]
        qblk = q_ref[0, :, c * LANES:(c + 1) * LANES]
        o = jnp.zeros((tq, LANES), F32)
        for r in range(2):
            qm = (jnp.where(masks[r], qblk, 0.0) * ATTN_SCALE).astype(BF16)
            s = jnp.where(keep, _dot_nt(qm, kt), NEG_INF)
            sink = sink_ref[2 * c + r]
            mx = jnp.maximum(jnp.max(s, axis=1, keepdims=True), sink)
            e = jnp.exp(s - mx)
            den = jnp.sum(e, axis=1, keepdims=True) + jnp.exp(sink - mx)
            pv = _dot(e.astype(BF16), vt)
            o = o + jnp.where(masks[r], pv / den, 0.0)
        o_ref[0, :, c * LANES:(c + 1) * LANES] = o.astype(BF16)


def _swa_attention(q3, kv3, sinks, tq=128):
    b, s, nq = q3.shape
    nkv = kv3.shape[2]
    kern = functools.partial(_swa_kernel, tq=tq)
    return pl.pallas_call(
        kern,
        grid=(b, s // tq),
        in_specs=[
            pl.BlockSpec(memory_space=pltpu.SMEM),
            pl.BlockSpec((1, tq, nq), lambda bi, qi: (bi, qi, 0)),
            pl.BlockSpec((1, s, nkv), lambda bi, qi: (bi, 0, 0)),
        ],
        out_specs=pl.BlockSpec((1, tq, nq), lambda bi, qi: (bi, qi, 0)),
        out_shape=jax.ShapeDtypeStruct((b, s, nq), BF16),
        compiler_params=_cparams("parallel", "arbitrary"),
        name="swa_attn",
    )(sinks, q3, kv3)


def _layer_norm(z, g, b):
    mu = jnp.mean(z, axis=-1, keepdims=True)
    zc = z - mu
    var = jnp.mean(zc * zc, axis=-1, keepdims=True)
    return zc * lax.rsqrt(var + LN_EPS) * g + b


def _outln_kernel(*refs, n_in):
    o_refs = refs[:n_in]
    w_refs = refs[n_in:2 * n_in]
    x_ref, g_ref, b_ref, y_ref, yb_ref = refs[2 * n_in:]
    h = _dot(o_refs[0][...], w_refs[0][...])
    for i in range(1, n_in):
        h = h + _dot(o_refs[i][...], w_refs[i][...])
    y = _layer_norm(DEEPNORM_ALPHA * x_ref[...] + h, g_ref[...], b_ref[...])
    y_ref[...] = y
    yb_ref[...] = y.astype(BF16)


def _out_proj_ln(os_, ws, x2d, g, b, tm=512):
    t = x2d.shape[0]
    n_in = len(os_)
    in_specs = [pl.BlockSpec((tm, o.shape[1]), lambda i: (i, 0)) for o in os_]
    in_specs += [pl.BlockSpec(w.shape, lambda i: (0, 0)) for w in ws]
    in_specs += [
        pl.BlockSpec((tm, D_MODEL), lambda i: (i, 0)),
        pl.BlockSpec((1, D_MODEL), lambda i: (0, 0)),
        pl.BlockSpec((1, D_MODEL), lambda i: (0, 0)),
    ]
    return pl.pallas_call(
        functools.partial(_outln_kernel, n_in=n_in),
        grid=(t // tm,),
        in_specs=in_specs,
        out_specs=[pl.BlockSpec((tm, D_MODEL), lambda i: (i, 0))] * 2,
        out_shape=[jax.ShapeDtypeStruct((t, D_MODEL), F32), jax.ShapeDtypeStruct((t, D_MODEL), BF16)],
        compiler_params=_cparams("parallel"),
        name="out_proj_ln",
    )(*os_, *ws, x2d, g, b)


def _router_kernel(x_ref, w_ref, b_ref, c_ref):
    x = x_ref[...]
    w = w_ref[...]
    xh = x.astype(BF16)
    xl = (x - xh.astype(F32)).astype(BF16)
    wh = w.astype(BF16)
    wl = (w - wh.astype(F32)).astype(BF16)
    logits = _dot(xh, wh) + (_dot(xl, wh) + _dot(xh, wl)) + b_ref[...]
    lane = lax.broadcasted_iota(jnp.int32, logits.shape, 1)
    logits = jnp.where(lane < N_EXPERTS, logits, NEG_INF)
    m1 = jnp.max(logits, axis=1, keepdims=True)
    i1 = jnp.min(jnp.where(logits == m1, lane, LANES), axis=1, keepdims=True)
    rest = jnp.where(lane == i1, NEG_INF, logits)
    m2 = jnp.max(rest, axis=1, keepdims=True)
    i2 = jnp.min(jnp.where(rest == m2, lane, LANES), axis=1, keepdims=True)
    e2 = jnp.exp(m2 - m1)
    den = 1.0 + e2
    c_ref[...] = jnp.where(lane == i1, 1.0 / den, 0.0) + jnp.where(lane == i2, e2 / den, 0.0)


def _router(x2d, w_pad, b_pad, tm=512):
    t = x2d.shape[0]
    return pl.pallas_call(
        _router_kernel,
        grid=(t // tm,),
        in_specs=[
            pl.BlockSpec((tm, D_MODEL), lambda i: (i, 0)),
            pl.BlockSpec((D_MODEL, LANES), lambda i: (0, 0)),
            pl.BlockSpec((1, LANES), lambda i: (0, 0)),
        ],
        out_specs=pl.BlockSpec((tm, LANES), lambda i: (i, 0)),
        out_shape=jax.ShapeDtypeStruct((t, LANES), F32),
        compiler_params=_cparams("parallel"),
        name="router",
    )(x2d, w_pad, b_pad)


def _ffn_kernel(eid_ref, x_ref, wg_ref, wu_ref, wd_ref, *rest, scaled):
    if scaled:
        c_ref, o_ref = rest
    else:
        (o_ref,) = rest
    e = pl.program_id(1)
    c = pl.program_id(2)
    x = x_ref[...]
    gate = _dot(x, wg_ref[0])
    up = _dot(x, wu_ref[0])
    h = gate * jax.nn.sigmoid(gate) * up
    if scaled:
        comb = c_ref[...]
        lane = lax.broadcasted_iota(jnp.int32, comb.shape, 1)
        h = h * jnp.sum(jnp.where(lane == eid_ref[e], comb, 0.0), axis=1, keepdims=True)
    contrib = _dot(h.astype(BF16), wd_ref[0])
    first = (e == 0) & (c == 0)

    @pl.when(first)
    def _():
        o_ref[...] = contrib

    @pl.when(jnp.logical_not(first))
    def _():
        o_ref[...] += contrib


def _ffn(xb, w_gu, w_d, eids, combine=None, tm=512, fc=512):
    r = xb.shape[0]
    f = w_d.shape[1]
    ncf = f // fc
    ne = eids.shape[0]
    scaled = combine is not None
    in_specs = [
        pl.BlockSpec((tm, D_MODEL), lambda i, e, c, eid: (i, 0)),
        pl.BlockSpec((1, D_MODEL, fc), lambda i, e, c, eid: (eid[e], 0, c)),
        pl.BlockSpec((1, D_MODEL, fc), lambda i, e, c, eid: (eid[e], 0, ncf + c)),
        pl.BlockSpec((1, fc, D_MODEL), lambda i, e, c, eid: (eid[e], c, 0)),
    ]
    args = [xb, w_gu, w_gu, w_d]
    if scaled:
        in_specs.append(pl.BlockSpec((tm, LANES), lambda i, e, c, eid: (i, 0)))
        args.append(combine)
    return pl.pallas_call(
        functools.partial(_ffn_kernel, scaled=scaled),
        grid_spec=pltpu.PrefetchScalarGridSpec(
            num_scalar_prefetch=1,
            grid=(r // tm, ne, ncf),
            in_specs=in_specs,
            out_specs=pl.BlockSpec((tm, D_MODEL), lambda i, e, c, eid: (i, 0)),
        ),
        out_shape=jax.ShapeDtypeStruct((r, D_MODEL), F32),
        compiler_params=_cparams("parallel", "arbitrary", "arbitrary"),
        name="ffn_moe" if scaled else "ffn_dense",
    )(eids, *args)


def _lnple_kernel(x_ref, h_ref, g_ref, b_ref, p_ref, wg_ref, wp_ref, o_ref):
    y = _layer_norm(DEEPNORM_ALPHA * x_ref[...] + h_ref[...], g_ref[...], b_ref[...])
    gate = jax.nn.sigmoid(_dot(y.astype(BF16), wg_ref[...]))
    proj = _dot(p_ref[...].astype(BF16), wp_ref[...])
    o_ref[...] = y + gate * proj


def _ln_ple(x2d, h2d, g, b, p2d, wg, wp, tm=512):
    t = x2d.shape[0]
    row = lambda n: pl.BlockSpec((tm, n), lambda i: (i, 0))
    full = lambda a: pl.BlockSpec(a.shape, lambda i: (0, 0))
    return pl.pallas_call(
        _lnple_kernel,
        grid=(t // tm,),
        in_specs=[row(D_MODEL), row(D_MODEL), full(g), full(b), row(D_PLE), full(wg), full(wp)],
        out_specs=row(D_MODEL),
        out_shape=jax.ShapeDtypeStruct((t, D_MODEL), F32),
        compiler_params=_cparams("parallel"),
        name="ln_ple",
    )(x2d, h2d, g, b, p2d, wg, wp)


def _even_mixer(x2d, b, s, w_in, b_f, w_out):
    nf, ns = FOX_W, SB_W
    o_f = 3 * nf
    o_s = o_f + N_HEADS_FOX
    qa, ka, va = w_in[:, :nf], w_in[:, nf:2 * nf], w_in[:, 2 * nf:o_f]
    qs, ks, vs = w_in[:, o_s:o_s + ns], w_in[:, o_s + ns:o_s + 2 * ns], w_in[:, o_s + 2 * ns:]
    w_k = jnp.concatenate([ka, ks], axis=1).astype(BF16)
    w_t = jnp.concatenate([qa * ATTN_SCALE, va, qs * ATTN_SCALE, vs], axis=1).T.astype(BF16)
    w_f = jnp.pad(w_in[:, o_f:o_s], ((0, 0), (0, LANES - N_HEADS_FOX))).astype(BF16)
    bias_f = jnp.pad(b_f, (0, LANES - N_HEADS_FOX)).reshape(1, LANES)
    k2d, ht, lf = _ab_proj(x2d, w_k, w_t, w_f, bias_f, b, s)
    lf_t = lf.reshape(b, s, LANES)[:, :, :N_HEADS_FOX].transpose(0, 2, 1)
    crow, caug = _cumsum_seq(lf_t)
    k3 = k2d.reshape(b, s, -1)
    oa = _fox_attention(k3, ht, caug, crow)
    ob = _sb_attention(k3, ht)
    w_out_b = w_out.astype(BF16)
    return [oa.reshape(b * s, nf), ob.reshape(b * s, ns)], [w_out_b[:nf], w_out_b[nf:]]


def _odd_mixer(x2d, b, s, w_qkv, sinks, w_out, tables):
    nq = N_HEADS_SWA * HEAD_DIM
    wq = w_qkv[:, :nq]
    wk = [w_qkv[:, nq + i * HEAD_DIM: nq + (i + 1) * HEAD_DIM] for i in range(N_KV_SWA)]
    wv = [w_qkv[:, nq + (N_KV_SWA + i) * HEAD_DIM: nq + (N_KV_SWA + i + 1) * HEAD_DIM] for i in range(N_KV_SWA)]
    dup = lambda ws: [w for w in ws for _ in range(2)]
    w = jnp.concatenate([wq] + dup(wk) + dup(wv), axis=1).astype(BF16)
    q2d, kv2d = _swa_proj(x2d, w, tables, s)
    o = _swa_attention(q2d.reshape(b, s, nq), kv2d.reshape(b, s, -1), sinks)
    return [o.reshape(b * s, nq)], [w_out.astype(BF16)]


def kernel(x, p, ln_mix_g, ln_mix_b, ln_ffn_g, ln_ffn_b, ab_w_in, ab_b_f, ab_w_out, c_w_qkv, c_sinks, c_w_out,
           ffn_w_gate_up, ffn_w_down, router_w, router_b, moe_w_gate_up, moe_w_down, ple_w_gate, ple_w_proj):
    b, s, d = x.shape
    t = b * s
    x2d = x.reshape(t, d)
    tables = _rope_tables(s)
    row = lambda v: v.reshape(1, -1)
    for i in range(DEPTH):
        j = i // 2
        if i % 2 == 0:
            os_, ws = _even_mixer(x2d, b, s, ab_w_in[j], ab_b_f[j], ab_w_out[j])
        else:
            os_, ws = _odd_mixer(x2d, b, s, c_w_qkv[j], c_sinks[j], c_w_out[j], tables)
        x2d, xb = _out_proj_ln(os_, ws, x2d, row(ln_mix_g[i]), row(ln_mix_b[i]))
        if i % 2 == 0:
            h = _ffn(xb, ffn_w_gate_up[j][None].astype(BF16), ffn_w_down[j][None].astype(BF16),
                     jnp.zeros((1,), jnp.int32), tm=512, fc=1408)
        else:
            rw = jnp.pad(router_w[j], ((0, 0), (0, LANES - N_EXPERTS)))
            rb = jnp.pad(router_b[j], (0, LANES - N_EXPERTS)).reshape(1, LANES)
            combine = _router(x2d, rw, rb)
            h = _ffn(xb, moe_w_gate_up[j].astype(BF16), moe_w_down[j].astype(BF16),
                     jnp.arange(N_EXPERTS, dtype=jnp.int32), combine=combine, tm=1024, fc=512)
        x2d = _ln_ple(x2d, h, row(ln_ffn_g[i]), row(ln_ffn_b[i]), p[i].reshape(t, D_PLE),
                      ple_w_gate[i].astype(BF16), ple_w_proj[i].astype(BF16))
    return x2d.reshape(b, s, d)
```

```python
import functools
import math

import jax
import jax.numpy as jnp
from jax import lax
from jax.experimental import pallas as pl
from jax.experimental.pallas import tpu as pltpu

F32 = jnp.float32
BF16 = jnp.bfloat16

D_MODEL = 1024
HEAD_DIM = 64
LANES = 128
N_HEADS_FOX = 8
N_HEADS_SB = 8
N_HEADS_SWA = 16
N_KV_SWA = 2
WINDOW = 128
ROPE_THETA = 500000.0
ROPE_DIM = HEAD_DIM // 4
N_EXPERTS = 8
D_PLE = 256
LN_EPS = 1e-5
DEPTH = 4
DEEPNORM_ALPHA = (2.0 * DEPTH) ** 0.25
ATTN_SCALE = HEAD_DIM ** -0.5
FOX_W = N_HEADS_FOX * HEAD_DIM
SB_W = N_HEADS_SB * HEAD_DIM
VMEM_LIMIT = 56 * 1024 * 1024

NEG_INF = float("-inf")
LOG2E = math.log2(math.e)


def _cparams(*sem):
    return pltpu.CompilerParams(dimension_semantics=sem, vmem_limit_bytes=VMEM_LIMIT)


def _dot(a, b):
    return jnp.dot(a, b, preferred_element_type=F32)


def _dot_nt(a, b):
    return lax.dot_general(a, b, (((1,), (1,)), ((), ())), preferred_element_type=F32)


def _log_sigmoid(x):
    return jnp.minimum(x, 0.0) - jnp.log1p(jnp.exp(-jnp.abs(x)))


def _head_masks(shape):
    lane = lax.broadcasted_iota(jnp.int32, shape, len(shape) - 1)
    return lane < HEAD_DIM, lane >= HEAD_DIM


def _ab_proj_kernel(x_ref, wk_ref, wt_ref, wf_ref, bf_ref, k_ref, ht_ref, lf_ref):
    xb = x_ref[...].astype(BF16)
    k_ref[...] = _dot(xb, wk_ref[...]).astype(BF16)
    ht_ref[0] = _dot_nt(wt_ref[...], xb).astype(BF16)
    lf_ref[...] = _log_sigmoid(_dot(xb, wf_ref[...]) + bf_ref[...])


def _ab_proj(x2d, wk, wt, wf, bf, b, s, tm=512):
    t = x2d.shape[0]
    nk = wk.shape[1]
    nt = wt.shape[0]
    spt = s // tm
    return pl.pallas_call(
        _ab_proj_kernel,
        grid=(t // tm,),
        in_specs=[
            pl.BlockSpec((tm, D_MODEL), lambda i: (i, 0)),
            pl.BlockSpec((D_MODEL, nk), lambda i: (0, 0)),
            pl.BlockSpec((nt, D_MODEL), lambda i: (0, 0)),
            pl.BlockSpec((D_MODEL, LANES), lambda i: (0, 0)),
            pl.BlockSpec((1, LANES), lambda i: (0, 0)),
        ],
        out_specs=[
            pl.BlockSpec((tm, nk), lambda i: (i, 0)),
            pl.BlockSpec((1, nt, tm), lambda i: (i // spt, 0, i % spt)),
            pl.BlockSpec((tm, LANES), lambda i: (i, 0)),
        ],
        out_shape=[
            jax.ShapeDtypeStruct((t, nk), BF16),
            jax.ShapeDtypeStruct((b, nt, s), BF16),
            jax.ShapeDtypeStruct((t, LANES), F32),
        ],
        compiler_params=_cparams("parallel"),
        name="ab_proj",
    )(x2d, wk, wt, wf, bf)


def _split3(v):
    hi = v.astype(BF16).astype(F32)
    r1 = v - hi
    mid = r1.astype(BF16).astype(F32)
    return hi, mid, r1 - mid


def _cumsum_kernel(x_ref, c_ref, aug_ref, st_ref):
    x = x_ref[0]
    s = x.shape[1]
    lane = lax.broadcasted_iota(jnp.int32, x.shape, 1) & (LANES - 1)
    sh = 1
    while sh < LANES:
        x = x + jnp.where(lane >= sh, pltpu.roll(x, sh, axis=1), 0.0)
        sh *= 2
    carry = jnp.zeros((x.shape[0], 1), F32)
    for c in range(s // LANES):
        blk = x[:, c * LANES:(c + 1) * LANES] + carry
        c_ref[0, :, c * LANES:(c + 1) * LANES] = blk
        carry = blk[:, LANES - 1:LANES]

    ones = jnp.ones((3, s), F32)
    st_ref[...] = jnp.zeros(st_ref.shape, F32)
    for p in range(x.shape[0] // 2):
        for which, base in ((0, HEAD_DIM), (1, 0)):
            negc = -c_ref[0, 2 * p + which:2 * p + which + 1, :]
            hi, mid, lo = _split3(negc)
            st_ref[base:base + 1, :] = hi
            st_ref[base + 1:base + 2, :] = mid
            st_ref[base + 2:base + 3, :] = lo
            st_ref[base + 3:base + 6, :] = ones
        for c in range(s // LANES):
            blk = st_ref[:, c * LANES:(c + 1) * LANES]
            aug_ref[0, p, c * LANES:(c + 1) * LANES, :] = blk.T.astype(BF16)


def _cumsum_seq(lf_t):
    b, h, s = lf_t.shape
    return pl.pallas_call(
        _cumsum_kernel,
        grid=(b,),
        in_specs=[pl.BlockSpec((1, h, s), lambda i: (i, 0, 0))],
        out_specs=[
            pl.BlockSpec((1, h, s), lambda i: (i, 0, 0)),
            pl.BlockSpec((1, h // 2, s, LANES), lambda i: (i, 0, 0, 0)),
        ],
        out_shape=[
            jax.ShapeDtypeStruct((b, h, s), F32),
            jax.ShapeDtypeStruct((b, h // 2, s, LANES), BF16),
        ],
        scratch_shapes=[pltpu.VMEM((LANES, s), F32)],
        compiler_params=_cparams("parallel"),
        name="fox_cumsum",
    )(lf_t)


def _fox_kernel(qt_ref, k_ref, aug_ref, vt_ref, c_ref, o_ref, m_ref, acc_ref, sa_ref, sb_ref, *, tq, tk):
    hp = pl.program_id(1)
    qi = pl.program_id(2)
    q0 = qi * tq
    qt = qt_ref[0].astype(F32)
    sub = lax.broadcasted_iota(jnp.int32, (LANES, tq), 0)
    lane = lax.broadcasted_iota(jnp.int32, (tk, LANES), 1)
    vsub = lax.broadcasted_iota(jnp.int32, (LANES, tk), 0)
    own_rows = [sub < HEAD_DIM, sub >= HEAD_DIM]
    own_lanes = [lane < HEAD_DIM, lane >= HEAD_DIM]
    own_vrows = [vsub < HEAD_DIM, vsub >= HEAD_DIM]

    qaug = []
    for r in range(2):
        base = HEAD_DIM * (1 - r)
        ci = c_ref[0, pl.ds(2 * hp + r, 1), :]
        hi, mid, lo = _split3(ci)
        t = jnp.where(own_rows[r], qt, 0.0)
        t = jnp.where((sub >= base) & (sub < base + 3), 1.0, t)
        t = jnp.where(sub == base + 3, hi, t)
        t = jnp.where(sub == base + 4, mid, t)
        t = jnp.where(sub == base + 5, lo, t)
        qaug.append(t.astype(BF16))

    m_ref[...] = jnp.full(m_ref.shape, NEG_INF, F32)
    acc_ref[...] = jnp.zeros(acc_ref.shape, F32)

    def scores(k0, s_ref):
        kt = k_ref[0, pl.ds(k0, tk), :]
        ca = aug_ref[0, 0, pl.ds(k0, tk), :]
        for r in range(2):
            ka = jnp.where(own_lanes[r], kt, ca)
            s_ref[r] = _dot(ka, qaug[r])

    def softmax_pv(k0, s_ref, masked):
        vt = vt_ref[0, :, pl.ds(k0, tk)]
        if masked:
            key = k0 + lax.broadcasted_iota(jnp.int32, (tk, tq), 0)
            qry = q0 + lax.broadcasted_iota(jnp.int32, (tk, tq), 1)
            keep = key <= qry
        for r in range(2):
            st = s_ref[r]
            if masked:
                st = jnp.where(keep, st, NEG_INF)
            m_prev = m_ref[r]
            m_new = jnp.maximum(m_prev, jnp.max(st, axis=0, keepdims=True))
            alpha = jnp.exp(m_prev - m_new)
            pt = jnp.exp(st - m_new).astype(BF16)
            va = jnp.where(own_vrows[r], vt, 1.0)
            acc_ref[r] = alpha * acc_ref[r] + _dot(va, pt)
            m_ref[r] = m_new

    tile = lambda j: pl.multiple_of(j * tk, tk)
    scores(tile(0), sa_ref)

    def body(i, carry):
        scores(tile(2 * i + 1), sb_ref)
        softmax_pv(tile(2 * i), sa_ref, False)
        scores(tile(2 * i + 2), sa_ref)
        softmax_pv(tile(2 * i + 1), sb_ref, False)
        return carry

    lax.fori_loop(0, qi, body, 0)
    scores(tile(2 * qi + 1), sb_ref)
    softmax_pv(tile(2 * qi), sa_ref, True)
    softmax_pv(tile(2 * qi + 1), sb_ref, True)

    ot = jnp.zeros((LANES, tq), F32)
    for r in range(2):
        a = acc_ref[r]
        den = a[HEAD_DIM * (1 - r):HEAD_DIM * (1 - r) + 1, :]
        ot = ot + jnp.where(own_rows[r], a / den, 0.0)
    o_ref[0] = ot.T.astype(BF16)


def _fox_attention(k3, ht, caug, crow, tq=512):
    b, s, _ = k3.shape
    nhp = N_HEADS_FOX // 2
    tk = tq // 2
    kern = functools.partial(_fox_kernel, tq=tq, tk=tk)
    return pl.pallas_call(
        kern,
        grid=(b, nhp, s // tq),
        in_specs=[
            pl.BlockSpec((1, LANES, tq), lambda bi, hp, qi: (bi, hp, qi)),
            pl.BlockSpec((1, s, LANES), lambda bi, hp, qi: (bi, 0, hp)),
            pl.BlockSpec((1, 1, s, LANES), lambda bi, hp, qi: (bi, hp, 0, 0)),
            pl.BlockSpec((1, LANES, s), lambda bi, hp, qi: (bi, nhp + hp, 0)),
            pl.BlockSpec((1, N_HEADS_FOX, tq), lambda bi, hp, qi: (bi, 0, qi)),
        ],
        out_specs=pl.BlockSpec((1, tq, LANES), lambda bi, hp, qi: (bi, qi, hp)),
        out_shape=jax.ShapeDtypeStruct((b, s, FOX_W), BF16),
        scratch_shapes=[
            pltpu.VMEM((2, 1, tq), F32),
            pltpu.VMEM((2, LANES, tq), F32),
            pltpu.VMEM((2, tk, tq), F32),
            pltpu.VMEM((2, tk, tq), F32),
        ],
        compiler_params=_cparams("parallel", "parallel", "arbitrary"),
        name="fox_attn",
    )(ht, k3, caug, ht, crow)


def _sb_kernel(qt_ref, k_ref, vt_ref, o_ref, run_ref, acc_ref, za_ref, zb_ref, *, tq, cw):
    qi = pl.program_id(2)
    q0 = qi * tq
    qt = qt_ref[0]
    sub = lax.broadcasted_iota(jnp.int32, (LANES, tq), 0)
    own_rows = [sub < HEAD_DIM, sub >= HEAD_DIM]
    qm = [jnp.where(mk, qt, 0.0) for mk in own_rows]

    ss = lax.broadcasted_iota(jnp.int32, (cw, cw), 0)
    jj = lax.broadcasted_iota(jnp.int32, (cw, cw), 1)
    tri = jnp.where(jj > ss, 1.0, 0.0).astype(BF16)

    run_ref[...] = jnp.zeros(run_ref.shape, F32)
    acc_ref[...] = jnp.zeros(acc_ref.shape, F32)

    def start_of(c):
        return pl.multiple_of(jnp.maximum(q0 + tq - (c + 1) * cw, 0), cw)

    def logits(c, z_ref):
        kt = k_ref[0, pl.ds(start_of(c), cw), :]
        for r in range(2):
            z_ref[r] = _dot(kt, qm[r])

    def weights_pv(c, z_ref, masked):
        k0 = start_of(c)
        vt = vt_ref[0, :, pl.ds(k0, cw)]
        if masked:
            key = k0 + lax.broadcasted_iota(jnp.int32, (cw, tq), 0)
            qry = q0 + lax.broadcasted_iota(jnp.int32, (cw, tq), 1)
            keep = key < qry
        for r in range(2):
            z = z_ref[r]
            e = jnp.exp2(jnp.abs(z) * -LOG2E)
            sp = jnp.maximum(z, 0.0) + jnp.log(1.0 + e)
            if masked:
                sp = jnp.where(keep, sp, 0.0)
            later = _dot(tri, sp.astype(BF16))
            run = run_ref[r]
            a = jnp.exp((z - sp) - (later + run))
            if masked:
                a = jnp.where(keep, a, 0.0)
            acc_ref[r] = acc_ref[r] + _dot(vt, a.astype(BF16))
            run_ref[r] = run + jnp.sum(sp, axis=0, keepdims=True)

    logits(0, za_ref)
    logits(1, zb_ref)
    weights_pv(0, za_ref, True)
    logits(2, za_ref)
    weights_pv(1, zb_ref, True)

    def body(i, carry):
        logits(2 * i + 1, zb_ref)
        weights_pv(2 * i, za_ref, False)
        logits(2 * i + 2, za_ref)
        weights_pv(2 * i + 1, zb_ref, False)
        return carry

    lax.fori_loop(1, qi + 1, body, 0)

    ot = jnp.where(own_rows[0], acc_ref[0], acc_ref[1])
    o_ref[0] = ot.T.astype(BF16)


def _sb_attention(k3, ht, tq=512):
    b, s, _ = k3.shape
    cw = tq // 2
    nhp = N_HEADS_SB // 2
    kern = functools.partial(_sb_kernel, tq=tq, cw=cw)
    return pl.pallas_call(
        kern,
        grid=(b, nhp, s // tq),
        in_specs=[
            pl.BlockSpec((1, LANES, tq), lambda bi, hp, qi: (bi, 2 * nhp + hp, qi)),
            pl.BlockSpec((1, s, LANES), lambda bi, hp, qi: (bi, 0, nhp + hp)),
            pl.BlockSpec((1, LANES, s), lambda bi, hp, qi: (bi, 3 * nhp + hp, 0)),
        ],
        out_specs=pl.BlockSpec((1, tq, LANES), lambda bi, hp, qi: (bi, qi, hp)),
        out_shape=jax.ShapeDtypeStruct((b, s, SB_W), BF16),
        scratch_shapes=[
            pltpu.VMEM((2, 1, tq), F32),
            pltpu.VMEM((2, LANES, tq), F32),
            pltpu.VMEM((2, cw, tq), F32),
            pltpu.VMEM((2, cw, tq), F32),
        ],
        compiler_params=_cparams("parallel", "parallel", "arbitrary"),
        name="sb_attn",
    )(ht, k3, ht)


def _swa_proj_kernel(x_ref, w_ref, cos_ref, sa_ref, sb_ref, q_ref, kv_ref, *, n_rope_blocks, n_q_blocks):
    xb = x_ref[...].astype(BF16)
    y = _dot(xb, w_ref[...])
    cos = cos_ref[...]
    sin_a = sa_ref[...]
    sin_b = sb_ref[...]
    n_blocks = y.shape[1] // LANES
    for c in range(n_blocks):
        blk = y[:, c * LANES:(c + 1) * LANES]
        if c < n_rope_blocks:
            half = ROPE_DIM // 2
            blk = (blk * cos + pltpu.roll(blk, half, axis=1) * sin_a
                   + pltpu.roll(blk, LANES - half, axis=1) * sin_b)
        blk = blk.astype(BF16)
        if c < n_q_blocks:
            q_ref[:, c * LANES:(c + 1) * LANES] = blk
        else:
            kv_ref[:, (c - n_q_blocks) * LANES:(c - n_q_blocks + 1) * LANES] = blk


def _rope_tables(s):
    half = ROPE_DIM // 2
    inv = ROPE_THETA ** (-jnp.arange(half, dtype=F32) * 2.0 / ROPE_DIM)
    ang = jnp.arange(s, dtype=F32)[:, None] * inv[None, :]
    cos, sin = jnp.cos(ang), jnp.sin(ang)
    d = jnp.arange(LANES) % HEAD_DIM
    idx = d % half
    cos_t = jnp.where(d[None, :] < ROPE_DIM, cos[:, idx], 1.0)
    sin_a = jnp.where((d[None, :] >= half) & (d[None, :] < ROPE_DIM), sin[:, idx], 0.0)
    sin_b = jnp.where(d[None, :] < half, -sin[:, idx], 0.0)
    return cos_t.astype(F32), sin_a.astype(F32), sin_b.astype(F32)


def _swa_proj(x2d, w, tables, s, tm=512):
    t = x2d.shape[0]
    n = w.shape[1]
    nq = N_HEADS_SWA * HEAD_DIM
    nkv = n - nq
    spt = s // tm
    kern = functools.partial(_swa_proj_kernel, n_rope_blocks=(nq + nkv // 2) // LANES, n_q_blocks=nq // LANES)
    tab_spec = pl.BlockSpec((tm, LANES), lambda i: (i % spt, 0))
    return pl.pallas_call(
        kern,
        grid=(t // tm,),
        in_specs=[
            pl.BlockSpec((tm, D_MODEL), lambda i: (i, 0)),
            pl.BlockSpec((D_MODEL, n), lambda i: (0, 0)),
            tab_spec, tab_spec, tab_spec,
        ],
        out_specs=[
            pl.BlockSpec((tm, nq), lambda i: (i, 0)),
            pl.BlockSpec((tm, nkv), lambda i: (i, 0)),
        ],
        out_shape=[jax.ShapeDtypeStruct((t, nq), BF16), jax.ShapeDtypeStruct((t, nkv), BF16)],
        compiler_params=_cparams("parallel"),
        name="swa_proj",
    )(x2d, w, *tables)


def _swa_kernel(sink_ref, q_ref, kv_ref, o_ref, *, tq):
    qi = pl.program_id(1)
    q0 = qi * tq
    kw = tq + WINDOW
    kstart = pl.multiple_of(jnp.maximum(q0 - WINDOW, 0), WINDOW)
    kv = kv_ref[0, pl.ds(kstart, kw), :]
    row = q0 + lax.broadcasted_iota(jnp.int32, (tq, kw), 0)
    col = kstart + lax.broadcasted_iota(jnp.int32, (tq, kw), 1)
    diff = row - col
    keep = (diff >= 0) & (diff < WINDOW)
    masks = _head_masks((tq, LANES))
    group = N_HEADS_SWA // N_KV_SWA
    for c in range(N_HEADS_SWA // 2):
        g = (2 * c) // group
        kt = kv[:, g * LANES:(g + 1) * LANES]
        vt = kv[:, (N_KV_SWA + g) * LANES:(N_KV_SWA + g + 1) * LANES]
        qblk = q_ref[0, :, c * LANES:(c + 1) * LANES]
        o = jnp.zeros((tq, LANES), F32)
        for r in range(2):
            qm = (jnp.where(masks[r], qblk, 0.0) * ATTN_SCALE).astype(BF16)
            s = jnp.where(keep, _dot_nt(qm, kt), NEG_INF)
            sink = sink_ref[2 * c + r]
            mx = jnp.maximum(jnp.max(s, axis=1, keepdims=True), sink)
            e = jnp.exp(s - mx)
            den = jnp.sum(e, axis=1, keepdims=True) + jnp.exp(sink - mx)
            pv = _dot(e.astype(BF16), vt)
            o = o + jnp.where(masks[r], pv / den, 0.0)
        o_ref[0, :, c * LANES:(c + 1) * LANES] = o.astype(BF16)


def _swa_attention(q3, kv3, sinks, tq=128):
    b, s, nq = q3.shape
    nkv = kv3.shape[2]
    kern = functools.partial(_swa_kernel, tq=tq)
    return pl.pallas_call(
        kern,
        grid=(b, s // tq),
        in_specs=[
            pl.BlockSpec(memory_space=pltpu.SMEM),
            pl.BlockSpec((1, tq, nq), lambda bi, qi: (bi, qi, 0)),
            pl.BlockSpec((1, s, nkv), lambda bi, qi: (bi, 0, 0)),
        ],
        out_specs=pl.BlockSpec((1, tq, nq), lambda bi, qi: (bi, qi, 0)),
        out_shape=jax.ShapeDtypeStruct((b, s, nq), BF16),
        compiler_params=_cparams("parallel", "arbitrary"),
        name="swa_attn",
    )(sinks, q3, kv3)


def _layer_norm(z, g, b):
    mu = jnp.mean(z, axis=-1, keepdims=True)
    zc = z - mu
    var = jnp.mean(zc * zc, axis=-1, keepdims=True)
    return zc * lax.rsqrt(var + LN_EPS) * g + b


def _outln_kernel(*refs, n_in):
    o_refs = refs[:n_in]
    w_refs = refs[n_in:2 * n_in]
    x_ref, g_ref, b_ref, y_ref, yb_ref = refs[2 * n_in:]
    h = _dot(o_refs[0][...], w_refs[0][...])
    for i in range(1, n_in):
        h = h + _dot(o_refs[i][...], w_refs[i][...])
    y = _layer_norm(DEEPNORM_ALPHA * x_ref[...] + h, g_ref[...], b_ref[...])
    y_ref[...] = y
    yb_ref[...] = y.astype(BF16)


def _out_proj_ln(os_, ws, x2d, g, b, tm=512):
    t = x2d.shape[0]
    n_in = len(os_)
    in_specs = [pl.BlockSpec((tm, o.shape[1]), lambda i: (i, 0)) for o in os_]
    in_specs += [pl.BlockSpec(w.shape, lambda i: (0, 0)) for w in ws]
    in_specs += [
        pl.BlockSpec((tm, D_MODEL), lambda i: (i, 0)),
        pl.BlockSpec((1, D_MODEL), lambda i: (0, 0)),
        pl.BlockSpec((1, D_MODEL), lambda i: (0, 0)),
    ]
    return pl.pallas_call(
        functools.partial(_outln_kernel, n_in=n_in),
        grid=(t // tm,),
        in_specs=in_specs,
        out_specs=[pl.BlockSpec((tm, D_MODEL), lambda i: (i, 0))] * 2,
        out_shape=[jax.ShapeDtypeStruct((t, D_MODEL), F32), jax.ShapeDtypeStruct((t, D_MODEL), BF16)],
        compiler_params=_cparams("parallel"),
        name="out_proj_ln",
    )(*os_, *ws, x2d, g, b)


def _router_kernel(x_ref, w_ref, b_ref, c_ref):
    x = x_ref[...]
    w = w_ref[...]
    xh = x.astype(BF16)
    xl = (x - xh.astype(F32)).astype(BF16)
    wh = w.astype(BF16)
    wl = (w - wh.astype(F32)).astype(BF16)
    logits = _dot(xh, wh) + (_dot(xl, wh) + _dot(xh, wl)) + b_ref[...]
    lane = lax.broadcasted_iota(jnp.int32, logits.shape, 1)
    logits = jnp.where(lane < N_EXPERTS, logits, NEG_INF)
    m1 = jnp.max(logits, axis=1, keepdims=True)
    i1 = jnp.min(jnp.where(logits == m1, lane, LANES), axis=1, keepdims=True)
    rest = jnp.where(lane == i1, NEG_INF, logits)
    m2 = jnp.max(rest, axis=1, keepdims=True)
    i2 = jnp.min(jnp.where(rest == m2, lane, LANES), axis=1, keepdims=True)
    e2 = jnp.exp(m2 - m1)
    den = 1.0 + e2
    c_ref[...] = jnp.where(lane == i1, 1.0 / den, 0.0) + jnp.where(lane == i2, e2 / den, 0.0)


def _router(x2d, w_pad, b_pad, tm=512):
    t = x2d.shape[0]
    return pl.pallas_call(
        _router_kernel,
        grid=(t // tm,),
        in_specs=[
            pl.BlockSpec((tm, D_MODEL), lambda i: (i, 0)),
            pl.BlockSpec((D_MODEL, LANES), lambda i: (0, 0)),
            pl.BlockSpec((1, LANES), lambda i: (0, 0)),
        ],
        out_specs=pl.BlockSpec((tm, LANES), lambda i: (i, 0)),
        out_shape=jax.ShapeDtypeStruct((t, LANES), F32),
        compiler_params=_cparams("parallel"),
        name="router",
    )(x2d, w_pad, b_pad)


def _ffn_kernel(eid_ref, valid_ref, x_ref, wg_ref, wu_ref, wd_ref, o_ref, acc_ref):
    i = pl.program_id(0)
    c = pl.program_id(1)
    last = c == pl.num_programs(1) - 1
    valid = valid_ref[i] != 0

    @pl.when(valid)
    def _():
        x = x_ref[...]
        gate = _dot(x, wg_ref[0])
        up = _dot(x, wu_ref[0])
        h = gate * jax.nn.sigmoid(gate) * up
        contrib = _dot(h.astype(BF16), wd_ref[0])

        @pl.when(c == 0)
        def _():
            acc_ref[...] = contrib

        @pl.when(c != 0)
        def _():
            acc_ref[...] += contrib

        @pl.when(last)
        def _():
            o_ref[...] = acc_ref[...].astype(o_ref.dtype)

    @pl.when(jnp.logical_not(valid) & last)
    def _():
        o_ref[...] = jnp.zeros(o_ref.shape, o_ref.dtype)


def _ffn(xb, w_gu, w_d, eids, valid, out_dtype, tm, fc, name):
    r = xb.shape[0]
    f = w_d.shape[1]
    ncf = f // fc
    in_specs = [
        pl.BlockSpec((tm, D_MODEL), lambda i, c, eid, ok: (i, 0)),
        pl.BlockSpec((1, D_MODEL, fc), lambda i, c, eid, ok: (eid[i], 0, c * ok[i])),
        pl.BlockSpec((1, D_MODEL, fc), lambda i, c, eid, ok: (eid[i], 0, ncf + c * ok[i])),
        pl.BlockSpec((1, fc, D_MODEL), lambda i, c, eid, ok: (eid[i], c * ok[i], 0)),
    ]
    return pl.pallas_call(
        _ffn_kernel,
        grid_spec=pltpu.PrefetchScalarGridSpec(
            num_scalar_prefetch=2,
            grid=(r // tm, ncf),
            in_specs=in_specs,
            out_specs=pl.BlockSpec((tm, D_MODEL), lambda i, c, eid, ok: (i, 0)),
            scratch_shapes=[pltpu.VMEM((tm, D_MODEL), F32)],
        ),
        out_shape=jax.ShapeDtypeStruct((r, D_MODEL), out_dtype),
        compiler_params=_cparams("parallel", "arbitrary"),
        name=name,
    )(eids, valid, xb, w_gu, w_gu, w_d)


MOE_TM = 512


def _moe_gather_kernel(te_ref, clo_ref, nch_ref, x_ref, pos_ref, o_ref, acc_ref, *, tm):
    r = pl.program_id(0)
    e = te_ref[r]
    row = r * tm + lax.broadcasted_iota(jnp.int32, (tm, tm), 0)
    acc_ref[...] = jnp.zeros(acc_ref.shape, F32)

    def body(i, carry):
        t0 = pl.multiple_of((clo_ref[r] + i) * tm, tm)
        p = pos_ref[pl.ds(e, 1), pl.ds(t0, tm)]
        onehot = jnp.where(p == row, 1.0, 0.0).astype(BF16)
        acc_ref[...] += _dot(onehot, x_ref[pl.ds(t0, tm), :])
        return carry

    lax.fori_loop(0, nch_ref[r], body, 0)
    o_ref[...] = acc_ref[...].astype(BF16)


def _moe_gather(xb, pos_t, tile_e, clo, nch, n_rows, tm=MOE_TM):
    t = xb.shape[0]
    return pl.pallas_call(
        functools.partial(_moe_gather_kernel, tm=tm),
        grid_spec=pltpu.PrefetchScalarGridSpec(
            num_scalar_prefetch=3,
            grid=(n_rows // tm,),
            in_specs=[
                pl.BlockSpec((t, D_MODEL), lambda r, *_: (0, 0), pipeline_mode=pl.Buffered(1)),
                pl.BlockSpec(pos_t.shape, lambda r, *_: (0, 0)),
            ],
            out_specs=pl.BlockSpec((tm, D_MODEL), lambda r, *_: (r, 0)),
            scratch_shapes=[pltpu.VMEM((tm, D_MODEL), F32)],
        ),
        out_shape=jax.ShapeDtypeStruct((n_rows, D_MODEL), BF16),
        compiler_params=_cparams("arbitrary"),
        name="moe_gather",
    )(tile_e, clo, nch, xb, pos_t)


def _moe_combine_kernel(cidx_ref, need_ref, ys_ref, pos_ref, comb_ref, o_ref, *, tm, n_slots):
    i = pl.program_id(0)
    j = pl.program_id(1)

    @pl.when(j == 0)
    def _():
        o_ref[...] = jnp.zeros(o_ref.shape, F32)

    @pl.when(need_ref[i * n_slots + j] != 0)
    def _():
        e = j // 2
        lane = lax.broadcasted_iota(jnp.int32, pos_ref.shape, 1)
        pos_e = jnp.sum(jnp.where(lane == e, pos_ref[...], 0), axis=1, keepdims=True)
        lane_c = lax.broadcasted_iota(jnp.int32, comb_ref.shape, 1)
        g_e = jnp.sum(jnp.where(lane_c == e, comb_ref[...], 0.0), axis=1, keepdims=True)
        col = cidx_ref[i * n_slots + j] * tm + lax.broadcasted_iota(jnp.int32, (tm, tm), 1)
        onehot = jnp.where(pos_e == col, 1.0, 0.0).astype(BF16)
        o_ref[...] += g_e * _dot(onehot, ys_ref[...])


def _moe_combine(ys, pos, combine, cidx, need, tm=MOE_TM):
    t = pos.shape[0]
    n_slots = 2 * N_EXPERTS
    return pl.pallas_call(
        functools.partial(_moe_combine_kernel, tm=tm, n_slots=n_slots),
        grid_spec=pltpu.PrefetchScalarGridSpec(
            num_scalar_prefetch=2,
            grid=(t // tm, n_slots),
            in_specs=[
                pl.BlockSpec((tm, D_MODEL), lambda i, j, cidx, need: (cidx[i * n_slots + j], 0)),
                pl.BlockSpec((tm, N_EXPERTS), lambda i, j, cidx, need: (i, 0)),
                pl.BlockSpec((tm, LANES), lambda i, j, cidx, need: (i, 0)),
            ],
            out_specs=pl.BlockSpec((tm, D_MODEL), lambda i, j, cidx, need: (i, 0)),
        ),
        out_shape=jax.ShapeDtypeStruct((t, D_MODEL), F32),
        compiler_params=_cparams("parallel", "arbitrary"),
        name="moe_combine",
    )(cidx, need, ys, pos, combine)


def _moe_plan(combine, tm=MOE_TM):
    t = combine.shape[0]
    n_tok_tiles = t // tm
    n_row_tiles = (2 * t) // tm + N_EXPERTS
    sel = (combine[:, :N_EXPERTS] != 0.0).astype(jnp.int32)
    csum = jnp.cumsum(sel, axis=0)
    count = csum[-1]
    ntile_e = (count + tm - 1) // tm
    tile_end = jnp.cumsum(ntile_e)
    gstart = (tile_end - ntile_e) * tm
    pos = jnp.where(sel != 0, gstart[None, :] + csum - sel, -1)
    rt = jnp.arange(n_row_tiles, dtype=jnp.int32)
    tile_e = jnp.minimum(jnp.sum(rt[:, None] >= tile_end[None, :], axis=1), N_EXPERTS - 1).astype(jnp.int32)
    valid = (rt < tile_end[-1]).astype(jnp.int32)
    cb = jnp.concatenate([jnp.zeros((1, N_EXPERTS), jnp.int32), csum[tm - 1::tm]], axis=0)
    off = rt * tm - gstart[tile_e]
    cb_t = cb[:, tile_e].T
    clo = jnp.sum(cb_t[:, 1:] <= off[:, None], axis=1).astype(jnp.int32)
    cend = jnp.sum(cb_t[:, :-1] < (off + tm)[:, None], axis=1).astype(jnp.int32)
    nch = jnp.where(valid != 0, jnp.maximum(cend - clo, 0), 0).astype(jnp.int32)
    clo = jnp.minimum(clo, n_tok_tiles - 1)
    first = gstart[None, :] + cb[:-1]
    n_ie = cb[1:] - cb[:-1]
    lo = jnp.clip(first // tm, 0, n_row_tiles - 1)
    hi = jnp.clip((first + jnp.maximum(n_ie, 1) - 1) // tm, 0, n_row_tiles - 1)
    need_lo = n_ie > 0
    need_hi = need_lo & (hi > lo)
    cidx = jnp.stack([lo, jnp.where(need_hi, hi, lo)], axis=-1).reshape(-1).astype(jnp.int32)
    need = jnp.stack([need_lo, need_hi], axis=-1).reshape(-1).astype(jnp.int32)
    return dict(pos=pos.astype(jnp.int32), tile_e=tile_e, valid=valid, clo=clo, nch=nch, cidx=cidx, need=need,
                n_rows=n_row_tiles * tm)


def _moe(xb, combine, w_gu, w_d):
    plan = _moe_plan(combine)
    xs = _moe_gather(xb, plan["pos"].T, plan["tile_e"], plan["clo"], plan["nch"], plan["n_rows"])
    ys = _ffn(xs, w_gu, w_d, plan["tile_e"], plan["valid"], BF16, tm=MOE_TM, fc=512, name="ffn_moe")
    return _moe_combine(ys, plan["pos"], combine, plan["cidx"], plan["need"])


def _lnple_kernel(x_ref, h_ref, g_ref, b_ref, p_ref, wg_ref, wp_ref, o_ref):
    y = _layer_norm(DEEPNORM_ALPHA * x_ref[...] + h_ref[...], g_ref[...], b_ref[...])
    gate = jax.nn.sigmoid(_dot(y.astype(BF16), wg_ref[...]))
    proj = _dot(p_ref[...].astype(BF16), wp_ref[...])
    o_ref[...] = y + gate * proj


def _ln_ple(x2d, h2d, g, b, p2d, wg, wp, tm=512):
    t = x2d.shape[0]
    row = lambda n: pl.BlockSpec((tm, n), lambda i: (i, 0))
    full = lambda a: pl.BlockSpec(a.shape, lambda i: (0, 0))
    return pl.pallas_call(
        _lnple_kernel,
        grid=(t // tm,),
        in_specs=[row(D_MODEL), row(D_MODEL), full(g), full(b), row(D_PLE), full(wg), full(wp)],
        out_specs=row(D_MODEL),
        out_shape=jax.ShapeDtypeStruct((t, D_MODEL), F32),
        compiler_params=_cparams("parallel"),
        name="ln_ple",
    )(x2d, h2d, g, b, p2d, wg, wp)


def _even_mixer(x2d, b, s, w_in, b_f, w_out):
    nf, ns = FOX_W, SB_W
    o_f = 3 * nf
    o_s = o_f + N_HEADS_FOX
    qa, ka, va = w_in[:, :nf], w_in[:, nf:2 * nf], w_in[:, 2 * nf:o_f]
    qs, ks, vs = w_in[:, o_s:o_s + ns], w_in[:, o_s + ns:o_s + 2 * ns], w_in[:, o_s + 2 * ns:]
    w_k = jnp.concatenate([ka, ks], axis=1).astype(BF16)
    w_t = jnp.concatenate([qa * ATTN_SCALE, va, qs * ATTN_SCALE, vs], axis=1).T.astype(BF16)
    w_f = jnp.pad(w_in[:, o_f:o_s], ((0, 0), (0, LANES - N_HEADS_FOX))).astype(BF16)
    bias_f = jnp.pad(b_f, (0, LANES - N_HEADS_FOX)).reshape(1, LANES)
    k2d, ht, lf = _ab_proj(x2d, w_k, w_t, w_f, bias_f, b, s)
    lf_t = lf.reshape(b, s, LANES)[:, :, :N_HEADS_FOX].transpose(0, 2, 1)
    crow, caug = _cumsum_seq(lf_t)
    k3 = k2d.reshape(b, s, -1)
    oa = _fox_attention(k3, ht, caug, crow)
    ob = _sb_attention(k3, ht)
    w_out_b = w_out.astype(BF16)
    return [oa.reshape(b * s, nf), ob.reshape(b * s, ns)], [w_out_b[:nf], w_out_b[nf:]]


def _odd_mixer(x2d, b, s, w_qkv, sinks, w_out, tables):
    nq = N_HEADS_SWA * HEAD_DIM
    wq = w_qkv[:, :nq]
    wk = [w_qkv[:, nq + i * HEAD_DIM: nq + (i + 1) * HEAD_DIM] for i in range(N_KV_SWA)]
    wv = [w_qkv[:, nq + (N_KV_SWA + i) * HEAD_DIM: nq + (N_KV_SWA + i + 1) * HEAD_DIM] for i in range(N_KV_SWA)]
    dup = lambda ws: [w for w in ws for _ in range(2)]
    w = jnp.concatenate([wq] + dup(wk) + dup(wv), axis=1).astype(BF16)
    q2d, kv2d = _swa_proj(x2d, w, tables, s)
    o = _swa_attention(q2d.reshape(b, s, nq), kv2d.reshape(b, s, -1), sinks)
    return [o.reshape(b * s, nq)], [w_out.astype(BF16)]


def kernel(x, p, ln_mix_g, ln_mix_b, ln_ffn_g, ln_ffn_b, ab_w_in, ab_b_f, ab_w_out, c_w_qkv, c_sinks, c_w_out,
           ffn_w_gate_up, ffn_w_down, router_w, router_b, moe_w_gate_up, moe_w_down, ple_w_gate, ple_w_proj):
    b, s, d = x.shape
    t = b * s
    x2d = x.reshape(t, d)
    tables = _rope_tables(s)
    row = lambda v: v.reshape(1, -1)
    for i in range(DEPTH):
        j = i // 2
        if i % 2 == 0:
            os_, ws = _even_mixer(x2d, b, s, ab_w_in[j], ab_b_f[j], ab_w_out[j])
        else:
            os_, ws = _odd_mixer(x2d, b, s, c_w_qkv[j], c_sinks[j], c_w_out[j], tables)
        x2d, xb = _out_proj_ln(os_, ws, x2d, row(ln_mix_g[i]), row(ln_mix_b[i]))
        if i % 2 == 0:
            n_tiles = t // 512
            h = _ffn(xb, ffn_w_gate_up[j][None].astype(BF16), ffn_w_down[j][None].astype(BF16),
                     jnp.zeros((n_tiles,), jnp.int32), jnp.ones((n_tiles,), jnp.int32), F32,
                     tm=512, fc=1408, name="ffn_dense")
        else:
            rw = jnp.pad(router_w[j], ((0, 0), (0, LANES - N_EXPERTS)))
            rb = jnp.pad(router_b[j], (0, LANES - N_EXPERTS)).reshape(1, LANES)
            combine = _router(x2d, rw, rb)
            h = _moe(xb, combine, moe_w_gate_up[j].astype(BF16), moe_w_down[j].astype(BF16))
        x2d = _ln_ple(x2d, h, row(ln_ffn_g[i]), row(ln_ffn_b[i]), p[i].reshape(t, D_PLE),
                      ple_w_gate[i].astype(BF16), ple_w_proj[i].astype(BF16))
    return x2d.reshape(b, s, d)
```

```python
import functools
import math

import jax
import jax.numpy as jnp
from jax import lax
from jax.experimental import pallas as pl
from jax.experimental.pallas import tpu as pltpu

F32 = jnp.float32
BF16 = jnp.bfloat16

D_MODEL = 1024
HEAD_DIM = 64
LANES = 128
N_HEADS_FOX = 8
N_HEADS_SB = 8
N_HEADS_SWA = 16
N_KV_SWA = 2
WINDOW = 128
ROPE_THETA = 500000.0
ROPE_DIM = HEAD_DIM // 4
N_EXPERTS = 8
D_PLE = 256
LN_EPS = 1e-5
DEPTH = 4
DEEPNORM_ALPHA = (2.0 * DEPTH) ** 0.25
ATTN_SCALE = HEAD_DIM ** -0.5
FOX_W = N_HEADS_FOX * HEAD_DIM
SB_W = N_HEADS_SB * HEAD_DIM
VMEM_LIMIT = 56 * 1024 * 1024

NEG_INF = float("-inf")
LOG2E = math.log2(math.e)


def _cparams(*sem):
    return pltpu.CompilerParams(dimension_semantics=sem, vmem_limit_bytes=VMEM_LIMIT)


def _dot(a, b):
    return jnp.dot(a, b, preferred_element_type=F32)


def _dot_nt(a, b):
    return lax.dot_general(a, b, (((1,), (1,)), ((), ())), preferred_element_type=F32)


def _log_sigmoid(x):
    return jnp.minimum(x, 0.0) - jnp.log1p(jnp.exp(-jnp.abs(x)))


def _head_masks(shape):
    lane = lax.broadcasted_iota(jnp.int32, shape, len(shape) - 1)
    return lane < HEAD_DIM, lane >= HEAD_DIM


def _ab_proj_kernel(x_ref, wk_ref, wt_ref, wf_ref, bf_ref, k_ref, ht_ref, lf_ref):
    xb = x_ref[...].astype(BF16)
    k_ref[...] = _dot(xb, wk_ref[...]).astype(BF16)
    ht_ref[0] = _dot_nt(wt_ref[...], xb).astype(BF16)
    lf_ref[...] = _log_sigmoid(_dot(xb, wf_ref[...]) + bf_ref[...])


def _ab_proj(x2d, wk, wt, wf, bf, b, s, tm=512):
    t = x2d.shape[0]
    nk = wk.shape[1]
    nt = wt.shape[0]
    spt = s // tm
    return pl.pallas_call(
        _ab_proj_kernel,
        grid=(t // tm,),
        in_specs=[
            pl.BlockSpec((tm, D_MODEL), lambda i: (i, 0)),
            pl.BlockSpec((D_MODEL, nk), lambda i: (0, 0)),
            pl.BlockSpec((nt, D_MODEL), lambda i: (0, 0)),
            pl.BlockSpec((D_MODEL, LANES), lambda i: (0, 0)),
            pl.BlockSpec((1, LANES), lambda i: (0, 0)),
        ],
        out_specs=[
            pl.BlockSpec((tm, nk), lambda i: (i, 0)),
            pl.BlockSpec((1, nt, tm), lambda i: (i // spt, 0, i % spt)),
            pl.BlockSpec((tm, LANES), lambda i: (i, 0)),
        ],
        out_shape=[
            jax.ShapeDtypeStruct((t, nk), BF16),
            jax.ShapeDtypeStruct((b, nt, s), BF16),
            jax.ShapeDtypeStruct((t, LANES), F32),
        ],
        compiler_params=_cparams("parallel"),
        name="ab_proj",
    )(x2d, wk, wt, wf, bf)


def _split3(v):
    hi = v.astype(BF16).astype(F32)
    r1 = v - hi
    mid = r1.astype(BF16).astype(F32)
    return hi, mid, r1 - mid


def _cumsum_kernel(x_ref, c_ref, aug_ref, st_ref):
    x = x_ref[0]
    s = x.shape[1]
    lane = lax.broadcasted_iota(jnp.int32, x.shape, 1) & (LANES - 1)
    sh = 1
    while sh < LANES:
        x = x + jnp.where(lane >= sh, pltpu.roll(x, sh, axis=1), 0.0)
        sh *= 2
    carry = jnp.zeros((x.shape[0], 1), F32)
    for c in range(s // LANES):
        blk = x[:, c * LANES:(c + 1) * LANES] + carry
        c_ref[0, :, c * LANES:(c + 1) * LANES] = blk
        carry = blk[:, LANES - 1:LANES]

    ones = jnp.ones((3, s), F32)
    st_ref[...] = jnp.zeros(st_ref.shape, F32)
    for p in range(x.shape[0] // 2):
        for which, base in ((0, HEAD_DIM), (1, 0)):
            negc = -c_ref[0, 2 * p + which:2 * p + which + 1, :]
            hi, mid, lo = _split3(negc)
            st_ref[base:base + 1, :] = hi
            st_ref[base + 1:base + 2, :] = mid
            st_ref[base + 2:base + 3, :] = lo
            st_ref[base + 3:base + 6, :] = ones
        for c in range(s // LANES):
            blk = st_ref[:, c * LANES:(c + 1) * LANES]
            aug_ref[0, p, c * LANES:(c + 1) * LANES, :] = blk.T.astype(BF16)


def _cumsum_seq(lf_t):
    b, h, s = lf_t.shape
    return pl.pallas_call(
        _cumsum_kernel,
        grid=(b,),
        in_specs=[pl.BlockSpec((1, h, s), lambda i: (i, 0, 0))],
        out_specs=[
            pl.BlockSpec((1, h, s), lambda i: (i, 0, 0)),
            pl.BlockSpec((1, h // 2, s, LANES), lambda i: (i, 0, 0, 0)),
        ],
        out_shape=[
            jax.ShapeDtypeStruct((b, h, s), F32),
            jax.ShapeDtypeStruct((b, h // 2, s, LANES), BF16),
        ],
        scratch_shapes=[pltpu.VMEM((LANES, s), F32)],
        compiler_params=_cparams("parallel"),
        name="fox_cumsum",
    )(lf_t)


def _fox_kernel(qt_ref, k_ref, aug_ref, vt_ref, c_ref, o_ref, m_ref, acc_ref, sa_ref, sb_ref, *, tq, tk):
    hp = pl.program_id(1)
    qi = pl.program_id(2)
    q0 = qi * tq
    qt = qt_ref[0].astype(F32)
    sub = lax.broadcasted_iota(jnp.int32, (LANES, tq), 0)
    lane = lax.broadcasted_iota(jnp.int32, (tk, LANES), 1)
    vsub = lax.broadcasted_iota(jnp.int32, (LANES, tk), 0)
    own_rows = [sub < HEAD_DIM, sub >= HEAD_DIM]
    own_lanes = [lane < HEAD_DIM, lane >= HEAD_DIM]
    own_vrows = [vsub < HEAD_DIM, vsub >= HEAD_DIM]

    qaug = []
    for r in range(2):
        base = HEAD_DIM * (1 - r)
        ci = c_ref[0, pl.ds(2 * hp + r, 1), :]
        hi, mid, lo = _split3(ci)
        t = jnp.where(own_rows[r], qt, 0.0)
        t = jnp.where((sub >= base) & (sub < base + 3), 1.0, t)
        t = jnp.where(sub == base + 3, hi, t)
        t = jnp.where(sub == base + 4, mid, t)
        t = jnp.where(sub == base + 5, lo, t)
        qaug.append(t.astype(BF16))

    m_ref[...] = jnp.full(m_ref.shape, NEG_INF, F32)
    acc_ref[...] = jnp.zeros(acc_ref.shape, F32)

    def scores(k0, s_ref):
        kt = k_ref[0, pl.ds(k0, tk), :]
        ca = aug_ref[0, 0, pl.ds(k0, tk), :]
        for r in range(2):
            ka = jnp.where(own_lanes[r], kt, ca)
            s_ref[r] = _dot(ka, qaug[r])

    def softmax_pv(k0, s_ref, masked):
        vt = vt_ref[0, :, pl.ds(k0, tk)]
        if masked:
            key = k0 + lax.broadcasted_iota(jnp.int32, (tk, tq), 0)
            qry = q0 + lax.broadcasted_iota(jnp.int32, (tk, tq), 1)
            keep = key <= qry
        for r in range(2):
            st = s_ref[r]
            if masked:
                st = jnp.where(keep, st, NEG_INF)
            m_prev = m_ref[r]
            m_new = jnp.maximum(m_prev, jnp.max(st, axis=0, keepdims=True))
            alpha = jnp.exp(m_prev - m_new)
            pt = jnp.exp(st - m_new).astype(BF16)
            va = jnp.where(own_vrows[r], vt, 1.0)
            acc_ref[r] = alpha * acc_ref[r] + _dot(va, pt)
            m_ref[r] = m_new

    tile = lambda j: pl.multiple_of(j * tk, tk)
    scores(tile(0), sa_ref)

    def body(i, carry):
        scores(tile(2 * i + 1), sb_ref)
        softmax_pv(tile(2 * i), sa_ref, False)
        scores(tile(2 * i + 2), sa_ref)
        softmax_pv(tile(2 * i + 1), sb_ref, False)
        return carry

    lax.fori_loop(0, qi, body, 0)
    scores(tile(2 * qi + 1), sb_ref)
    softmax_pv(tile(2 * qi), sa_ref, True)
    softmax_pv(tile(2 * qi + 1), sb_ref, True)

    ot = jnp.zeros((LANES, tq), F32)
    for r in range(2):
        a = acc_ref[r]
        den = a[HEAD_DIM * (1 - r):HEAD_DIM * (1 - r) + 1, :]
        ot = ot + jnp.where(own_rows[r], a / den, 0.0)
    o_ref[0] = ot.T.astype(BF16)


def _fox_attention(k3, ht, caug, crow, tq=512):
    b, s, _ = k3.shape
    nhp = N_HEADS_FOX // 2
    tk = tq // 2
    kern = functools.partial(_fox_kernel, tq=tq, tk=tk)
    return pl.pallas_call(
        kern,
        grid=(b, nhp, s // tq),
        in_specs=[
            pl.BlockSpec((1, LANES, tq), lambda bi, hp, qi: (bi, hp, qi)),
            pl.BlockSpec((1, s, LANES), lambda bi, hp, qi: (bi, 0, hp)),
            pl.BlockSpec((1, 1, s, LANES), lambda bi, hp, qi: (bi, hp, 0, 0)),
            pl.BlockSpec((1, LANES, s), lambda bi, hp, qi: (bi, nhp + hp, 0)),
            pl.BlockSpec((1, N_HEADS_FOX, tq), lambda bi, hp, qi: (bi, 0, qi)),
        ],
        out_specs=pl.BlockSpec((1, tq, LANES), lambda bi, hp, qi: (bi, qi, hp)),
        out_shape=jax.ShapeDtypeStruct((b, s, FOX_W), BF16),
        scratch_shapes=[
            pltpu.VMEM((2, 1, tq), F32),
            pltpu.VMEM((2, LANES, tq), F32),
            pltpu.VMEM((2, tk, tq), F32),
            pltpu.VMEM((2, tk, tq), F32),
        ],
        compiler_params=_cparams("parallel", "parallel", "arbitrary"),
        name="fox_attn",
    )(ht, k3, caug, ht, crow)


def _sb_kernel(qt_ref, k_ref, vt_ref, o_ref, run_ref, acc_ref, za_ref, zb_ref, *, tq, cw):
    qi = pl.program_id(2)
    q0 = qi * tq
    qt = qt_ref[0]
    sub = lax.broadcasted_iota(jnp.int32, (LANES, tq), 0)
    own_rows = [sub < HEAD_DIM, sub >= HEAD_DIM]
    qm = [jnp.where(mk, qt, 0.0) for mk in own_rows]

    ss = lax.broadcasted_iota(jnp.int32, (cw, cw), 0)
    jj = lax.broadcasted_iota(jnp.int32, (cw, cw), 1)
    tri = jnp.where(jj > ss, 1.0, 0.0).astype(BF16)

    run_ref[...] = jnp.zeros(run_ref.shape, F32)
    acc_ref[...] = jnp.zeros(acc_ref.shape, F32)

    def start_of(c):
        return pl.multiple_of(jnp.maximum(q0 + tq - (c + 1) * cw, 0), cw)

    def logits(c, z_ref):
        kt = k_ref[0, pl.ds(start_of(c), cw), :]
        for r in range(2):
            z_ref[r] = _dot(kt, qm[r])

    def weights_pv(c, z_ref, masked):
        k0 = start_of(c)
        vt = vt_ref[0, :, pl.ds(k0, cw)]
        if masked:
            key = k0 + lax.broadcasted_iota(jnp.int32, (cw, tq), 0)
            qry = q0 + lax.broadcasted_iota(jnp.int32, (cw, tq), 1)
            keep = key < qry
        for r in range(2):
            z = z_ref[r]
            e = jnp.exp2(jnp.abs(z) * -LOG2E)
            sp = jnp.maximum(z, 0.0) + jnp.log(1.0 + e)
            if masked:
                sp = jnp.where(keep, sp, 0.0)
            later = _dot(tri, sp.astype(BF16))
            run = run_ref[r]
            a = jnp.exp((z - sp) - (later + run))
            if masked:
                a = jnp.where(keep, a, 0.0)
            acc_ref[r] = acc_ref[r] + _dot(vt, a.astype(BF16))
            run_ref[r] = run + jnp.sum(sp, axis=0, keepdims=True)

    logits(0, za_ref)
    logits(1, zb_ref)
    weights_pv(0, za_ref, True)
    logits(2, za_ref)
    weights_pv(1, zb_ref, True)

    def body(i, carry):
        logits(2 * i + 1, zb_ref)
        weights_pv(2 * i, za_ref, False)
        logits(2 * i + 2, za_ref)
        weights_pv(2 * i + 1, zb_ref, False)
        return carry

    lax.fori_loop(1, qi + 1, body, 0)

    ot = jnp.where(own_rows[0], acc_ref[0], acc_ref[1])
    o_ref[0] = ot.T.astype(BF16)


def _sb_attention(k3, ht, tq=512):
    b, s, _ = k3.shape
    cw = tq // 2
    nhp = N_HEADS_SB // 2
    kern = functools.partial(_sb_kernel, tq=tq, cw=cw)
    return pl.pallas_call(
        kern,
        grid=(b, nhp, s // tq),
        in_specs=[
            pl.BlockSpec((1, LANES, tq), lambda bi, hp, qi: (bi, 2 * nhp + hp, qi)),
            pl.BlockSpec((1, s, LANES), lambda bi, hp, qi: (bi, 0, nhp + hp)),
            pl.BlockSpec((1, LANES, s), lambda bi, hp, qi: (bi, 3 * nhp + hp, 0)),
        ],
        out_specs=pl.BlockSpec((1, tq, LANES), lambda bi, hp, qi: (bi, qi, hp)),
        out_shape=jax.ShapeDtypeStruct((b, s, SB_W), BF16),
        scratch_shapes=[
            pltpu.VMEM((2, 1, tq), F32),
            pltpu.VMEM((2, LANES, tq), F32),
            pltpu.VMEM((2, cw, tq), F32),
            pltpu.VMEM((2, cw, tq), F32),
        ],
        compiler_params=_cparams("parallel", "parallel", "arbitrary"),
        name="sb_attn",
    )(ht, k3, ht)


def _swa_proj_kernel(x_ref, w_ref, cos_ref, sa_ref, sb_ref, q_ref, kv_ref, *, n_rope_blocks, n_q_blocks):
    xb = x_ref[...].astype(BF16)
    y = _dot(xb, w_ref[...])
    cos = cos_ref[...]
    sin_a = sa_ref[...]
    sin_b = sb_ref[...]
    n_blocks = y.shape[1] // LANES
    for c in range(n_blocks):
        blk = y[:, c * LANES:(c + 1) * LANES]
        if c < n_rope_blocks:
            half = ROPE_DIM // 2
            blk = (blk * cos + pltpu.roll(blk, half, axis=1) * sin_a
                   + pltpu.roll(blk, LANES - half, axis=1) * sin_b)
        blk = blk.astype(BF16)
        if c < n_q_blocks:
            q_ref[:, c * LANES:(c + 1) * LANES] = blk
        else:
            kv_ref[:, (c - n_q_blocks) * LANES:(c - n_q_blocks + 1) * LANES] = blk


def _rope_tables(s):
    half = ROPE_DIM // 2
    inv = ROPE_THETA ** (-jnp.arange(half, dtype=F32) * 2.0 / ROPE_DIM)
    ang = jnp.arange(s, dtype=F32)[:, None] * inv[None, :]
    cos, sin = jnp.cos(ang), jnp.sin(ang)
    d = jnp.arange(LANES) % HEAD_DIM
    idx = d % half
    cos_t = jnp.where(d[None, :] < ROPE_DIM, cos[:, idx], 1.0)
    sin_a = jnp.where((d[None, :] >= half) & (d[None, :] < ROPE_DIM), sin[:, idx], 0.0)
    sin_b = jnp.where(d[None, :] < half, -sin[:, idx], 0.0)
    return cos_t.astype(F32), sin_a.astype(F32), sin_b.astype(F32)


def _swa_proj(x2d, w, tables, s, tm=512):
    t = x2d.shape[0]
    n = w.shape[1]
    nq = N_HEADS_SWA * HEAD_DIM
    nkv = n - nq
    spt = s // tm
    kern = functools.partial(_swa_proj_kernel, n_rope_blocks=(nq + nkv // 2) // LANES, n_q_blocks=nq // LANES)
    tab_spec = pl.BlockSpec((tm, LANES), lambda i: (i % spt, 0))
    return pl.pallas_call(
        kern,
        grid=(t // tm,),
        in_specs=[
            pl.BlockSpec((tm, D_MODEL), lambda i: (i, 0)),
            pl.BlockSpec((D_MODEL, n), lambda i: (0, 0)),
            tab_spec, tab_spec, tab_spec,
        ],
        out_specs=[
            pl.BlockSpec((tm, nq), lambda i: (i, 0)),
            pl.BlockSpec((tm, nkv), lambda i: (i, 0)),
        ],
        out_shape=[jax.ShapeDtypeStruct((t, nq), BF16), jax.ShapeDtypeStruct((t, nkv), BF16)],
        compiler_params=_cparams("parallel"),
        name="swa_proj",
    )(x2d, w, *tables)


def _swa_kernel(sink_ref, q_ref, kv_ref, o_ref, *, tq):
    qi = pl.program_id(1)
    q0 = qi * tq
    kw = tq + WINDOW
    kstart = pl.multiple_of(jnp.maximum(q0 - WINDOW, 0), WINDOW)
    kv = kv_ref[0, pl.ds(kstart, kw), :]
    row = q0 + lax.broadcasted_iota(jnp.int32, (tq, kw), 0)
    col = kstart + lax.broadcasted_iota(jnp.int32, (tq, kw), 1)
    diff = row - col
    keep = (diff >= 0) & (diff < WINDOW)
    masks = _head_masks((tq, LANES))
    group = N_HEADS_SWA // N_KV_SWA
    for c in range(N_HEADS_SWA // 2):
        g = (2 * c) // group
        kt = kv[:, g * LANES:(g + 1) * LANES]
        vt = kv[:, (N_KV_SWA + g) * LANES:(N_KV_SWA + g + 1) * LANES]
        qblk = q_ref[0, :, c * LANES:(c + 1) * LANES]
        o = jnp.zeros((tq, LANES), F32)
        for r in range(2):
            qm = (jnp.where(masks[r], qblk, 0.0) * ATTN_SCALE).astype(BF16)
            s = jnp.where(keep, _dot_nt(qm, kt), NEG_INF)
            sink = sink_ref[2 * c + r]
            mx = jnp.maximum(jnp.max(s, axis=1, keepdims=True), sink)
            e = jnp.exp(s - mx)
            den = jnp.sum(e, axis=1, keepdims=True) + jnp.exp(sink - mx)
            pv = _dot(e.astype(BF16), vt)
            o = o + jnp.where(masks[r], pv / den, 0.0)
        o_ref[0, :, c * LANES:(c + 1) * LANES] = o.astype(BF16)


def _swa_attention(q3, kv3, sinks, tq=128):
    b, s, nq = q3.shape
    nkv = kv3.shape[2]
    kern = functools.partial(_swa_kernel, tq=tq)
    return pl.pallas_call(
        kern,
        grid=(b, s // tq),
        in_specs=[
            pl.BlockSpec(memory_space=pltpu.SMEM),
            pl.BlockSpec((1, tq, nq), lambda bi, qi: (bi, qi, 0)),
            pl.BlockSpec((1, s, nkv), lambda bi, qi: (bi, 0, 0)),
        ],
        out_specs=pl.BlockSpec((1, tq, nq), lambda bi, qi: (bi, qi, 0)),
        out_shape=jax.ShapeDtypeStruct((b, s, nq), BF16),
        compiler_params=_cparams("parallel", "arbitrary"),
        name="swa_attn",
    )(sinks, q3, kv3)


def _layer_norm(z, g, b):
    mu = jnp.mean(z, axis=-1, keepdims=True)
    zc = z - mu
    var = jnp.mean(zc * zc, axis=-1, keepdims=True)
    return zc * lax.rsqrt(var + LN_EPS) * g + b


def _outln_kernel(*refs, n_in):
    o_refs = refs[:n_in]
    w_refs = refs[n_in:2 * n_in]
    x_ref, g_ref, b_ref, y_ref, yb_ref = refs[2 * n_in:]
    h = _dot(o_refs[0][...], w_refs[0][...])
    for i in range(1, n_in):
        h = h + _dot(o_refs[i][...], w_refs[i][...])
    y = _layer_norm(DEEPNORM_ALPHA * x_ref[...] + h, g_ref[...], b_ref[...])
    y_ref[...] = y
    yb_ref[...] = y.astype(BF16)


def _out_proj_ln(os_, ws, x2d, g, b, tm=512):
    t = x2d.shape[0]
    n_in = len(os_)
    in_specs = [pl.BlockSpec((tm, o.shape[1]), lambda i: (i, 0)) for o in os_]
    in_specs += [pl.BlockSpec(w.shape, lambda i: (0, 0)) for w in ws]
    in_specs += [
        pl.BlockSpec((tm, D_MODEL), lambda i: (i, 0)),
        pl.BlockSpec((1, D_MODEL), lambda i: (0, 0)),
        pl.BlockSpec((1, D_MODEL), lambda i: (0, 0)),
    ]
    return pl.pallas_call(
        functools.partial(_outln_kernel, n_in=n_in),
        grid=(t // tm,),
        in_specs=in_specs,
        out_specs=[pl.BlockSpec((tm, D_MODEL), lambda i: (i, 0))] * 2,
        out_shape=[jax.ShapeDtypeStruct((t, D_MODEL), F32), jax.ShapeDtypeStruct((t, D_MODEL), BF16)],
        compiler_params=_cparams("parallel"),
        name="out_proj_ln",
    )(*os_, *ws, x2d, g, b)


def _router_kernel(x_ref, w_ref, b_ref, c_ref):
    x = x_ref[...]
    w = w_ref[...]
    xh = x.astype(BF16)
    xl = (x - xh.astype(F32)).astype(BF16)
    wh = w.astype(BF16)
    wl = (w - wh.astype(F32)).astype(BF16)
    logits = _dot(xh, wh) + (_dot(xl, wh) + _dot(xh, wl)) + b_ref[...]
    lane = lax.broadcasted_iota(jnp.int32, logits.shape, 1)
    logits = jnp.where(lane < N_EXPERTS, logits, NEG_INF)
    m1 = jnp.max(logits, axis=1, keepdims=True)
    i1 = jnp.min(jnp.where(logits == m1, lane, LANES), axis=1, keepdims=True)
    rest = jnp.where(lane == i1, NEG_INF, logits)
    m2 = jnp.max(rest, axis=1, keepdims=True)
    i2 = jnp.min(jnp.where(rest == m2, lane, LANES), axis=1, keepdims=True)
    e2 = jnp.exp(m2 - m1)
    den = 1.0 + e2
    c_ref[...] = jnp.where(lane == i1, 1.0 / den, 0.0) + jnp.where(lane == i2, e2 / den, 0.0)


def _router(x2d, w_pad, b_pad, tm=512):
    t = x2d.shape[0]
    return pl.pallas_call(
        _router_kernel,
        grid=(t // tm,),
        in_specs=[
            pl.BlockSpec((tm, D_MODEL), lambda i: (i, 0)),
            pl.BlockSpec((D_MODEL, LANES), lambda i: (0, 0)),
            pl.BlockSpec((1, LANES), lambda i: (0, 0)),
        ],
        out_specs=pl.BlockSpec((tm, LANES), lambda i: (i, 0)),
        out_shape=jax.ShapeDtypeStruct((t, LANES), F32),
        compiler_params=_cparams("parallel"),
        name="router",
    )(x2d, w_pad, b_pad)


def _ffn_kernel(eid_ref, valid_ref, x_ref, wg_ref, wu_ref, wd_ref, o_ref, acc_ref):
    i = pl.program_id(0)
    c = pl.program_id(1)
    last = c == pl.num_programs(1) - 1
    valid = valid_ref[i] != 0

    @pl.when(valid)
    def _():
        x = x_ref[...]
        gate = _dot(x, wg_ref[0])
        up = _dot(x, wu_ref[0])
        h = gate * jax.nn.sigmoid(gate) * up
        contrib = _dot(h.astype(BF16), wd_ref[0])

        @pl.when(c == 0)
        def _():
            acc_ref[...] = contrib

        @pl.when(c != 0)
        def _():
            acc_ref[...] += contrib

        @pl.when(last)
        def _():
            o_ref[...] = acc_ref[...].astype(o_ref.dtype)

    @pl.when(jnp.logical_not(valid) & last)
    def _():
        o_ref[...] = jnp.zeros(o_ref.shape, o_ref.dtype)


def _ffn(xb, w_gu, w_d, eids, valid, out_dtype, tm, fc, name):
    r = xb.shape[0]
    f = w_d.shape[1]
    ncf = f // fc
    in_specs = [
        pl.BlockSpec((tm, D_MODEL), lambda i, c, eid, ok: (i, 0)),
        pl.BlockSpec((1, D_MODEL, fc), lambda i, c, eid, ok: (eid[i], 0, c * ok[i])),
        pl.BlockSpec((1, D_MODEL, fc), lambda i, c, eid, ok: (eid[i], 0, ncf + c * ok[i])),
        pl.BlockSpec((1, fc, D_MODEL), lambda i, c, eid, ok: (eid[i], c * ok[i], 0)),
    ]
    return pl.pallas_call(
        _ffn_kernel,
        grid_spec=pltpu.PrefetchScalarGridSpec(
            num_scalar_prefetch=2,
            grid=(r // tm, ncf),
            in_specs=in_specs,
            out_specs=pl.BlockSpec((tm, D_MODEL), lambda i, c, eid, ok: (i, 0)),
            scratch_shapes=[pltpu.VMEM((tm, D_MODEL), F32)],
        ),
        out_shape=jax.ShapeDtypeStruct((r, D_MODEL), out_dtype),
        compiler_params=_cparams("parallel", "arbitrary"),
        name=name,
    )(eids, valid, xb, w_gu, w_gu, w_d)


MOE_TM = 512


def _moe_gather_kernel(te_ref, clo_ref, nch_ref, x_ref, pos_ref, o_ref, acc_ref, *, tm):
    r = pl.program_id(0)
    e = te_ref[r]
    row = r * tm + lax.broadcasted_iota(jnp.int32, (tm, tm), 0)
    acc_ref[...] = jnp.zeros(acc_ref.shape, F32)

    def body(i, carry):
        t0 = pl.multiple_of((clo_ref[r] + i) * tm, tm)
        p = pos_ref[pl.ds(e, 1), pl.ds(t0, tm)]
        onehot = jnp.where(p == row, 1.0, 0.0).astype(BF16)
        acc_ref[...] += _dot(onehot, x_ref[pl.ds(t0, tm), :])
        return carry

    lax.fori_loop(0, nch_ref[r], body, 0)
    o_ref[...] = acc_ref[...].astype(BF16)


def _moe_gather(xb, pos_t, tile_e, clo, nch, n_rows, tm=MOE_TM):
    t = xb.shape[0]
    return pl.pallas_call(
        functools.partial(_moe_gather_kernel, tm=tm),
        grid_spec=pltpu.PrefetchScalarGridSpec(
            num_scalar_prefetch=3,
            grid=(n_rows // tm,),
            in_specs=[
                pl.BlockSpec((t, D_MODEL), lambda r, *_: (0, 0), pipeline_mode=pl.Buffered(1)),
                pl.BlockSpec(pos_t.shape, lambda r, *_: (0, 0)),
            ],
            out_specs=pl.BlockSpec((tm, D_MODEL), lambda r, *_: (r, 0)),
            scratch_shapes=[pltpu.VMEM((tm, D_MODEL), F32)],
        ),
        out_shape=jax.ShapeDtypeStruct((n_rows, D_MODEL), BF16),
        compiler_params=_cparams("arbitrary"),
        name="moe_gather",
    )(tile_e, clo, nch, xb, pos_t)


MOE_CS = 256
MOE_SLOTS = N_EXPERTS * (MOE_TM // MOE_CS + 1)


def _moe_combine_kernel(n_ref, cid_ref, eid_ref, ys_hbm, pos_ref, comb_ref, o_ref, buf_ref, sem_ref,
                        *, tm, cs, n_slots):
    i = pl.program_id(0)
    n = n_ref[i]
    base = i * n_slots

    def chunk_copy(k, slot):
        row0 = pl.multiple_of(cid_ref[base + k] * cs, cs)
        return pltpu.make_async_copy(ys_hbm.at[pl.ds(row0, cs), :], buf_ref.at[slot], sem_ref.at[slot])

    o_ref[...] = jnp.zeros(o_ref.shape, F32)

    @pl.when(n > 0)
    def _():
        chunk_copy(0, 0).start()

    def body(k, carry):
        slot = k % 2
        chunk_copy(k, slot).wait()

        @pl.when(k + 1 < n)
        def _():
            chunk_copy(k + 1, 1 - slot).start()

        e = eid_ref[base + k]
        lane = lax.broadcasted_iota(jnp.int32, pos_ref.shape, 1)
        pos_e = jnp.sum(jnp.where(lane == e, pos_ref[...], 0), axis=1, keepdims=True)
        lane_c = lax.broadcasted_iota(jnp.int32, comb_ref.shape, 1)
        g_e = jnp.sum(jnp.where(lane_c == e, comb_ref[...], 0.0), axis=1, keepdims=True)
        col = cid_ref[base + k] * cs + lax.broadcasted_iota(jnp.int32, (tm, cs), 1)
        onehot = jnp.where(pos_e == col, 1.0, 0.0).astype(BF16)
        o_ref[...] += g_e * _dot(onehot, buf_ref[slot])
        return carry

    lax.fori_loop(0, n, body, 0)


def _moe_combine(ys, pos, combine, n_need, cid, eid, tm=MOE_TM, cs=MOE_CS):
    t = pos.shape[0]
    return pl.pallas_call(
        functools.partial(_moe_combine_kernel, tm=tm, cs=cs, n_slots=MOE_SLOTS),
        grid_spec=pltpu.PrefetchScalarGridSpec(
            num_scalar_prefetch=3,
            grid=(t // tm,),
            in_specs=[
                pl.BlockSpec(memory_space=pl.ANY),
                pl.BlockSpec((tm, N_EXPERTS), lambda i, *_: (i, 0)),
                pl.BlockSpec((tm, LANES), lambda i, *_: (i, 0)),
            ],
            out_specs=pl.BlockSpec((tm, D_MODEL), lambda i, *_: (i, 0)),
            scratch_shapes=[pltpu.VMEM((2, cs, D_MODEL), BF16), pltpu.SemaphoreType.DMA((2,))],
        ),
        out_shape=jax.ShapeDtypeStruct((t, D_MODEL), F32),
        compiler_params=_cparams("arbitrary"),
        name="moe_combine",
    )(n_need, cid, eid, ys, pos, combine)


def _moe_plan(combine, tm=MOE_TM):
    t = combine.shape[0]
    n_tok_tiles = t // tm
    n_row_tiles = (2 * t) // tm + N_EXPERTS
    sel = (combine[:, :N_EXPERTS] != 0.0).astype(jnp.int32)
    csum = jnp.cumsum(sel, axis=0)
    count = csum[-1]
    ntile_e = (count + tm - 1) // tm
    tile_end = jnp.cumsum(ntile_e)
    gstart = (tile_end - ntile_e) * tm
    pos = jnp.where(sel != 0, gstart[None, :] + csum - sel, -1)
    rt = jnp.arange(n_row_tiles, dtype=jnp.int32)
    tile_e = jnp.minimum(jnp.sum(rt[:, None] >= tile_end[None, :], axis=1), N_EXPERTS - 1).astype(jnp.int32)
    valid = (rt < tile_end[-1]).astype(jnp.int32)
    cb = jnp.concatenate([jnp.zeros((1, N_EXPERTS), jnp.int32), csum[tm - 1::tm]], axis=0)
    off = rt * tm - gstart[tile_e]
    cb_t = cb[:, tile_e].T
    clo = jnp.sum(cb_t[:, 1:] <= off[:, None], axis=1).astype(jnp.int32)
    cend = jnp.sum(cb_t[:, :-1] < (off + tm)[:, None], axis=1).astype(jnp.int32)
    nch = jnp.where(valid != 0, jnp.maximum(cend - clo, 0), 0).astype(jnp.int32)
    clo = jnp.minimum(clo, n_tok_tiles - 1)
    first = gstart[None, :] + cb[:-1]
    n_ie = cb[1:] - cb[:-1]
    per_e = MOE_SLOTS // N_EXPERTS
    k = jnp.arange(per_e, dtype=jnp.int32)
    c_lo = first // MOE_CS
    c_hi = (first + n_ie - 1) // MOE_CS
    cand = c_lo[..., None] + k
    need = ((n_ie[..., None] > 0) & (cand <= c_hi[..., None])).reshape(n_tok_tiles, MOE_SLOTS)
    cand = cand.reshape(n_tok_tiles, MOE_SLOTS)
    eids = jnp.broadcast_to(jnp.arange(N_EXPERTS, dtype=jnp.int32)[:, None], (N_EXPERTS, per_e)).reshape(-1)
    order = jnp.argsort(jnp.logical_not(need), axis=1, stable=True)
    cid = jnp.where(need, cand, 0)
    cid = jnp.take_along_axis(cid, order, axis=1).reshape(-1).astype(jnp.int32)
    eid = eids[order].reshape(-1).astype(jnp.int32)
    n_need = jnp.sum(need, axis=1).astype(jnp.int32)
    return dict(pos=pos.astype(jnp.int32), tile_e=tile_e, valid=valid, clo=clo, nch=nch,
                n_need=n_need, cid=cid, eid=eid, n_rows=n_row_tiles * tm)


def _moe(xb, combine, w_gu, w_d, first_expert):
    plan = _moe_plan(combine)
    xs = _moe_gather(xb, plan["pos"].T, plan["tile_e"], plan["clo"], plan["nch"], plan["n_rows"])
    ys = _ffn(xs, w_gu, w_d, plan["tile_e"] + first_expert, plan["valid"], BF16, tm=MOE_TM, fc=1792,
              name="ffn_moe")
    return _moe_combine(ys, plan["pos"], combine, plan["n_need"], plan["cid"], plan["eid"])


def _lnple_kernel(x_ref, h_ref, g_ref, b_ref, p_ref, wg_ref, wp_ref, o_ref):
    y = _layer_norm(DEEPNORM_ALPHA * x_ref[...] + h_ref[...], g_ref[...], b_ref[...])
    gate = jax.nn.sigmoid(_dot(y.astype(BF16), wg_ref[...]))
    proj = _dot(p_ref[...].astype(BF16), wp_ref[...])
    o_ref[...] = y + gate * proj


def _ln_ple(x2d, h2d, g, b, p2d, wg, wp, tm=512):
    t = x2d.shape[0]
    row = lambda n: pl.BlockSpec((tm, n), lambda i: (i, 0))
    full = lambda a: pl.BlockSpec(a.shape, lambda i: (0, 0))
    return pl.pallas_call(
        _lnple_kernel,
        grid=(t // tm,),
        in_specs=[row(D_MODEL), row(D_MODEL), full(g), full(b), row(D_PLE), full(wg), full(wp)],
        out_specs=row(D_MODEL),
        out_shape=jax.ShapeDtypeStruct((t, D_MODEL), F32),
        compiler_params=_cparams("parallel"),
        name="ln_ple",
    )(x2d, h2d, g, b, p2d, wg, wp)


def _even_mixer(x2d, b, s, w_in, b_f, w_out):
    nf, ns = FOX_W, SB_W
    o_f = 3 * nf
    o_s = o_f + N_HEADS_FOX
    qa, ka, va = w_in[:, :nf], w_in[:, nf:2 * nf], w_in[:, 2 * nf:o_f]
    qs, ks, vs = w_in[:, o_s:o_s + ns], w_in[:, o_s + ns:o_s + 2 * ns], w_in[:, o_s + 2 * ns:]
    w_k = jnp.concatenate([ka, ks], axis=1).astype(BF16)
    w_t = jnp.concatenate([qa * ATTN_SCALE, va, qs * ATTN_SCALE, vs], axis=1).T.astype(BF16)
    w_f = jnp.pad(w_in[:, o_f:o_s], ((0, 0), (0, LANES - N_HEADS_FOX))).astype(BF16)
    bias_f = jnp.pad(b_f, (0, LANES - N_HEADS_FOX)).reshape(1, LANES)
    k2d, ht, lf = _ab_proj(x2d, w_k, w_t, w_f, bias_f, b, s)
    lf_t = lf.reshape(b, s, LANES)[:, :, :N_HEADS_FOX].transpose(0, 2, 1)
    crow, caug = _cumsum_seq(lf_t)
    k3 = k2d.reshape(b, s, -1)
    oa = _fox_attention(k3, ht, caug, crow)
    ob = _sb_attention(k3, ht)
    w_out_b = w_out.astype(BF16)
    return [oa.reshape(b * s, nf), ob.reshape(b * s, ns)], [w_out_b[:nf], w_out_b[nf:]]


def _odd_mixer(x2d, b, s, w_qkv, sinks, w_out, tables):
    nq = N_HEADS_SWA * HEAD_DIM
    wq = w_qkv[:, :nq]
    wk = [w_qkv[:, nq + i * HEAD_DIM: nq + (i + 1) * HEAD_DIM] for i in range(N_KV_SWA)]
    wv = [w_qkv[:, nq + (N_KV_SWA + i) * HEAD_DIM: nq + (N_KV_SWA + i + 1) * HEAD_DIM] for i in range(N_KV_SWA)]
    dup = lambda ws: [w for w in ws for _ in range(2)]
    w = jnp.concatenate([wq] + dup(wk) + dup(wv), axis=1).astype(BF16)
    q2d, kv2d = _swa_proj(x2d, w, tables, s)
    o = _swa_attention(q2d.reshape(b, s, nq), kv2d.reshape(b, s, -1), sinks)
    return [o.reshape(b * s, nq)], [w_out.astype(BF16)]


def kernel(x, p, ln_mix_g, ln_mix_b, ln_ffn_g, ln_ffn_b, ab_w_in, ab_b_f, ab_w_out, c_w_qkv, c_sinks, c_w_out,
           ffn_w_gate_up, ffn_w_down, router_w, router_b, moe_w_gate_up, moe_w_down, ple_w_gate, ple_w_proj):
    b, s, d = x.shape
    t = b * s
    x2d = x.reshape(t, d)
    tables = _rope_tables(s)
    row = lambda v: v.reshape(1, -1)
    moe_gu = moe_w_gate_up.astype(BF16).reshape((-1,) + moe_w_gate_up.shape[2:])
    moe_d = moe_w_down.astype(BF16).reshape((-1,) + moe_w_down.shape[2:])
    for i in range(DEPTH):
        j = i // 2
        if i % 2 == 0:
            os_, ws = _even_mixer(x2d, b, s, ab_w_in[j], ab_b_f[j], ab_w_out[j])
        else:
            os_, ws = _odd_mixer(x2d, b, s, c_w_qkv[j], c_sinks[j], c_w_out[j], tables)
        x2d, xb = _out_proj_ln(os_, ws, x2d, row(ln_mix_g[i]), row(ln_mix_b[i]))
        if i % 2 == 0:
            n_tiles = t // 512
            h = _ffn(xb, ffn_w_gate_up[j][None].astype(BF16), ffn_w_down[j][None].astype(BF16),
                     jnp.zeros((n_tiles,), jnp.int32), jnp.ones((n_tiles,), jnp.int32), F32,
                     tm=512, fc=1408, name="ffn_dense")
        else:
            rw = jnp.pad(router_w[j], ((0, 0), (0, LANES - N_EXPERTS)))
            rb = jnp.pad(router_b[j], (0, LANES - N_EXPERTS)).reshape(1, LANES)
            combine = _router(x2d, rw, rb)
            h = _moe(xb, combine, moe_gu, moe_d, j * N_EXPERTS)
        x2d = _ln_ple(x2d, h, row(ln_ffn_g[i]), row(ln_ffn_b[i]), p[i].reshape(t, D_PLE),
                      ple_w_gate[i].astype(BF16), ple_w_proj[i].astype(BF16))
    return x2d.reshape(b, s, d)
```

```python
import functools
import math

import jax
import jax.numpy as jnp
from jax import lax
from jax.experimental import pallas as pl
from jax.experimental.pallas import tpu as pltpu

F32 = jnp.float32
BF16 = jnp.bfloat16

D_MODEL = 1024
HEAD_DIM = 64
LANES = 128
N_HEADS_FOX = 8
N_HEADS_SB = 8
N_HEADS_SWA = 16
N_KV_SWA = 2
WINDOW = 128
ROPE_THETA = 500000.0
ROPE_DIM = HEAD_DIM // 4
N_EXPERTS = 8
D_PLE = 256
LN_EPS = 1e-5
DEPTH = 4
DEEPNORM_ALPHA = (2.0 * DEPTH) ** 0.25
ATTN_SCALE = HEAD_DIM ** -0.5
FOX_W = N_HEADS_FOX * HEAD_DIM
SB_W = N_HEADS_SB * HEAD_DIM
VMEM_LIMIT = 56 * 1024 * 1024

NEG_INF = float("-inf")
LOG2E = math.log2(math.e)


def _cparams(*sem):
    return pltpu.CompilerParams(dimension_semantics=sem, vmem_limit_bytes=VMEM_LIMIT)


def _dot(a, b):
    return jnp.dot(a, b, preferred_element_type=F32)


def _dot_nt(a, b):
    return lax.dot_general(a, b, (((1,), (1,)), ((), ())), preferred_element_type=F32)


def _log_sigmoid(x):
    return jnp.minimum(x, 0.0) - jnp.log1p(jnp.exp(-jnp.abs(x)))


def _head_masks(shape):
    lane = lax.broadcasted_iota(jnp.int32, shape, len(shape) - 1)
    return lane < HEAD_DIM, lane >= HEAD_DIM


def _ab_proj_kernel(x_ref, wk_ref, wt_ref, wf_ref, bf_ref, k_ref, ht_ref, lf_ref):
    xb = x_ref[...].astype(BF16)
    k_ref[...] = _dot(xb, wk_ref[...]).astype(BF16)
    ht_ref[0] = _dot_nt(wt_ref[...], xb).astype(BF16)
    lf_ref[...] = _log_sigmoid(_dot(xb, wf_ref[...]) + bf_ref[...])


def _ab_proj(x2d, wk, wt, wf, bf, b, s, tm=512):
    t = x2d.shape[0]
    nk = wk.shape[1]
    nt = wt.shape[0]
    spt = s // tm
    return pl.pallas_call(
        _ab_proj_kernel,
        grid=(t // tm,),
        in_specs=[
            pl.BlockSpec((tm, D_MODEL), lambda i: (i, 0)),
            pl.BlockSpec((D_MODEL, nk), lambda i: (0, 0)),
            pl.BlockSpec((nt, D_MODEL), lambda i: (0, 0)),
            pl.BlockSpec((D_MODEL, LANES), lambda i: (0, 0)),
            pl.BlockSpec((1, LANES), lambda i: (0, 0)),
        ],
        out_specs=[
            pl.BlockSpec((tm, nk), lambda i: (i, 0)),
            pl.BlockSpec((1, nt, tm), lambda i: (i // spt, 0, i % spt)),
            pl.BlockSpec((tm, LANES), lambda i: (i, 0)),
        ],
        out_shape=[
            jax.ShapeDtypeStruct((t, nk), BF16),
            jax.ShapeDtypeStruct((b, nt, s), BF16),
            jax.ShapeDtypeStruct((t, LANES), F32),
        ],
        compiler_params=_cparams("parallel"),
        name="ab_proj",
    )(x2d, wk, wt, wf, bf)


def _split3(v):
    hi = v.astype(BF16).astype(F32)
    r1 = v - hi
    mid = r1.astype(BF16).astype(F32)
    return hi, mid, r1 - mid


def _cumsum_kernel(x_ref, c_ref, aug_ref, st_ref):
    x = x_ref[0]
    s = x.shape[1]
    lane = lax.broadcasted_iota(jnp.int32, x.shape, 1) & (LANES - 1)
    sh = 1
    while sh < LANES:
        x = x + jnp.where(lane >= sh, pltpu.roll(x, sh, axis=1), 0.0)
        sh *= 2
    carry = jnp.zeros((x.shape[0], 1), F32)
    for c in range(s // LANES):
        blk = x[:, c * LANES:(c + 1) * LANES] + carry
        c_ref[0, :, c * LANES:(c + 1) * LANES] = blk
        carry = blk[:, LANES - 1:LANES]

    ones = jnp.ones((3, s), F32)
    st_ref[...] = jnp.zeros(st_ref.shape, F32)
    for p in range(x.shape[0] // 2):
        for which, base in ((0, HEAD_DIM), (1, 0)):
            negc = -c_ref[0, 2 * p + which:2 * p + which + 1, :]
            hi, mid, lo = _split3(negc)
            st_ref[base:base + 1, :] = hi
            st_ref[base + 1:base + 2, :] = mid
            st_ref[base + 2:base + 3, :] = lo
            st_ref[base + 3:base + 6, :] = ones
        for c in range(s // LANES):
            blk = st_ref[:, c * LANES:(c + 1) * LANES]
            aug_ref[0, p, c * LANES:(c + 1) * LANES, :] = blk.T.astype(BF16)


def _cumsum_seq(lf_t):
    b, h, s = lf_t.shape
    return pl.pallas_call(
        _cumsum_kernel,
        grid=(b,),
        in_specs=[pl.BlockSpec((1, h, s), lambda i: (i, 0, 0))],
        out_specs=[
            pl.BlockSpec((1, h, s), lambda i: (i, 0, 0)),
            pl.BlockSpec((1, h // 2, s, LANES), lambda i: (i, 0, 0, 0)),
        ],
        out_shape=[
            jax.ShapeDtypeStruct((b, h, s), F32),
            jax.ShapeDtypeStruct((b, h // 2, s, LANES), BF16),
        ],
        scratch_shapes=[pltpu.VMEM((LANES, s), F32)],
        compiler_params=_cparams("parallel"),
        name="fox_cumsum",
    )(lf_t)


def _fox_kernel(qt_ref, k_ref, aug_ref, vt_ref, c_ref, o_ref, m_ref, acc_ref, sa_ref, sb_ref, *, tq, tk):
    hp = pl.program_id(1)
    qi = pl.program_id(2)
    q0 = qi * tq
    qt = qt_ref[0].astype(F32)
    sub = lax.broadcasted_iota(jnp.int32, (LANES, tq), 0)
    lane = lax.broadcasted_iota(jnp.int32, (tk, LANES), 1)
    vsub = lax.broadcasted_iota(jnp.int32, (LANES, tk), 0)
    own_rows = [sub < HEAD_DIM, sub >= HEAD_DIM]
    own_lanes = [lane < HEAD_DIM, lane >= HEAD_DIM]
    own_vrows = [vsub < HEAD_DIM, vsub >= HEAD_DIM]

    qaug = []
    for r in range(2):
        base = HEAD_DIM * (1 - r)
        ci = c_ref[0, pl.ds(2 * hp + r, 1), :]
        hi, mid, lo = _split3(ci)
        t = jnp.where(own_rows[r], qt, 0.0)
        t = jnp.where((sub >= base) & (sub < base + 3), 1.0, t)
        t = jnp.where(sub == base + 3, hi, t)
        t = jnp.where(sub == base + 4, mid, t)
        t = jnp.where(sub == base + 5, lo, t)
        qaug.append(t.astype(BF16))

    m_ref[...] = jnp.full(m_ref.shape, NEG_INF, F32)
    acc_ref[...] = jnp.zeros(acc_ref.shape, F32)

    def scores(k0, s_ref):
        kt = k_ref[0, pl.ds(k0, tk), :]
        ca = aug_ref[0, 0, pl.ds(k0, tk), :]
        for r in range(2):
            ka = jnp.where(own_lanes[r], kt, ca)
            s_ref[r] = _dot(ka, qaug[r])

    def softmax_pv(k0, s_ref, masked):
        vt = vt_ref[0, :, pl.ds(k0, tk)]
        if masked:
            key = k0 + lax.broadcasted_iota(jnp.int32, (tk, tq), 0)
            qry = q0 + lax.broadcasted_iota(jnp.int32, (tk, tq), 1)
            keep = key <= qry
        for r in range(2):
            st = s_ref[r]
            if masked:
                st = jnp.where(keep, st, NEG_INF)
            m_prev = m_ref[r]
            m_new = jnp.maximum(m_prev, jnp.max(st, axis=0, keepdims=True))
            alpha = jnp.exp(m_prev - m_new)
            pt = jnp.exp(st - m_new).astype(BF16)
            va = jnp.where(own_vrows[r], vt, 1.0)
            acc_ref[r] = alpha * acc_ref[r] + _dot(va, pt)
            m_ref[r] = m_new

    tile = lambda j: pl.multiple_of(j * tk, tk)
    scores(tile(0), sa_ref)

    def pair(i):
        scores(tile(2 * i + 1), sb_ref)
        softmax_pv(tile(2 * i), sa_ref, False)
        scores(tile(2 * i + 2), sa_ref)
        softmax_pv(tile(2 * i + 1), sb_ref, False)

    def body(j, carry):
        pair(2 * j)
        pair(2 * j + 1)
        return carry

    lax.fori_loop(0, qi // 2, body, 0)

    @pl.when(qi % 2 == 1)
    def _():
        pair(qi - 1)

    scores(tile(2 * qi + 1), sb_ref)
    softmax_pv(tile(2 * qi), sa_ref, True)
    softmax_pv(tile(2 * qi + 1), sb_ref, True)

    ot = jnp.zeros((LANES, tq), F32)
    for r in range(2):
        a = acc_ref[r]
        den = a[HEAD_DIM * (1 - r):HEAD_DIM * (1 - r) + 1, :]
        ot = ot + jnp.where(own_rows[r], a / den, 0.0)
    o_ref[0] = ot.T.astype(BF16)


def _fox_attention(k3, ht, caug, crow, tq=512):
    b, s, _ = k3.shape
    nhp = N_HEADS_FOX // 2
    tk = tq // 2
    kern = functools.partial(_fox_kernel, tq=tq, tk=tk)
    return pl.pallas_call(
        kern,
        grid=(b, nhp, s // tq),
        in_specs=[
            pl.BlockSpec((1, LANES, tq), lambda bi, hp, qi: (bi, hp, qi)),
            pl.BlockSpec((1, s, LANES), lambda bi, hp, qi: (bi, 0, hp)),
            pl.BlockSpec((1, 1, s, LANES), lambda bi, hp, qi: (bi, hp, 0, 0)),
            pl.BlockSpec((1, LANES, s), lambda bi, hp, qi: (bi, nhp + hp, 0)),
            pl.BlockSpec((1, N_HEADS_FOX, tq), lambda bi, hp, qi: (bi, 0, qi)),
        ],
        out_specs=pl.BlockSpec((1, tq, LANES), lambda bi, hp, qi: (bi, qi, hp)),
        out_shape=jax.ShapeDtypeStruct((b, s, FOX_W), BF16),
        scratch_shapes=[
            pltpu.VMEM((2, 1, tq), F32),
            pltpu.VMEM((2, LANES, tq), F32),
            pltpu.VMEM((2, tk, tq), F32),
            pltpu.VMEM((2, tk, tq), F32),
        ],
        compiler_params=_cparams("parallel", "parallel", "arbitrary"),
        name="fox_attn",
    )(ht, k3, caug, ht, crow)


def _sb_kernel(qt_ref, k_ref, vt_ref, o_ref, run_ref, acc_ref, za_ref, zb_ref, *, tq, cw):
    qi = pl.program_id(2)
    q0 = qi * tq
    qt = qt_ref[0]
    sub = lax.broadcasted_iota(jnp.int32, (LANES, tq), 0)
    own_rows = [sub < HEAD_DIM, sub >= HEAD_DIM]
    qm = [jnp.where(mk, qt, 0.0) for mk in own_rows]

    ss = lax.broadcasted_iota(jnp.int32, (cw, cw), 0)
    jj = lax.broadcasted_iota(jnp.int32, (cw, cw), 1)
    tri = jnp.where(jj > ss, 1.0, 0.0).astype(BF16)

    run_ref[...] = jnp.zeros(run_ref.shape, F32)
    acc_ref[...] = jnp.zeros(acc_ref.shape, F32)

    def start_of(c):
        return pl.multiple_of(jnp.maximum(q0 + tq - (c + 1) * cw, 0), cw)

    def logits(c, z_ref):
        kt = k_ref[0, pl.ds(start_of(c), cw), :]
        for r in range(2):
            z_ref[r] = _dot(kt, qm[r])

    def weights_pv(c, z_ref, masked):
        k0 = start_of(c)
        vt = vt_ref[0, :, pl.ds(k0, cw)]
        if masked:
            key = k0 + lax.broadcasted_iota(jnp.int32, (cw, tq), 0)
            qry = q0 + lax.broadcasted_iota(jnp.int32, (cw, tq), 1)
            keep = key < qry
        zs, sps, laters, ws = [], [], [], []
        for r in range(2):
            z = z_ref[r]
            e = jnp.exp2(jnp.abs(z) * -LOG2E)
            sp = jnp.maximum(z, 0.0) + jnp.log(1.0 + e)
            if masked:
                sp = jnp.where(keep, sp, 0.0)
            zs.append(z)
            sps.append(sp)
        for r in range(2):
            laters.append(_dot(tri, sps[r].astype(BF16)))
        for r in range(2):
            run = run_ref[r]
            a = jnp.exp((zs[r] - sps[r]) - (laters[r] + run))
            if masked:
                a = jnp.where(keep, a, 0.0)
            ws.append(a.astype(BF16))
            run_ref[r] = run + jnp.sum(sps[r], axis=0, keepdims=True)
        for r in range(2):
            acc_ref[r] = acc_ref[r] + _dot(vt, ws[r])

    logits(0, za_ref)
    logits(1, zb_ref)
    weights_pv(0, za_ref, True)
    logits(2, za_ref)
    weights_pv(1, zb_ref, True)

    def body(i, carry):
        logits(2 * i + 1, zb_ref)
        weights_pv(2 * i, za_ref, False)
        logits(2 * i + 2, za_ref)
        weights_pv(2 * i + 1, zb_ref, False)
        return carry

    lax.fori_loop(1, qi + 1, body, 0)

    ot = jnp.where(own_rows[0], acc_ref[0], acc_ref[1])
    o_ref[0] = ot.T.astype(BF16)


def _sb_attention(k3, ht, tq=512):
    b, s, _ = k3.shape
    cw = tq // 2
    nhp = N_HEADS_SB // 2
    kern = functools.partial(_sb_kernel, tq=tq, cw=cw)
    return pl.pallas_call(
        kern,
        grid=(b, nhp, s // tq),
        in_specs=[
            pl.BlockSpec((1, LANES, tq), lambda bi, hp, qi: (bi, 2 * nhp + hp, qi)),
            pl.BlockSpec((1, s, LANES), lambda bi, hp, qi: (bi, 0, nhp + hp)),
            pl.BlockSpec((1, LANES, s), lambda bi, hp, qi: (bi, 3 * nhp + hp, 0)),
        ],
        out_specs=pl.BlockSpec((1, tq, LANES), lambda bi, hp, qi: (bi, qi, hp)),
        out_shape=jax.ShapeDtypeStruct((b, s, SB_W), BF16),
        scratch_shapes=[
            pltpu.VMEM((2, 1, tq), F32),
            pltpu.VMEM((2, LANES, tq), F32),
            pltpu.VMEM((2, cw, tq), F32),
            pltpu.VMEM((2, cw, tq), F32),
        ],
        compiler_params=_cparams("parallel", "parallel", "arbitrary"),
        name="sb_attn",
    )(ht, k3, ht)


def _swa_proj_kernel(x_ref, w_ref, cos_ref, sa_ref, sb_ref, q_ref, kv_ref, *, n_rope_blocks, n_q_blocks):
    xb = x_ref[...].astype(BF16)
    y = _dot(xb, w_ref[...])
    cos = cos_ref[...]
    sin_a = sa_ref[...]
    sin_b = sb_ref[...]
    n_blocks = y.shape[1] // LANES
    for c in range(n_blocks):
        blk = y[:, c * LANES:(c + 1) * LANES]
        if c < n_rope_blocks:
            half = ROPE_DIM // 2
            blk = (blk * cos + pltpu.roll(blk, half, axis=1) * sin_a
                   + pltpu.roll(blk, LANES - half, axis=1) * sin_b)
        blk = blk.astype(BF16)
        if c < n_q_blocks:
            q_ref[:, c * LANES:(c + 1) * LANES] = blk
        else:
            kv_ref[:, (c - n_q_blocks) * LANES:(c - n_q_blocks + 1) * LANES] = blk


def _rope_tables(s):
    half = ROPE_DIM // 2
    inv = ROPE_THETA ** (-jnp.arange(half, dtype=F32) * 2.0 / ROPE_DIM)
    ang = jnp.arange(s, dtype=F32)[:, None] * inv[None, :]
    cos, sin = jnp.cos(ang), jnp.sin(ang)
    d = jnp.arange(LANES) % HEAD_DIM
    idx = d % half
    cos_t = jnp.where(d[None, :] < ROPE_DIM, cos[:, idx], 1.0)
    sin_a = jnp.where((d[None, :] >= half) & (d[None, :] < ROPE_DIM), sin[:, idx], 0.0)
    sin_b = jnp.where(d[None, :] < half, -sin[:, idx], 0.0)
    return cos_t.astype(F32), sin_a.astype(F32), sin_b.astype(F32)


def _swa_proj(x2d, w, tables, s, tm=512):
    t = x2d.shape[0]
    n = w.shape[1]
    nq = N_HEADS_SWA * HEAD_DIM
    nkv = n - nq
    spt = s // tm
    kern = functools.partial(_swa_proj_kernel, n_rope_blocks=(nq + nkv // 2) // LANES, n_q_blocks=nq // LANES)
    tab_spec = pl.BlockSpec((tm, LANES), lambda i: (i % spt, 0))
    return pl.pallas_call(
        kern,
        grid=(t // tm,),
        in_specs=[
            pl.BlockSpec((tm, D_MODEL), lambda i: (i, 0)),
            pl.BlockSpec((D_MODEL, n), lambda i: (0, 0)),
            tab_spec, tab_spec, tab_spec,
        ],
        out_specs=[
            pl.BlockSpec((tm, nq), lambda i: (i, 0)),
            pl.BlockSpec((tm, nkv), lambda i: (i, 0)),
        ],
        out_shape=[jax.ShapeDtypeStruct((t, nq), BF16), jax.ShapeDtypeStruct((t, nkv), BF16)],
        compiler_params=_cparams("parallel"),
        name="swa_proj",
    )(x2d, w, *tables)


def _swa_kernel(sink_ref, q_ref, kv_ref, o_ref, *, tq):
    qi = pl.program_id(1)
    q0 = qi * tq
    kw = tq + WINDOW
    kstart = pl.multiple_of(jnp.maximum(q0 - WINDOW, 0), WINDOW)
    kv = kv_ref[0, pl.ds(kstart, kw), :]
    row = q0 + lax.broadcasted_iota(jnp.int32, (tq, kw), 0)
    col = kstart + lax.broadcasted_iota(jnp.int32, (tq, kw), 1)
    diff = row - col
    keep = (diff >= 0) & (diff < WINDOW)
    masks = _head_masks((tq, LANES))
    group = N_HEADS_SWA // N_KV_SWA
    for c in range(N_HEADS_SWA // 2):
        g = (2 * c) // group
        kt = kv[:, g * LANES:(g + 1) * LANES]
        vt = kv[:, (N_KV_SWA + g) * LANES:(N_KV_SWA + g + 1) * LANES]
        qblk = q_ref[0, :, c * LANES:(c + 1) * LANES]
        o = jnp.zeros((tq, LANES), F32)
        for r in range(2):
            qm = (jnp.where(masks[r], qblk, 0.0) * ATTN_SCALE).astype(BF16)
            s = jnp.where(keep, _dot_nt(qm, kt), NEG_INF)
            sink = sink_ref[2 * c + r]
            mx = jnp.maximum(jnp.max(s, axis=1, keepdims=True), sink)
            e = jnp.exp(s - mx)
            den = jnp.sum(e, axis=1, keepdims=True) + jnp.exp(sink - mx)
            pv = _dot(e.astype(BF16), vt)
            o = o + jnp.where(masks[r], pv / den, 0.0)
        o_ref[0, :, c * LANES:(c + 1) * LANES] = o.astype(BF16)


def _swa_attention(q3, kv3, sinks, tq=128):
    b, s, nq = q3.shape
    nkv = kv3.shape[2]
    kern = functools.partial(_swa_kernel, tq=tq)
    return pl.pallas_call(
        kern,
        grid=(b, s // tq),
        in_specs=[
            pl.BlockSpec(memory_space=pltpu.SMEM),
            pl.BlockSpec((1, tq, nq), lambda bi, qi: (bi, qi, 0)),
            pl.BlockSpec((1, s, nkv), lambda bi, qi: (bi, 0, 0)),
        ],
        out_specs=pl.BlockSpec((1, tq, nq), lambda bi, qi: (bi, qi, 0)),
        out_shape=jax.ShapeDtypeStruct((b, s, nq), BF16),
        compiler_params=_cparams("parallel", "arbitrary"),
        name="swa_attn",
    )(sinks, q3, kv3)


def _layer_norm(z, g, b):
    mu = jnp.mean(z, axis=-1, keepdims=True)
    zc = z - mu
    var = jnp.mean(zc * zc, axis=-1, keepdims=True)
    return zc * lax.rsqrt(var + LN_EPS) * g + b


def _outln_kernel(*refs, n_in):
    o_refs = refs[:n_in]
    w_refs = refs[n_in:2 * n_in]
    x_ref, g_ref, b_ref, y_ref, yb_ref = refs[2 * n_in:]
    h = _dot(o_refs[0][...], w_refs[0][...])
    for i in range(1, n_in):
        h = h + _dot(o_refs[i][...], w_refs[i][...])
    y = _layer_norm(DEEPNORM_ALPHA * x_ref[...] + h, g_ref[...], b_ref[...])
    y_ref[...] = y
    yb_ref[...] = y.astype(BF16)


def _out_proj_ln(os_, ws, x2d, g, b, tm=512):
    t = x2d.shape[0]
    n_in = len(os_)
    in_specs = [pl.BlockSpec((tm, o.shape[1]), lambda i: (i, 0)) for o in os_]
    in_specs += [pl.BlockSpec(w.shape, lambda i: (0, 0)) for w in ws]
    in_specs += [
        pl.BlockSpec((tm, D_MODEL), lambda i: (i, 0)),
        pl.BlockSpec((1, D_MODEL), lambda i: (0, 0)),
        pl.BlockSpec((1, D_MODEL), lambda i: (0, 0)),
    ]
    return pl.pallas_call(
        functools.partial(_outln_kernel, n_in=n_in),
        grid=(t // tm,),
        in_specs=in_specs,
        out_specs=[pl.BlockSpec((tm, D_MODEL), lambda i: (i, 0))] * 2,
        out_shape=[jax.ShapeDtypeStruct((t, D_MODEL), F32), jax.ShapeDtypeStruct((t, D_MODEL), BF16)],
        compiler_params=_cparams("parallel"),
        name="out_proj_ln",
    )(*os_, *ws, x2d, g, b)


def _router_kernel(x_ref, w_ref, b_ref, c_ref):
    x = x_ref[...]
    w = w_ref[...]
    xh = x.astype(BF16)
    xl = (x - xh.astype(F32)).astype(BF16)
    wh = w.astype(BF16)
    wl = (w - wh.astype(F32)).astype(BF16)
    logits = _dot(xh, wh) + (_dot(xl, wh) + _dot(xh, wl)) + b_ref[...]
    lane = lax.broadcasted_iota(jnp.int32, logits.shape, 1)
    logits = jnp.where(lane < N_EXPERTS, logits, NEG_INF)
    m1 = jnp.max(logits, axis=1, keepdims=True)
    i1 = jnp.min(jnp.where(logits == m1, lane, LANES), axis=1, keepdims=True)
    rest = jnp.where(lane == i1, NEG_INF, logits)
    m2 = jnp.max(rest, axis=1, keepdims=True)
    i2 = jnp.min(jnp.where(rest == m2, lane, LANES), axis=1, keepdims=True)
    e2 = jnp.exp(m2 - m1)
    den = 1.0 + e2
    c_ref[...] = jnp.where(lane == i1, 1.0 / den, 0.0) + jnp.where(lane == i2, e2 / den, 0.0)


def _router(x2d, w_pad, b_pad, tm=512):
    t = x2d.shape[0]
    return pl.pallas_call(
        _router_kernel,
        grid=(t // tm,),
        in_specs=[
            pl.BlockSpec((tm, D_MODEL), lambda i: (i, 0)),
            pl.BlockSpec((D_MODEL, LANES), lambda i: (0, 0)),
            pl.BlockSpec((1, LANES), lambda i: (0, 0)),
        ],
        out_specs=pl.BlockSpec((tm, LANES), lambda i: (i, 0)),
        out_shape=jax.ShapeDtypeStruct((t, LANES), F32),
        compiler_params=_cparams("parallel"),
        name="router",
    )(x2d, w_pad, b_pad)


def _ffn_kernel(eid_ref, valid_ref, x_ref, wg_ref, wu_ref, wd_ref, o_ref, acc_ref):
    i = pl.program_id(0)
    c = pl.program_id(1)
    last = c == pl.num_programs(1) - 1
    valid = valid_ref[i] != 0

    @pl.when(valid)
    def _():
        x = x_ref[...]
        gate = _dot(x, wg_ref[0])
        up = _dot(x, wu_ref[0])
        h = gate * jax.nn.sigmoid(gate) * up
        contrib = _dot(h.astype(BF16), wd_ref[0])

        @pl.when(c == 0)
        def _():
            acc_ref[...] = contrib

        @pl.when(c != 0)
        def _():
            acc_ref[...] += contrib

        @pl.when(last)
        def _():
            o_ref[...] = acc_ref[...].astype(o_ref.dtype)

    @pl.when(jnp.logical_not(valid) & last)
    def _():
        o_ref[...] = jnp.zeros(o_ref.shape, o_ref.dtype)


def _ffn(xb, w_gu, w_d, eids, valid, out_dtype, tm, fc, name):
    r = xb.shape[0]
    f = w_d.shape[1]
    ncf = f // fc
    in_specs = [
        pl.BlockSpec((tm, D_MODEL), lambda i, c, eid, ok: (i, 0)),
        pl.BlockSpec((1, D_MODEL, fc), lambda i, c, eid, ok: (eid[i], 0, c * ok[i])),
        pl.BlockSpec((1, D_MODEL, fc), lambda i, c, eid, ok: (eid[i], 0, ncf + c * ok[i])),
        pl.BlockSpec((1, fc, D_MODEL), lambda i, c, eid, ok: (eid[i], c * ok[i], 0)),
    ]
    return pl.pallas_call(
        _ffn_kernel,
        grid_spec=pltpu.PrefetchScalarGridSpec(
            num_scalar_prefetch=2,
            grid=(r // tm, ncf),
            in_specs=in_specs,
            out_specs=pl.BlockSpec((tm, D_MODEL), lambda i, c, eid, ok: (i, 0)),
            scratch_shapes=[pltpu.VMEM((tm, D_MODEL), F32)],
        ),
        out_shape=jax.ShapeDtypeStruct((r, D_MODEL), out_dtype),
        compiler_params=_cparams("parallel", "arbitrary"),
        name=name,
    )(eids, valid, xb, w_gu, w_gu, w_d)


MOE_TM = 512


def _moe_gather_kernel(te_ref, clo_ref, nch_ref, x_ref, pos_ref, o_ref, acc_ref, *, tm):
    r = pl.program_id(0)
    e = te_ref[r]
    row = r * tm + lax.broadcasted_iota(jnp.int32, (tm, tm), 0)
    acc_ref[...] = jnp.zeros(acc_ref.shape, F32)

    def body(i, carry):
        t0 = pl.multiple_of((clo_ref[r] + i) * tm, tm)
        p = pos_ref[pl.ds(e, 1), pl.ds(t0, tm)]
        onehot = jnp.where(p == row, 1.0, 0.0).astype(BF16)
        acc_ref[...] += _dot(onehot, x_ref[pl.ds(t0, tm), :])
        return carry

    lax.fori_loop(0, nch_ref[r], body, 0)
    o_ref[...] = acc_ref[...].astype(BF16)


def _moe_gather(xb, pos_t, tile_e, clo, nch, n_rows, tm=MOE_TM):
    t = xb.shape[0]
    return pl.pallas_call(
        functools.partial(_moe_gather_kernel, tm=tm),
        grid_spec=pltpu.PrefetchScalarGridSpec(
            num_scalar_prefetch=3,
            grid=(n_rows // tm,),
            in_specs=[
                pl.BlockSpec((t, D_MODEL), lambda r, *_: (0, 0), pipeline_mode=pl.Buffered(1)),
                pl.BlockSpec(pos_t.shape, lambda r, *_: (0, 0)),
            ],
            out_specs=pl.BlockSpec((tm, D_MODEL), lambda r, *_: (r, 0)),
            scratch_shapes=[pltpu.VMEM((tm, D_MODEL), F32)],
        ),
        out_shape=jax.ShapeDtypeStruct((n_rows, D_MODEL), BF16),
        compiler_params=_cparams("arbitrary"),
        name="moe_gather",
    )(tile_e, clo, nch, xb, pos_t)


MOE_CS = 256
MOE_SLOTS = N_EXPERTS * (MOE_TM // MOE_CS + 1)
MOE_RING = 4


def _moe_combine_kernel(n_ref, cid_ref, eid_ref, ys_hbm, pos_ref, comb_ref, o_ref, buf_ref, sem_ref,
                        *, tm, cs, n_slots, n_buf):
    i = pl.program_id(0)
    n = n_ref[i]
    base = i * n_slots

    def chunk_copy(k, slot):
        row0 = pl.multiple_of(cid_ref[base + k] * cs, cs)
        return pltpu.make_async_copy(ys_hbm.at[pl.ds(row0, cs), :], buf_ref.at[slot], sem_ref.at[slot])

    o_ref[...] = jnp.zeros(o_ref.shape, F32)

    for d in range(n_buf):

        @pl.when(d < n)
        def _():
            chunk_copy(d, d).start()

    def body(k, carry):
        slot = k % n_buf
        chunk_copy(k, slot).wait()
        e = eid_ref[base + k]
        lane = lax.broadcasted_iota(jnp.int32, pos_ref.shape, 1)
        pos_e = jnp.sum(jnp.where(lane == e, pos_ref[...], 0), axis=1, keepdims=True)
        lane_c = lax.broadcasted_iota(jnp.int32, comb_ref.shape, 1)
        g_e = jnp.sum(jnp.where(lane_c == e, comb_ref[...], 0.0), axis=1, keepdims=True)
        col = cid_ref[base + k] * cs + lax.broadcasted_iota(jnp.int32, (tm, cs), 1)
        onehot = jnp.where(pos_e == col, 1.0, 0.0).astype(BF16)
        o_ref[...] += g_e * _dot(onehot, buf_ref[slot])

        @pl.when(k + n_buf < n)
        def _():
            chunk_copy(k + n_buf, slot).start()

        return carry

    lax.fori_loop(0, n, body, 0)


def _moe_combine(ys, pos, combine, n_need, cid, eid, tm=MOE_TM, cs=MOE_CS):
    t = pos.shape[0]
    return pl.pallas_call(
        functools.partial(_moe_combine_kernel, tm=tm, cs=cs, n_slots=MOE_SLOTS, n_buf=MOE_RING),
        grid_spec=pltpu.PrefetchScalarGridSpec(
            num_scalar_prefetch=3,
            grid=(t // tm,),
            in_specs=[
                pl.BlockSpec(memory_space=pl.ANY),
                pl.BlockSpec((tm, N_EXPERTS), lambda i, *_: (i, 0)),
                pl.BlockSpec((tm, LANES), lambda i, *_: (i, 0)),
            ],
            out_specs=pl.BlockSpec((tm, D_MODEL), lambda i, *_: (i, 0)),
            scratch_shapes=[pltpu.VMEM((MOE_RING, cs, D_MODEL), BF16), pltpu.SemaphoreType.DMA((MOE_RING,))],
        ),
        out_shape=jax.ShapeDtypeStruct((t, D_MODEL), F32),
        compiler_params=_cparams("arbitrary"),
        name="moe_combine",
    )(n_need, cid, eid, ys, pos, combine)


def _moe_plan(combine, tm=MOE_TM):
    t = combine.shape[0]
    n_tok_tiles = t // tm
    n_row_tiles = (2 * t) // tm + N_EXPERTS
    sel = (combine[:, :N_EXPERTS] != 0.0).astype(jnp.int32)
    csum = jnp.cumsum(sel, axis=0)
    count = csum[-1]
    ntile_e = (count + tm - 1) // tm
    tile_end = jnp.cumsum(ntile_e)
    gstart = (tile_end - ntile_e) * tm
    pos = jnp.where(sel != 0, gstart[None, :] + csum - sel, -1)
    rt = jnp.arange(n_row_tiles, dtype=jnp.int32)
    tile_e = jnp.minimum(jnp.sum(rt[:, None] >= tile_end[None, :], axis=1), N_EXPERTS - 1).astype(jnp.int32)
    valid = (rt < tile_end[-1]).astype(jnp.int32)
    cb = jnp.concatenate([jnp.zeros((1, N_EXPERTS), jnp.int32), csum[tm - 1::tm]], axis=0)
    off = rt * tm - gstart[tile_e]
    cb_t = cb[:, tile_e].T
    clo = jnp.sum(cb_t[:, 1:] <= off[:, None], axis=1).astype(jnp.int32)
    cend = jnp.sum(cb_t[:, :-1] < (off + tm)[:, None], axis=1).astype(jnp.int32)
    nch = jnp.where(valid != 0, jnp.maximum(cend - clo, 0), 0).astype(jnp.int32)
    clo = jnp.minimum(clo, n_tok_tiles - 1)
    first = gstart[None, :] + cb[:-1]
    n_ie = cb[1:] - cb[:-1]
    per_e = MOE_SLOTS // N_EXPERTS
    k = jnp.arange(per_e, dtype=jnp.int32)
    c_lo = first // MOE_CS
    c_hi = (first + n_ie - 1) // MOE_CS
    cand = c_lo[..., None] + k
    need = ((n_ie[..., None] > 0) & (cand <= c_hi[..., None])).reshape(n_tok_tiles, MOE_SLOTS)
    cand = cand.reshape(n_tok_tiles, MOE_SLOTS)
    eids = jnp.broadcast_to(jnp.arange(N_EXPERTS, dtype=jnp.int32)[:, None], (N_EXPERTS, per_e)).reshape(-1)
    order = jnp.argsort(jnp.logical_not(need), axis=1, stable=True)
    cid = jnp.where(need, cand, 0)
    cid = jnp.take_along_axis(cid, order, axis=1).reshape(-1).astype(jnp.int32)
    eid = eids[order].reshape(-1).astype(jnp.int32)
    n_need = jnp.sum(need, axis=1).astype(jnp.int32)
    return dict(pos=pos.astype(jnp.int32), tile_e=tile_e, valid=valid, clo=clo, nch=nch,
                n_need=n_need, cid=cid, eid=eid, n_rows=n_row_tiles * tm)


def _moe(xb, combine, w_gu, w_d, first_expert):
    plan = _moe_plan(combine)
    xs = _moe_gather(xb, plan["pos"].T, plan["tile_e"], plan["clo"], plan["nch"], plan["n_rows"])
    ys = _ffn(xs, w_gu, w_d, plan["tile_e"] + first_expert, plan["valid"], BF16, tm=MOE_TM, fc=1792,
              name="ffn_moe")
    return _moe_combine(ys, plan["pos"], combine, plan["n_need"], plan["cid"], plan["eid"])


def _lnple_kernel(x_ref, h_ref, g_ref, b_ref, p_ref, wg_ref, wp_ref, o_ref):
    y = _layer_norm(DEEPNORM_ALPHA * x_ref[...] + h_ref[...], g_ref[...], b_ref[...])
    gate = jax.nn.sigmoid(_dot(y.astype(BF16), wg_ref[...]))
    proj = _dot(p_ref[...].astype(BF16), wp_ref[...])
    o_ref[...] = y + gate * proj


def _ln_ple(x2d, h2d, g, b, p2d, wg, wp, tm=512):
    t = x2d.shape[0]
    row = lambda n: pl.BlockSpec((tm, n), lambda i: (i, 0))
    full = lambda a: pl.BlockSpec(a.shape, lambda i: (0, 0))
    return pl.pallas_call(
        _lnple_kernel,
        grid=(t // tm,),
        in_specs=[row(D_MODEL), row(D_MODEL), full(g), full(b), row(D_PLE), full(wg), full(wp)],
        out_specs=row(D_MODEL),
        out_shape=jax.ShapeDtypeStruct((t, D_MODEL), F32),
        compiler_params=_cparams("parallel"),
        name="ln_ple",
    )(x2d, h2d, g, b, p2d, wg, wp)


def _even_mixer(x2d, b, s, w_in, b_f, w_out):
    nf, ns = FOX_W, SB_W
    o_f = 3 * nf
    o_s = o_f + N_HEADS_FOX
    qa, ka, va = w_in[:, :nf], w_in[:, nf:2 * nf], w_in[:, 2 * nf:o_f]
    qs, ks, vs = w_in[:, o_s:o_s + ns], w_in[:, o_s + ns:o_s + 2 * ns], w_in[:, o_s + 2 * ns:]
    w_k = jnp.concatenate([ka, ks], axis=1).astype(BF16)
    w_t = jnp.concatenate([qa * ATTN_SCALE, va, qs * ATTN_SCALE, vs], axis=1).T.astype(BF16)
    w_f = jnp.pad(w_in[:, o_f:o_s], ((0, 0), (0, LANES - N_HEADS_FOX))).astype(BF16)
    bias_f = jnp.pad(b_f, (0, LANES - N_HEADS_FOX)).reshape(1, LANES)
    k2d, ht, lf = _ab_proj(x2d, w_k, w_t, w_f, bias_f, b, s)
    lf_t = lf.reshape(b, s, LANES)[:, :, :N_HEADS_FOX].transpose(0, 2, 1)
    crow, caug = _cumsum_seq(lf_t)
    k3 = k2d.reshape(b, s, -1)
    oa = _fox_attention(k3, ht, caug, crow)
    ob = _sb_attention(k3, ht)
    w_out_b = w_out.astype(BF16)
    return [oa.reshape(b * s, nf), ob.reshape(b * s, ns)], [w_out_b[:nf], w_out_b[nf:]]


def _odd_mixer(x2d, b, s, w_qkv, sinks, w_out, tables):
    nq = N_HEADS_SWA * HEAD_DIM
    wq = w_qkv[:, :nq]
    wk = [w_qkv[:, nq + i * HEAD_DIM: nq + (i + 1) * HEAD_DIM] for i in range(N_KV_SWA)]
    wv = [w_qkv[:, nq + (N_KV_SWA + i) * HEAD_DIM: nq + (N_KV_SWA + i + 1) * HEAD_DIM] for i in range(N_KV_SWA)]
    dup = lambda ws: [w for w in ws for _ in range(2)]
    w = jnp.concatenate([wq] + dup(wk) + dup(wv), axis=1).astype(BF16)
    q2d, kv2d = _swa_proj(x2d, w, tables, s)
    o = _swa_attention(q2d.reshape(b, s, nq), kv2d.reshape(b, s, -1), sinks)
    return [o.reshape(b * s, nq)], [w_out.astype(BF16)]


def kernel(x, p, ln_mix_g, ln_mix_b, ln_ffn_g, ln_ffn_b, ab_w_in, ab_b_f, ab_w_out, c_w_qkv, c_sinks, c_w_out,
           ffn_w_gate_up, ffn_w_down, router_w, router_b, moe_w_gate_up, moe_w_down, ple_w_gate, ple_w_proj):
    b, s, d = x.shape
    t = b * s
    x2d = x.reshape(t, d)
    tables = _rope_tables(s)
    row = lambda v: v.reshape(1, -1)
    moe_gu = moe_w_gate_up.astype(BF16).reshape((-1,) + moe_w_gate_up.shape[2:])
    moe_d = moe_w_down.astype(BF16).reshape((-1,) + moe_w_down.shape[2:])
    for i in range(DEPTH):
        j = i // 2
        if i % 2 == 0:
            os_, ws = _even_mixer(x2d, b, s, ab_w_in[j], ab_b_f[j], ab_w_out[j])
        else:
            os_, ws = _odd_mixer(x2d, b, s, c_w_qkv[j], c_sinks[j], c_w_out[j], tables)
        x2d, xb = _out_proj_ln(os_, ws, x2d, row(ln_mix_g[i]), row(ln_mix_b[i]))
        if i % 2 == 0:
            n_tiles = t // 512
            h = _ffn(xb, ffn_w_gate_up[j][None].astype(BF16), ffn_w_down[j][None].astype(BF16),
                     jnp.zeros((n_tiles,), jnp.int32), jnp.ones((n_tiles,), jnp.int32), F32,
                     tm=512, fc=1408, name="ffn_dense")
        else:
            rw = jnp.pad(router_w[j], ((0, 0), (0, LANES - N_EXPERTS)))
            rb = jnp.pad(router_b[j], (0, LANES - N_EXPERTS)).reshape(1, LANES)
            combine = _router(x2d, rw, rb)
            h = _moe(xb, combine, moe_gu, moe_d, j * N_EXPERTS)
        x2d = _ln_ple(x2d, h, row(ln_ffn_g[i]), row(ln_ffn_b[i]), p[i].reshape(t, D_PLE),
                      ple_w_gate[i].astype(BF16), ple_w_proj[i].astype(BF16))
    return x2d.reshape(b, s, d)
```

```python
import functools
import math

import jax
import jax.numpy as jnp
from jax import lax
from jax.experimental import pallas as pl
from jax.experimental.pallas import tpu as pltpu

F32 = jnp.float32
BF16 = jnp.bfloat16

D_MODEL = 1024
HEAD_DIM = 64
LANES = 128
N_HEADS_FOX = 8
N_HEADS_SB = 8
N_HEADS_SWA = 16
N_KV_SWA = 2
WINDOW = 128
ROPE_THETA = 500000.0
ROPE_DIM = HEAD_DIM // 4
N_EXPERTS = 8
D_PLE = 256
LN_EPS = 1e-5
DEPTH = 4
DEEPNORM_ALPHA = (2.0 * DEPTH) ** 0.25
ATTN_SCALE = HEAD_DIM ** -0.5
FOX_W = N_HEADS_FOX * HEAD_DIM
SB_W = N_HEADS_SB * HEAD_DIM
VMEM_LIMIT = 56 * 1024 * 1024

NEG_INF = float("-inf")
LOG2E = math.log2(math.e)
Q_SCALE = ATTN_SCALE * LOG2E
SUM_ROWS = 16


def _cparams(*sem):
    return pltpu.CompilerParams(dimension_semantics=sem, vmem_limit_bytes=VMEM_LIMIT)


def _dot(a, b):
    return jnp.dot(a, b, preferred_element_type=F32)


def _dot_nt(a, b):
    return lax.dot_general(a, b, (((1,), (1,)), ((), ())), preferred_element_type=F32)


def _log_sigmoid(x):
    return jnp.minimum(x, 0.0) - jnp.log1p(jnp.exp(-jnp.abs(x)))


def _head_masks(shape):
    lane = lax.broadcasted_iota(jnp.int32, shape, len(shape) - 1)
    return lane < HEAD_DIM, lane >= HEAD_DIM


def _ab_proj_kernel(x_ref, wk_ref, wt_ref, wf_ref, bf_ref, k_ref, ht_ref, lf_ref):
    xb = x_ref[...].astype(BF16)
    k_ref[...] = _dot(xb, wk_ref[...]).astype(BF16)
    ht_ref[0] = _dot_nt(wt_ref[...], xb).astype(BF16)
    lf_ref[...] = _log_sigmoid(_dot(xb, wf_ref[...]) + bf_ref[...])


def _ab_proj(x2d, wk, wt, wf, bf, b, s, tm=512):
    t = x2d.shape[0]
    nk = wk.shape[1]
    nt = wt.shape[0]
    spt = s // tm
    return pl.pallas_call(
        _ab_proj_kernel,
        grid=(t // tm,),
        in_specs=[
            pl.BlockSpec((tm, D_MODEL), lambda i: (i, 0)),
            pl.BlockSpec((D_MODEL, nk), lambda i: (0, 0)),
            pl.BlockSpec((nt, D_MODEL), lambda i: (0, 0)),
            pl.BlockSpec((D_MODEL, LANES), lambda i: (0, 0)),
            pl.BlockSpec((1, LANES), lambda i: (0, 0)),
        ],
        out_specs=[
            pl.BlockSpec((tm, nk), lambda i: (i, 0)),
            pl.BlockSpec((1, nt, tm), lambda i: (i // spt, 0, i % spt)),
            pl.BlockSpec((tm, LANES), lambda i: (i, 0)),
        ],
        out_shape=[
            jax.ShapeDtypeStruct((t, nk), BF16),
            jax.ShapeDtypeStruct((b, nt, s), BF16),
            jax.ShapeDtypeStruct((t, LANES), F32),
        ],
        compiler_params=_cparams("parallel"),
        name="ab_proj",
    )(x2d, wk, wt, wf, bf)


def _split3(v):
    hi = v.astype(BF16).astype(F32)
    r1 = v - hi
    mid = r1.astype(BF16).astype(F32)
    return hi, mid, r1 - mid


def _cumsum_kernel(x_ref, c_ref, aug_ref, st_ref):
    x = x_ref[0]
    s = x.shape[1]
    lane = lax.broadcasted_iota(jnp.int32, x.shape, 1) & (LANES - 1)
    sh = 1
    while sh < LANES:
        x = x + jnp.where(lane >= sh, pltpu.roll(x, sh, axis=1), 0.0)
        sh *= 2
    carry = jnp.zeros((x.shape[0], 1), F32)
    for c in range(s // LANES):
        blk = x[:, c * LANES:(c + 1) * LANES] + carry
        c_ref[0, :, c * LANES:(c + 1) * LANES] = blk
        carry = blk[:, LANES - 1:LANES]

    ones = jnp.ones((3, s), F32)
    st_ref[...] = jnp.zeros(st_ref.shape, F32)
    for p in range(x.shape[0] // 2):
        for which, base in ((0, HEAD_DIM), (1, 0)):
            negc = c_ref[0, 2 * p + which:2 * p + which + 1, :] * -LOG2E
            hi, mid, lo = _split3(negc)
            st_ref[base:base + 1, :] = hi
            st_ref[base + 1:base + 2, :] = mid
            st_ref[base + 2:base + 3, :] = lo
            st_ref[base + 3:base + 6, :] = ones
        for c in range(s // LANES):
            blk = st_ref[:, c * LANES:(c + 1) * LANES]
            aug_ref[0, p, c * LANES:(c + 1) * LANES, :] = blk.T.astype(BF16)


def _cumsum_seq(lf_t):
    b, h, s = lf_t.shape
    return pl.pallas_call(
        _cumsum_kernel,
        grid=(b,),
        in_specs=[pl.BlockSpec((1, h, s), lambda i: (i, 0, 0))],
        out_specs=[
            pl.BlockSpec((1, h, s), lambda i: (i, 0, 0)),
            pl.BlockSpec((1, h // 2, s, LANES), lambda i: (i, 0, 0, 0)),
        ],
        out_shape=[
            jax.ShapeDtypeStruct((b, h, s), F32),
            jax.ShapeDtypeStruct((b, h // 2, s, LANES), BF16),
        ],
        scratch_shapes=[pltpu.VMEM((LANES, s), F32)],
        compiler_params=_cparams("parallel"),
        name="fox_cumsum",
    )(lf_t)


def _fox_kernel(qt_ref, k_ref, aug_ref, vt_ref, c_ref, o_ref, m_ref, acc_ref, sa_ref, sb_ref, *, tq, tk):
    hp = pl.program_id(1)
    qi = pl.program_id(2)
    q0 = qi * tq
    qt = qt_ref[0].astype(F32)
    sub = lax.broadcasted_iota(jnp.int32, (LANES, tq), 0)
    lane = lax.broadcasted_iota(jnp.int32, (tk, LANES), 1)
    vsub = lax.broadcasted_iota(jnp.int32, (LANES, tk), 0)
    own_rows = [sub < HEAD_DIM, sub >= HEAD_DIM]
    own_lanes = [lane < HEAD_DIM, lane >= HEAD_DIM]
    own_vrows = [vsub < HEAD_DIM, vsub >= HEAD_DIM]

    qaug = []
    for r in range(2):
        base = HEAD_DIM * (1 - r)
        ci = c_ref[0, pl.ds(2 * hp + r, 1), :]
        hi, mid, lo = _split3(ci * LOG2E)
        t = jnp.where(own_rows[r], qt, 0.0)
        t = jnp.where((sub >= base) & (sub < base + 3), 1.0, t)
        t = jnp.where(sub == base + 3, hi, t)
        t = jnp.where(sub == base + 4, mid, t)
        t = jnp.where(sub == base + 5, lo, t)
        qaug.append(t.astype(BF16))

    m_ref[...] = jnp.full(m_ref.shape, NEG_INF, F32)
    acc_ref[...] = jnp.zeros(acc_ref.shape, F32)

    def scores(k0, s_ref):
        kt = k_ref[0, pl.ds(k0, tk), :]
        ca = aug_ref[0, 0, pl.ds(k0, tk), :]
        for r in range(2):
            ka = jnp.where(own_lanes[r], kt, ca)
            s_ref[r] = _dot(ka, qaug[r])

    def softmax_pv(k0, s_ref, masked):
        vt = vt_ref[0, :, pl.ds(k0, tk)]
        if masked:
            key = k0 + lax.broadcasted_iota(jnp.int32, (tk, tq), 0)
            qry = q0 + lax.broadcasted_iota(jnp.int32, (tk, tq), 1)
            keep = key <= qry
        for r in range(2):
            st = s_ref[r]
            if masked:
                st = jnp.where(keep, st, NEG_INF)
            m_prev = m_ref[r]
            m_new = jnp.maximum(m_prev, jnp.max(st, axis=0, keepdims=True))
            alpha = jnp.exp2(m_prev - m_new)
            pt = jnp.exp2(st - m_new).astype(BF16)
            va = jnp.where(own_vrows[r], vt, 1.0)
            acc_ref[r] = alpha * acc_ref[r] + _dot(va, pt)
            m_ref[r] = m_new

    tile = lambda j: pl.multiple_of(j * tk, tk)
    scores(tile(0), sa_ref)

    def pair(i):
        scores(tile(2 * i + 1), sb_ref)
        softmax_pv(tile(2 * i), sa_ref, False)
        scores(tile(2 * i + 2), sa_ref)
        softmax_pv(tile(2 * i + 1), sb_ref, False)

    def body(j, carry):
        pair(2 * j)
        pair(2 * j + 1)
        return carry

    lax.fori_loop(0, qi // 2, body, 0)

    @pl.when(qi % 2 == 1)
    def _():
        pair(qi - 1)

    scores(tile(2 * qi + 1), sb_ref)
    softmax_pv(tile(2 * qi), sa_ref, True)
    softmax_pv(tile(2 * qi + 1), sb_ref, True)

    ot = jnp.zeros((LANES, tq), F32)
    for r in range(2):
        a = acc_ref[r]
        den = a[HEAD_DIM * (1 - r):HEAD_DIM * (1 - r) + 1, :]
        ot = ot + jnp.where(own_rows[r], a / den, 0.0)
    o_ref[0] = ot.T.astype(BF16)


def _fox_attention(k3, ht, caug, crow, tq=512):
    b, s, _ = k3.shape
    nhp = N_HEADS_FOX // 2
    tk = tq // 2
    kern = functools.partial(_fox_kernel, tq=tq, tk=tk)
    return pl.pallas_call(
        kern,
        grid=(b, nhp, s // tq),
        in_specs=[
            pl.BlockSpec((1, LANES, tq), lambda bi, hp, qi: (bi, hp, qi)),
            pl.BlockSpec((1, s, LANES), lambda bi, hp, qi: (bi, 0, hp)),
            pl.BlockSpec((1, 1, s, LANES), lambda bi, hp, qi: (bi, hp, 0, 0)),
            pl.BlockSpec((1, LANES, s), lambda bi, hp, qi: (bi, nhp + hp, 0)),
            pl.BlockSpec((1, N_HEADS_FOX, tq), lambda bi, hp, qi: (bi, 0, qi)),
        ],
        out_specs=pl.BlockSpec((1, tq, LANES), lambda bi, hp, qi: (bi, qi, hp)),
        out_shape=jax.ShapeDtypeStruct((b, s, FOX_W), BF16),
        scratch_shapes=[
            pltpu.VMEM((2, 1, tq), F32),
            pltpu.VMEM((2, LANES, tq), F32),
            pltpu.VMEM((2, tk, tq), F32),
            pltpu.VMEM((2, tk, tq), F32),
        ],
        compiler_params=_cparams("parallel", "parallel", "arbitrary"),
        name="fox_attn",
    )(ht, k3, caug, ht, crow)


def _sb_kernel(qt_ref, k_ref, vt_ref, o_ref, run_ref, acc_ref, za_ref, zb_ref, *, tq, cw):
    qi = pl.program_id(2)
    q0 = qi * tq
    qt = qt_ref[0]
    sub = lax.broadcasted_iota(jnp.int32, (LANES, tq), 0)
    own_rows = [sub < HEAD_DIM, sub >= HEAD_DIM]
    qm = [jnp.where(mk, qt, 0.0) for mk in own_rows]

    ss = lax.broadcasted_iota(jnp.int32, (cw, cw), 0)
    jj = lax.broadcasted_iota(jnp.int32, (cw, cw), 1)
    tri = jnp.where(jj > ss, 1.0, 0.0).astype(BF16)
    tri = jnp.concatenate([tri, jnp.ones((SUM_ROWS, cw), BF16)], axis=0)

    run_ref[...] = jnp.zeros(run_ref.shape, F32)
    acc_ref[...] = jnp.zeros(acc_ref.shape, F32)

    def start_of(c):
        return pl.multiple_of(jnp.maximum(q0 + tq - (c + 1) * cw, 0), cw)

    def logits(c, z_ref):
        kt = k_ref[0, pl.ds(start_of(c), cw), :]
        for r in range(2):
            z_ref[r] = _dot(kt, qm[r])

    def weights_pv(c, z_ref, masked):
        k0 = start_of(c)
        vt = vt_ref[0, :, pl.ds(k0, cw)]
        if masked:
            key = k0 + lax.broadcasted_iota(jnp.int32, (cw, tq), 0)
            qry = q0 + lax.broadcasted_iota(jnp.int32, (cw, tq), 1)
            keep = key < qry
        zs, sps, laters, ws = [], [], [], []
        for r in range(2):
            z = z_ref[r]
            e = jnp.exp2(-jnp.abs(z))
            sp = jnp.maximum(z, 0.0) + jnp.log(1.0 + e) * LOG2E
            if masked:
                sp = jnp.where(keep, sp, 0.0)
            zs.append(z)
            sps.append(sp)
        for r in range(2):
            laters.append(_dot(tri, sps[r].astype(BF16)))
        for r in range(2):
            run = run_ref[r]
            a = jnp.exp2((zs[r] - sps[r]) - (laters[r][:cw] + run))
            if masked:
                a = jnp.where(keep, a, 0.0)
            ws.append(a.astype(BF16))
            run_ref[r] = run + laters[r][cw:cw + 1]
        for r in range(2):
            acc_ref[r] = acc_ref[r] + _dot(vt, ws[r])

    logits(0, za_ref)
    logits(1, zb_ref)
    weights_pv(0, za_ref, True)
    logits(2, za_ref)
    weights_pv(1, zb_ref, True)

    def body(i, carry):
        logits(2 * i + 1, zb_ref)
        weights_pv(2 * i, za_ref, False)
        logits(2 * i + 2, za_ref)
        weights_pv(2 * i + 1, zb_ref, False)
        return carry

    lax.fori_loop(1, qi + 1, body, 0)

    ot = jnp.where(own_rows[0], acc_ref[0], acc_ref[1])
    o_ref[0] = ot.T.astype(BF16)


def _sb_attention(k3, ht, tq=512):
    b, s, _ = k3.shape
    cw = tq // 2
    nhp = N_HEADS_SB // 2
    kern = functools.partial(_sb_kernel, tq=tq, cw=cw)
    return pl.pallas_call(
        kern,
        grid=(b, nhp, s // tq),
        in_specs=[
            pl.BlockSpec((1, LANES, tq), lambda bi, hp, qi: (bi, 2 * nhp + hp, qi)),
            pl.BlockSpec((1, s, LANES), lambda bi, hp, qi: (bi, 0, nhp + hp)),
            pl.BlockSpec((1, LANES, s), lambda bi, hp, qi: (bi, 3 * nhp + hp, 0)),
        ],
        out_specs=pl.BlockSpec((1, tq, LANES), lambda bi, hp, qi: (bi, qi, hp)),
        out_shape=jax.ShapeDtypeStruct((b, s, SB_W), BF16),
        scratch_shapes=[
            pltpu.VMEM((2, 1, tq), F32),
            pltpu.VMEM((2, LANES, tq), F32),
            pltpu.VMEM((2, cw, tq), F32),
            pltpu.VMEM((2, cw, tq), F32),
        ],
        compiler_params=_cparams("parallel", "parallel", "arbitrary"),
        name="sb_attn",
    )(ht, k3, ht)


def _swa_proj_kernel(x_ref, w_ref, cos_ref, sa_ref, sb_ref, q_ref, kv_ref, *, n_rope_blocks, n_q_blocks):
    xb = x_ref[...].astype(BF16)
    y = _dot(xb, w_ref[...])
    cos = cos_ref[...]
    sin_a = sa_ref[...]
    sin_b = sb_ref[...]
    n_blocks = y.shape[1] // LANES
    for c in range(n_blocks):
        blk = y[:, c * LANES:(c + 1) * LANES]
        if c < n_rope_blocks:
            half = ROPE_DIM // 2
            blk = (blk * cos + pltpu.roll(blk, half, axis=1) * sin_a
                   + pltpu.roll(blk, LANES - half, axis=1) * sin_b)
        blk = blk.astype(BF16)
        if c < n_q_blocks:
            q_ref[:, c * LANES:(c + 1) * LANES] = blk
        else:
            kv_ref[:, (c - n_q_blocks) * LANES:(c - n_q_blocks + 1) * LANES] = blk


def _rope_tables(s):
    half = ROPE_DIM // 2
    inv = ROPE_THETA ** (-jnp.arange(half, dtype=F32) * 2.0 / ROPE_DIM)
    ang = jnp.arange(s, dtype=F32)[:, None] * inv[None, :]
    cos, sin = jnp.cos(ang), jnp.sin(ang)
    d = jnp.arange(LANES) % HEAD_DIM
    idx = d % half
    cos_t = jnp.where(d[None, :] < ROPE_DIM, cos[:, idx], 1.0)
    sin_a = jnp.where((d[None, :] >= half) & (d[None, :] < ROPE_DIM), sin[:, idx], 0.0)
    sin_b = jnp.where(d[None, :] < half, -sin[:, idx], 0.0)
    return cos_t.astype(F32), sin_a.astype(F32), sin_b.astype(F32)


def _swa_proj(x2d, w, tables, s, tm=512):
    t = x2d.shape[0]
    n = w.shape[1]
    nq = N_HEADS_SWA * HEAD_DIM
    nkv = n - nq
    spt = s // tm
    kern = functools.partial(_swa_proj_kernel, n_rope_blocks=(nq + nkv // 2) // LANES, n_q_blocks=nq // LANES)
    tab_spec = pl.BlockSpec((tm, LANES), lambda i: (i % spt, 0))
    return pl.pallas_call(
        kern,
        grid=(t // tm,),
        in_specs=[
            pl.BlockSpec((tm, D_MODEL), lambda i: (i, 0)),
            pl.BlockSpec((D_MODEL, n), lambda i: (0, 0)),
            tab_spec, tab_spec, tab_spec,
        ],
        out_specs=[
            pl.BlockSpec((tm, nq), lambda i: (i, 0)),
            pl.BlockSpec((tm, nkv), lambda i: (i, 0)),
        ],
        out_shape=[jax.ShapeDtypeStruct((t, nq), BF16), jax.ShapeDtypeStruct((t, nkv), BF16)],
        compiler_params=_cparams("parallel"),
        name="swa_proj",
    )(x2d, w, *tables)


def _swa_kernel(sink_ref, q_ref, kv_ref, o_ref, *, tq):
    qi = pl.program_id(1)
    q0 = qi * tq
    kw = tq + WINDOW
    kstart = pl.multiple_of(jnp.maximum(q0 - WINDOW, 0), WINDOW)
    kv = kv_ref[0, pl.ds(kstart, kw), :]
    row = q0 + lax.broadcasted_iota(jnp.int32, (tq, kw), 0)
    col = kstart + lax.broadcasted_iota(jnp.int32, (tq, kw), 1)
    diff = row - col
    keep = (diff >= 0) & (diff < WINDOW)
    masks = _head_masks((tq, LANES))
    group = N_HEADS_SWA // N_KV_SWA
    for c in range(N_HEADS_SWA // 2):
        g = (2 * c) // group
        kt = kv[:, g * LANES:(g + 1) * LANES]
        vt = kv[:, (N_KV_SWA + g) * LANES:(N_KV_SWA + g + 1) * LANES]
        qblk = q_ref[0, :, c * LANES:(c + 1) * LANES]
        o = jnp.zeros((tq, LANES), F32)
        for r in range(2):
            qm = (jnp.where(masks[r], qblk, 0.0) * ATTN_SCALE).astype(BF16)
            s = jnp.where(keep, _dot_nt(qm, kt), NEG_INF)
            sink = sink_ref[2 * c + r]
            mx = jnp.maximum(jnp.max(s, axis=1, keepdims=True), sink)
            e = jnp.exp(s - mx)
            den = jnp.sum(e, axis=1, keepdims=True) + jnp.exp(sink - mx)
            pv = _dot(e.astype(BF16), vt)
            o = o + jnp.where(masks[r], pv / den, 0.0)
        o_ref[0, :, c * LANES:(c + 1) * LANES] = o.astype(BF16)


def _swa_attention(q3, kv3, sinks, tq=128):
    b, s, nq = q3.shape
    nkv = kv3.shape[2]
    kern = functools.partial(_swa_kernel, tq=tq)
    return pl.pallas_call(
        kern,
        grid=(b, s // tq),
        in_specs=[
            pl.BlockSpec(memory_space=pltpu.SMEM),
            pl.BlockSpec((1, tq, nq), lambda bi, qi: (bi, qi, 0)),
            pl.BlockSpec((1, s, nkv), lambda bi, qi: (bi, 0, 0)),
        ],
        out_specs=pl.BlockSpec((1, tq, nq), lambda bi, qi: (bi, qi, 0)),
        out_shape=jax.ShapeDtypeStruct((b, s, nq), BF16),
        compiler_params=_cparams("parallel", "arbitrary"),
        name="swa_attn",
    )(sinks, q3, kv3)


def _layer_norm(z, g, b):
    mu = jnp.mean(z, axis=-1, keepdims=True)
    zc = z - mu
    var = jnp.mean(zc * zc, axis=-1, keepdims=True)
    return zc * lax.rsqrt(var + LN_EPS) * g + b


def _outln_kernel(*refs, n_in):
    o_refs = refs[:n_in]
    w_refs = refs[n_in:2 * n_in]
    x_ref, g_ref, b_ref, y_ref, yb_ref = refs[2 * n_in:]
    h = _dot(o_refs[0][...], w_refs[0][...])
    for i in range(1, n_in):
        h = h + _dot(o_refs[i][...], w_refs[i][...])
    y = _layer_norm(DEEPNORM_ALPHA * x_ref[...] + h, g_ref[...], b_ref[...])
    y_ref[...] = y
    yb_ref[...] = y.astype(BF16)


def _out_proj_ln(os_, ws, x2d, g, b, tm=512):
    t = x2d.shape[0]
    n_in = len(os_)
    in_specs = [pl.BlockSpec((tm, o.shape[1]), lambda i: (i, 0)) for o in os_]
    in_specs += [pl.BlockSpec(w.shape, lambda i: (0, 0)) for w in ws]
    in_specs += [
        pl.BlockSpec((tm, D_MODEL), lambda i: (i, 0)),
        pl.BlockSpec((1, D_MODEL), lambda i: (0, 0)),
        pl.BlockSpec((1, D_MODEL), lambda i: (0, 0)),
    ]
    return pl.pallas_call(
        functools.partial(_outln_kernel, n_in=n_in),
        grid=(t // tm,),
        in_specs=in_specs,
        out_specs=[pl.BlockSpec((tm, D_MODEL), lambda i: (i, 0))] * 2,
        out_shape=[jax.ShapeDtypeStruct((t, D_MODEL), F32), jax.ShapeDtypeStruct((t, D_MODEL), BF16)],
        compiler_params=_cparams("parallel"),
        name="out_proj_ln",
    )(*os_, *ws, x2d, g, b)


def _router_kernel(x_ref, w_ref, b_ref, c_ref):
    x = x_ref[...]
    w = w_ref[...]
    xh = x.astype(BF16)
    xl = (x - xh.astype(F32)).astype(BF16)
    wh = w.astype(BF16)
    wl = (w - wh.astype(F32)).astype(BF16)
    logits = _dot(xh, wh) + (_dot(xl, wh) + _dot(xh, wl)) + b_ref[...]
    lane = lax.broadcasted_iota(jnp.int32, logits.shape, 1)
    logits = jnp.where(lane < N_EXPERTS, logits, NEG_INF)
    m1 = jnp.max(logits, axis=1, keepdims=True)
    i1 = jnp.min(jnp.where(logits == m1, lane, LANES), axis=1, keepdims=True)
    rest = jnp.where(lane == i1, NEG_INF, logits)
    m2 = jnp.max(rest, axis=1, keepdims=True)
    i2 = jnp.min(jnp.where(rest == m2, lane, LANES), axis=1, keepdims=True)
    e2 = jnp.exp(m2 - m1)
    den = 1.0 + e2
    c_ref[...] = jnp.where(lane == i1, 1.0 / den, 0.0) + jnp.where(lane == i2, e2 / den, 0.0)


def _router(x2d, w_pad, b_pad, tm=512):
    t = x2d.shape[0]
    return pl.pallas_call(
        _router_kernel,
        grid=(t // tm,),
        in_specs=[
            pl.BlockSpec((tm, D_MODEL), lambda i: (i, 0)),
            pl.BlockSpec((D_MODEL, LANES), lambda i: (0, 0)),
            pl.BlockSpec((1, LANES), lambda i: (0, 0)),
        ],
        out_specs=pl.BlockSpec((tm, LANES), lambda i: (i, 0)),
        out_shape=jax.ShapeDtypeStruct((t, LANES), F32),
        compiler_params=_cparams("parallel"),
        name="router",
    )(x2d, w_pad, b_pad)


def _ffn_kernel(eid_ref, valid_ref, x_ref, wg_ref, wu_ref, wd_ref, o_ref, acc_ref):
    i = pl.program_id(0)
    c = pl.program_id(1)
    last = c == pl.num_programs(1) - 1
    valid = valid_ref[i] != 0

    @pl.when(valid)
    def _():
        x = x_ref[...]
        gate = _dot(x, wg_ref[0])
        up = _dot(x, wu_ref[0])
        h = gate * jax.nn.sigmoid(gate) * up
        contrib = _dot(h.astype(BF16), wd_ref[0])

        @pl.when(c == 0)
        def _():
            acc_ref[...] = contrib

        @pl.when(c != 0)
        def _():
            acc_ref[...] += contrib

        @pl.when(last)
        def _():
            o_ref[...] = acc_ref[...].astype(o_ref.dtype)

    @pl.when(jnp.logical_not(valid) & last)
    def _():
        o_ref[...] = jnp.zeros(o_ref.shape, o_ref.dtype)


def _ffn(xb, w_gu, w_d, eids, valid, out_dtype, tm, fc, name):
    r = xb.shape[0]
    f = w_d.shape[1]
    ncf = f // fc
    in_specs = [
        pl.BlockSpec((tm, D_MODEL), lambda i, c, eid, ok: (i, 0)),
        pl.BlockSpec((1, D_MODEL, fc), lambda i, c, eid, ok: (eid[i], 0, c * ok[i])),
        pl.BlockSpec((1, D_MODEL, fc), lambda i, c, eid, ok: (eid[i], 0, ncf + c * ok[i])),
        pl.BlockSpec((1, fc, D_MODEL), lambda i, c, eid, ok: (eid[i], c * ok[i], 0)),
    ]
    return pl.pallas_call(
        _ffn_kernel,
        grid_spec=pltpu.PrefetchScalarGridSpec(
            num_scalar_prefetch=2,
            grid=(r // tm, ncf),
            in_specs=in_specs,
            out_specs=pl.BlockSpec((tm, D_MODEL), lambda i, c, eid, ok: (i, 0)),
            scratch_shapes=[pltpu.VMEM((tm, D_MODEL), F32)],
        ),
        out_shape=jax.ShapeDtypeStruct((r, D_MODEL), out_dtype),
        compiler_params=_cparams("parallel", "arbitrary"),
        name=name,
    )(eids, valid, xb, w_gu, w_gu, w_d)


MOE_TM = 512


def _moe_gather_kernel(te_ref, clo_ref, nch_ref, x_ref, pos_ref, o_ref, acc_ref, *, tm):
    r = pl.program_id(0)
    e = te_ref[r]
    row = r * tm + lax.broadcasted_iota(jnp.int32, (tm, tm), 0)
    acc_ref[...] = jnp.zeros(acc_ref.shape, F32)

    def body(i, carry):
        t0 = pl.multiple_of((clo_ref[r] + i) * tm, tm)
        p = pos_ref[pl.ds(e, 1), pl.ds(t0, tm)]
        onehot = jnp.where(p == row, 1.0, 0.0).astype(BF16)
        acc_ref[...] += _dot(onehot, x_ref[pl.ds(t0, tm), :])
        return carry

    lax.fori_loop(0, nch_ref[r], body, 0)
    o_ref[...] = acc_ref[...].astype(BF16)


def _moe_gather(xb, pos_t, tile_e, clo, nch, n_rows, tm=MOE_TM):
    t = xb.shape[0]
    return pl.pallas_call(
        functools.partial(_moe_gather_kernel, tm=tm),
        grid_spec=pltpu.PrefetchScalarGridSpec(
            num_scalar_prefetch=3,
            grid=(n_rows // tm,),
            in_specs=[
                pl.BlockSpec((t, D_MODEL), lambda r, *_: (0, 0), pipeline_mode=pl.Buffered(1)),
                pl.BlockSpec(pos_t.shape, lambda r, *_: (0, 0)),
            ],
            out_specs=pl.BlockSpec((tm, D_MODEL), lambda r, *_: (r, 0)),
            scratch_shapes=[pltpu.VMEM((tm, D_MODEL), F32)],
        ),
        out_shape=jax.ShapeDtypeStruct((n_rows, D_MODEL), BF16),
        compiler_params=_cparams("arbitrary"),
        name="moe_gather",
    )(tile_e, clo, nch, xb, pos_t)


MOE_CS = 256
MOE_SLOTS = N_EXPERTS * (MOE_TM // MOE_CS + 1)
MOE_RING = 4


def _moe_combine_kernel(n_ref, cid_ref, eid_ref, ys_hbm, pos_ref, comb_ref, x_ref, *rest, tm, cs, n_slots, n_buf):
    *ple_refs, o_ref, buf_ref, sem_ref = rest
    i = pl.program_id(0)
    n = n_ref[i]
    base = i * n_slots

    def chunk_copy(k, slot):
        row0 = pl.multiple_of(cid_ref[base + k] * cs, cs)
        return pltpu.make_async_copy(ys_hbm.at[pl.ds(row0, cs), :], buf_ref.at[slot], sem_ref.at[slot])

    o_ref[...] = jnp.zeros(o_ref.shape, F32)

    for d in range(n_buf):

        @pl.when(d < n)
        def _():
            chunk_copy(d, d).start()

    def body(k, carry):
        slot = k % n_buf
        chunk_copy(k, slot).wait()
        e = eid_ref[base + k]
        lane = lax.broadcasted_iota(jnp.int32, pos_ref.shape, 1)
        pos_e = jnp.sum(jnp.where(lane == e, pos_ref[...], 0), axis=1, keepdims=True)
        lane_c = lax.broadcasted_iota(jnp.int32, comb_ref.shape, 1)
        g_e = jnp.sum(jnp.where(lane_c == e, comb_ref[...], 0.0), axis=1, keepdims=True)
        col = cid_ref[base + k] * cs + lax.broadcasted_iota(jnp.int32, (tm, cs), 1)
        onehot = jnp.where(pos_e == col, 1.0, 0.0).astype(BF16)
        o_ref[...] += g_e * _dot(onehot, buf_ref[slot])

        @pl.when(k + n_buf < n)
        def _():
            chunk_copy(k + n_buf, slot).start()

        return carry

    lax.fori_loop(0, n, body, 0)
    o_ref[...] = _ln_ple(x_ref[...], o_ref[...], ple_refs)


def _moe_combine(ys, pos, combine, n_need, cid, eid, x2d, ple_args, tm=MOE_TM, cs=MOE_CS):
    t = pos.shape[0]
    return pl.pallas_call(
        functools.partial(_moe_combine_kernel, tm=tm, cs=cs, n_slots=MOE_SLOTS, n_buf=MOE_RING),
        grid_spec=pltpu.PrefetchScalarGridSpec(
            num_scalar_prefetch=3,
            grid=(t // tm,),
            in_specs=[
                pl.BlockSpec(memory_space=pl.ANY),
                pl.BlockSpec((tm, N_EXPERTS), lambda i, *_: (i, 0)),
                pl.BlockSpec((tm, LANES), lambda i, *_: (i, 0)),
                pl.BlockSpec((tm, D_MODEL), lambda i, *_: (i, 0)),
            ] + _ple_specs(tm, ple_args),
            out_specs=pl.BlockSpec((tm, D_MODEL), lambda i, *_: (i, 0)),
            scratch_shapes=[pltpu.VMEM((MOE_RING, cs, D_MODEL), BF16), pltpu.SemaphoreType.DMA((MOE_RING,))],
        ),
        out_shape=jax.ShapeDtypeStruct((t, D_MODEL), F32),
        compiler_params=_cparams("arbitrary"),
        name="moe_combine",
    )(n_need, cid, eid, ys, pos, combine, x2d, *ple_args)


def _moe_plan(combine, tm=MOE_TM):
    t = combine.shape[0]
    n_tok_tiles = t // tm
    n_row_tiles = (2 * t) // tm + N_EXPERTS
    sel = (combine[:, :N_EXPERTS] != 0.0).astype(jnp.int32)
    csum = jnp.cumsum(sel, axis=0)
    count = csum[-1]
    ntile_e = (count + tm - 1) // tm
    tile_end = jnp.cumsum(ntile_e)
    gstart = (tile_end - ntile_e) * tm
    pos = jnp.where(sel != 0, gstart[None, :] + csum - sel, -1)
    rt = jnp.arange(n_row_tiles, dtype=jnp.int32)
    tile_e = jnp.minimum(jnp.sum(rt[:, None] >= tile_end[None, :], axis=1), N_EXPERTS - 1).astype(jnp.int32)
    valid = (rt < tile_end[-1]).astype(jnp.int32)
    cb = jnp.concatenate([jnp.zeros((1, N_EXPERTS), jnp.int32), csum[tm - 1::tm]], axis=0)
    off = rt * tm - gstart[tile_e]
    cb_t = cb[:, tile_e].T
    clo = jnp.sum(cb_t[:, 1:] <= off[:, None], axis=1).astype(jnp.int32)
    cend = jnp.sum(cb_t[:, :-1] < (off + tm)[:, None], axis=1).astype(jnp.int32)
    nch = jnp.where(valid != 0, jnp.maximum(cend - clo, 0), 0).astype(jnp.int32)
    clo = jnp.minimum(clo, n_tok_tiles - 1)
    first = gstart[None, :] + cb[:-1]
    n_ie = cb[1:] - cb[:-1]
    per_e = MOE_SLOTS // N_EXPERTS
    k = jnp.arange(per_e, dtype=jnp.int32)
    c_lo = first // MOE_CS
    c_hi = (first + n_ie - 1) // MOE_CS
    cand = c_lo[..., None] + k
    need = ((n_ie[..., None] > 0) & (cand <= c_hi[..., None])).reshape(n_tok_tiles, MOE_SLOTS)
    cand = cand.reshape(n_tok_tiles, MOE_SLOTS)
    eids = jnp.broadcast_to(jnp.arange(N_EXPERTS, dtype=jnp.int32)[:, None], (N_EXPERTS, per_e)).reshape(-1)
    order = jnp.argsort(jnp.logical_not(need), axis=1, stable=True)
    cid = jnp.where(need, cand, 0)
    cid = jnp.take_along_axis(cid, order, axis=1).reshape(-1).astype(jnp.int32)
    eid = eids[order].reshape(-1).astype(jnp.int32)
    n_need = jnp.sum(need, axis=1).astype(jnp.int32)
    return dict(pos=pos.astype(jnp.int32), tile_e=tile_e, valid=valid, clo=clo, nch=nch,
                n_need=n_need, cid=cid, eid=eid, n_rows=n_row_tiles * tm)


def _moe(xb, x2d, combine, w_gu, w_d, first_expert, ple_args):
    plan = _moe_plan(combine)
    xs = _moe_gather(xb, plan["pos"].T, plan["tile_e"], plan["clo"], plan["nch"], plan["n_rows"])
    ys = _ffn(xs, w_gu, w_d, plan["tile_e"] + first_expert, plan["valid"], BF16, tm=MOE_TM, fc=1792,
              name="ffn_moe")
    return _moe_combine(ys, plan["pos"], combine, plan["n_need"], plan["cid"], plan["eid"], x2d, ple_args)


def _ln_ple(x, h, ple_refs):
    g_ref, b_ref, p_ref, wg_ref, wp_ref = ple_refs
    y = _layer_norm(DEEPNORM_ALPHA * x + h, g_ref[...], b_ref[...])
    gate = jax.nn.sigmoid(_dot(y.astype(BF16), wg_ref[...]))
    proj = _dot(p_ref[...].astype(BF16), wp_ref[...])
    return y + gate * proj


def _ple_specs(tm, ple_args):
    g, b, p2d, wg, wp = ple_args
    ign = lambda f: (lambda *idx: f(idx[0]))
    full = lambda a: pl.BlockSpec(a.shape, ign(lambda i: (0, 0)))
    return [full(g), full(b), pl.BlockSpec((tm, D_PLE), ign(lambda i: (i, 0))), full(wg), full(wp)]


def _ffn_ple_kernel(xb_ref, wg_ref, wu_ref, wd_ref, x_ref, *rest):
    *ple_refs, o_ref, acc_ref = rest
    c = pl.program_id(1)
    xb = xb_ref[...]
    gate = _dot(xb, wg_ref[...])
    up = _dot(xb, wu_ref[...])
    h = gate * jax.nn.sigmoid(gate) * up
    contrib = _dot(h.astype(BF16), wd_ref[...])

    @pl.when(c == 0)
    def _():
        acc_ref[...] = contrib

    @pl.when(c != 0)
    def _():
        acc_ref[...] += contrib

    @pl.when(c == pl.num_programs(1) - 1)
    def _():
        o_ref[...] = _ln_ple(x_ref[...], acc_ref[...], ple_refs)


def _ffn_ple(xb, x2d, w_gu, w_d, ple_args, tm=512, fc=1408):
    t = xb.shape[0]
    f = w_d.shape[0]
    ncf = f // fc
    row = lambda n: pl.BlockSpec((tm, n), lambda i, c: (i, 0))
    return pl.pallas_call(
        _ffn_ple_kernel,
        grid=(t // tm, ncf),
        in_specs=[
            row(D_MODEL),
            pl.BlockSpec((D_MODEL, fc), lambda i, c: (0, c)),
            pl.BlockSpec((D_MODEL, fc), lambda i, c: (0, ncf + c)),
            pl.BlockSpec((fc, D_MODEL), lambda i, c: (c, 0)),
            row(D_MODEL),
        ] + _ple_specs(tm, ple_args),
        out_specs=row(D_MODEL),
        out_shape=jax.ShapeDtypeStruct((t, D_MODEL), F32),
        scratch_shapes=[pltpu.VMEM((tm, D_MODEL), F32)],
        compiler_params=_cparams("parallel", "arbitrary"),
        name="ffn_dense",
    )(xb, w_gu, w_gu, w_d, x2d, *ple_args)


def _even_mixer(x2d, b, s, w_in, b_f, w_out):
    nf, ns = FOX_W, SB_W
    o_f = 3 * nf
    o_s = o_f + N_HEADS_FOX
    qa, ka, va = w_in[:, :nf], w_in[:, nf:2 * nf], w_in[:, 2 * nf:o_f]
    qs, ks, vs = w_in[:, o_s:o_s + ns], w_in[:, o_s + ns:o_s + 2 * ns], w_in[:, o_s + 2 * ns:]
    w_k = jnp.concatenate([ka, ks], axis=1).astype(BF16)
    w_t = jnp.concatenate([qa * Q_SCALE, va, qs * Q_SCALE, vs], axis=1).T.astype(BF16)
    w_f = jnp.pad(w_in[:, o_f:o_s], ((0, 0), (0, LANES - N_HEADS_FOX))).astype(BF16)
    bias_f = jnp.pad(b_f, (0, LANES - N_HEADS_FOX)).reshape(1, LANES)
    k2d, ht, lf = _ab_proj(x2d, w_k, w_t, w_f, bias_f, b, s)
    lf_t = lf.reshape(b, s, LANES)[:, :, :N_HEADS_FOX].transpose(0, 2, 1)
    crow, caug = _cumsum_seq(lf_t)
    k3 = k2d.reshape(b, s, -1)
    oa = _fox_attention(k3, ht, caug, crow)
    ob = _sb_attention(k3, ht)
    w_out_b = w_out.astype(BF16)
    return [oa.reshape(b * s, nf), ob.reshape(b * s, ns)], [w_out_b[:nf], w_out_b[nf:]]


def _odd_mixer(x2d, b, s, w_qkv, sinks, w_out, tables):
    nq = N_HEADS_SWA * HEAD_DIM
    wq = w_qkv[:, :nq]
    wk = [w_qkv[:, nq + i * HEAD_DIM: nq + (i + 1) * HEAD_DIM] for i in range(N_KV_SWA)]
    wv = [w_qkv[:, nq + (N_KV_SWA + i) * HEAD_DIM: nq + (N_KV_SWA + i + 1) * HEAD_DIM] for i in range(N_KV_SWA)]
    dup = lambda ws: [w for w in ws for _ in range(2)]
    w = jnp.concatenate([wq] + dup(wk) + dup(wv), axis=1).astype(BF16)
    q2d, kv2d = _swa_proj(x2d, w, tables, s)
    o = _swa_attention(q2d.reshape(b, s, nq), kv2d.reshape(b, s, -1), sinks)
    return [o.reshape(b * s, nq)], [w_out.astype(BF16)]


def kernel(x, p, ln_mix_g, ln_mix_b, ln_ffn_g, ln_ffn_b, ab_w_in, ab_b_f, ab_w_out, c_w_qkv, c_sinks, c_w_out,
           ffn_w_gate_up, ffn_w_down, router_w, router_b, moe_w_gate_up, moe_w_down, ple_w_gate, ple_w_proj):
    b, s, d = x.shape
    t = b * s
    x2d = x.reshape(t, d)
    tables = _rope_tables(s)
    row = lambda v: v.reshape(1, -1)
    moe_gu = moe_w_gate_up.astype(BF16).reshape((-1,) + moe_w_gate_up.shape[2:])
    moe_d = moe_w_down.astype(BF16).reshape((-1,) + moe_w_down.shape[2:])
    for i in range(DEPTH):
        j = i // 2
        if i % 2 == 0:
            os_, ws = _even_mixer(x2d, b, s, ab_w_in[j], ab_b_f[j], ab_w_out[j])
        else:
            os_, ws = _odd_mixer(x2d, b, s, c_w_qkv[j], c_sinks[j], c_w_out[j], tables)
        x2d, xb = _out_proj_ln(os_, ws, x2d, row(ln_mix_g[i]), row(ln_mix_b[i]))
        ple_args = (row(ln_ffn_g[i]), row(ln_ffn_b[i]), p[i].reshape(t, D_PLE),
                    ple_w_gate[i].astype(BF16), ple_w_proj[i].astype(BF16))
        if i % 2 == 0:
            x2d = _ffn_ple(xb, x2d, ffn_w_gate_up[j].astype(BF16), ffn_w_down[j].astype(BF16), ple_args)
        else:
            rw = jnp.pad(router_w[j], ((0, 0), (0, LANES - N_EXPERTS)))
            rb = jnp.pad(router_b[j], (0, LANES - N_EXPERTS)).reshape(1, LANES)
            combine = _router(x2d, rw, rb)
            x2d = _moe(xb, x2d, combine, moe_gu, moe_d, j * N_EXPERTS, ple_args)
    return x2d.reshape(b, s, d)
```

```python
import functools
import math

import jax
import jax.numpy as jnp
from jax import lax
from jax.experimental import pallas as pl
from jax.experimental.pallas import tpu as pltpu

F32 = jnp.float32
BF16 = jnp.bfloat16

D_MODEL = 1024
HEAD_DIM = 64
LANES = 128
N_HEADS_FOX = 8
N_HEADS_SB = 8
N_HEADS_SWA = 16
N_KV_SWA = 2
WINDOW = 128
ROPE_THETA = 500000.0
ROPE_DIM = HEAD_DIM // 4
N_EXPERTS = 8
D_PLE = 256
LN_EPS = 1e-5
DEPTH = 4
DEEPNORM_ALPHA = (2.0 * DEPTH) ** 0.25
ATTN_SCALE = HEAD_DIM ** -0.5
FOX_W = N_HEADS_FOX * HEAD_DIM
SB_W = N_HEADS_SB * HEAD_DIM
VMEM_LIMIT = 56 * 1024 * 1024

NEG_INF = float("-inf")
LOG2E = math.log2(math.e)
Q_SCALE = ATTN_SCALE * LOG2E
SUM_ROWS = 16
SWA_HEAD_BATCH = 4


def _cparams(*sem):
    return pltpu.CompilerParams(dimension_semantics=sem, vmem_limit_bytes=VMEM_LIMIT)


def _dot(a, b):
    return jnp.dot(a, b, preferred_element_type=F32)


def _dot_nt(a, b):
    return lax.dot_general(a, b, (((1,), (1,)), ((), ())), preferred_element_type=F32)


def _log_sigmoid(x):
    return jnp.minimum(x, 0.0) - jnp.log1p(jnp.exp(-jnp.abs(x)))


def _head_masks(shape):
    lane = lax.broadcasted_iota(jnp.int32, shape, len(shape) - 1)
    return lane < HEAD_DIM, lane >= HEAD_DIM


def _ab_proj_kernel(x_ref, wk_ref, wt_ref, wf_ref, bf_ref, k_ref, ht_ref, lf_ref):
    xb = x_ref[...].astype(BF16)
    k_ref[...] = _dot(xb, wk_ref[...]).astype(BF16)
    ht_ref[0] = _dot_nt(wt_ref[...], xb).astype(BF16)
    lf_ref[...] = _log_sigmoid(_dot(xb, wf_ref[...]) + bf_ref[...])


def _ab_proj(x2d, wk, wt, wf, bf, b, s, tm=512):
    t = x2d.shape[0]
    nk = wk.shape[1]
    nt = wt.shape[0]
    spt = s // tm
    return pl.pallas_call(
        _ab_proj_kernel,
        grid=(t // tm,),
        in_specs=[
            pl.BlockSpec((tm, D_MODEL), lambda i: (i, 0)),
            pl.BlockSpec((D_MODEL, nk), lambda i: (0, 0)),
            pl.BlockSpec((nt, D_MODEL), lambda i: (0, 0)),
            pl.BlockSpec((D_MODEL, LANES), lambda i: (0, 0)),
            pl.BlockSpec((1, LANES), lambda i: (0, 0)),
        ],
        out_specs=[
            pl.BlockSpec((tm, nk), lambda i: (i, 0)),
            pl.BlockSpec((1, nt, tm), lambda i: (i // spt, 0, i % spt)),
            pl.BlockSpec((tm, LANES), lambda i: (i, 0)),
        ],
        out_shape=[
            jax.ShapeDtypeStruct((t, nk), BF16),
            jax.ShapeDtypeStruct((b, nt, s), BF16),
            jax.ShapeDtypeStruct((t, LANES), F32),
        ],
        compiler_params=_cparams("parallel"),
        name="ab_proj",
    )(x2d, wk, wt, wf, bf)


def _split3(v):
    hi = v.astype(BF16).astype(F32)
    r1 = v - hi
    mid = r1.astype(BF16).astype(F32)
    return hi, mid, r1 - mid


def _cumsum_kernel(x_ref, c_ref, aug_ref, st_ref):
    x = x_ref[0]
    s = x.shape[1]
    lane = lax.broadcasted_iota(jnp.int32, x.shape, 1) & (LANES - 1)
    sh = 1
    while sh < LANES:
        x = x + jnp.where(lane >= sh, pltpu.roll(x, sh, axis=1), 0.0)
        sh *= 2
    carry = jnp.zeros((x.shape[0], 1), F32)
    for c in range(s // LANES):
        blk = x[:, c * LANES:(c + 1) * LANES] + carry
        c_ref[0, :, c * LANES:(c + 1) * LANES] = blk
        carry = blk[:, LANES - 1:LANES]

    ones = jnp.ones((3, s), F32)
    st_ref[...] = jnp.zeros(st_ref.shape, F32)
    for p in range(x.shape[0] // 2):
        for which, base in ((0, HEAD_DIM), (1, 0)):
            negc = c_ref[0, 2 * p + which:2 * p + which + 1, :] * -LOG2E
            hi, mid, lo = _split3(negc)
            st_ref[base:base + 1, :] = hi
            st_ref[base + 1:base + 2, :] = mid
            st_ref[base + 2:base + 3, :] = lo
            st_ref[base + 3:base + 6, :] = ones
        for c in range(s // LANES):
            blk = st_ref[:, c * LANES:(c + 1) * LANES]
            aug_ref[0, p, c * LANES:(c + 1) * LANES, :] = blk.T.astype(BF16)


def _cumsum_seq(lf_t):
    b, h, s = lf_t.shape
    return pl.pallas_call(
        _cumsum_kernel,
        grid=(b,),
        in_specs=[pl.BlockSpec((1, h, s), lambda i: (i, 0, 0))],
        out_specs=[
            pl.BlockSpec((1, h, s), lambda i: (i, 0, 0)),
            pl.BlockSpec((1, h // 2, s, LANES), lambda i: (i, 0, 0, 0)),
        ],
        out_shape=[
            jax.ShapeDtypeStruct((b, h, s), F32),
            jax.ShapeDtypeStruct((b, h // 2, s, LANES), BF16),
        ],
        scratch_shapes=[pltpu.VMEM((LANES, s), F32)],
        compiler_params=_cparams("parallel"),
        name="fox_cumsum",
    )(lf_t)


def _fox_kernel(qt_ref, k_ref, aug_ref, vt_ref, c_ref, o_ref, m_ref, acc_ref, sa_ref, sb_ref, *, tq, tk):
    hp = pl.program_id(1)
    qi = pl.program_id(2)
    q0 = qi * tq
    qt = qt_ref[0].astype(F32)
    sub = lax.broadcasted_iota(jnp.int32, (LANES, tq), 0)
    lane = lax.broadcasted_iota(jnp.int32, (tk, LANES), 1)
    vsub = lax.broadcasted_iota(jnp.int32, (LANES, tk), 0)
    own_rows = [sub < HEAD_DIM, sub >= HEAD_DIM]
    own_lanes = [lane < HEAD_DIM, lane >= HEAD_DIM]
    own_vrows = [vsub < HEAD_DIM, vsub >= HEAD_DIM]

    qaug = []
    for r in range(2):
        base = HEAD_DIM * (1 - r)
        ci = c_ref[0, pl.ds(2 * hp + r, 1), :]
        hi, mid, lo = _split3(ci * LOG2E)
        t = jnp.where(own_rows[r], qt, 0.0)
        t = jnp.where((sub >= base) & (sub < base + 3), 1.0, t)
        t = jnp.where(sub == base + 3, hi, t)
        t = jnp.where(sub == base + 4, mid, t)
        t = jnp.where(sub == base + 5, lo, t)
        qaug.append(t.astype(BF16))

    m_ref[...] = jnp.full(m_ref.shape, NEG_INF, F32)
    acc_ref[...] = jnp.zeros(acc_ref.shape, F32)

    def scores(k0, s_ref):
        kt = k_ref[0, pl.ds(k0, tk), :]
        ca = aug_ref[0, 0, pl.ds(k0, tk), :]
        for r in range(2):
            ka = jnp.where(own_lanes[r], kt, ca)
            s_ref[r] = _dot(ka, qaug[r])

    def softmax_pv(k0, s_ref, masked):
        vt = vt_ref[0, :, pl.ds(k0, tk)]
        if masked:
            key = k0 + lax.broadcasted_iota(jnp.int32, (tk, tq), 0)
            qry = q0 + lax.broadcasted_iota(jnp.int32, (tk, tq), 1)
            keep = key <= qry
        for r in range(2):
            st = s_ref[r]
            if masked:
                st = jnp.where(keep, st, NEG_INF)
            m_prev = m_ref[r]
            m_new = jnp.maximum(m_prev, jnp.max(st, axis=0, keepdims=True))
            alpha = jnp.exp2(m_prev - m_new)
            pt = jnp.exp2(st - m_new).astype(BF16)
            va = jnp.where(own_vrows[r], vt, 1.0)
            acc_ref[r] = alpha * acc_ref[r] + _dot(va, pt)
            m_ref[r] = m_new

    tile = lambda j: pl.multiple_of(j * tk, tk)
    scores(tile(0), sa_ref)

    def pair(i):
        scores(tile(2 * i + 1), sb_ref)
        softmax_pv(tile(2 * i), sa_ref, False)
        scores(tile(2 * i + 2), sa_ref)
        softmax_pv(tile(2 * i + 1), sb_ref, False)

    def body(j, carry):
        pair(2 * j)
        pair(2 * j + 1)
        return carry

    lax.fori_loop(0, qi // 2, body, 0)

    @pl.when(qi % 2 == 1)
    def _():
        pair(qi - 1)

    scores(tile(2 * qi + 1), sb_ref)
    softmax_pv(tile(2 * qi), sa_ref, True)
    softmax_pv(tile(2 * qi + 1), sb_ref, True)

    ot = jnp.zeros((LANES, tq), F32)
    for r in range(2):
        a = acc_ref[r]
        den = a[HEAD_DIM * (1 - r):HEAD_DIM * (1 - r) + 1, :]
        ot = ot + jnp.where(own_rows[r], a / den, 0.0)
    o_ref[0] = ot.T.astype(BF16)


def _fox_attention(k3, ht, caug, crow, tq=512):
    b, s, _ = k3.shape
    nhp = N_HEADS_FOX // 2
    tk = tq // 2
    kern = functools.partial(_fox_kernel, tq=tq, tk=tk)
    return pl.pallas_call(
        kern,
        grid=(b, nhp, s // tq),
        in_specs=[
            pl.BlockSpec((1, LANES, tq), lambda bi, hp, qi: (bi, hp, qi)),
            pl.BlockSpec((1, s, LANES), lambda bi, hp, qi: (bi, 0, hp)),
            pl.BlockSpec((1, 1, s, LANES), lambda bi, hp, qi: (bi, hp, 0, 0)),
            pl.BlockSpec((1, LANES, s), lambda bi, hp, qi: (bi, nhp + hp, 0)),
            pl.BlockSpec((1, N_HEADS_FOX, tq), lambda bi, hp, qi: (bi, 0, qi)),
        ],
        out_specs=pl.BlockSpec((1, tq, LANES), lambda bi, hp, qi: (bi, qi, hp)),
        out_shape=jax.ShapeDtypeStruct((b, s, FOX_W), BF16),
        scratch_shapes=[
            pltpu.VMEM((2, 1, tq), F32),
            pltpu.VMEM((2, LANES, tq), F32),
            pltpu.VMEM((2, tk, tq), F32),
            pltpu.VMEM((2, tk, tq), F32),
        ],
        compiler_params=_cparams("parallel", "parallel", "arbitrary"),
        name="fox_attn",
    )(ht, k3, caug, ht, crow)


def _sb_kernel(qt_ref, k_ref, vt_ref, o_ref, run_ref, acc_ref, za_ref, zb_ref, *, tq, cw):
    qi = pl.program_id(2)
    q0 = qi * tq
    qt = qt_ref[0]
    sub = lax.broadcasted_iota(jnp.int32, (LANES, tq), 0)
    own_rows = [sub < HEAD_DIM, sub >= HEAD_DIM]
    qm = [jnp.where(mk, qt, 0.0) for mk in own_rows]

    ss = lax.broadcasted_iota(jnp.int32, (cw, cw), 0)
    jj = lax.broadcasted_iota(jnp.int32, (cw, cw), 1)
    tri = jnp.where(jj > ss, 1.0, 0.0).astype(BF16)
    tri = jnp.concatenate([tri, jnp.ones((SUM_ROWS, cw), BF16)], axis=0)

    run_ref[...] = jnp.zeros(run_ref.shape, F32)
    acc_ref[...] = jnp.zeros(acc_ref.shape, F32)

    def start_of(c):
        return pl.multiple_of(jnp.maximum(q0 + tq - (c + 1) * cw, 0), cw)

    def logits(c, z_ref):
        kt = k_ref[0, pl.ds(start_of(c), cw), :]
        for r in range(2):
            z_ref[r] = _dot(kt, qm[r])

    def weights_pv(c, z_ref, masked):
        k0 = start_of(c)
        vt = vt_ref[0, :, pl.ds(k0, cw)]
        if masked:
            key = k0 + lax.broadcasted_iota(jnp.int32, (cw, tq), 0)
            qry = q0 + lax.broadcasted_iota(jnp.int32, (cw, tq), 1)
            keep = key < qry
        zs, sps, laters, ws = [], [], [], []
        for r in range(2):
            z = z_ref[r]
            e = jnp.exp2(-jnp.abs(z))
            sp = jnp.maximum(z, 0.0) + jnp.log(1.0 + e) * LOG2E
            if masked:
                sp = jnp.where(keep, sp, 0.0)
            zs.append(z)
            sps.append(sp)
        for r in range(2):
            laters.append(_dot(tri, sps[r].astype(BF16)))
        for r in range(2):
            run = run_ref[r]
            a = jnp.exp2((zs[r] - sps[r]) - (laters[r][:cw] + run))
            if masked:
                a = jnp.where(keep, a, 0.0)
            ws.append(a.astype(BF16))
            run_ref[r] = run + laters[r][cw:cw + 1]
        for r in range(2):
            acc_ref[r] = acc_ref[r] + _dot(vt, ws[r])

    logits(0, za_ref)
    logits(1, zb_ref)
    weights_pv(0, za_ref, True)
    logits(2, za_ref)
    weights_pv(1, zb_ref, True)

    def body(i, carry):
        logits(2 * i + 1, zb_ref)
        weights_pv(2 * i, za_ref, False)
        logits(2 * i + 2, za_ref)
        weights_pv(2 * i + 1, zb_ref, False)
        return carry

    lax.fori_loop(1, qi + 1, body, 0)

    ot = jnp.where(own_rows[0], acc_ref[0], acc_ref[1])
    o_ref[0] = ot.T.astype(BF16)


def _sb_attention(k3, ht, tq=512):
    b, s, _ = k3.shape
    cw = tq // 2
    nhp = N_HEADS_SB // 2
    kern = functools.partial(_sb_kernel, tq=tq, cw=cw)
    return pl.pallas_call(
        kern,
        grid=(b, nhp, s // tq),
        in_specs=[
            pl.BlockSpec((1, LANES, tq), lambda bi, hp, qi: (bi, 2 * nhp + hp, qi)),
            pl.BlockSpec((1, s, LANES), lambda bi, hp, qi: (bi, 0, nhp + hp)),
            pl.BlockSpec((1, LANES, s), lambda bi, hp, qi: (bi, 3 * nhp + hp, 0)),
        ],
        out_specs=pl.BlockSpec((1, tq, LANES), lambda bi, hp, qi: (bi, qi, hp)),
        out_shape=jax.ShapeDtypeStruct((b, s, SB_W), BF16),
        scratch_shapes=[
            pltpu.VMEM((2, 1, tq), F32),
            pltpu.VMEM((2, LANES, tq), F32),
            pltpu.VMEM((2, cw, tq), F32),
            pltpu.VMEM((2, cw, tq), F32),
        ],
        compiler_params=_cparams("parallel", "parallel", "arbitrary"),
        name="sb_attn",
    )(ht, k3, ht)


def _swa_proj_kernel(x_ref, wk_ref, wt_ref, cos_ref, sa_ref, sb_ref, cost_ref, sint_ref, k_ref, qt_ref, vt_ref):
    xb = x_ref[...].astype(BF16)
    half = ROPE_DIM // 2
    yk = _dot(xb, wk_ref[...])
    for c in range(yk.shape[1] // LANES):
        blk = yk[:, c * LANES:(c + 1) * LANES]
        blk = (blk * cos_ref[...] + pltpu.roll(blk, half, axis=1) * sa_ref[...]
               + pltpu.roll(blk, LANES - half, axis=1) * sb_ref[...])
        k_ref[:, c * LANES:(c + 1) * LANES] = blk.astype(BF16)
    yt = _dot_nt(wt_ref[...], xb)
    cos = cost_ref[...]
    sin = sint_ref[...]
    nq = N_HEADS_SWA * HEAD_DIM
    for h in range(N_HEADS_SWA):
        base = h * HEAD_DIM
        x1 = yt[base:base + half]
        x2 = yt[base + half:base + ROPE_DIM]
        rot = jnp.concatenate([x1 * cos - x2 * sin, x2 * cos + x1 * sin], axis=0)
        qt_ref[0, base:base + ROPE_DIM, :] = rot.astype(BF16)
        qt_ref[0, base + ROPE_DIM:base + HEAD_DIM, :] = yt[base + ROPE_DIM:base + HEAD_DIM].astype(BF16)
    vt_ref[0] = yt[nq:].astype(BF16)


def _rope_tables(s):
    half = ROPE_DIM // 2
    inv = ROPE_THETA ** (-jnp.arange(half, dtype=F32) * 2.0 / ROPE_DIM)
    ang = jnp.arange(s, dtype=F32)[:, None] * inv[None, :]
    cos, sin = jnp.cos(ang), jnp.sin(ang)
    d = jnp.arange(LANES) % HEAD_DIM
    idx = d % half
    cos_t = jnp.where(d[None, :] < ROPE_DIM, cos[:, idx], 1.0)
    sin_a = jnp.where((d[None, :] >= half) & (d[None, :] < ROPE_DIM), sin[:, idx], 0.0)
    sin_b = jnp.where(d[None, :] < half, -sin[:, idx], 0.0)
    return cos_t.astype(F32), sin_a.astype(F32), sin_b.astype(F32), cos.T.astype(F32), sin.T.astype(F32)


def _swa_proj(x2d, wk, wt, tables, b, s, tm=512):
    t = x2d.shape[0]
    nk = wk.shape[1]
    nq = N_HEADS_SWA * HEAD_DIM
    nv = wt.shape[0] - nq
    spt = s // tm
    half = ROPE_DIM // 2
    lane_tab = pl.BlockSpec((tm, LANES), lambda i: (i % spt, 0))
    row_tab = pl.BlockSpec((half, tm), lambda i: (0, i % spt))
    return pl.pallas_call(
        _swa_proj_kernel,
        grid=(t // tm,),
        in_specs=[
            pl.BlockSpec((tm, D_MODEL), lambda i: (i, 0)),
            pl.BlockSpec(wk.shape, lambda i: (0, 0)),
            pl.BlockSpec(wt.shape, lambda i: (0, 0)),
            lane_tab, lane_tab, lane_tab, row_tab, row_tab,
        ],
        out_specs=[
            pl.BlockSpec((tm, nk), lambda i: (i, 0)),
            pl.BlockSpec((1, nq, tm), lambda i: (i // spt, 0, i % spt)),
            pl.BlockSpec((1, nv, tm), lambda i: (i // spt, 0, i % spt)),
        ],
        out_shape=[
            jax.ShapeDtypeStruct((t, nk), BF16),
            jax.ShapeDtypeStruct((b, nq, s), BF16),
            jax.ShapeDtypeStruct((b, nv, s), BF16),
        ],
        compiler_params=_cparams("parallel"),
        name="swa_proj",
    )(x2d, wk, wt, *tables)


def _swa_kernel(sink_ref, qt_ref, k_ref, vt_ref, o_ref, *, tq):
    qi = pl.program_id(1)
    q0 = qi * tq
    kw = tq + WINDOW
    kstart = pl.multiple_of(jnp.maximum(q0 - WINDOW, 0), WINDOW)
    kwin = k_ref[0, pl.ds(kstart, kw), :]
    vwin = vt_ref[0, :, pl.ds(kstart, kw)]
    key = kstart + lax.broadcasted_iota(jnp.int32, (kw, tq), 0)
    qry = q0 + lax.broadcasted_iota(jnp.int32, (kw, tq), 1)
    diff = qry - key
    keep = (diff >= 0) & (diff < WINDOW)
    sub = lax.broadcasted_iota(jnp.int32, (LANES, tq), 0)
    vsub = lax.broadcasted_iota(jnp.int32, (LANES, kw), 0)
    own_rows = [sub < HEAD_DIM, sub >= HEAD_DIM]
    own_vrows = [vsub < HEAD_DIM, vsub >= HEAD_DIM]
    group = N_HEADS_SWA // N_KV_SWA

    def scores(h):
        c, r = divmod(h, 2)
        g = h // group
        qblk = qt_ref[0, c * LANES:(c + 1) * LANES, :]
        return _dot(kwin[:, g * LANES:(g + 1) * LANES], jnp.where(own_rows[r], qblk, 0.0))

    def weights(h, st):
        st = jnp.where(keep, st, NEG_INF)
        sink = sink_ref[h] * LOG2E
        mx = jnp.maximum(jnp.max(st, axis=0, keepdims=True), sink)
        return jnp.exp2(st - mx).astype(BF16), jnp.exp2(sink - mx)

    def values(h, pt):
        r = h % 2
        g = h // group
        va = jnp.where(own_vrows[r], vwin[g * LANES:(g + 1) * LANES, :], 1.0)
        return _dot(va, pt)

    batches = [list(range(i, i + SWA_HEAD_BATCH)) for i in range(0, N_HEADS_SWA, SWA_HEAD_BATCH)]
    sts = [scores(h) for h in batches[0]]
    for bi, heads in enumerate(batches):
        nxt = [scores(h) for h in batches[bi + 1]] if bi + 1 < len(batches) else None
        pts = [weights(h, st) for h, st in zip(heads, sts)]
        accs = [values(h, pt) for h, (pt, _) in zip(heads, pts)]
        outs = []
        for h, acc, (_, esink) in zip(heads, accs, pts):
            r = h % 2
            den = acc[HEAD_DIM * (1 - r):HEAD_DIM * (1 - r) + 1, :] + esink
            outs.append(acc / den)
            if r == 1:
                c = h // 2
                ot = jnp.where(own_rows[0], outs[-2], outs[-1])
                o_ref[0, :, c * LANES:(c + 1) * LANES] = ot.T.astype(BF16)
        sts = nxt


def _swa_attention(qt3, k3, vt3, sinks, tq=256):
    b, nq, s = qt3.shape
    kern = functools.partial(_swa_kernel, tq=tq)
    return pl.pallas_call(
        kern,
        grid=(b, s // tq),
        in_specs=[
            pl.BlockSpec(memory_space=pltpu.SMEM),
            pl.BlockSpec((1, nq, tq), lambda bi, qi: (bi, 0, qi)),
            pl.BlockSpec((1, s, k3.shape[2]), lambda bi, qi: (bi, 0, 0)),
            pl.BlockSpec((1, vt3.shape[1], s), lambda bi, qi: (bi, 0, 0)),
        ],
        out_specs=pl.BlockSpec((1, tq, nq), lambda bi, qi: (bi, qi, 0)),
        out_shape=jax.ShapeDtypeStruct((b, s, nq), BF16),
        compiler_params=_cparams("parallel", "arbitrary"),
        name="swa_attn",
    )(sinks, qt3, k3, vt3)


def _layer_norm(z, g, b):
    mu = jnp.mean(z, axis=-1, keepdims=True)
    zc = z - mu
    var = jnp.mean(zc * zc, axis=-1, keepdims=True)
    return zc * lax.rsqrt(var + LN_EPS) * g + b


def _outln_kernel(*refs, n_in):
    o_refs = refs[:n_in]
    w_refs = refs[n_in:2 * n_in]
    x_ref, g_ref, b_ref, y_ref, yb_ref = refs[2 * n_in:]
    h = _dot(o_refs[0][...], w_refs[0][...])
    for i in range(1, n_in):
        h = h + _dot(o_refs[i][...], w_refs[i][...])
    y = _layer_norm(DEEPNORM_ALPHA * x_ref[...] + h, g_ref[...], b_ref[...])
    y_ref[...] = y
    yb_ref[...] = y.astype(BF16)


def _out_proj_ln(os_, ws, x2d, g, b, tm=512):
    t = x2d.shape[0]
    n_in = len(os_)
    in_specs = [pl.BlockSpec((tm, o.shape[1]), lambda i: (i, 0)) for o in os_]
    in_specs += [pl.BlockSpec(w.shape, lambda i: (0, 0)) for w in ws]
    in_specs += [
        pl.BlockSpec((tm, D_MODEL), lambda i: (i, 0)),
        pl.BlockSpec((1, D_MODEL), lambda i: (0, 0)),
        pl.BlockSpec((1, D_MODEL), lambda i: (0, 0)),
    ]
    return pl.pallas_call(
        functools.partial(_outln_kernel, n_in=n_in),
        grid=(t // tm,),
        in_specs=in_specs,
        out_specs=[pl.BlockSpec((tm, D_MODEL), lambda i: (i, 0))] * 2,
        out_shape=[jax.ShapeDtypeStruct((t, D_MODEL), F32), jax.ShapeDtypeStruct((t, D_MODEL), BF16)],
        compiler_params=_cparams("parallel"),
        name="out_proj_ln",
    )(*os_, *ws, x2d, g, b)


def _router_kernel(x_ref, w_ref, b_ref, c_ref):
    x = x_ref[...]
    w = w_ref[...]
    xh = x.astype(BF16)
    xl = (x - xh.astype(F32)).astype(BF16)
    wh = w.astype(BF16)
    wl = (w - wh.astype(F32)).astype(BF16)
    logits = _dot(xh, wh) + (_dot(xl, wh) + _dot(xh, wl)) + b_ref[...]
    lane = lax.broadcasted_iota(jnp.int32, logits.shape, 1)
    logits = jnp.where(lane < N_EXPERTS, logits, NEG_INF)
    m1 = jnp.max(logits, axis=1, keepdims=True)
    i1 = jnp.min(jnp.where(logits == m1, lane, LANES), axis=1, keepdims=True)
    rest = jnp.where(lane == i1, NEG_INF, logits)
    m2 = jnp.max(rest, axis=1, keepdims=True)
    i2 = jnp.min(jnp.where(rest == m2, lane, LANES), axis=1, keepdims=True)
    e2 = jnp.exp(m2 - m1)
    den = 1.0 + e2
    c_ref[...] = jnp.where(lane == i1, 1.0 / den, 0.0) + jnp.where(lane == i2, e2 / den, 0.0)


def _router(x2d, w_pad, b_pad, tm=512):
    t = x2d.shape[0]
    return pl.pallas_call(
        _router_kernel,
        grid=(t // tm,),
        in_specs=[
            pl.BlockSpec((tm, D_MODEL), lambda i: (i, 0)),
            pl.BlockSpec((D_MODEL, LANES), lambda i: (0, 0)),
            pl.BlockSpec((1, LANES), lambda i: (0, 0)),
        ],
        out_specs=pl.BlockSpec((tm, LANES), lambda i: (i, 0)),
        out_shape=jax.ShapeDtypeStruct((t, LANES), F32),
        compiler_params=_cparams("parallel"),
        name="router",
    )(x2d, w_pad, b_pad)


def _ffn_kernel(eid_ref, valid_ref, x_ref, wg_ref, wu_ref, wd_ref, o_ref, acc_ref):
    i = pl.program_id(0)
    c = pl.program_id(1)
    last = c == pl.num_programs(1) - 1
    valid = valid_ref[i] != 0

    @pl.when(valid)
    def _():
        x = x_ref[...]
        gate = _dot(x, wg_ref[0])
        up = _dot(x, wu_ref[0])
        h = gate * jax.nn.sigmoid(gate) * up
        contrib = _dot(h.astype(BF16), wd_ref[0])

        @pl.when(c == 0)
        def _():
            acc_ref[...] = contrib

        @pl.when(c != 0)
        def _():
            acc_ref[...] += contrib

        @pl.when(last)
        def _():
            o_ref[...] = acc_ref[...].astype(o_ref.dtype)

    @pl.when(jnp.logical_not(valid) & last)
    def _():
        o_ref[...] = jnp.zeros(o_ref.shape, o_ref.dtype)


def _ffn(xb, w_gu, w_d, eids, valid, out_dtype, tm, fc, name):
    r = xb.shape[0]
    f = w_d.shape[1]
    ncf = f // fc
    in_specs = [
        pl.BlockSpec((tm, D_MODEL), lambda i, c, eid, ok: (i, 0)),
        pl.BlockSpec((1, D_MODEL, fc), lambda i, c, eid, ok: (eid[i], 0, c * ok[i])),
        pl.BlockSpec((1, D_MODEL, fc), lambda i, c, eid, ok: (eid[i], 0, ncf + c * ok[i])),
        pl.BlockSpec((1, fc, D_MODEL), lambda i, c, eid, ok: (eid[i], c * ok[i], 0)),
    ]
    return pl.pallas_call(
        _ffn_kernel,
        grid_spec=pltpu.PrefetchScalarGridSpec(
            num_scalar_prefetch=2,
            grid=(r // tm, ncf),
            in_specs=in_specs,
            out_specs=pl.BlockSpec((tm, D_MODEL), lambda i, c, eid, ok: (i, 0)),
            scratch_shapes=[pltpu.VMEM((tm, D_MODEL), F32)],
        ),
        out_shape=jax.ShapeDtypeStruct((r, D_MODEL), out_dtype),
        compiler_params=_cparams("parallel", "arbitrary"),
        name=name,
    )(eids, valid, xb, w_gu, w_gu, w_d)


MOE_TM = 512
MOE_SR = 128


def _moe_gather_kernel(te_ref, clo_ref, nch_ref, s0_ref, ns_ref, x_ref, pos_ref, o_ref, acc_ref, *, tm, sr, n_chunks):
    r = pl.program_id(0)
    e = te_ref[r]
    sub_row = lax.broadcasted_iota(jnp.int32, (sr, tm), 0)
    acc_ref[...] = jnp.zeros(acc_ref.shape, F32)

    def body(i, carry):
        c = clo_ref[r] + i
        t0 = pl.multiple_of(c * tm, tm)
        p = pos_ref[pl.ds(e, 1), pl.ds(t0, tm)]

        def sub(j, carry2):
            r0 = pl.multiple_of((s0_ref[r * n_chunks + c] + j) * sr, sr)
            onehot = jnp.where(p == r * tm + r0 + sub_row, 1.0, 0.0).astype(BF16)
            acc_ref[pl.ds(r0, sr), :] += _dot(onehot, x_ref[pl.ds(t0, tm), :])
            return carry2

        lax.fori_loop(0, ns_ref[r * n_chunks + c], sub, 0)
        return carry

    lax.fori_loop(0, nch_ref[r], body, 0)
    o_ref[...] = acc_ref[...].astype(BF16)


def _moe_gather(xb, pos_t, tile_e, clo, nch, sub0, nsub, n_rows, tm=MOE_TM, sr=MOE_SR):
    t = xb.shape[0]
    return pl.pallas_call(
        functools.partial(_moe_gather_kernel, tm=tm, sr=sr, n_chunks=t // tm),
        grid_spec=pltpu.PrefetchScalarGridSpec(
            num_scalar_prefetch=5,
            grid=(n_rows // tm,),
            in_specs=[
                pl.BlockSpec((t, D_MODEL), lambda r, *_: (0, 0), pipeline_mode=pl.Buffered(1)),
                pl.BlockSpec(pos_t.shape, lambda r, *_: (0, 0)),
            ],
            out_specs=pl.BlockSpec((tm, D_MODEL), lambda r, *_: (r, 0)),
            scratch_shapes=[pltpu.VMEM((tm, D_MODEL), F32)],
        ),
        out_shape=jax.ShapeDtypeStruct((n_rows, D_MODEL), BF16),
        compiler_params=_cparams("arbitrary"),
        name="moe_gather",
    )(tile_e, clo, nch, sub0, nsub, xb, pos_t)


MOE_CS = 256
MOE_SLOTS = N_EXPERTS * (MOE_TM // MOE_CS + 1)
MOE_RING = 4


def _moe_combine_kernel(n_ref, cid_ref, eid_ref, ys_hbm, pos_ref, comb_ref, x_ref, *rest, tm, cs, n_slots, n_buf):
    *ple_refs, o_ref, buf_ref, sem_ref = rest
    i = pl.program_id(0)
    n = n_ref[i]
    base = i * n_slots

    def chunk_copy(k, slot):
        row0 = pl.multiple_of(cid_ref[base + k] * cs, cs)
        return pltpu.make_async_copy(ys_hbm.at[pl.ds(row0, cs), :], buf_ref.at[slot], sem_ref.at[slot])

    o_ref[...] = jnp.zeros(o_ref.shape, F32)

    for d in range(n_buf):

        @pl.when(d < n)
        def _():
            chunk_copy(d, d).start()

    def body(k, carry):
        slot = k % n_buf
        chunk_copy(k, slot).wait()
        e = eid_ref[base + k]
        lane = lax.broadcasted_iota(jnp.int32, pos_ref.shape, 1)
        pos_e = jnp.sum(jnp.where(lane == e, pos_ref[...], 0), axis=1, keepdims=True)
        lane_c = lax.broadcasted_iota(jnp.int32, comb_ref.shape, 1)
        g_e = jnp.sum(jnp.where(lane_c == e, comb_ref[...], 0.0), axis=1, keepdims=True)
        col = cid_ref[base + k] * cs + lax.broadcasted_iota(jnp.int32, (tm, cs), 1)
        onehot = jnp.where(pos_e == col, 1.0, 0.0).astype(BF16)
        o_ref[...] += g_e * _dot(onehot, buf_ref[slot])

        @pl.when(k + n_buf < n)
        def _():
            chunk_copy(k + n_buf, slot).start()

        return carry

    lax.fori_loop(0, n, body, 0)
    o_ref[...] = _ln_ple(x_ref[...], o_ref[...], ple_refs)


def _moe_combine(ys, pos, combine, n_need, cid, eid, x2d, ple_args, tm=MOE_TM, cs=MOE_CS):
    t = pos.shape[0]
    return pl.pallas_call(
        functools.partial(_moe_combine_kernel, tm=tm, cs=cs, n_slots=MOE_SLOTS, n_buf=MOE_RING),
        grid_spec=pltpu.PrefetchScalarGridSpec(
            num_scalar_prefetch=3,
            grid=(t // tm,),
            in_specs=[
                pl.BlockSpec(memory_space=pl.ANY),
                pl.BlockSpec((tm, N_EXPERTS), lambda i, *_: (i, 0)),
                pl.BlockSpec((tm, LANES), lambda i, *_: (i, 0)),
                pl.BlockSpec((tm, D_MODEL), lambda i, *_: (i, 0)),
            ] + _ple_specs(tm, ple_args),
            out_specs=pl.BlockSpec((tm, D_MODEL), lambda i, *_: (i, 0)),
            scratch_shapes=[pltpu.VMEM((MOE_RING, cs, D_MODEL), BF16), pltpu.SemaphoreType.DMA((MOE_RING,))],
        ),
        out_shape=jax.ShapeDtypeStruct((t, D_MODEL), F32),
        compiler_params=_cparams("arbitrary"),
        name="moe_combine",
    )(n_need, cid, eid, ys, pos, combine, x2d, *ple_args)


def _moe_plan(combine, tm=MOE_TM):
    t = combine.shape[0]
    n_tok_tiles = t // tm
    n_row_tiles = (2 * t) // tm + N_EXPERTS
    sel = (combine[:, :N_EXPERTS] != 0.0).astype(jnp.int32)
    csum = jnp.cumsum(sel, axis=0)
    count = csum[-1]
    ntile_e = (count + tm - 1) // tm
    tile_end = jnp.cumsum(ntile_e)
    gstart = (tile_end - ntile_e) * tm
    pos = jnp.where(sel != 0, gstart[None, :] + csum - sel, -1)
    rt = jnp.arange(n_row_tiles, dtype=jnp.int32)
    tile_e = jnp.minimum(jnp.sum(rt[:, None] >= tile_end[None, :], axis=1), N_EXPERTS - 1).astype(jnp.int32)
    valid = (rt < tile_end[-1]).astype(jnp.int32)
    cb = jnp.concatenate([jnp.zeros((1, N_EXPERTS), jnp.int32), csum[tm - 1::tm]], axis=0)
    off = rt * tm - gstart[tile_e]
    cb_t = cb[:, tile_e].T
    clo = jnp.sum(cb_t[:, 1:] <= off[:, None], axis=1).astype(jnp.int32)
    cend = jnp.sum(cb_t[:, :-1] < (off + tm)[:, None], axis=1).astype(jnp.int32)
    nch = jnp.where(valid != 0, jnp.maximum(cend - clo, 0), 0).astype(jnp.int32)
    clo = jnp.minimum(clo, n_tok_tiles - 1)
    r_lo = jnp.clip(cb_t[:, :-1] - off[:, None], 0, tm)
    r_hi = jnp.clip(cb_t[:, 1:] - off[:, None], 0, tm)
    sub0 = (r_lo // MOE_SR).astype(jnp.int32)
    nsub = jnp.where(r_hi > r_lo, (r_hi + MOE_SR - 1) // MOE_SR - sub0, 0).astype(jnp.int32)
    first = gstart[None, :] + cb[:-1]
    n_ie = cb[1:] - cb[:-1]
    per_e = MOE_SLOTS // N_EXPERTS
    k = jnp.arange(per_e, dtype=jnp.int32)
    c_lo = first // MOE_CS
    c_hi = (first + n_ie - 1) // MOE_CS
    cand = c_lo[..., None] + k
    need = ((n_ie[..., None] > 0) & (cand <= c_hi[..., None])).reshape(n_tok_tiles, MOE_SLOTS)
    cand = cand.reshape(n_tok_tiles, MOE_SLOTS)
    eids = jnp.broadcast_to(jnp.arange(N_EXPERTS, dtype=jnp.int32)[:, None], (N_EXPERTS, per_e)).reshape(-1)
    order = jnp.argsort(jnp.logical_not(need), axis=1, stable=True)
    cid = jnp.where(need, cand, 0)
    cid = jnp.take_along_axis(cid, order, axis=1).reshape(-1).astype(jnp.int32)
    eid = eids[order].reshape(-1).astype(jnp.int32)
    n_need = jnp.sum(need, axis=1).astype(jnp.int32)
    return dict(pos=pos.astype(jnp.int32), tile_e=tile_e, valid=valid, clo=clo, nch=nch,
                sub0=sub0.reshape(-1), nsub=nsub.reshape(-1),
                n_need=n_need, cid=cid, eid=eid, n_rows=n_row_tiles * tm)


def _moe(xb, x2d, combine, w_gu, w_d, first_expert, ple_args):
    plan = _moe_plan(combine)
    xs = _moe_gather(xb, plan["pos"].T, plan["tile_e"], plan["clo"], plan["nch"], plan["sub0"], plan["nsub"],
                     plan["n_rows"])
    ys = _ffn(xs, w_gu, w_d, plan["tile_e"] + first_expert, plan["valid"], BF16, tm=MOE_TM, fc=1792,
              name="ffn_moe")
    return _moe_combine(ys, plan["pos"], combine, plan["n_need"], plan["cid"], plan["eid"], x2d, ple_args)


def _ln_ple(x, h, ple_refs):
    g_ref, b_ref, p_ref, wg_ref, wp_ref = ple_refs
    y = _layer_norm(DEEPNORM_ALPHA * x + h, g_ref[...], b_ref[...])
    gate = jax.nn.sigmoid(_dot(y.astype(BF16), wg_ref[...]))
    proj = _dot(p_ref[...].astype(BF16), wp_ref[...])
    return y + gate * proj


def _ple_specs(tm, ple_args):
    g, b, p2d, wg, wp = ple_args
    ign = lambda f: (lambda *idx: f(idx[0]))
    full = lambda a: pl.BlockSpec(a.shape, ign(lambda i: (0, 0)))
    return [full(g), full(b), pl.BlockSpec((tm, D_PLE), ign(lambda i: (i, 0))), full(wg), full(wp)]


def _ffn_ple_kernel(xb_ref, wg_ref, wu_ref, wd_ref, x_ref, *rest):
    *ple_refs, o_ref, acc_ref = rest
    c = pl.program_id(1)
    xb = xb_ref[...]
    gate = _dot(xb, wg_ref[...])
    up = _dot(xb, wu_ref[...])
    h = gate * jax.nn.sigmoid(gate) * up
    contrib = _dot(h.astype(BF16), wd_ref[...])

    @pl.when(c == 0)
    def _():
        acc_ref[...] = contrib

    @pl.when(c != 0)
    def _():
        acc_ref[...] += contrib

    @pl.when(c == pl.num_programs(1) - 1)
    def _():
        o_ref[...] = _ln_ple(x_ref[...], acc_ref[...], ple_refs)


def _ffn_ple(xb, x2d, w_gu, w_d, ple_args, tm=512, fc=1408):
    t = xb.shape[0]
    f = w_d.shape[0]
    ncf = f // fc
    row = lambda n: pl.BlockSpec((tm, n), lambda i, c: (i, 0))
    return pl.pallas_call(
        _ffn_ple_kernel,
        grid=(t // tm, ncf),
        in_specs=[
            row(D_MODEL),
            pl.BlockSpec((D_MODEL, fc), lambda i, c: (0, c)),
            pl.BlockSpec((D_MODEL, fc), lambda i, c: (0, ncf + c)),
            pl.BlockSpec((fc, D_MODEL), lambda i, c: (c, 0)),
            row(D_MODEL),
        ] + _ple_specs(tm, ple_args),
        out_specs=row(D_MODEL),
        out_shape=jax.ShapeDtypeStruct((t, D_MODEL), F32),
        scratch_shapes=[pltpu.VMEM((tm, D_MODEL), F32)],
        compiler_params=_cparams("parallel", "arbitrary"),
        name="ffn_dense",
    )(xb, w_gu, w_gu, w_d, x2d, *ple_args)


def _even_mixer(x2d, b, s, w_in, b_f, w_out):
    nf, ns = FOX_W, SB_W
    o_f = 3 * nf
    o_s = o_f + N_HEADS_FOX
    qa, ka, va = w_in[:, :nf], w_in[:, nf:2 * nf], w_in[:, 2 * nf:o_f]
    qs, ks, vs = w_in[:, o_s:o_s + ns], w_in[:, o_s + ns:o_s + 2 * ns], w_in[:, o_s + 2 * ns:]
    w_k = jnp.concatenate([ka, ks], axis=1).astype(BF16)
    w_t = jnp.concatenate([qa * Q_SCALE, va, qs * Q_SCALE, vs], axis=1).T.astype(BF16)
    w_f = jnp.pad(w_in[:, o_f:o_s], ((0, 0), (0, LANES - N_HEADS_FOX))).astype(BF16)
    bias_f = jnp.pad(b_f, (0, LANES - N_HEADS_FOX)).reshape(1, LANES)
    k2d, ht, lf = _ab_proj(x2d, w_k, w_t, w_f, bias_f, b, s)
    lf_t = lf.reshape(b, s, LANES)[:, :, :N_HEADS_FOX].transpose(0, 2, 1)
    crow, caug = _cumsum_seq(lf_t)
    k3 = k2d.reshape(b, s, -1)
    oa = _fox_attention(k3, ht, caug, crow)
    ob = _sb_attention(k3, ht)
    w_out_b = w_out.astype(BF16)
    return [oa.reshape(b * s, nf), ob.reshape(b * s, ns)], [w_out_b[:nf], w_out_b[nf:]]


def _odd_mixer(x2d, b, s, w_qkv, sinks, w_out, tables):
    nq = N_HEADS_SWA * HEAD_DIM
    wq = w_qkv[:, :nq]
    wk = [w_qkv[:, nq + i * HEAD_DIM: nq + (i + 1) * HEAD_DIM] for i in range(N_KV_SWA)]
    wv = [w_qkv[:, nq + (N_KV_SWA + i) * HEAD_DIM: nq + (N_KV_SWA + i + 1) * HEAD_DIM] for i in range(N_KV_SWA)]
    dup = lambda ws: [w for w in ws for _ in range(2)]
    w_k = jnp.concatenate(dup(wk), axis=1).astype(BF16)
    w_t = jnp.concatenate([wq * Q_SCALE] + dup(wv), axis=1).T.astype(BF16)
    k2d, qt, vt = _swa_proj(x2d, w_k, w_t, tables, b, s)
    o = _swa_attention(qt, k2d.reshape(b, s, -1), vt, sinks)
    return [o.reshape(b * s, nq)], [w_out.astype(BF16)]


def kernel(x, p, ln_mix_g, ln_mix_b, ln_ffn_g, ln_ffn_b, ab_w_in, ab_b_f, ab_w_out, c_w_qkv, c_sinks, c_w_out,
           ffn_w_gate_up, ffn_w_down, router_w, router_b, moe_w_gate_up, moe_w_down, ple_w_gate, ple_w_proj):
    b, s, d = x.shape
    t = b * s
    x2d = x.reshape(t, d)
    tables = _rope_tables(s)
    row = lambda v: v.reshape(1, -1)
    moe_gu = moe_w_gate_up.astype(BF16).reshape((-1,) + moe_w_gate_up.shape[2:])
    moe_d = moe_w_down.astype(BF16).reshape((-1,) + moe_w_down.shape[2:])
    for i in range(DEPTH):
        j = i // 2
        if i % 2 == 0:
            os_, ws = _even_mixer(x2d, b, s, ab_w_in[j], ab_b_f[j], ab_w_out[j])
        else:
            os_, ws = _odd_mixer(x2d, b, s, c_w_qkv[j], c_sinks[j], c_w_out[j], tables)
        x2d, xb = _out_proj_ln(os_, ws, x2d, row(ln_mix_g[i]), row(ln_mix_b[i]))
        ple_args = (row(ln_ffn_g[i]), row(ln_ffn_b[i]), p[i].reshape(t, D_PLE),
                    ple_w_gate[i].astype(BF16), ple_w_proj[i].astype(BF16))
        if i % 2 == 0:
            x2d = _ffn_ple(xb, x2d, ffn_w_gate_up[j].astype(BF16), ffn_w_down[j].astype(BF16), ple_args)
        else:
            rw = jnp.pad(router_w[j], ((0, 0), (0, LANES - N_EXPERTS)))
            rb = jnp.pad(router_b[j], (0, LANES - N_EXPERTS)).reshape(1, LANES)
            combine = _router(x2d, rw, rb)
            x2d = _moe(xb, x2d, combine, moe_gu, moe_d, j * N_EXPERTS, ple_args)
    return x2d.reshape(b, s, d)
```

```python
import functools
import math

import jax
import jax.numpy as jnp
from jax import lax
from jax.experimental import pallas as pl
from jax.experimental.pallas import tpu as pltpu

F32 = jnp.float32
BF16 = jnp.bfloat16

D_MODEL = 1024
HEAD_DIM = 64
LANES = 128
N_HEADS_FOX = 8
N_HEADS_SB = 8
N_HEADS_SWA = 16
N_KV_SWA = 2
WINDOW = 128
ROPE_THETA = 500000.0
ROPE_DIM = HEAD_DIM // 4
N_EXPERTS = 8
D_PLE = 256
LN_EPS = 1e-5
DEPTH = 4
DEEPNORM_ALPHA = (2.0 * DEPTH) ** 0.25
ATTN_SCALE = HEAD_DIM ** -0.5
FOX_W = N_HEADS_FOX * HEAD_DIM
SB_W = N_HEADS_SB * HEAD_DIM
VMEM_LIMIT = 56 * 1024 * 1024

NEG_INF = float("-inf")
LOG2E = math.log2(math.e)
Q_SCALE = ATTN_SCALE * LOG2E
SUM_ROWS = 16
SWA_HEAD_BATCH = 4
MOE_TM = 512
MOE_SR = 128
MOE_FC = 512
MOE_CS = 256
MOE_SLOTS = N_EXPERTS * (MOE_TM // MOE_CS + 1)
MOE_RING = 4


def _cparams(*sem):
    return pltpu.CompilerParams(dimension_semantics=sem, vmem_limit_bytes=VMEM_LIMIT)


def _dot(a, b):
    return jnp.dot(a, b, preferred_element_type=F32)


def _dot_nt(a, b):
    return lax.dot_general(a, b, (((1,), (1,)), ((), ())), preferred_element_type=F32)


def _log_sigmoid(x):
    return jnp.minimum(x, 0.0) - jnp.log1p(jnp.exp(-jnp.abs(x)))


def _head_masks(shape):
    lane = lax.broadcasted_iota(jnp.int32, shape, len(shape) - 1)
    return lane < HEAD_DIM, lane >= HEAD_DIM


def _ab_proj_kernel(x_ref, wk_ref, wt_ref, wf_ref, bf_ref, k_ref, ht_ref, lf_ref):
    xb = x_ref[...].astype(BF16)
    k_ref[...] = _dot(xb, wk_ref[...]).astype(BF16)
    ht_ref[0] = _dot_nt(wt_ref[...], xb).astype(BF16)
    lf_ref[...] = _log_sigmoid(_dot(xb, wf_ref[...]) + bf_ref[...])


def _ab_proj(x2d, wk, wt, wf, bf, b, s, tm=512):
    t = x2d.shape[0]
    nk = wk.shape[1]
    nt = wt.shape[0]
    spt = s // tm
    return pl.pallas_call(
        _ab_proj_kernel,
        grid=(t // tm,),
        in_specs=[
            pl.BlockSpec((tm, D_MODEL), lambda i: (i, 0)),
            pl.BlockSpec((D_MODEL, nk), lambda i: (0, 0)),
            pl.BlockSpec((nt, D_MODEL), lambda i: (0, 0)),
            pl.BlockSpec((D_MODEL, LANES), lambda i: (0, 0)),
            pl.BlockSpec((1, LANES), lambda i: (0, 0)),
        ],
        out_specs=[
            pl.BlockSpec((tm, nk), lambda i: (i, 0)),
            pl.BlockSpec((1, nt, tm), lambda i: (i // spt, 0, i % spt)),
            pl.BlockSpec((tm, LANES), lambda i: (i, 0)),
        ],
        out_shape=[
            jax.ShapeDtypeStruct((t, nk), BF16),
            jax.ShapeDtypeStruct((b, nt, s), BF16),
            jax.ShapeDtypeStruct((t, LANES), F32),
        ],
        compiler_params=_cparams("parallel"),
        name="ab_proj",
    )(x2d, wk, wt, wf, bf)


def _split3(v):
    hi = v.astype(BF16).astype(F32)
    r1 = v - hi
    mid = r1.astype(BF16).astype(F32)
    return hi, mid, r1 - mid


def _cumsum_kernel(x_ref, c_ref, aug_ref, st_ref):
    x = x_ref[0]
    s = x.shape[1]
    lane = lax.broadcasted_iota(jnp.int32, x.shape, 1) & (LANES - 1)
    sh = 1
    while sh < LANES:
        x = x + jnp.where(lane >= sh, pltpu.roll(x, sh, axis=1), 0.0)
        sh *= 2
    carry = jnp.zeros((x.shape[0], 1), F32)
    for c in range(s // LANES):
        blk = x[:, c * LANES:(c + 1) * LANES] + carry
        c_ref[0, :, c * LANES:(c + 1) * LANES] = blk
        carry = blk[:, LANES - 1:LANES]

    ones = jnp.ones((3, s), F32)
    st_ref[...] = jnp.zeros(st_ref.shape, F32)
    for p in range(x.shape[0] // 2):
        for which, base in ((0, HEAD_DIM), (1, 0)):
            negc = c_ref[0, 2 * p + which:2 * p + which + 1, :] * -LOG2E
            hi, mid, lo = _split3(negc)
            st_ref[base:base + 1, :] = hi
            st_ref[base + 1:base + 2, :] = mid
            st_ref[base + 2:base + 3, :] = lo
            st_ref[base + 3:base + 6, :] = ones
        for c in range(s // LANES):
            blk = st_ref[:, c * LANES:(c + 1) * LANES]
            aug_ref[0, p, c * LANES:(c + 1) * LANES, :] = blk.T.astype(BF16)


def _cumsum_seq(lf_t):
    b, h, s = lf_t.shape
    return pl.pallas_call(
        _cumsum_kernel,
        grid=(b,),
        in_specs=[pl.BlockSpec((1, h, s), lambda i: (i, 0, 0))],
        out_specs=[
            pl.BlockSpec((1, h, s), lambda i: (i, 0, 0)),
            pl.BlockSpec((1, h // 2, s, LANES), lambda i: (i, 0, 0, 0)),
        ],
        out_shape=[
            jax.ShapeDtypeStruct((b, h, s), F32),
            jax.ShapeDtypeStruct((b, h // 2, s, LANES), BF16),
        ],
        scratch_shapes=[pltpu.VMEM((LANES, s), F32)],
        compiler_params=_cparams("parallel"),
        name="fox_cumsum",
    )(lf_t)


def _fox_kernel(qt_ref, k_ref, aug_ref, vt_ref, c_ref, o_ref, m_ref, acc_ref, sa_ref, sb_ref, *, tq, tk):
    hp = pl.program_id(1)
    qi = pl.program_id(2)
    q0 = qi * tq
    qt = qt_ref[0].astype(F32)
    sub = lax.broadcasted_iota(jnp.int32, (LANES, tq), 0)
    lane = lax.broadcasted_iota(jnp.int32, (tk, LANES), 1)
    vsub = lax.broadcasted_iota(jnp.int32, (LANES, tk), 0)
    own_rows = [sub < HEAD_DIM, sub >= HEAD_DIM]
    own_lanes = [lane < HEAD_DIM, lane >= HEAD_DIM]
    own_vrows = [vsub < HEAD_DIM, vsub >= HEAD_DIM]

    qaug = []
    for r in range(2):
        base = HEAD_DIM * (1 - r)
        ci = c_ref[0, pl.ds(2 * hp + r, 1), :]
        hi, mid, lo = _split3(ci * LOG2E)
        t = jnp.where(own_rows[r], qt, 0.0)
        t = jnp.where((sub >= base) & (sub < base + 3), 1.0, t)
        t = jnp.where(sub == base + 3, hi, t)
        t = jnp.where(sub == base + 4, mid, t)
        t = jnp.where(sub == base + 5, lo, t)
        qaug.append(t.astype(BF16))

    m_ref[...] = jnp.full(m_ref.shape, NEG_INF, F32)
    acc_ref[...] = jnp.zeros(acc_ref.shape, F32)

    def scores(k0, s_ref):
        kt = k_ref[0, pl.ds(k0, tk), :]
        ca = aug_ref[0, 0, pl.ds(k0, tk), :]
        for r in range(2):
            ka = jnp.where(own_lanes[r], kt, ca)
            s_ref[r] = _dot(ka, qaug[r])

    def softmax_pv(k0, s_ref, masked):
        vt = vt_ref[0, :, pl.ds(k0, tk)]
        if masked:
            key = k0 + lax.broadcasted_iota(jnp.int32, (tk, tq), 0)
            qry = q0 + lax.broadcasted_iota(jnp.int32, (tk, tq), 1)
            keep = key <= qry
        for r in range(2):
            st = s_ref[r]
            if masked:
                st = jnp.where(keep, st, NEG_INF)
            m_prev = m_ref[r]
            m_new = jnp.maximum(m_prev, jnp.max(st, axis=0, keepdims=True))
            alpha = jnp.exp2(m_prev - m_new)
            pt = jnp.exp2(st - m_new).astype(BF16)
            va = jnp.where(own_vrows[r], vt, 1.0)
            acc_ref[r] = alpha * acc_ref[r] + _dot(va, pt)
            m_ref[r] = m_new

    tile = lambda j: pl.multiple_of(j * tk, tk)
    scores(tile(0), sa_ref)

    def pair(i):
        scores(tile(2 * i + 1), sb_ref)
        softmax_pv(tile(2 * i), sa_ref, False)
        scores(tile(2 * i + 2), sa_ref)
        softmax_pv(tile(2 * i + 1), sb_ref, False)

    def body(j, carry):
        pair(2 * j)
        pair(2 * j + 1)
        return carry

    lax.fori_loop(0, qi // 2, body, 0)

    @pl.when(qi % 2 == 1)
    def _():
        pair(qi - 1)

    scores(tile(2 * qi + 1), sb_ref)
    softmax_pv(tile(2 * qi), sa_ref, True)
    softmax_pv(tile(2 * qi + 1), sb_ref, True)

    ot = jnp.zeros((LANES, tq), F32)
    for r in range(2):
        a = acc_ref[r]
        den = a[HEAD_DIM * (1 - r):HEAD_DIM * (1 - r) + 1, :]
        ot = ot + jnp.where(own_rows[r], a / den, 0.0)
    o_ref[0] = ot.T.astype(BF16)


def _fox_attention(k3, ht, caug, crow, tq=512):
    b, s, _ = k3.shape
    nhp = N_HEADS_FOX // 2
    tk = tq // 2
    kern = functools.partial(_fox_kernel, tq=tq, tk=tk)
    return pl.pallas_call(
        kern,
        grid=(b, nhp, s // tq),
        in_specs=[
            pl.BlockSpec((1, LANES, tq), lambda bi, hp, qi: (bi, hp, qi)),
            pl.BlockSpec((1, s, LANES), lambda bi, hp, qi: (bi, 0, hp)),
            pl.BlockSpec((1, 1, s, LANES), lambda bi, hp, qi: (bi, hp, 0, 0)),
            pl.BlockSpec((1, LANES, s), lambda bi, hp, qi: (bi, nhp + hp, 0)),
            pl.BlockSpec((1, N_HEADS_FOX, tq), lambda bi, hp, qi: (bi, 0, qi)),
        ],
        out_specs=pl.BlockSpec((1, tq, LANES), lambda bi, hp, qi: (bi, qi, hp)),
        out_shape=jax.ShapeDtypeStruct((b, s, FOX_W), BF16),
        scratch_shapes=[
            pltpu.VMEM((2, 1, tq), F32),
            pltpu.VMEM((2, LANES, tq), F32),
            pltpu.VMEM((2, tk, tq), F32),
            pltpu.VMEM((2, tk, tq), F32),
        ],
        compiler_params=_cparams("parallel", "parallel", "arbitrary"),
        name="fox_attn",
    )(ht, k3, caug, ht, crow)


def _sb_kernel(qt_ref, k_ref, vt_ref, o_ref, run_ref, acc_ref, za_ref, zb_ref, *, tq, cw):
    qi = pl.program_id(2)
    q0 = qi * tq
    qt = qt_ref[0]
    sub = lax.broadcasted_iota(jnp.int32, (LANES, tq), 0)
    own_rows = [sub < HEAD_DIM, sub >= HEAD_DIM]
    qm = [jnp.where(mk, qt, 0.0) for mk in own_rows]

    ss = lax.broadcasted_iota(jnp.int32, (cw, cw), 0)
    jj = lax.broadcasted_iota(jnp.int32, (cw, cw), 1)
    tri = jnp.where(jj > ss, 1.0, 0.0).astype(BF16)
    tri = jnp.concatenate([tri, jnp.ones((SUM_ROWS, cw), BF16)], axis=0)

    run_ref[...] = jnp.zeros(run_ref.shape, F32)
    acc_ref[...] = jnp.zeros(acc_ref.shape, F32)

    def start_of(c):
        return pl.multiple_of(jnp.maximum(q0 + tq - (c + 1) * cw, 0), cw)

    def logits(c, z_ref):
        kt = k_ref[0, pl.ds(start_of(c), cw), :]
        for r in range(2):
            z_ref[r] = _dot(kt, qm[r])

    def weights_pv(c, z_ref, masked):
        k0 = start_of(c)
        vt = vt_ref[0, :, pl.ds(k0, cw)]
        if masked:
            key = k0 + lax.broadcasted_iota(jnp.int32, (cw, tq), 0)
            qry = q0 + lax.broadcasted_iota(jnp.int32, (cw, tq), 1)
            keep = key < qry
        zs, sps, laters, ws = [], [], [], []
        for r in range(2):
            z = z_ref[r]
            e = jnp.exp2(-jnp.abs(z))
            sp = jnp.maximum(z, 0.0) + jnp.log(1.0 + e) * LOG2E
            if masked:
                sp = jnp.where(keep, sp, 0.0)
            zs.append(z)
            sps.append(sp)
        for r in range(2):
            laters.append(_dot(tri, sps[r].astype(BF16)))
        for r in range(2):
            run = run_ref[r]
            a = jnp.exp2((zs[r] - sps[r]) - (laters[r][:cw] + run))
            if masked:
                a = jnp.where(keep, a, 0.0)
            ws.append(a.astype(BF16))
            run_ref[r] = run + laters[r][cw:cw + 1]
        for r in range(2):
            acc_ref[r] = acc_ref[r] + _dot(vt, ws[r])

    logits(0, za_ref)
    logits(1, zb_ref)
    weights_pv(0, za_ref, True)
    logits(2, za_ref)
    weights_pv(1, zb_ref, True)

    def body(i, carry):
        logits(2 * i + 1, zb_ref)
        weights_pv(2 * i, za_ref, False)
        logits(2 * i + 2, za_ref)
        weights_pv(2 * i + 1, zb_ref, False)
        return carry

    lax.fori_loop(1, qi + 1, body, 0)

    ot = jnp.where(own_rows[0], acc_ref[0], acc_ref[1])
    o_ref[0] = ot.T.astype(BF16)


def _sb_attention(k3, ht, tq=512):
    b, s, _ = k3.shape
    cw = tq // 2
    nhp = N_HEADS_SB // 2
    kern = functools.partial(_sb_kernel, tq=tq, cw=cw)
    return pl.pallas_call(
        kern,
        grid=(b, nhp, s // tq),
        in_specs=[
            pl.BlockSpec((1, LANES, tq), lambda bi, hp, qi: (bi, 2 * nhp + hp, qi)),
            pl.BlockSpec((1, s, LANES), lambda bi, hp, qi: (bi, 0, nhp + hp)),
            pl.BlockSpec((1, LANES, s), lambda bi, hp, qi: (bi, 3 * nhp + hp, 0)),
        ],
        out_specs=pl.BlockSpec((1, tq, LANES), lambda bi, hp, qi: (bi, qi, hp)),
        out_shape=jax.ShapeDtypeStruct((b, s, SB_W), BF16),
        scratch_shapes=[
            pltpu.VMEM((2, 1, tq), F32),
            pltpu.VMEM((2, LANES, tq), F32),
            pltpu.VMEM((2, cw, tq), F32),
            pltpu.VMEM((2, cw, tq), F32),
        ],
        compiler_params=_cparams("parallel", "parallel", "arbitrary"),
        name="sb_attn",
    )(ht, k3, ht)


def _swa_proj_kernel(x_ref, wk_ref, wt_ref, cos_ref, sa_ref, sb_ref, cost_ref, sint_ref, k_ref, qt_ref, vt_ref):
    xb = x_ref[...].astype(BF16)
    half = ROPE_DIM // 2
    yk = _dot(xb, wk_ref[...])
    for c in range(yk.shape[1] // LANES):
        blk = yk[:, c * LANES:(c + 1) * LANES]
        blk = (blk * cos_ref[...] + pltpu.roll(blk, half, axis=1) * sa_ref[...]
               + pltpu.roll(blk, LANES - half, axis=1) * sb_ref[...])
        k_ref[:, c * LANES:(c + 1) * LANES] = blk.astype(BF16)
    yt = _dot_nt(wt_ref[...], xb)
    cos = cost_ref[...]
    sin = sint_ref[...]
    nq = N_HEADS_SWA * HEAD_DIM
    for h in range(N_HEADS_SWA):
        base = h * HEAD_DIM
        x1 = yt[base:base + half]
        x2 = yt[base + half:base + ROPE_DIM]
        rot = jnp.concatenate([x1 * cos - x2 * sin, x2 * cos + x1 * sin], axis=0)
        qt_ref[0, base:base + ROPE_DIM, :] = rot.astype(BF16)
        qt_ref[0, base + ROPE_DIM:base + HEAD_DIM, :] = yt[base + ROPE_DIM:base + HEAD_DIM].astype(BF16)
    vt_ref[0] = yt[nq:].astype(BF16)


def _rope_tables(s):
    half = ROPE_DIM // 2
    inv = ROPE_THETA ** (-jnp.arange(half, dtype=F32) * 2.0 / ROPE_DIM)
    ang = jnp.arange(s, dtype=F32)[:, None] * inv[None, :]
    cos, sin = jnp.cos(ang), jnp.sin(ang)
    d = jnp.arange(LANES) % HEAD_DIM
    idx = d % half
    cos_t = jnp.where(d[None, :] < ROPE_DIM, cos[:, idx], 1.0)
    sin_a = jnp.where((d[None, :] >= half) & (d[None, :] < ROPE_DIM), sin[:, idx], 0.0)
    sin_b = jnp.where(d[None, :] < half, -sin[:, idx], 0.0)
    return cos_t.astype(F32), sin_a.astype(F32), sin_b.astype(F32), cos.T.astype(F32), sin.T.astype(F32)


def _swa_proj(x2d, wk, wt, tables, b, s, tm=512):
    t = x2d.shape[0]
    nk = wk.shape[1]
    nq = N_HEADS_SWA * HEAD_DIM
    nv = wt.shape[0] - nq
    spt = s // tm
    half = ROPE_DIM // 2
    lane_tab = pl.BlockSpec((tm, LANES), lambda i: (i % spt, 0))
    row_tab = pl.BlockSpec((half, tm), lambda i: (0, i % spt))
    return pl.pallas_call(
        _swa_proj_kernel,
        grid=(t // tm,),
        in_specs=[
            pl.BlockSpec((tm, D_MODEL), lambda i: (i, 0)),
            pl.BlockSpec(wk.shape, lambda i: (0, 0)),
            pl.BlockSpec(wt.shape, lambda i: (0, 0)),
            lane_tab, lane_tab, lane_tab, row_tab, row_tab,
        ],
        out_specs=[
            pl.BlockSpec((tm, nk), lambda i: (i, 0)),
            pl.BlockSpec((1, nq, tm), lambda i: (i // spt, 0, i % spt)),
            pl.BlockSpec((1, nv, tm), lambda i: (i // spt, 0, i % spt)),
        ],
        out_shape=[
            jax.ShapeDtypeStruct((t, nk), BF16),
            jax.ShapeDtypeStruct((b, nq, s), BF16),
            jax.ShapeDtypeStruct((b, nv, s), BF16),
        ],
        compiler_params=_cparams("parallel"),
        name="swa_proj",
    )(x2d, wk, wt, *tables)


def _swa_kernel(sink_ref, qt_ref, k_ref, vt_ref, o_ref, *, tq):
    qi = pl.program_id(1)
    q0 = qi * tq
    kw = tq + WINDOW
    kstart = pl.multiple_of(jnp.maximum(q0 - WINDOW, 0), WINDOW)
    kwin = k_ref[0, pl.ds(kstart, kw), :]
    vwin = vt_ref[0, :, pl.ds(kstart, kw)]
    key = kstart + lax.broadcasted_iota(jnp.int32, (kw, tq), 0)
    qry = q0 + lax.broadcasted_iota(jnp.int32, (kw, tq), 1)
    diff = qry - key
    keep = (diff >= 0) & (diff < WINDOW)
    sub = lax.broadcasted_iota(jnp.int32, (LANES, tq), 0)
    vsub = lax.broadcasted_iota(jnp.int32, (LANES, kw), 0)
    own_rows = [sub < HEAD_DIM, sub >= HEAD_DIM]
    own_vrows = [vsub < HEAD_DIM, vsub >= HEAD_DIM]
    group = N_HEADS_SWA // N_KV_SWA

    def scores(h):
        c, r = divmod(h, 2)
        g = h // group
        qblk = qt_ref[0, c * LANES:(c + 1) * LANES, :]
        return _dot(kwin[:, g * LANES:(g + 1) * LANES], jnp.where(own_rows[r], qblk, 0.0))

    def weights(h, st):
        st = jnp.where(keep, st, NEG_INF)
        sink = sink_ref[h] * LOG2E
        mx = jnp.maximum(jnp.max(st, axis=0, keepdims=True), sink)
        return jnp.exp2(st - mx).astype(BF16), jnp.exp2(sink - mx)

    def values(h, pt):
        r = h % 2
        g = h // group
        va = jnp.where(own_vrows[r], vwin[g * LANES:(g + 1) * LANES, :], 1.0)
        return _dot(va, pt)

    batches = [list(range(i, i + SWA_HEAD_BATCH)) for i in range(0, N_HEADS_SWA, SWA_HEAD_BATCH)]
    sts = [scores(h) for h in batches[0]]
    for bi, heads in enumerate(batches):
        nxt = [scores(h) for h in batches[bi + 1]] if bi + 1 < len(batches) else None
        pts = [weights(h, st) for h, st in zip(heads, sts)]
        accs = [values(h, pt) for h, (pt, _) in zip(heads, pts)]
        outs = []
        for h, acc, (_, esink) in zip(heads, accs, pts):
            r = h % 2
            den = acc[HEAD_DIM * (1 - r):HEAD_DIM * (1 - r) + 1, :] + esink
            outs.append(acc / den)
            if r == 1:
                c = h // 2
                ot = jnp.where(own_rows[0], outs[-2], outs[-1])
                o_ref[0, :, c * LANES:(c + 1) * LANES] = ot.T.astype(BF16)
        sts = nxt


def _swa_attention(qt3, k3, vt3, sinks, tq=256):
    b, nq, s = qt3.shape
    kern = functools.partial(_swa_kernel, tq=tq)
    return pl.pallas_call(
        kern,
        grid=(b, s // tq),
        in_specs=[
            pl.BlockSpec(memory_space=pltpu.SMEM),
            pl.BlockSpec((1, nq, tq), lambda bi, qi: (bi, 0, qi)),
            pl.BlockSpec((1, s, k3.shape[2]), lambda bi, qi: (bi, 0, 0)),
            pl.BlockSpec((1, vt3.shape[1], s), lambda bi, qi: (bi, 0, 0)),
        ],
        out_specs=pl.BlockSpec((1, tq, nq), lambda bi, qi: (bi, qi, 0)),
        out_shape=jax.ShapeDtypeStruct((b, s, nq), BF16),
        compiler_params=_cparams("parallel", "arbitrary"),
        name="swa_attn",
    )(sinks, qt3, k3, vt3)


def _layer_norm(z, g, b):
    mu = jnp.mean(z, axis=-1, keepdims=True)
    zc = z - mu
    var = jnp.mean(zc * zc, axis=-1, keepdims=True)
    return zc * lax.rsqrt(var + LN_EPS) * g + b


def _outln_kernel(*refs, n_in):
    o_refs = refs[:n_in]
    w_refs = refs[n_in:2 * n_in]
    x_ref, g_ref, b_ref, y_ref, yb_ref = refs[2 * n_in:]
    h = _dot(o_refs[0][...], w_refs[0][...])
    for i in range(1, n_in):
        h = h + _dot(o_refs[i][...], w_refs[i][...])
    y = _layer_norm(DEEPNORM_ALPHA * x_ref[...] + h, g_ref[...], b_ref[...])
    y_ref[...] = y
    yb_ref[...] = y.astype(BF16)


def _out_proj_ln(os_, ws, x2d, g, b, tm=512):
    t = x2d.shape[0]
    n_in = len(os_)
    in_specs = [pl.BlockSpec((tm, o.shape[1]), lambda i: (i, 0)) for o in os_]
    in_specs += [pl.BlockSpec(w.shape, lambda i: (0, 0)) for w in ws]
    in_specs += [
        pl.BlockSpec((tm, D_MODEL), lambda i: (i, 0)),
        pl.BlockSpec((1, D_MODEL), lambda i: (0, 0)),
        pl.BlockSpec((1, D_MODEL), lambda i: (0, 0)),
    ]
    return pl.pallas_call(
        functools.partial(_outln_kernel, n_in=n_in),
        grid=(t // tm,),
        in_specs=in_specs,
        out_specs=[pl.BlockSpec((tm, D_MODEL), lambda i: (i, 0))] * 2,
        out_shape=[jax.ShapeDtypeStruct((t, D_MODEL), F32), jax.ShapeDtypeStruct((t, D_MODEL), BF16)],
        compiler_params=_cparams("parallel"),
        name="out_proj_ln",
    )(*os_, *ws, x2d, g, b)


def _router_kernel(x_ref, w_ref, b_ref, c_ref):
    x = x_ref[...]
    w = w_ref[...]
    xh = x.astype(BF16)
    xl = (x - xh.astype(F32)).astype(BF16)
    wh = w.astype(BF16)
    wl = (w - wh.astype(F32)).astype(BF16)
    logits = _dot(xh, wh) + (_dot(xl, wh) + _dot(xh, wl)) + b_ref[...]
    lane = lax.broadcasted_iota(jnp.int32, logits.shape, 1)
    logits = jnp.where(lane < N_EXPERTS, logits, NEG_INF)
    m1 = jnp.max(logits, axis=1, keepdims=True)
    i1 = jnp.min(jnp.where(logits == m1, lane, LANES), axis=1, keepdims=True)
    rest = jnp.where(lane == i1, NEG_INF, logits)
    m2 = jnp.max(rest, axis=1, keepdims=True)
    i2 = jnp.min(jnp.where(rest == m2, lane, LANES), axis=1, keepdims=True)
    e2 = jnp.exp(m2 - m1)
    den = 1.0 + e2
    c_ref[...] = jnp.where(lane == i1, 1.0 / den, 0.0) + jnp.where(lane == i2, e2 / den, 0.0)


def _router(x2d, w_pad, b_pad, tm=512):
    t = x2d.shape[0]
    return pl.pallas_call(
        _router_kernel,
        grid=(t // tm,),
        in_specs=[
            pl.BlockSpec((tm, D_MODEL), lambda i: (i, 0)),
            pl.BlockSpec((D_MODEL, LANES), lambda i: (0, 0)),
            pl.BlockSpec((1, LANES), lambda i: (0, 0)),
        ],
        out_specs=pl.BlockSpec((tm, LANES), lambda i: (i, 0)),
        out_shape=jax.ShapeDtypeStruct((t, LANES), F32),
        compiler_params=_cparams("parallel"),
        name="router",
    )(x2d, w_pad, b_pad)


def _ffn_kernel(eid_ref, valid_ref, x_ref, wg_ref, wu_ref, wd_ref, o_ref, acc_ref):
    i = pl.program_id(0)
    c = pl.program_id(1)
    last = c == pl.num_programs(1) - 1
    valid = valid_ref[i] != 0

    @pl.when(valid)
    def _():
        x = x_ref[...]
        gate = _dot(x, wg_ref[0])
        up = _dot(x, wu_ref[0])
        h = gate * jax.nn.sigmoid(gate) * up
        contrib = _dot(h.astype(BF16), wd_ref[0])

        @pl.when(c == 0)
        def _():
            acc_ref[...] = contrib

        @pl.when(c != 0)
        def _():
            acc_ref[...] += contrib

        @pl.when(last)
        def _():
            o_ref[...] = acc_ref[...].astype(o_ref.dtype)

    @pl.when(jnp.logical_not(valid) & last)
    def _():
        o_ref[...] = jnp.zeros(o_ref.shape, o_ref.dtype)


def _ffn(xb, w_gu, w_d, eids, valid, out_dtype, tm, fc, name):
    r = xb.shape[0]
    f = w_d.shape[1]
    ncf = f // fc
    in_specs = [
        pl.BlockSpec((tm, D_MODEL), lambda i, c, eid, ok: (i, 0)),
        pl.BlockSpec((1, D_MODEL, fc), lambda i, c, eid, ok: (eid[i], 0, c * ok[i])),
        pl.BlockSpec((1, D_MODEL, fc), lambda i, c, eid, ok: (eid[i], 0, ncf + c * ok[i])),
        pl.BlockSpec((1, fc, D_MODEL), lambda i, c, eid, ok: (eid[i], c * ok[i], 0)),
    ]
    return pl.pallas_call(
        _ffn_kernel,
        grid_spec=pltpu.PrefetchScalarGridSpec(
            num_scalar_prefetch=2,
            grid=(r // tm, ncf),
            in_specs=in_specs,
            out_specs=pl.BlockSpec((tm, D_MODEL), lambda i, c, eid, ok: (i, 0)),
            scratch_shapes=[pltpu.VMEM((tm, D_MODEL), F32)],
        ),
        out_shape=jax.ShapeDtypeStruct((r, D_MODEL), out_dtype),
        compiler_params=_cparams("parallel", "arbitrary"),
        name=name,
    )(eids, valid, xb, w_gu, w_gu, w_d)


def _moe_ffn_kernel(eid_ref, valid_ref, first_ref, x_ref, wgu_hbm, wd_hbm, o_ref,
                    cg_ref, cu_ref, cd_ref, sg_ref, su_ref, sd_ref, sem_ref, acc_ref, *, fc):
    i = pl.program_id(0)
    e = eid_ref[i]
    f = cd_ref.shape[0]
    n_chunks = f // fc

    def chunk_copies(c, slot):
        return (
            pltpu.make_async_copy(wgu_hbm.at[e, :, pl.ds(c * fc, fc)], sg_ref.at[slot], sem_ref.at[0, slot]),
            pltpu.make_async_copy(wgu_hbm.at[e, :, pl.ds(f + c * fc, fc)], su_ref.at[slot], sem_ref.at[1, slot]),
            pltpu.make_async_copy(wd_hbm.at[e, pl.ds(c * fc, fc), :], sd_ref.at[slot], sem_ref.at[2, slot]),
        )

    def compute(c):
        x = x_ref[...]
        gate = _dot(x, cg_ref[:, c * fc:(c + 1) * fc])
        up = _dot(x, cu_ref[:, c * fc:(c + 1) * fc])
        h = gate * jax.nn.sigmoid(gate) * up
        contrib = _dot(h.astype(BF16), cd_ref[c * fc:(c + 1) * fc, :])
        if c == 0:
            acc_ref[...] = contrib
        else:
            acc_ref[...] += contrib

    valid = valid_ref[i] != 0
    first = first_ref[i] != 0

    @pl.when(valid & first)
    def _():
        for cp in chunk_copies(0, 0):
            cp.start()
        for c in range(n_chunks):
            slot = c % 2
            if c + 1 < n_chunks:
                for cp in chunk_copies(c + 1, 1 - slot):
                    cp.start()
            for cp in chunk_copies(c, slot):
                cp.wait()
            cg_ref[:, c * fc:(c + 1) * fc] = sg_ref[slot].astype(BF16)
            cu_ref[:, c * fc:(c + 1) * fc] = su_ref[slot].astype(BF16)
            cd_ref[c * fc:(c + 1) * fc, :] = sd_ref[slot].astype(BF16)
            compute(c)
        o_ref[...] = acc_ref[...].astype(o_ref.dtype)

    @pl.when(valid & jnp.logical_not(first))
    def _():
        for c in range(n_chunks):
            compute(c)
        o_ref[...] = acc_ref[...].astype(o_ref.dtype)

    @pl.when(jnp.logical_not(valid))
    def _():
        o_ref[...] = jnp.zeros(o_ref.shape, o_ref.dtype)


def _moe_ffn(xs, w_gu, w_d, eids, valid, first, tm=MOE_TM, fc=MOE_FC):
    r = xs.shape[0]
    f = w_d.shape[1]
    return pl.pallas_call(
        functools.partial(_moe_ffn_kernel, fc=fc),
        grid_spec=pltpu.PrefetchScalarGridSpec(
            num_scalar_prefetch=3,
            grid=(r // tm,),
            in_specs=[
                pl.BlockSpec((tm, D_MODEL), lambda i, *_: (i, 0)),
                pl.BlockSpec(memory_space=pl.ANY),
                pl.BlockSpec(memory_space=pl.ANY),
            ],
            out_specs=pl.BlockSpec((tm, D_MODEL), lambda i, *_: (i, 0)),
            scratch_shapes=[
                pltpu.VMEM((D_MODEL, f), BF16),
                pltpu.VMEM((D_MODEL, f), BF16),
                pltpu.VMEM((f, D_MODEL), BF16),
                pltpu.VMEM((2, D_MODEL, fc), F32),
                pltpu.VMEM((2, D_MODEL, fc), F32),
                pltpu.VMEM((2, fc, D_MODEL), F32),
                pltpu.SemaphoreType.DMA((3, 2)),
                pltpu.VMEM((tm, D_MODEL), F32),
            ],
        ),
        out_shape=jax.ShapeDtypeStruct((r, D_MODEL), BF16),
        compiler_params=_cparams("arbitrary"),
        name="ffn_moe",
    )(eids, valid, first, xs, w_gu, w_d)


def _moe_gather_kernel(te_ref, clo_ref, nch_ref, s0_ref, ns_ref, x_ref, pos_ref, o_ref, acc_ref, *, tm, sr, n_chunks):
    r = pl.program_id(0)
    e = te_ref[r]
    sub_row = lax.broadcasted_iota(jnp.int32, (sr, tm), 0)
    acc_ref[...] = jnp.zeros(acc_ref.shape, F32)

    def body(i, carry):
        c = clo_ref[r] + i
        t0 = pl.multiple_of(c * tm, tm)
        p = pos_ref[pl.ds(e, 1), pl.ds(t0, tm)]

        def sub(j, carry2):
            r0 = pl.multiple_of((s0_ref[r * n_chunks + c] + j) * sr, sr)
            onehot = jnp.where(p == r * tm + r0 + sub_row, 1.0, 0.0).astype(BF16)
            acc_ref[pl.ds(r0, sr), :] += _dot(onehot, x_ref[pl.ds(t0, tm), :])
            return carry2

        lax.fori_loop(0, ns_ref[r * n_chunks + c], sub, 0)
        return carry

    lax.fori_loop(0, nch_ref[r], body, 0)
    o_ref[...] = acc_ref[...].astype(BF16)


def _moe_gather(xb, pos_t, tile_e, clo, nch, sub0, nsub, n_rows, tm=MOE_TM, sr=MOE_SR):
    t = xb.shape[0]
    return pl.pallas_call(
        functools.partial(_moe_gather_kernel, tm=tm, sr=sr, n_chunks=t // tm),
        grid_spec=pltpu.PrefetchScalarGridSpec(
            num_scalar_prefetch=5,
            grid=(n_rows // tm,),
            in_specs=[
                pl.BlockSpec((t, D_MODEL), lambda r, *_: (0, 0), pipeline_mode=pl.Buffered(1)),
                pl.BlockSpec(pos_t.shape, lambda r, *_: (0, 0)),
            ],
            out_specs=pl.BlockSpec((tm, D_MODEL), lambda r, *_: (r, 0)),
            scratch_shapes=[pltpu.VMEM((tm, D_MODEL), F32)],
        ),
        out_shape=jax.ShapeDtypeStruct((n_rows, D_MODEL), BF16),
        compiler_params=_cparams("arbitrary"),
        name="moe_gather",
    )(tile_e, clo, nch, sub0, nsub, xb, pos_t)


def _moe_combine_kernel(n_ref, cid_ref, eid_ref, ys_hbm, pos_ref, comb_ref, x_ref, *rest, tm, cs, n_slots, n_buf):
    *ple_refs, o_ref, buf_ref, sem_ref = rest
    i = pl.program_id(0)
    n = n_ref[i]
    base = i * n_slots

    def chunk_copy(k, slot):
        row0 = pl.multiple_of(cid_ref[base + k] * cs, cs)
        return pltpu.make_async_copy(ys_hbm.at[pl.ds(row0, cs), :], buf_ref.at[slot], sem_ref.at[slot])

    o_ref[...] = jnp.zeros(o_ref.shape, F32)

    for d in range(n_buf):

        @pl.when(d < n)
        def _():
            chunk_copy(d, d).start()

    def body(k, carry):
        slot = k % n_buf
        chunk_copy(k, slot).wait()
        e = eid_ref[base + k]
        lane = lax.broadcasted_iota(jnp.int32, pos_ref.shape, 1)
        pos_e = jnp.sum(jnp.where(lane == e, pos_ref[...], 0), axis=1, keepdims=True)
        lane_c = lax.broadcasted_iota(jnp.int32, comb_ref.shape, 1)
        g_e = jnp.sum(jnp.where(lane_c == e, comb_ref[...], 0.0), axis=1, keepdims=True)
        col = cid_ref[base + k] * cs + lax.broadcasted_iota(jnp.int32, (tm, cs), 1)
        onehot = jnp.where(pos_e == col, 1.0, 0.0).astype(BF16)
        o_ref[...] += g_e * _dot(onehot, buf_ref[slot])

        @pl.when(k + n_buf < n)
        def _():
            chunk_copy(k + n_buf, slot).start()

        return carry

    lax.fori_loop(0, n, body, 0)
    o_ref[...] = _ln_ple(x_ref[...], o_ref[...], ple_refs)


def _moe_combine(ys, pos, combine, n_need, cid, eid, x2d, ple_args, tm=MOE_TM, cs=MOE_CS):
    t = pos.shape[0]
    return pl.pallas_call(
        functools.partial(_moe_combine_kernel, tm=tm, cs=cs, n_slots=MOE_SLOTS, n_buf=MOE_RING),
        grid_spec=pltpu.PrefetchScalarGridSpec(
            num_scalar_prefetch=3,
            grid=(t // tm,),
            in_specs=[
                pl.BlockSpec(memory_space=pl.ANY),
                pl.BlockSpec((tm, N_EXPERTS), lambda i, *_: (i, 0)),
                pl.BlockSpec((tm, LANES), lambda i, *_: (i, 0)),
                pl.BlockSpec((tm, D_MODEL), lambda i, *_: (i, 0)),
            ] + _ple_specs(tm, ple_args),
            out_specs=pl.BlockSpec((tm, D_MODEL), lambda i, *_: (i, 0)),
            scratch_shapes=[pltpu.VMEM((MOE_RING, cs, D_MODEL), BF16), pltpu.SemaphoreType.DMA((MOE_RING,))],
        ),
        out_shape=jax.ShapeDtypeStruct((t, D_MODEL), F32),
        compiler_params=_cparams("arbitrary"),
        name="moe_combine",
    )(n_need, cid, eid, ys, pos, combine, x2d, *ple_args)


def _moe_plan(combine, tm=MOE_TM):
    t = combine.shape[0]
    n_tok_tiles = t // tm
    n_row_tiles = (2 * t) // tm + N_EXPERTS
    sel = (combine[:, :N_EXPERTS] != 0.0).astype(jnp.int32)
    csum = jnp.cumsum(sel, axis=0)
    count = csum[-1]
    ntile_e = (count + tm - 1) // tm
    tile_end = jnp.cumsum(ntile_e)
    gstart = (tile_end - ntile_e) * tm
    pos = jnp.where(sel != 0, gstart[None, :] + csum - sel, -1)
    rt = jnp.arange(n_row_tiles, dtype=jnp.int32)
    tile_e = jnp.minimum(jnp.sum(rt[:, None] >= tile_end[None, :], axis=1), N_EXPERTS - 1).astype(jnp.int32)
    valid = (rt < tile_end[-1]).astype(jnp.int32)
    prev_e = jnp.concatenate([jnp.full((1,), -1, jnp.int32), tile_e[:-1]])
    group_head = ((valid != 0) & (tile_e != prev_e)).astype(jnp.int32)
    cb = jnp.concatenate([jnp.zeros((1, N_EXPERTS), jnp.int32), csum[tm - 1::tm]], axis=0)
    off = rt * tm - gstart[tile_e]
    cb_t = cb[:, tile_e].T
    clo = jnp.sum(cb_t[:, 1:] <= off[:, None], axis=1).astype(jnp.int32)
    cend = jnp.sum(cb_t[:, :-1] < (off + tm)[:, None], axis=1).astype(jnp.int32)
    nch = jnp.where(valid != 0, jnp.maximum(cend - clo, 0), 0).astype(jnp.int32)
    clo = jnp.minimum(clo, n_tok_tiles - 1)
    r_lo = jnp.clip(cb_t[:, :-1] - off[:, None], 0, tm)
    r_hi = jnp.clip(cb_t[:, 1:] - off[:, None], 0, tm)
    sub0 = (r_lo // MOE_SR).astype(jnp.int32)
    nsub = jnp.where(r_hi > r_lo, (r_hi + MOE_SR - 1) // MOE_SR - sub0, 0).astype(jnp.int32)
    first = gstart[None, :] + cb[:-1]
    n_ie = cb[1:] - cb[:-1]
    per_e = MOE_SLOTS // N_EXPERTS
    k = jnp.arange(per_e, dtype=jnp.int32)
    c_lo = first // MOE_CS
    c_hi = (first + n_ie - 1) // MOE_CS
    cand = c_lo[..., None] + k
    need = ((n_ie[..., None] > 0) & (cand <= c_hi[..., None])).reshape(n_tok_tiles, MOE_SLOTS)
    cand = cand.reshape(n_tok_tiles, MOE_SLOTS)
    eids = jnp.broadcast_to(jnp.arange(N_EXPERTS, dtype=jnp.int32)[:, None], (N_EXPERTS, per_e)).reshape(-1)
    order = jnp.argsort(jnp.logical_not(need), axis=1, stable=True)
    cid = jnp.where(need, cand, 0)
    cid = jnp.take_along_axis(cid, order, axis=1).reshape(-1).astype(jnp.int32)
    eid = eids[order].reshape(-1).astype(jnp.int32)
    n_need = jnp.sum(need, axis=1).astype(jnp.int32)
    return dict(pos=pos.astype(jnp.int32), tile_e=tile_e, valid=valid, first=group_head, clo=clo, nch=nch,
                sub0=sub0.reshape(-1), nsub=nsub.reshape(-1),
                n_need=n_need, cid=cid, eid=eid, n_rows=n_row_tiles * tm)


def _moe(xb, x2d, combine, w_gu, w_d, first_expert, ple_args):
    plan = _moe_plan(combine)
    xs = _moe_gather(xb, plan["pos"].T, plan["tile_e"], plan["clo"], plan["nch"], plan["sub0"], plan["nsub"],
                     plan["n_rows"])
    ys = _moe_ffn(xs, w_gu, w_d, plan["tile_e"] + first_expert, plan["valid"], plan["first"])
    return _moe_combine(ys, plan["pos"], combine, plan["n_need"], plan["cid"], plan["eid"], x2d, ple_args)


def _ln_ple(x, h, ple_refs):
    g_ref, b_ref, p_ref, wg_ref, wp_ref = ple_refs
    y = _layer_norm(DEEPNORM_ALPHA * x + h, g_ref[...], b_ref[...])
    gate = jax.nn.sigmoid(_dot(y.astype(BF16), wg_ref[...]))
    proj = _dot(p_ref[...].astype(BF16), wp_ref[...])
    return y + gate * proj


def _ple_specs(tm, ple_args):
    g, b, p2d, wg, wp = ple_args
    ign = lambda f: (lambda *idx: f(idx[0]))
    full = lambda a: pl.BlockSpec(a.shape, ign(lambda i: (0, 0)))
    return [full(g), full(b), pl.BlockSpec((tm, D_PLE), ign(lambda i: (i, 0))), full(wg), full(wp)]


def _ffn_ple_kernel(xb_ref, wg_ref, wu_ref, wd_ref, x_ref, *rest):
    *ple_refs, o_ref, acc_ref = rest
    c = pl.program_id(1)
    xb = xb_ref[...]
    gate = _dot(xb, wg_ref[...])
    up = _dot(xb, wu_ref[...])
    h = gate * jax.nn.sigmoid(gate) * up
    contrib = _dot(h.astype(BF16), wd_ref[...])

    @pl.when(c == 0)
    def _():
        acc_ref[...] = contrib

    @pl.when(c != 0)
    def _():
        acc_ref[...] += contrib

    @pl.when(c == pl.num_programs(1) - 1)
    def _():
        o_ref[...] = _ln_ple(x_ref[...], acc_ref[...], ple_refs)


def _ffn_ple(xb, x2d, w_gu, w_d, ple_args, tm=512, fc=1408):
    t = xb.shape[0]
    f = w_d.shape[0]
    ncf = f // fc
    row = lambda n: pl.BlockSpec((tm, n), lambda i, c: (i, 0))
    return pl.pallas_call(
        _ffn_ple_kernel,
        grid=(t // tm, ncf),
        in_specs=[
            row(D_MODEL),
            pl.BlockSpec((D_MODEL, fc), lambda i, c: (0, c)),
            pl.BlockSpec((D_MODEL, fc), lambda i, c: (0, ncf + c)),
            pl.BlockSpec((fc, D_MODEL), lambda i, c: (c, 0)),
            row(D_MODEL),
        ] + _ple_specs(tm, ple_args),
        out_specs=row(D_MODEL),
        out_shape=jax.ShapeDtypeStruct((t, D_MODEL), F32),
        scratch_shapes=[pltpu.VMEM((tm, D_MODEL), F32)],
        compiler_params=_cparams("parallel", "arbitrary"),
        name="ffn_dense",
    )(xb, w_gu, w_gu, w_d, x2d, *ple_args)


def _even_mixer(x2d, b, s, w_in, b_f, w_out):
    nf, ns = FOX_W, SB_W
    o_f = 3 * nf
    o_s = o_f + N_HEADS_FOX
    qa, ka, va = w_in[:, :nf], w_in[:, nf:2 * nf], w_in[:, 2 * nf:o_f]
    qs, ks, vs = w_in[:, o_s:o_s + ns], w_in[:, o_s + ns:o_s + 2 * ns], w_in[:, o_s + 2 * ns:]
    w_k = jnp.concatenate([ka, ks], axis=1).astype(BF16)
    w_t = jnp.concatenate([qa * Q_SCALE, va, qs * Q_SCALE, vs], axis=1).T.astype(BF16)
    w_f = jnp.pad(w_in[:, o_f:o_s], ((0, 0), (0, LANES - N_HEADS_FOX))).astype(BF16)
    bias_f = jnp.pad(b_f, (0, LANES - N_HEADS_FOX)).reshape(1, LANES)
    k2d, ht, lf = _ab_proj(x2d, w_k, w_t, w_f, bias_f, b, s)
    lf_t = lf.reshape(b, s, LANES)[:, :, :N_HEADS_FOX].transpose(0, 2, 1)
    crow, caug = _cumsum_seq(lf_t)
    k3 = k2d.reshape(b, s, -1)
    oa = _fox_attention(k3, ht, caug, crow)
    ob = _sb_attention(k3, ht)
    w_out_b = w_out.astype(BF16)
    return [oa.reshape(b * s, nf), ob.reshape(b * s, ns)], [w_out_b[:nf], w_out_b[nf:]]


def _odd_mixer(x2d, b, s, w_qkv, sinks, w_out, tables):
    nq = N_HEADS_SWA * HEAD_DIM
    wq = w_qkv[:, :nq]
    wk = [w_qkv[:, nq + i * HEAD_DIM: nq + (i + 1) * HEAD_DIM] for i in range(N_KV_SWA)]
    wv = [w_qkv[:, nq + (N_KV_SWA + i) * HEAD_DIM: nq + (N_KV_SWA + i + 1) * HEAD_DIM] for i in range(N_KV_SWA)]
    dup = lambda ws: [w for w in ws for _ in range(2)]
    w_k = jnp.concatenate(dup(wk), axis=1).astype(BF16)
    w_t = jnp.concatenate([wq * Q_SCALE] + dup(wv), axis=1).T.astype(BF16)
    k2d, qt, vt = _swa_proj(x2d, w_k, w_t, tables, b, s)
    o = _swa_attention(qt, k2d.reshape(b, s, -1), vt, sinks)
    return [o.reshape(b * s, nq)], [w_out.astype(BF16)]


def kernel(x, p, ln_mix_g, ln_mix_b, ln_ffn_g, ln_ffn_b, ab_w_in, ab_b_f, ab_w_out, c_w_qkv, c_sinks, c_w_out,
           ffn_w_gate_up, ffn_w_down, router_w, router_b, moe_w_gate_up, moe_w_down, ple_w_gate, ple_w_proj):
    b, s, d = x.shape
    t = b * s
    x2d = x.reshape(t, d)
    tables = _rope_tables(s)
    row = lambda v: v.reshape(1, -1)
    moe_gu = moe_w_gate_up.reshape((-1,) + moe_w_gate_up.shape[2:])
    moe_d = moe_w_down.reshape((-1,) + moe_w_down.shape[2:])
    for i in range(DEPTH):
        j = i // 2
        if i % 2 == 0:
            os_, ws = _even_mixer(x2d, b, s, ab_w_in[j], ab_b_f[j], ab_w_out[j])
        else:
            os_, ws = _odd_mixer(x2d, b, s, c_w_qkv[j], c_sinks[j], c_w_out[j], tables)
        x2d, xb = _out_proj_ln(os_, ws, x2d, row(ln_mix_g[i]), row(ln_mix_b[i]))
        ple_args = (row(ln_ffn_g[i]), row(ln_ffn_b[i]), p[i].reshape(t, D_PLE),
                    ple_w_gate[i].astype(BF16), ple_w_proj[i].astype(BF16))
        if i % 2 == 0:
            x2d = _ffn_ple(xb, x2d, ffn_w_gate_up[j].astype(BF16), ffn_w_down[j].astype(BF16), ple_args)
        else:
            rw = jnp.pad(router_w[j], ((0, 0), (0, LANES - N_EXPERTS)))
            rb = jnp.pad(router_b[j], (0, LANES - N_EXPERTS)).reshape(1, LANES)
            combine = _router(x2d, rw, rb)
            x2d = _moe(xb, x2d, combine, moe_gu, moe_d, j * N_EXPERTS, ple_args)
    return x2d.reshape(b, s, d)
```

```python
import functools
import math

import jax
import jax.numpy as jnp
from jax import lax
from jax.experimental import pallas as pl
from jax.experimental.pallas import tpu as pltpu

F32 = jnp.float32
BF16 = jnp.bfloat16

D_MODEL = 1024
HEAD_DIM = 64
LANES = 128
N_HEADS_FOX = 8
N_HEADS_SB = 8
N_HEADS_SWA = 16
N_KV_SWA = 2
WINDOW = 128
ROPE_THETA = 500000.0
ROPE_DIM = HEAD_DIM // 4
N_EXPERTS = 8
D_PLE = 256
LN_EPS = 1e-5
DEPTH = 4
DEEPNORM_ALPHA = (2.0 * DEPTH) ** 0.25
ATTN_SCALE = HEAD_DIM ** -0.5
FOX_W = N_HEADS_FOX * HEAD_DIM
SB_W = N_HEADS_SB * HEAD_DIM
VMEM_LIMIT = 56 * 1024 * 1024

NEG_INF = float("-inf")
LOG2E = math.log2(math.e)
Q_SCALE = ATTN_SCALE * LOG2E
SUM_ROWS = 16
SWA_HEAD_BATCH = 4
MOE_TM = 512
MOE_SR = 128
MOE_FC = 512
MOE_CS = 256
MOE_SLOTS = N_EXPERTS * (MOE_TM // MOE_CS + 1)
MOE_RING = 4


def _cparams(*sem):
    return pltpu.CompilerParams(dimension_semantics=sem, vmem_limit_bytes=VMEM_LIMIT)


def _dot(a, b):
    return jnp.dot(a, b, preferred_element_type=F32)


def _dot_nt(a, b):
    return lax.dot_general(a, b, (((1,), (1,)), ((), ())), preferred_element_type=F32)


def _log_sigmoid(x):
    return jnp.minimum(x, 0.0) - jnp.log1p(jnp.exp(-jnp.abs(x)))


def _head_masks(shape):
    lane = lax.broadcasted_iota(jnp.int32, shape, len(shape) - 1)
    return lane < HEAD_DIM, lane >= HEAD_DIM


def _ab_proj_kernel(x_ref, wk_ref, wt_ref, wf_ref, bf_ref, k_ref, ht_ref, lf_ref):
    xb = x_ref[...].astype(BF16)
    k_ref[...] = _dot(xb, wk_ref[...]).astype(BF16)
    ht_ref[0] = _dot_nt(wt_ref[...], xb).astype(BF16)
    lf_ref[...] = _log_sigmoid(_dot(xb, wf_ref[...]) + bf_ref[...])


def _ab_proj(x2d, wk, wt, wf, bf, b, s, tm=512):
    t = x2d.shape[0]
    nk = wk.shape[1]
    nt = wt.shape[0]
    spt = s // tm
    return pl.pallas_call(
        _ab_proj_kernel,
        grid=(t // tm,),
        in_specs=[
            pl.BlockSpec((tm, D_MODEL), lambda i: (i, 0)),
            pl.BlockSpec((D_MODEL, nk), lambda i: (0, 0)),
            pl.BlockSpec((nt, D_MODEL), lambda i: (0, 0)),
            pl.BlockSpec((D_MODEL, LANES), lambda i: (0, 0)),
            pl.BlockSpec((1, LANES), lambda i: (0, 0)),
        ],
        out_specs=[
            pl.BlockSpec((tm, nk), lambda i: (i, 0)),
            pl.BlockSpec((1, nt, tm), lambda i: (i // spt, 0, i % spt)),
            pl.BlockSpec((tm, LANES), lambda i: (i, 0)),
        ],
        out_shape=[
            jax.ShapeDtypeStruct((t, nk), BF16),
            jax.ShapeDtypeStruct((b, nt, s), BF16),
            jax.ShapeDtypeStruct((t, LANES), F32),
        ],
        compiler_params=_cparams("parallel"),
        name="ab_proj",
    )(x2d, wk, wt, wf, bf)


def _split3(v):
    hi = v.astype(BF16).astype(F32)
    r1 = v - hi
    mid = r1.astype(BF16).astype(F32)
    return hi, mid, r1 - mid


def _cumsum_kernel(x_ref, c_ref, aug_ref, st_ref):
    x = x_ref[0]
    s = x.shape[1]
    lane = lax.broadcasted_iota(jnp.int32, x.shape, 1) & (LANES - 1)
    sh = 1
    while sh < LANES:
        x = x + jnp.where(lane >= sh, pltpu.roll(x, sh, axis=1), 0.0)
        sh *= 2
    carry = jnp.zeros((x.shape[0], 1), F32)
    for c in range(s // LANES):
        blk = x[:, c * LANES:(c + 1) * LANES] + carry
        c_ref[0, :, c * LANES:(c + 1) * LANES] = blk
        carry = blk[:, LANES - 1:LANES]

    ones = jnp.ones((3, s), F32)
    st_ref[...] = jnp.zeros(st_ref.shape, F32)
    for p in range(x.shape[0] // 2):
        for which, base in ((0, HEAD_DIM), (1, 0)):
            negc = c_ref[0, 2 * p + which:2 * p + which + 1, :] * -LOG2E
            hi, mid, lo = _split3(negc)
            st_ref[base:base + 1, :] = hi
            st_ref[base + 1:base + 2, :] = mid
            st_ref[base + 2:base + 3, :] = lo
            st_ref[base + 3:base + 6, :] = ones
        for c in range(s // LANES):
            blk = st_ref[:, c * LANES:(c + 1) * LANES]
            aug_ref[0, p, c * LANES:(c + 1) * LANES, :] = blk.T.astype(BF16)


def _cumsum_seq(lf_t):
    b, h, s = lf_t.shape
    return pl.pallas_call(
        _cumsum_kernel,
        grid=(b,),
        in_specs=[pl.BlockSpec((1, h, s), lambda i: (i, 0, 0))],
        out_specs=[
            pl.BlockSpec((1, h, s), lambda i: (i, 0, 0)),
            pl.BlockSpec((1, h // 2, s, LANES), lambda i: (i, 0, 0, 0)),
        ],
        out_shape=[
            jax.ShapeDtypeStruct((b, h, s), F32),
            jax.ShapeDtypeStruct((b, h // 2, s, LANES), BF16),
        ],
        scratch_shapes=[pltpu.VMEM((LANES, s), F32)],
        compiler_params=_cparams("parallel"),
        name="fox_cumsum",
    )(lf_t)


def _fox_kernel(qt_ref, k_ref, aug_ref, vt_ref, c_ref, o_ref, m_ref, acc_ref, sa_ref, sb_ref, *, tq, tk):
    hp = pl.program_id(1)
    qi = pl.program_id(2)
    q0 = qi * tq
    qt = qt_ref[0].astype(F32)
    sub = lax.broadcasted_iota(jnp.int32, (LANES, tq), 0)
    lane = lax.broadcasted_iota(jnp.int32, (tk, LANES), 1)
    vsub = lax.broadcasted_iota(jnp.int32, (LANES, tk), 0)
    own_rows = [sub < HEAD_DIM, sub >= HEAD_DIM]
    own_lanes = [lane < HEAD_DIM, lane >= HEAD_DIM]
    own_vrows = [vsub < HEAD_DIM, vsub >= HEAD_DIM]

    qaug = []
    for r in range(2):
        base = HEAD_DIM * (1 - r)
        ci = c_ref[0, pl.ds(2 * hp + r, 1), :]
        hi, mid, lo = _split3(ci * LOG2E)
        t = jnp.where(own_rows[r], qt, 0.0)
        t = jnp.where((sub >= base) & (sub < base + 3), 1.0, t)
        t = jnp.where(sub == base + 3, hi, t)
        t = jnp.where(sub == base + 4, mid, t)
        t = jnp.where(sub == base + 5, lo, t)
        qaug.append(t.astype(BF16))

    m_ref[...] = jnp.full(m_ref.shape, NEG_INF, F32)
    acc_ref[...] = jnp.zeros(acc_ref.shape, F32)

    def scores(k0, s_ref, lo=0):
        kt = k_ref[0, pl.ds(k0, tk), :]
        ca = aug_ref[0, 0, pl.ds(k0, tk), :]
        for r in range(2):
            ka = jnp.where(own_lanes[r], kt, ca)
            s_ref[r, :, lo:] = _dot(ka, qaug[r][:, lo:])

    def softmax_pv(k0, s_ref, masked, lo=0):
        vt = vt_ref[0, :, pl.ds(k0, tk)]
        if masked:
            key = k0 + lax.broadcasted_iota(jnp.int32, (tk, tq - lo), 0)
            qry = q0 + lo + lax.broadcasted_iota(jnp.int32, (tk, tq - lo), 1)
            keep = key <= qry
        for r in range(2):
            st = s_ref[r, :, lo:]
            if masked:
                st = jnp.where(keep, st, NEG_INF)
            m_prev = m_ref[r, :, lo:]
            m_new = jnp.maximum(m_prev, jnp.max(st, axis=0, keepdims=True))
            alpha = jnp.exp2(m_prev - m_new)
            pt = jnp.exp2(st - m_new).astype(BF16)
            va = jnp.where(own_vrows[r], vt, 1.0)
            acc_ref[r, :, lo:] = alpha * acc_ref[r, :, lo:] + _dot(va, pt)
            m_ref[r, :, lo:] = m_new

    tile = lambda j: pl.multiple_of(j * tk, tk)
    scores(tile(0), sa_ref)

    def pair(i):
        scores(tile(2 * i + 1), sb_ref)
        softmax_pv(tile(2 * i), sa_ref, False)
        scores(tile(2 * i + 2), sa_ref)
        softmax_pv(tile(2 * i + 1), sb_ref, False)

    def body(j, carry):
        pair(2 * j)
        pair(2 * j + 1)
        return carry

    lax.fori_loop(0, qi // 2, body, 0)

    @pl.when(qi % 2 == 1)
    def _():
        pair(qi - 1)

    scores(tile(2 * qi + 1), sb_ref, lo=tk)
    softmax_pv(tile(2 * qi), sa_ref, True)
    softmax_pv(tile(2 * qi + 1), sb_ref, True, lo=tk)

    ot = jnp.zeros((LANES, tq), F32)
    for r in range(2):
        a = acc_ref[r]
        den = a[HEAD_DIM * (1 - r):HEAD_DIM * (1 - r) + 1, :]
        ot = ot + jnp.where(own_rows[r], a / den, 0.0)
    o_ref[0] = ot.T.astype(BF16)


def _fox_attention(k3, ht, caug, crow, tq=512):
    b, s, _ = k3.shape
    nhp = N_HEADS_FOX // 2
    tk = tq // 2
    kern = functools.partial(_fox_kernel, tq=tq, tk=tk)
    return pl.pallas_call(
        kern,
        grid=(b, nhp, s // tq),
        in_specs=[
            pl.BlockSpec((1, LANES, tq), lambda bi, hp, qi: (bi, hp, qi)),
            pl.BlockSpec((1, s, LANES), lambda bi, hp, qi: (bi, 0, hp)),
            pl.BlockSpec((1, 1, s, LANES), lambda bi, hp, qi: (bi, hp, 0, 0)),
            pl.BlockSpec((1, LANES, s), lambda bi, hp, qi: (bi, nhp + hp, 0)),
            pl.BlockSpec((1, N_HEADS_FOX, tq), lambda bi, hp, qi: (bi, 0, qi)),
        ],
        out_specs=pl.BlockSpec((1, tq, LANES), lambda bi, hp, qi: (bi, qi, hp)),
        out_shape=jax.ShapeDtypeStruct((b, s, FOX_W), BF16),
        scratch_shapes=[
            pltpu.VMEM((2, 1, tq), F32),
            pltpu.VMEM((2, LANES, tq), F32),
            pltpu.VMEM((2, tk, tq), F32),
            pltpu.VMEM((2, tk, tq), F32),
        ],
        compiler_params=_cparams("parallel", "parallel", "arbitrary"),
        name="fox_attn",
    )(ht, k3, caug, ht, crow)


def _sb_kernel(qt_ref, k_ref, vt_ref, o_ref, run_ref, acc_ref, za_ref, zb_ref, *, tq, cw):
    qi = pl.program_id(2)
    q0 = qi * tq
    qt = qt_ref[0]
    sub = lax.broadcasted_iota(jnp.int32, (LANES, tq), 0)
    own_rows = [sub < HEAD_DIM, sub >= HEAD_DIM]
    qm = [jnp.where(mk, qt, 0.0) for mk in own_rows]

    ss = lax.broadcasted_iota(jnp.int32, (cw, cw), 0)
    jj = lax.broadcasted_iota(jnp.int32, (cw, cw), 1)
    tri = jnp.where(jj > ss, 1.0, 0.0).astype(BF16)
    tri = jnp.concatenate([tri, jnp.ones((SUM_ROWS, cw), BF16)], axis=0)

    run_ref[...] = jnp.zeros(run_ref.shape, F32)
    acc_ref[...] = jnp.zeros(acc_ref.shape, F32)

    def start_of(c):
        return pl.multiple_of(jnp.maximum(q0 + tq - (c + 1) * cw, 0), cw)

    def logits(c, z_ref, lo=0):
        kt = k_ref[0, pl.ds(start_of(c), cw), :]
        for r in range(2):
            z_ref[r, :, lo:] = _dot(kt, qm[r][:, lo:])

    def weights_pv(c, z_ref, masked, lo=0):
        k0 = start_of(c)
        vt = vt_ref[0, :, pl.ds(k0, cw)]
        if masked:
            key = k0 + lax.broadcasted_iota(jnp.int32, (cw, tq - lo), 0)
            qry = q0 + lo + lax.broadcasted_iota(jnp.int32, (cw, tq - lo), 1)
            keep = key < qry
        zs, sps, laters, ws = [], [], [], []
        for r in range(2):
            z = z_ref[r, :, lo:]
            e = jnp.exp2(-jnp.abs(z))
            sp = jnp.maximum(z, 0.0) + jnp.log(1.0 + e) * LOG2E
            if masked:
                sp = jnp.where(keep, sp, 0.0)
            zs.append(z)
            sps.append(sp)
        for r in range(2):
            laters.append(_dot(tri, sps[r].astype(BF16)))
        for r in range(2):
            run = run_ref[r, :, lo:]
            a = jnp.exp2((zs[r] - sps[r]) - (laters[r][:cw] + run))
            if masked:
                a = jnp.where(keep, a, 0.0)
            ws.append(a.astype(BF16))
            run_ref[r, :, lo:] = run + laters[r][cw:cw + 1]
        for r in range(2):
            acc_ref[r, :, lo:] = acc_ref[r, :, lo:] + _dot(vt, ws[r])

    logits(0, za_ref, lo=cw)
    logits(1, zb_ref)
    weights_pv(0, za_ref, True, lo=cw)
    logits(2, za_ref)
    weights_pv(1, zb_ref, True)

    def body(i, carry):
        logits(2 * i + 1, zb_ref)
        weights_pv(2 * i, za_ref, False)
        logits(2 * i + 2, za_ref)
        weights_pv(2 * i + 1, zb_ref, False)
        return carry

    lax.fori_loop(1, qi + 1, body, 0)

    ot = jnp.where(own_rows[0], acc_ref[0], acc_ref[1])
    o_ref[0] = ot.T.astype(BF16)


def _sb_attention(k3, ht, tq=512):
    b, s, _ = k3.shape
    cw = tq // 2
    nhp = N_HEADS_SB // 2
    kern = functools.partial(_sb_kernel, tq=tq, cw=cw)
    return pl.pallas_call(
        kern,
        grid=(b, nhp, s // tq),
        in_specs=[
            pl.BlockSpec((1, LANES, tq), lambda bi, hp, qi: (bi, 2 * nhp + hp, qi)),
            pl.BlockSpec((1, s, LANES), lambda bi, hp, qi: (bi, 0, nhp + hp)),
            pl.BlockSpec((1, LANES, s), lambda bi, hp, qi: (bi, 3 * nhp + hp, 0)),
        ],
        out_specs=pl.BlockSpec((1, tq, LANES), lambda bi, hp, qi: (bi, qi, hp)),
        out_shape=jax.ShapeDtypeStruct((b, s, SB_W), BF16),
        scratch_shapes=[
            pltpu.VMEM((2, 1, tq), F32),
            pltpu.VMEM((2, LANES, tq), F32),
            pltpu.VMEM((2, cw, tq), F32),
            pltpu.VMEM((2, cw, tq), F32),
        ],
        compiler_params=_cparams("parallel", "parallel", "arbitrary"),
        name="sb_attn",
    )(ht, k3, ht)


def _swa_proj_kernel(x_ref, wk_ref, wt_ref, cos_ref, sa_ref, sb_ref, cost_ref, sint_ref, k_ref, qt_ref, vt_ref):
    xb = x_ref[...].astype(BF16)
    half = ROPE_DIM // 2
    yk = _dot(xb, wk_ref[...])
    for c in range(yk.shape[1] // LANES):
        blk = yk[:, c * LANES:(c + 1) * LANES]
        blk = (blk * cos_ref[...] + pltpu.roll(blk, half, axis=1) * sa_ref[...]
               + pltpu.roll(blk, LANES - half, axis=1) * sb_ref[...])
        k_ref[:, c * LANES:(c + 1) * LANES] = blk.astype(BF16)
    yt = _dot_nt(wt_ref[...], xb)
    cos = cost_ref[...]
    sin = sint_ref[...]
    nq = N_HEADS_SWA * HEAD_DIM
    for h in range(N_HEADS_SWA):
        base = h * HEAD_DIM
        x1 = yt[base:base + half]
        x2 = yt[base + half:base + ROPE_DIM]
        rot = jnp.concatenate([x1 * cos - x2 * sin, x2 * cos + x1 * sin], axis=0)
        qt_ref[0, base:base + ROPE_DIM, :] = rot.astype(BF16)
        qt_ref[0, base + ROPE_DIM:base + HEAD_DIM, :] = yt[base + ROPE_DIM:base + HEAD_DIM].astype(BF16)
    vt_ref[0] = yt[nq:].astype(BF16)


def _rope_tables(s):
    half = ROPE_DIM // 2
    inv = ROPE_THETA ** (-jnp.arange(half, dtype=F32) * 2.0 / ROPE_DIM)
    ang = jnp.arange(s, dtype=F32)[:, None] * inv[None, :]
    cos, sin = jnp.cos(ang), jnp.sin(ang)
    d = jnp.arange(LANES) % HEAD_DIM
    idx = d % half
    cos_t = jnp.where(d[None, :] < ROPE_DIM, cos[:, idx], 1.0)
    sin_a = jnp.where((d[None, :] >= half) & (d[None, :] < ROPE_DIM), sin[:, idx], 0.0)
    sin_b = jnp.where(d[None, :] < half, -sin[:, idx], 0.0)
    return cos_t.astype(F32), sin_a.astype(F32), sin_b.astype(F32), cos.T.astype(F32), sin.T.astype(F32)


def _swa_proj(x2d, wk, wt, tables, b, s, tm=512):
    t = x2d.shape[0]
    nk = wk.shape[1]
    nq = N_HEADS_SWA * HEAD_DIM
    nv = wt.shape[0] - nq
    spt = s // tm
    half = ROPE_DIM // 2
    lane_tab = pl.BlockSpec((tm, LANES), lambda i: (i % spt, 0))
    row_tab = pl.BlockSpec((half, tm), lambda i: (0, i % spt))
    return pl.pallas_call(
        _swa_proj_kernel,
        grid=(t // tm,),
        in_specs=[
            pl.BlockSpec((tm, D_MODEL), lambda i: (i, 0)),
            pl.BlockSpec(wk.shape, lambda i: (0, 0)),
            pl.BlockSpec(wt.shape, lambda i: (0, 0)),
            lane_tab, lane_tab, lane_tab, row_tab, row_tab,
        ],
        out_specs=[
            pl.BlockSpec((tm, nk), lambda i: (i, 0)),
            pl.BlockSpec((1, nq, tm), lambda i: (i // spt, 0, i % spt)),
            pl.BlockSpec((1, nv, tm), lambda i: (i // spt, 0, i % spt)),
        ],
        out_shape=[
            jax.ShapeDtypeStruct((t, nk), BF16),
            jax.ShapeDtypeStruct((b, nq, s), BF16),
            jax.ShapeDtypeStruct((b, nv, s), BF16),
        ],
        compiler_params=_cparams("parallel"),
        name="swa_proj",
    )(x2d, wk, wt, *tables)


def _swa_kernel(sink_ref, qt_ref, k_ref, vt_ref, o_ref, *, tq):
    qi = pl.program_id(1)
    q0 = qi * tq
    kw = tq + WINDOW
    kstart = pl.multiple_of(jnp.maximum(q0 - WINDOW, 0), WINDOW)
    kwin = k_ref[0, pl.ds(kstart, kw), :]
    vwin = vt_ref[0, :, pl.ds(kstart, kw)]
    key = kstart + lax.broadcasted_iota(jnp.int32, (kw, tq), 0)
    qry = q0 + lax.broadcasted_iota(jnp.int32, (kw, tq), 1)
    diff = qry - key
    keep = (diff >= 0) & (diff < WINDOW)
    sub = lax.broadcasted_iota(jnp.int32, (LANES, tq), 0)
    vsub = lax.broadcasted_iota(jnp.int32, (LANES, kw), 0)
    own_rows = [sub < HEAD_DIM, sub >= HEAD_DIM]
    own_vrows = [vsub < HEAD_DIM, vsub >= HEAD_DIM]
    group = N_HEADS_SWA // N_KV_SWA

    def scores(h):
        c, r = divmod(h, 2)
        g = h // group
        qblk = qt_ref[0, c * LANES:(c + 1) * LANES, :]
        return _dot(kwin[:, g * LANES:(g + 1) * LANES], jnp.where(own_rows[r], qblk, 0.0))

    def weights(h, st):
        st = jnp.where(keep, st, NEG_INF)
        sink = sink_ref[h] * LOG2E
        mx = jnp.maximum(jnp.max(st, axis=0, keepdims=True), sink)
        return jnp.exp2(st - mx).astype(BF16), jnp.exp2(sink - mx)

    def values(h, pt):
        r = h % 2
        g = h // group
        va = jnp.where(own_vrows[r], vwin[g * LANES:(g + 1) * LANES, :], 1.0)
        return _dot(va, pt)

    batches = [list(range(i, i + SWA_HEAD_BATCH)) for i in range(0, N_HEADS_SWA, SWA_HEAD_BATCH)]
    sts = [scores(h) for h in batches[0]]
    for bi, heads in enumerate(batches):
        nxt = [scores(h) for h in batches[bi + 1]] if bi + 1 < len(batches) else None
        pts = [weights(h, st) for h, st in zip(heads, sts)]
        accs = [values(h, pt) for h, (pt, _) in zip(heads, pts)]
        outs = []
        for h, acc, (_, esink) in zip(heads, accs, pts):
            r = h % 2
            den = acc[HEAD_DIM * (1 - r):HEAD_DIM * (1 - r) + 1, :] + esink
            outs.append(acc / den)
            if r == 1:
                c = h // 2
                ot = jnp.where(own_rows[0], outs[-2], outs[-1])
                o_ref[0, :, c * LANES:(c + 1) * LANES] = ot.T.astype(BF16)
        sts = nxt


def _swa_attention(qt3, k3, vt3, sinks, tq=256):
    b, nq, s = qt3.shape
    kern = functools.partial(_swa_kernel, tq=tq)
    return pl.pallas_call(
        kern,
        grid=(b, s // tq),
        in_specs=[
            pl.BlockSpec(memory_space=pltpu.SMEM),
            pl.BlockSpec((1, nq, tq), lambda bi, qi: (bi, 0, qi)),
            pl.BlockSpec((1, s, k3.shape[2]), lambda bi, qi: (bi, 0, 0)),
            pl.BlockSpec((1, vt3.shape[1], s), lambda bi, qi: (bi, 0, 0)),
        ],
        out_specs=pl.BlockSpec((1, tq, nq), lambda bi, qi: (bi, qi, 0)),
        out_shape=jax.ShapeDtypeStruct((b, s, nq), BF16),
        compiler_params=_cparams("parallel", "arbitrary"),
        name="swa_attn",
    )(sinks, qt3, k3, vt3)


def _layer_norm(z, g, b):
    mu = jnp.mean(z, axis=-1, keepdims=True)
    zc = z - mu
    var = jnp.mean(zc * zc, axis=-1, keepdims=True)
    return zc * lax.rsqrt(var + LN_EPS) * g + b


def _outln_kernel(*refs, n_in, routed):
    o_refs = refs[:n_in]
    w_refs = refs[n_in:2 * n_in]
    x_ref, g_ref, b_ref = refs[2 * n_in:2 * n_in + 3]
    rest = refs[2 * n_in + 3:]
    h = _dot(o_refs[0][...], w_refs[0][...])
    for i in range(1, n_in):
        h = h + _dot(o_refs[i][...], w_refs[i][...])
    y = _layer_norm(DEEPNORM_ALPHA * x_ref[...] + h, g_ref[...], b_ref[...])
    if routed:
        rw_ref, rb_ref, y_ref, yb_ref, c_ref = rest
        c_ref[...] = _route(y, rw_ref[...], rb_ref[...])
    else:
        y_ref, yb_ref = rest
    y_ref[...] = y
    yb_ref[...] = y.astype(BF16)


def _out_proj_ln(os_, ws, x2d, g, b, router=None, tm=512):
    t = x2d.shape[0]
    n_in = len(os_)
    row = lambda n: pl.BlockSpec((tm, n), lambda i: (i, 0))
    full = lambda a: pl.BlockSpec(a.shape, lambda i: (0, 0))
    in_specs = [row(o.shape[1]) for o in os_] + [full(w) for w in ws] + [row(D_MODEL), full(g), full(b)]
    out_specs = [row(D_MODEL)] * 2
    out_shape = [jax.ShapeDtypeStruct((t, D_MODEL), F32), jax.ShapeDtypeStruct((t, D_MODEL), BF16)]
    args = [*os_, *ws, x2d, g, b]
    if router is not None:
        in_specs += [full(a) for a in router]
        out_specs.append(row(LANES))
        out_shape.append(jax.ShapeDtypeStruct((t, LANES), F32))
        args += list(router)
    return pl.pallas_call(
        functools.partial(_outln_kernel, n_in=n_in, routed=router is not None),
        grid=(t // tm,),
        in_specs=in_specs,
        out_specs=out_specs,
        out_shape=out_shape,
        compiler_params=_cparams("parallel"),
        name="out_proj_ln",
    )(*args)


def _route(x, w, b):
    xh = x.astype(BF16)
    xl = (x - xh.astype(F32)).astype(BF16)
    wh = w.astype(BF16)
    wl = (w - wh.astype(F32)).astype(BF16)
    logits = _dot(xh, wh) + (_dot(xl, wh) + _dot(xh, wl)) + b
    lane = lax.broadcasted_iota(jnp.int32, logits.shape, 1)
    logits = jnp.where(lane < N_EXPERTS, logits, NEG_INF)
    m1 = jnp.max(logits, axis=1, keepdims=True)
    i1 = jnp.min(jnp.where(logits == m1, lane, LANES), axis=1, keepdims=True)
    rest = jnp.where(lane == i1, NEG_INF, logits)
    m2 = jnp.max(rest, axis=1, keepdims=True)
    i2 = jnp.min(jnp.where(rest == m2, lane, LANES), axis=1, keepdims=True)
    e2 = jnp.exp(m2 - m1)
    den = 1.0 + e2
    return jnp.where(lane == i1, 1.0 / den, 0.0) + jnp.where(lane == i2, e2 / den, 0.0)


def _ffn_kernel(eid_ref, valid_ref, x_ref, wg_ref, wu_ref, wd_ref, o_ref, acc_ref):
    i = pl.program_id(0)
    c = pl.program_id(1)
    last = c == pl.num_programs(1) - 1
    valid = valid_ref[i] != 0

    @pl.when(valid)
    def _():
        x = x_ref[...]
        gate = _dot(x, wg_ref[0])
        up = _dot(x, wu_ref[0])
        h = gate * jax.nn.sigmoid(gate) * up
        contrib = _dot(h.astype(BF16), wd_ref[0])

        @pl.when(c == 0)
        def _():
            acc_ref[...] = contrib

        @pl.when(c != 0)
        def _():
            acc_ref[...] += contrib

        @pl.when(last)
        def _():
            o_ref[...] = acc_ref[...].astype(o_ref.dtype)

    @pl.when(jnp.logical_not(valid) & last)
    def _():
        o_ref[...] = jnp.zeros(o_ref.shape, o_ref.dtype)


def _ffn(xb, w_gu, w_d, eids, valid, out_dtype, tm, fc, name):
    r = xb.shape[0]
    f = w_d.shape[1]
    ncf = f // fc
    in_specs = [
        pl.BlockSpec((tm, D_MODEL), lambda i, c, eid, ok: (i, 0)),
        pl.BlockSpec((1, D_MODEL, fc), lambda i, c, eid, ok: (eid[i], 0, c * ok[i])),
        pl.BlockSpec((1, D_MODEL, fc), lambda i, c, eid, ok: (eid[i], 0, ncf + c * ok[i])),
        pl.BlockSpec((1, fc, D_MODEL), lambda i, c, eid, ok: (eid[i], c * ok[i], 0)),
    ]
    return pl.pallas_call(
        _ffn_kernel,
        grid_spec=pltpu.PrefetchScalarGridSpec(
            num_scalar_prefetch=2,
            grid=(r // tm, ncf),
            in_specs=in_specs,
            out_specs=pl.BlockSpec((tm, D_MODEL), lambda i, c, eid, ok: (i, 0)),
            scratch_shapes=[pltpu.VMEM((tm, D_MODEL), F32)],
        ),
        out_shape=jax.ShapeDtypeStruct((r, D_MODEL), out_dtype),
        compiler_params=_cparams("parallel", "arbitrary"),
        name=name,
    )(eids, valid, xb, w_gu, w_gu, w_d)


def _moe_ffn_kernel(eid_ref, valid_ref, first_ref, x_ref, wgu_hbm, wd_hbm, o_ref,
                    cg_ref, cu_ref, cd_ref, sg_ref, su_ref, sd_ref, sem_ref, acc_ref, *, fc):
    i = pl.program_id(0)
    e = eid_ref[i]
    f = cd_ref.shape[0]
    n_chunks = f // fc

    def chunk_copies(c, slot):
        return (
            pltpu.make_async_copy(wgu_hbm.at[e, :, pl.ds(c * fc, fc)], sg_ref.at[slot], sem_ref.at[0, slot]),
            pltpu.make_async_copy(wgu_hbm.at[e, :, pl.ds(f + c * fc, fc)], su_ref.at[slot], sem_ref.at[1, slot]),
            pltpu.make_async_copy(wd_hbm.at[e, pl.ds(c * fc, fc), :], sd_ref.at[slot], sem_ref.at[2, slot]),
        )

    def compute(c):
        x = x_ref[...]
        gate = _dot(x, cg_ref[:, c * fc:(c + 1) * fc])
        up = _dot(x, cu_ref[:, c * fc:(c + 1) * fc])
        h = gate * jax.nn.sigmoid(gate) * up
        contrib = _dot(h.astype(BF16), cd_ref[c * fc:(c + 1) * fc, :])
        if c == 0:
            acc_ref[...] = contrib
        else:
            acc_ref[...] += contrib

    valid = valid_ref[i] != 0
    first = first_ref[i] != 0

    @pl.when(valid & first)
    def _():
        for cp in chunk_copies(0, 0):
            cp.start()
        for c in range(n_chunks):
            slot = c % 2
            if c + 1 < n_chunks:
                for cp in chunk_copies(c + 1, 1 - slot):
                    cp.start()
            for cp in chunk_copies(c, slot):
                cp.wait()
            cg_ref[:, c * fc:(c + 1) * fc] = sg_ref[slot].astype(BF16)
            cu_ref[:, c * fc:(c + 1) * fc] = su_ref[slot].astype(BF16)
            cd_ref[c * fc:(c + 1) * fc, :] = sd_ref[slot].astype(BF16)
            compute(c)
        o_ref[...] = acc_ref[...].astype(o_ref.dtype)

    @pl.when(valid & jnp.logical_not(first))
    def _():
        for c in range(n_chunks):
            compute(c)
        o_ref[...] = acc_ref[...].astype(o_ref.dtype)

    @pl.when(jnp.logical_not(valid))
    def _():
        o_ref[...] = jnp.zeros(o_ref.shape, o_ref.dtype)


def _moe_ffn(xs, w_gu, w_d, eids, valid, first, tm=MOE_TM, fc=MOE_FC):
    r = xs.shape[0]
    f = w_d.shape[1]
    return pl.pallas_call(
        functools.partial(_moe_ffn_kernel, fc=fc),
        grid_spec=pltpu.PrefetchScalarGridSpec(
            num_scalar_prefetch=3,
            grid=(r // tm,),
            in_specs=[
                pl.BlockSpec((tm, D_MODEL), lambda i, *_: (i, 0)),
                pl.BlockSpec(memory_space=pl.ANY),
                pl.BlockSpec(memory_space=pl.ANY),
            ],
            out_specs=pl.BlockSpec((tm, D_MODEL), lambda i, *_: (i, 0)),
            scratch_shapes=[
                pltpu.VMEM((D_MODEL, f), BF16),
                pltpu.VMEM((D_MODEL, f), BF16),
                pltpu.VMEM((f, D_MODEL), BF16),
                pltpu.VMEM((2, D_MODEL, fc), F32),
                pltpu.VMEM((2, D_MODEL, fc), F32),
                pltpu.VMEM((2, fc, D_MODEL), F32),
                pltpu.SemaphoreType.DMA((3, 2)),
                pltpu.VMEM((tm, D_MODEL), F32),
            ],
        ),
        out_shape=jax.ShapeDtypeStruct((r, D_MODEL), BF16),
        compiler_params=_cparams("arbitrary"),
        name="ffn_moe",
    )(eids, valid, first, xs, w_gu, w_d)


def _moe_gather_kernel(te_ref, clo_ref, nch_ref, s0_ref, ns_ref, x_ref, pos_ref, o_ref, acc_ref, *, tm, sr, n_chunks):
    r = pl.program_id(0)
    e = te_ref[r]
    sub_row = lax.broadcasted_iota(jnp.int32, (sr, tm), 0)
    acc_ref[...] = jnp.zeros(acc_ref.shape, F32)

    def body(i, carry):
        c = clo_ref[r] + i
        t0 = pl.multiple_of(c * tm, tm)
        p = pos_ref[pl.ds(e, 1), pl.ds(t0, tm)]

        def sub(j, carry2):
            r0 = pl.multiple_of((s0_ref[r * n_chunks + c] + j) * sr, sr)
            onehot = jnp.where(p == r * tm + r0 + sub_row, 1.0, 0.0).astype(BF16)
            acc_ref[pl.ds(r0, sr), :] += _dot(onehot, x_ref[pl.ds(t0, tm), :])
            return carry2

        lax.fori_loop(0, ns_ref[r * n_chunks + c], sub, 0)
        return carry

    lax.fori_loop(0, nch_ref[r], body, 0)
    o_ref[...] = acc_ref[...].astype(BF16)


def _moe_gather(xb, pos_t, tile_e, clo, nch, sub0, nsub, n_rows, tm=MOE_TM, sr=MOE_SR):
    t = xb.shape[0]
    return pl.pallas_call(
        functools.partial(_moe_gather_kernel, tm=tm, sr=sr, n_chunks=t // tm),
        grid_spec=pltpu.PrefetchScalarGridSpec(
            num_scalar_prefetch=5,
            grid=(n_rows // tm,),
            in_specs=[
                pl.BlockSpec((t, D_MODEL), lambda r, *_: (0, 0), pipeline_mode=pl.Buffered(1)),
                pl.BlockSpec(pos_t.shape, lambda r, *_: (0, 0)),
            ],
            out_specs=pl.BlockSpec((tm, D_MODEL), lambda r, *_: (r, 0)),
            scratch_shapes=[pltpu.VMEM((tm, D_MODEL), F32)],
        ),
        out_shape=jax.ShapeDtypeStruct((n_rows, D_MODEL), BF16),
        compiler_params=_cparams("arbitrary"),
        name="moe_gather",
    )(tile_e, clo, nch, sub0, nsub, xb, pos_t)


def _moe_combine_kernel(n_ref, cid_ref, eid_ref, ys_hbm, pos_ref, comb_ref, x_ref, *rest, tm, cs, n_slots, n_buf):
    *ple_refs, o_ref, buf_ref, sem_ref = rest
    i = pl.program_id(0)
    n = n_ref[i]
    base = i * n_slots

    def chunk_copy(k, slot):
        row0 = pl.multiple_of(cid_ref[base + k] * cs, cs)
        return pltpu.make_async_copy(ys_hbm.at[pl.ds(row0, cs), :], buf_ref.at[slot], sem_ref.at[slot])

    o_ref[...] = jnp.zeros(o_ref.shape, F32)

    for d in range(n_buf):

        @pl.when(d < n)
        def _():
            chunk_copy(d, d).start()

    def body(k, carry):
        slot = k % n_buf
        chunk_copy(k, slot).wait()
        e = eid_ref[base + k]
        lane = lax.broadcasted_iota(jnp.int32, pos_ref.shape, 1)
        pos_e = jnp.sum(jnp.where(lane == e, pos_ref[...], 0), axis=1, keepdims=True)
        lane_c = lax.broadcasted_iota(jnp.int32, comb_ref.shape, 1)
        g_e = jnp.sum(jnp.where(lane_c == e, comb_ref[...], 0.0), axis=1, keepdims=True)
        col = cid_ref[base + k] * cs + lax.broadcasted_iota(jnp.int32, (tm, cs), 1)
        onehot = jnp.where(pos_e == col, 1.0, 0.0).astype(BF16)
        o_ref[...] += g_e * _dot(onehot, buf_ref[slot])

        @pl.when(k + n_buf < n)
        def _():
            chunk_copy(k + n_buf, slot).start()

        return carry

    lax.fori_loop(0, n, body, 0)
    o_ref[...] = _ln_ple(x_ref[...], o_ref[...], ple_refs)


def _moe_combine(ys, pos, combine, n_need, cid, eid, x2d, ple_args, tm=MOE_TM, cs=MOE_CS):
    t = pos.shape[0]
    return pl.pallas_call(
        functools.partial(_moe_combine_kernel, tm=tm, cs=cs, n_slots=MOE_SLOTS, n_buf=MOE_RING),
        grid_spec=pltpu.PrefetchScalarGridSpec(
            num_scalar_prefetch=3,
            grid=(t // tm,),
            in_specs=[
                pl.BlockSpec(memory_space=pl.ANY),
                pl.BlockSpec((tm, N_EXPERTS), lambda i, *_: (i, 0)),
                pl.BlockSpec((tm, LANES), lambda i, *_: (i, 0)),
                pl.BlockSpec((tm, D_MODEL), lambda i, *_: (i, 0)),
            ] + _ple_specs(tm, ple_args),
            out_specs=pl.BlockSpec((tm, D_MODEL), lambda i, *_: (i, 0)),
            scratch_shapes=[pltpu.VMEM((MOE_RING, cs, D_MODEL), BF16), pltpu.SemaphoreType.DMA((MOE_RING,))],
        ),
        out_shape=jax.ShapeDtypeStruct((t, D_MODEL), F32),
        compiler_params=_cparams("arbitrary"),
        name="moe_combine",
    )(n_need, cid, eid, ys, pos, combine, x2d, *ple_args)


def _moe_plan(combine, tm=MOE_TM):
    t = combine.shape[0]
    n_tok_tiles = t // tm
    n_row_tiles = (2 * t) // tm + N_EXPERTS
    sel = (combine[:, :N_EXPERTS] != 0.0).astype(jnp.int32)
    csum = jnp.cumsum(sel, axis=0)
    count = csum[-1]
    ntile_e = (count + tm - 1) // tm
    tile_end = jnp.cumsum(ntile_e)
    gstart = (tile_end - ntile_e) * tm
    pos = jnp.where(sel != 0, gstart[None, :] + csum - sel, -1)
    rt = jnp.arange(n_row_tiles, dtype=jnp.int32)
    tile_e = jnp.minimum(jnp.sum(rt[:, None] >= tile_end[None, :], axis=1), N_EXPERTS - 1).astype(jnp.int32)
    valid = (rt < tile_end[-1]).astype(jnp.int32)
    prev_e = jnp.concatenate([jnp.full((1,), -1, jnp.int32), tile_e[:-1]])
    group_head = ((valid != 0) & (tile_e != prev_e)).astype(jnp.int32)
    cb = jnp.concatenate([jnp.zeros((1, N_EXPERTS), jnp.int32), csum[tm - 1::tm]], axis=0)
    off = rt * tm - gstart[tile_e]
    cb_t = cb[:, tile_e].T
    clo = jnp.sum(cb_t[:, 1:] <= off[:, None], axis=1).astype(jnp.int32)
    cend = jnp.sum(cb_t[:, :-1] < (off + tm)[:, None], axis=1).astype(jnp.int32)
    nch = jnp.where(valid != 0, jnp.maximum(cend - clo, 0), 0).astype(jnp.int32)
    clo = jnp.minimum(clo, n_tok_tiles - 1)
    r_lo = jnp.clip(cb_t[:, :-1] - off[:, None], 0, tm)
    r_hi = jnp.clip(cb_t[:, 1:] - off[:, None], 0, tm)
    sub0 = (r_lo // MOE_SR).astype(jnp.int32)
    nsub = jnp.where(r_hi > r_lo, (r_hi + MOE_SR - 1) // MOE_SR - sub0, 0).astype(jnp.int32)
    first = gstart[None, :] + cb[:-1]
    n_ie = cb[1:] - cb[:-1]
    per_e = MOE_SLOTS // N_EXPERTS
    k = jnp.arange(per_e, dtype=jnp.int32)
    c_lo = first // MOE_CS
    c_hi = (first + n_ie - 1) // MOE_CS
    cand = c_lo[..., None] + k
    need = ((n_ie[..., None] > 0) & (cand <= c_hi[..., None])).reshape(n_tok_tiles, MOE_SLOTS)
    cand = cand.reshape(n_tok_tiles, MOE_SLOTS)
    eids = jnp.broadcast_to(jnp.arange(N_EXPERTS, dtype=jnp.int32)[:, None], (N_EXPERTS, per_e)).reshape(-1)
    order = jnp.argsort(jnp.logical_not(need), axis=1, stable=True)
    cid = jnp.where(need, cand, 0)
    cid = jnp.take_along_axis(cid, order, axis=1).reshape(-1).astype(jnp.int32)
    eid = eids[order].reshape(-1).astype(jnp.int32)
    n_need = jnp.sum(need, axis=1).astype(jnp.int32)
    return dict(pos=pos.astype(jnp.int32), tile_e=tile_e, valid=valid, first=group_head, clo=clo, nch=nch,
                sub0=sub0.reshape(-1), nsub=nsub.reshape(-1),
                n_need=n_need, cid=cid, eid=eid, n_rows=n_row_tiles * tm)


def _moe(xb, x2d, combine, w_gu, w_d, first_expert, ple_args):
    plan = _moe_plan(combine)
    xs = _moe_gather(xb, plan["pos"].T, plan["tile_e"], plan["clo"], plan["nch"], plan["sub0"], plan["nsub"],
                     plan["n_rows"])
    ys = _moe_ffn(xs, w_gu, w_d, plan["tile_e"] + first_expert, plan["valid"], plan["first"])
    return _moe_combine(ys, plan["pos"], combine, plan["n_need"], plan["cid"], plan["eid"], x2d, ple_args)


def _ln_ple(x, h, ple_refs):
    g_ref, b_ref, p_ref, wg_ref, wp_ref = ple_refs
    y = _layer_norm(DEEPNORM_ALPHA * x + h, g_ref[...], b_ref[...])
    gate = jax.nn.sigmoid(_dot(y.astype(BF16), wg_ref[...]))
    proj = _dot(p_ref[...].astype(BF16), wp_ref[...])
    return y + gate * proj


def _ple_specs(tm, ple_args):
    g, b, p2d, wg, wp = ple_args
    ign = lambda f: (lambda *idx: f(idx[0]))
    full = lambda a: pl.BlockSpec(a.shape, ign(lambda i: (0, 0)))
    return [full(g), full(b), pl.BlockSpec((tm, D_PLE), ign(lambda i: (i, 0))), full(wg), full(wp)]


def _ffn_ple_kernel(xb_ref, wg_ref, wu_ref, wd_ref, x_ref, *rest):
    *ple_refs, o_ref, acc_ref = rest
    c = pl.program_id(1)
    xb = xb_ref[...]
    gate = _dot(xb, wg_ref[...])
    up = _dot(xb, wu_ref[...])
    h = gate * jax.nn.sigmoid(gate) * up
    contrib = _dot(h.astype(BF16), wd_ref[...])

    @pl.when(c == 0)
    def _():
        acc_ref[...] = contrib

    @pl.when(c != 0)
    def _():
        acc_ref[...] += contrib

    @pl.when(c == pl.num_programs(1) - 1)
    def _():
        o_ref[...] = _ln_ple(x_ref[...], acc_ref[...], ple_refs)


def _ffn_ple(xb, x2d, w_gu, w_d, ple_args, tm=512, fc=1408):
    t = xb.shape[0]
    f = w_d.shape[0]
    ncf = f // fc
    row = lambda n: pl.BlockSpec((tm, n), lambda i, c: (i, 0))
    return pl.pallas_call(
        _ffn_ple_kernel,
        grid=(t // tm, ncf),
        in_specs=[
            row(D_MODEL),
            pl.BlockSpec((D_MODEL, fc), lambda i, c: (0, c)),
            pl.BlockSpec((D_MODEL, fc), lambda i, c: (0, ncf + c)),
            pl.BlockSpec((fc, D_MODEL), lambda i, c: (c, 0)),
            row(D_MODEL),
        ] + _ple_specs(tm, ple_args),
        out_specs=row(D_MODEL),
        out_shape=jax.ShapeDtypeStruct((t, D_MODEL), F32),
        scratch_shapes=[pltpu.VMEM((tm, D_MODEL), F32)],
        compiler_params=_cparams("parallel", "arbitrary"),
        name="ffn_dense",
    )(xb, w_gu, w_gu, w_d, x2d, *ple_args)


def _even_mixer(x2d, b, s, w_in, b_f, w_out):
    nf, ns = FOX_W, SB_W
    o_f = 3 * nf
    o_s = o_f + N_HEADS_FOX
    qa, ka, va = w_in[:, :nf], w_in[:, nf:2 * nf], w_in[:, 2 * nf:o_f]
    qs, ks, vs = w_in[:, o_s:o_s + ns], w_in[:, o_s + ns:o_s + 2 * ns], w_in[:, o_s + 2 * ns:]
    w_k = jnp.concatenate([ka, ks], axis=1).astype(BF16)
    w_t = jnp.concatenate([qa * Q_SCALE, va, qs * Q_SCALE, vs], axis=1).T.astype(BF16)
    w_f = jnp.pad(w_in[:, o_f:o_s], ((0, 0), (0, LANES - N_HEADS_FOX))).astype(BF16)
    bias_f = jnp.pad(b_f, (0, LANES - N_HEADS_FOX)).reshape(1, LANES)
    k2d, ht, lf = _ab_proj(x2d, w_k, w_t, w_f, bias_f, b, s)
    lf_t = lf.reshape(b, s, LANES)[:, :, :N_HEADS_FOX].transpose(0, 2, 1)
    crow, caug = _cumsum_seq(lf_t)
    k3 = k2d.reshape(b, s, -1)
    oa = _fox_attention(k3, ht, caug, crow)
    ob = _sb_attention(k3, ht)
    w_out_b = w_out.astype(BF16)
    return [oa.reshape(b * s, nf), ob.reshape(b * s, ns)], [w_out_b[:nf], w_out_b[nf:]]


def _odd_mixer(x2d, b, s, w_qkv, sinks, w_out, tables):
    nq = N_HEADS_SWA * HEAD_DIM
    wq = w_qkv[:, :nq]
    wk = [w_qkv[:, nq + i * HEAD_DIM: nq + (i + 1) * HEAD_DIM] for i in range(N_KV_SWA)]
    wv = [w_qkv[:, nq + (N_KV_SWA + i) * HEAD_DIM: nq + (N_KV_SWA + i + 1) * HEAD_DIM] for i in range(N_KV_SWA)]
    dup = lambda ws: [w for w in ws for _ in range(2)]
    w_k = jnp.concatenate(dup(wk), axis=1).astype(BF16)
    w_t = jnp.concatenate([wq * Q_SCALE] + dup(wv), axis=1).T.astype(BF16)
    k2d, qt, vt = _swa_proj(x2d, w_k, w_t, tables, b, s)
    o = _swa_attention(qt, k2d.reshape(b, s, -1), vt, sinks)
    return [o.reshape(b * s, nq)], [w_out.astype(BF16)]


def kernel(x, p, ln_mix_g, ln_mix_b, ln_ffn_g, ln_ffn_b, ab_w_in, ab_b_f, ab_w_out, c_w_qkv, c_sinks, c_w_out,
           ffn_w_gate_up, ffn_w_down, router_w, router_b, moe_w_gate_up, moe_w_down, ple_w_gate, ple_w_proj):
    b, s, d = x.shape
    t = b * s
    x2d = x.reshape(t, d)
    tables = _rope_tables(s)
    row = lambda v: v.reshape(1, -1)
    moe_gu = moe_w_gate_up.reshape((-1,) + moe_w_gate_up.shape[2:])
    moe_d = moe_w_down.reshape((-1,) + moe_w_down.shape[2:])
    for i in range(DEPTH):
        j = i // 2
        if i % 2 == 0:
            os_, ws = _even_mixer(x2d, b, s, ab_w_in[j], ab_b_f[j], ab_w_out[j])
        else:
            os_, ws = _odd_mixer(x2d, b, s, c_w_qkv[j], c_sinks[j], c_w_out[j], tables)
        ple_args = (row(ln_ffn_g[i]), row(ln_ffn_b[i]), p[i].reshape(t, D_PLE),
                    ple_w_gate[i].astype(BF16), ple_w_proj[i].astype(BF16))
        if i % 2 == 0:
            x2d, xb = _out_proj_ln(os_, ws, x2d, row(ln_mix_g[i]), row(ln_mix_b[i]))
            x2d = _ffn_ple(xb, x2d, ffn_w_gate_up[j].astype(BF16), ffn_w_down[j].astype(BF16), ple_args)
        else:
            rw = jnp.pad(router_w[j], ((0, 0), (0, LANES - N_EXPERTS)))
            rb = jnp.pad(router_b[j], (0, LANES - N_EXPERTS)).reshape(1, LANES)
            x2d, xb, combine = _out_proj_ln(os_, ws, x2d, row(ln_mix_g[i]), row(ln_mix_b[i]), router=(rw, rb))
            x2d = _moe(xb, x2d, combine, moe_gu, moe_d, j * N_EXPERTS, ple_args)
    return x2d.reshape(b, s, d)
```

```python
import functools
import math

import jax
import jax.numpy as jnp
from jax import lax
from jax.experimental import pallas as pl
from jax.experimental.pallas import tpu as pltpu

F32 = jnp.float32
BF16 = jnp.bfloat16

D_MODEL = 1024
HEAD_DIM = 64
LANES = 128
N_HEADS_FOX = 8
N_HEADS_SB = 8
N_HEADS_SWA = 16
N_KV_SWA = 2
WINDOW = 128
ROPE_THETA = 500000.0
ROPE_DIM = HEAD_DIM // 4
N_EXPERTS = 8
D_PLE = 256
LN_EPS = 1e-5
DEPTH = 4
DEEPNORM_ALPHA = (2.0 * DEPTH) ** 0.25
ATTN_SCALE = HEAD_DIM ** -0.5
FOX_W = N_HEADS_FOX * HEAD_DIM
SB_W = N_HEADS_SB * HEAD_DIM
VMEM_LIMIT = 56 * 1024 * 1024

NEG_INF = float("-inf")
LOG2E = math.log2(math.e)
Q_SCALE = ATTN_SCALE * LOG2E
SUM_ROWS = 16
SWA_HEAD_BATCH = 4
MOE_TM = 512
MOE_SR = 128
MOE_FC = 512
MOE_CS = 256
MOE_SLOTS = N_EXPERTS * (MOE_TM // MOE_CS + 1)
MOE_RING = 4


def _cparams(*sem):
    return pltpu.CompilerParams(dimension_semantics=sem, vmem_limit_bytes=VMEM_LIMIT)


def _dot(a, b):
    return jnp.dot(a, b, preferred_element_type=F32)


def _dot_nt(a, b):
    return lax.dot_general(a, b, (((1,), (1,)), ((), ())), preferred_element_type=F32)


def _log_sigmoid(x):
    return jnp.minimum(x, 0.0) - jnp.log1p(jnp.exp(-jnp.abs(x)))


def _head_masks(shape):
    lane = lax.broadcasted_iota(jnp.int32, shape, len(shape) - 1)
    return lane < HEAD_DIM, lane >= HEAD_DIM


def _ab_proj_kernel(x_ref, wk_ref, wt_ref, wf_ref, bf_ref, k_ref, ht_ref, lf_ref):
    xb = x_ref[...].astype(BF16)
    k_ref[...] = _dot(xb, wk_ref[...]).astype(BF16)
    ht_ref[0] = _dot_nt(wt_ref[...], xb).astype(BF16)
    lf_ref[...] = _log_sigmoid(_dot(xb, wf_ref[...]) + bf_ref[...])


def _ab_proj(x2d, wk, wt, wf, bf, b, s, tm=512):
    t = x2d.shape[0]
    nk = wk.shape[1]
    nt = wt.shape[0]
    spt = s // tm
    return pl.pallas_call(
        _ab_proj_kernel,
        grid=(t // tm,),
        in_specs=[
            pl.BlockSpec((tm, D_MODEL), lambda i: (i, 0)),
            pl.BlockSpec((D_MODEL, nk), lambda i: (0, 0)),
            pl.BlockSpec((nt, D_MODEL), lambda i: (0, 0)),
            pl.BlockSpec((D_MODEL, LANES), lambda i: (0, 0)),
            pl.BlockSpec((1, LANES), lambda i: (0, 0)),
        ],
        out_specs=[
            pl.BlockSpec((tm, nk), lambda i: (i, 0)),
            pl.BlockSpec((1, nt, tm), lambda i: (i // spt, 0, i % spt)),
            pl.BlockSpec((tm, LANES), lambda i: (i, 0)),
        ],
        out_shape=[
            jax.ShapeDtypeStruct((t, nk), BF16),
            jax.ShapeDtypeStruct((b, nt, s), BF16),
            jax.ShapeDtypeStruct((t, LANES), F32),
        ],
        compiler_params=_cparams("parallel"),
        name="ab_proj",
    )(x2d, wk, wt, wf, bf)


def _split3(v):
    hi = v.astype(BF16).astype(F32)
    r1 = v - hi
    mid = r1.astype(BF16).astype(F32)
    return hi, mid, r1 - mid


def _cumsum_kernel(x_ref, c_ref, aug_ref, st_ref):
    x = x_ref[0]
    s = x.shape[1]
    lane = lax.broadcasted_iota(jnp.int32, x.shape, 1) & (LANES - 1)
    sh = 1
    while sh < LANES:
        x = x + jnp.where(lane >= sh, pltpu.roll(x, sh, axis=1), 0.0)
        sh *= 2
    carry = jnp.zeros((x.shape[0], 1), F32)
    for c in range(s // LANES):
        blk = x[:, c * LANES:(c + 1) * LANES] + carry
        c_ref[0, :, c * LANES:(c + 1) * LANES] = blk
        carry = blk[:, LANES - 1:LANES]

    ones = jnp.ones((3, s), F32)
    st_ref[...] = jnp.zeros(st_ref.shape, F32)
    for p in range(x.shape[0] // 2):
        for which, base in ((0, HEAD_DIM), (1, 0)):
            negc = c_ref[0, 2 * p + which:2 * p + which + 1, :] * -LOG2E
            hi, mid, lo = _split3(negc)
            st_ref[base:base + 1, :] = hi
            st_ref[base + 1:base + 2, :] = mid
            st_ref[base + 2:base + 3, :] = lo
            st_ref[base + 3:base + 6, :] = ones
        for c in range(s // LANES):
            blk = st_ref[:, c * LANES:(c + 1) * LANES]
            aug_ref[0, p, c * LANES:(c + 1) * LANES, :] = blk.T.astype(BF16)


def _cumsum_seq(lf_t):
    b, h, s = lf_t.shape
    return pl.pallas_call(
        _cumsum_kernel,
        grid=(b,),
        in_specs=[pl.BlockSpec((1, h, s), lambda i: (i, 0, 0))],
        out_specs=[
            pl.BlockSpec((1, h, s), lambda i: (i, 0, 0)),
            pl.BlockSpec((1, h // 2, s, LANES), lambda i: (i, 0, 0, 0)),
        ],
        out_shape=[
            jax.ShapeDtypeStruct((b, h, s), F32),
            jax.ShapeDtypeStruct((b, h // 2, s, LANES), BF16),
        ],
        scratch_shapes=[pltpu.VMEM((LANES, s), F32)],
        compiler_params=_cparams("parallel"),
        name="fox_cumsum",
    )(lf_t)


def _fox_kernel(qt_ref, k_ref, aug_ref, vt_ref, c_ref, o_ref, m_ref, acc_ref, sa_ref, sb_ref, *, tq, tk):
    hp = pl.program_id(1)
    qi = pl.program_id(2)
    q0 = qi * tq
    qt = qt_ref[0].astype(F32)
    sub = lax.broadcasted_iota(jnp.int32, (LANES, tq), 0)
    lane = lax.broadcasted_iota(jnp.int32, (tk, LANES), 1)
    vsub = lax.broadcasted_iota(jnp.int32, (LANES, tk), 0)
    own_rows = [sub < HEAD_DIM, sub >= HEAD_DIM]
    own_lanes = [lane < HEAD_DIM, lane >= HEAD_DIM]
    own_vrows = [vsub < HEAD_DIM, vsub >= HEAD_DIM]

    qaug = []
    for r in range(2):
        base = HEAD_DIM * (1 - r)
        ci = c_ref[0, pl.ds(2 * hp + r, 1), :]
        hi, mid, lo = _split3(ci * LOG2E)
        t = jnp.where(own_rows[r], qt, 0.0)
        t = jnp.where((sub >= base) & (sub < base + 3), 1.0, t)
        t = jnp.where(sub == base + 3, hi, t)
        t = jnp.where(sub == base + 4, mid, t)
        t = jnp.where(sub == base + 5, lo, t)
        qaug.append(t.astype(BF16))

    m_ref[...] = jnp.full(m_ref.shape, NEG_INF, F32)
    acc_ref[...] = jnp.zeros(acc_ref.shape, F32)

    def scores(k0, s_ref, lo=0):
        kt = k_ref[0, pl.ds(k0, tk), :]
        ca = aug_ref[0, 0, pl.ds(k0, tk), :]
        for r in range(2):
            ka = jnp.where(own_lanes[r], kt, ca)
            s_ref[r, :, lo:] = _dot(ka, qaug[r][:, lo:])

    def softmax_pv(k0, s_ref, masked, lo=0):
        vt = vt_ref[0, :, pl.ds(k0, tk)]
        if masked:
            key = k0 + lax.broadcasted_iota(jnp.int32, (tk, tq - lo), 0)
            qry = q0 + lo + lax.broadcasted_iota(jnp.int32, (tk, tq - lo), 1)
            keep = key <= qry
        for r in range(2):
            st = s_ref[r, :, lo:]
            if masked:
                st = jnp.where(keep, st, NEG_INF)
            m_prev = m_ref[r, :, lo:]
            m_new = jnp.maximum(m_prev, jnp.max(st, axis=0, keepdims=True))
            alpha = jnp.exp2(m_prev - m_new)
            pt = jnp.exp2(st - m_new).astype(BF16)
            va = jnp.where(own_vrows[r], vt, 1.0)
            acc_ref[r, :, lo:] = alpha * acc_ref[r, :, lo:] + _dot(va, pt)
            m_ref[r, :, lo:] = m_new

    tile = lambda j: pl.multiple_of(j * tk, tk)
    scores(tile(0), sa_ref)

    def pair(i):
        scores(tile(2 * i + 1), sb_ref)
        softmax_pv(tile(2 * i), sa_ref, False)
        scores(tile(2 * i + 2), sa_ref)
        softmax_pv(tile(2 * i + 1), sb_ref, False)

    def body(j, carry):
        pair(2 * j)
        pair(2 * j + 1)
        return carry

    lax.fori_loop(0, qi // 2, body, 0)

    @pl.when(qi % 2 == 1)
    def _():
        pair(qi - 1)

    scores(tile(2 * qi + 1), sb_ref, lo=tk)
    softmax_pv(tile(2 * qi), sa_ref, True)
    softmax_pv(tile(2 * qi + 1), sb_ref, True, lo=tk)

    ot = jnp.zeros((LANES, tq), F32)
    for r in range(2):
        a = acc_ref[r]
        den = a[HEAD_DIM * (1 - r):HEAD_DIM * (1 - r) + 1, :]
        ot = ot + jnp.where(own_rows[r], a / den, 0.0)
    o_ref[0] = ot.T.astype(BF16)


def _fox_attention(k3, ht, caug, crow, tq=512):
    b, s, _ = k3.shape
    nhp = N_HEADS_FOX // 2
    tk = tq // 2
    kern = functools.partial(_fox_kernel, tq=tq, tk=tk)
    return pl.pallas_call(
        kern,
        grid=(b, nhp, s // tq),
        in_specs=[
            pl.BlockSpec((1, LANES, tq), lambda bi, hp, qi: (bi, hp, qi)),
            pl.BlockSpec((1, s, LANES), lambda bi, hp, qi: (bi, 0, hp)),
            pl.BlockSpec((1, 1, s, LANES), lambda bi, hp, qi: (bi, hp, 0, 0)),
            pl.BlockSpec((1, LANES, s), lambda bi, hp, qi: (bi, nhp + hp, 0)),
            pl.BlockSpec((1, N_HEADS_FOX, tq), lambda bi, hp, qi: (bi, 0, qi)),
        ],
        out_specs=pl.BlockSpec((1, tq, LANES), lambda bi, hp, qi: (bi, qi, hp)),
        out_shape=jax.ShapeDtypeStruct((b, s, FOX_W), BF16),
        scratch_shapes=[
            pltpu.VMEM((2, 1, tq), F32),
            pltpu.VMEM((2, LANES, tq), F32),
            pltpu.VMEM((2, tk, tq), F32),
            pltpu.VMEM((2, tk, tq), F32),
        ],
        compiler_params=_cparams("parallel", "parallel", "arbitrary"),
        name="fox_attn",
    )(ht, k3, caug, ht, crow)


def _sb_kernel(qt_ref, k_ref, vt_ref, o_ref, run_ref, acc_ref, za_ref, zb_ref, *, tq, cw):
    qi = pl.program_id(2)
    q0 = qi * tq
    qt = qt_ref[0]
    sub = lax.broadcasted_iota(jnp.int32, (LANES, tq), 0)
    own_rows = [sub < HEAD_DIM, sub >= HEAD_DIM]
    qm = [jnp.where(mk, qt, 0.0) for mk in own_rows]

    ss = lax.broadcasted_iota(jnp.int32, (cw, cw), 0)
    jj = lax.broadcasted_iota(jnp.int32, (cw, cw), 1)
    tri = jnp.where(jj > ss, 1.0, 0.0).astype(BF16)
    tri = jnp.concatenate([tri, jnp.ones((SUM_ROWS, cw), BF16)], axis=0)

    run_ref[...] = jnp.zeros(run_ref.shape, F32)
    acc_ref[...] = jnp.zeros(acc_ref.shape, F32)

    def start_of(c):
        return pl.multiple_of(jnp.maximum(q0 + tq - (c + 1) * cw, 0), cw)

    def logits(c, z_ref, lo=0):
        kt = k_ref[0, pl.ds(start_of(c), cw), :]
        for r in range(2):
            z_ref[r, :, lo:] = _dot(kt, qm[r][:, lo:])

    def weights_pv(c, z_ref, masked, lo=0):
        k0 = start_of(c)
        vt = vt_ref[0, :, pl.ds(k0, cw)]
        if masked:
            key = k0 + lax.broadcasted_iota(jnp.int32, (cw, tq - lo), 0)
            qry = q0 + lo + lax.broadcasted_iota(jnp.int32, (cw, tq - lo), 1)
            keep = key < qry
        zs, sps, laters, ws = [], [], [], []
        for r in range(2):
            z = z_ref[r, :, lo:]
            e = jnp.exp2(-jnp.abs(z))
            sp = jnp.maximum(z, 0.0) + jnp.log(1.0 + e) * LOG2E
            if masked:
                sp = jnp.where(keep, sp, 0.0)
            zs.append(z)
            sps.append(sp)
        for r in range(2):
            laters.append(_dot(tri, sps[r].astype(BF16)))
        for r in range(2):
            run = run_ref[r, :, lo:]
            a = jnp.exp2((zs[r] - sps[r]) - (laters[r][:cw] + run))
            if masked:
                a = jnp.where(keep, a, 0.0)
            ws.append(a.astype(BF16))
            run_ref[r, :, lo:] = run + laters[r][cw:cw + 1]
        for r in range(2):
            acc_ref[r, :, lo:] = acc_ref[r, :, lo:] + _dot(vt, ws[r])

    logits(0, za_ref, lo=cw)
    logits(1, zb_ref)
    weights_pv(0, za_ref, True, lo=cw)
    logits(2, za_ref)
    weights_pv(1, zb_ref, True)

    def body(i, carry):
        logits(2 * i + 1, zb_ref)
        weights_pv(2 * i, za_ref, False)
        logits(2 * i + 2, za_ref)
        weights_pv(2 * i + 1, zb_ref, False)
        return carry

    lax.fori_loop(1, qi + 1, body, 0)

    ot = jnp.where(own_rows[0], acc_ref[0], acc_ref[1])
    o_ref[0] = ot.T.astype(BF16)


def _sb_attention(k3, ht, tq=512):
    b, s, _ = k3.shape
    cw = tq // 2
    nhp = N_HEADS_SB // 2
    kern = functools.partial(_sb_kernel, tq=tq, cw=cw)
    return pl.pallas_call(
        kern,
        grid=(b, nhp, s // tq),
        in_specs=[
            pl.BlockSpec((1, LANES, tq), lambda bi, hp, qi: (bi, 2 * nhp + hp, qi)),
            pl.BlockSpec((1, s, LANES), lambda bi, hp, qi: (bi, 0, nhp + hp)),
            pl.BlockSpec((1, LANES, s), lambda bi, hp, qi: (bi, 3 * nhp + hp, 0)),
        ],
        out_specs=pl.BlockSpec((1, tq, LANES), lambda bi, hp, qi: (bi, qi, hp)),
        out_shape=jax.ShapeDtypeStruct((b, s, SB_W), BF16),
        scratch_shapes=[
            pltpu.VMEM((2, 1, tq), F32),
            pltpu.VMEM((2, LANES, tq), F32),
            pltpu.VMEM((2, cw, tq), F32),
            pltpu.VMEM((2, cw, tq), F32),
        ],
        compiler_params=_cparams("parallel", "parallel", "arbitrary"),
        name="sb_attn",
    )(ht, k3, ht)


def _swa_proj_kernel(x_ref, wk_ref, wt_ref, cos_ref, sa_ref, sb_ref, cost_ref, sint_ref, k_ref, qt_ref, vt_ref):
    xb = x_ref[...].astype(BF16)
    half = ROPE_DIM // 2
    yk = _dot(xb, wk_ref[...])
    for c in range(yk.shape[1] // LANES):
        blk = yk[:, c * LANES:(c + 1) * LANES]
        blk = (blk * cos_ref[...] + pltpu.roll(blk, half, axis=1) * sa_ref[...]
               + pltpu.roll(blk, LANES - half, axis=1) * sb_ref[...])
        k_ref[:, c * LANES:(c + 1) * LANES] = blk.astype(BF16)
    yt = _dot_nt(wt_ref[...], xb)
    cos = cost_ref[...]
    sin = sint_ref[...]
    nq = N_HEADS_SWA * HEAD_DIM
    for h in range(N_HEADS_SWA):
        base = h * HEAD_DIM
        x1 = yt[base:base + half]
        x2 = yt[base + half:base + ROPE_DIM]
        rot = jnp.concatenate([x1 * cos - x2 * sin, x2 * cos + x1 * sin], axis=0)
        qt_ref[0, base:base + ROPE_DIM, :] = rot.astype(BF16)
        qt_ref[0, base + ROPE_DIM:base + HEAD_DIM, :] = yt[base + ROPE_DIM:base + HEAD_DIM].astype(BF16)
    vt_ref[0] = yt[nq:].astype(BF16)


def _rope_tables(s):
    half = ROPE_DIM // 2
    inv = ROPE_THETA ** (-jnp.arange(half, dtype=F32) * 2.0 / ROPE_DIM)
    ang = jnp.arange(s, dtype=F32)[:, None] * inv[None, :]
    cos, sin = jnp.cos(ang), jnp.sin(ang)
    d = jnp.arange(LANES) % HEAD_DIM
    idx = d % half
    cos_t = jnp.where(d[None, :] < ROPE_DIM, cos[:, idx], 1.0)
    sin_a = jnp.where((d[None, :] >= half) & (d[None, :] < ROPE_DIM), sin[:, idx], 0.0)
    sin_b = jnp.where(d[None, :] < half, -sin[:, idx], 0.0)
    return cos_t.astype(F32), sin_a.astype(F32), sin_b.astype(F32), cos.T.astype(F32), sin.T.astype(F32)


def _swa_proj(x2d, wk, wt, tables, b, s, tm=512):
    t = x2d.shape[0]
    nk = wk.shape[1]
    nq = N_HEADS_SWA * HEAD_DIM
    nv = wt.shape[0] - nq
    spt = s // tm
    half = ROPE_DIM // 2
    lane_tab = pl.BlockSpec((tm, LANES), lambda i: (i % spt, 0))
    row_tab = pl.BlockSpec((half, tm), lambda i: (0, i % spt))
    return pl.pallas_call(
        _swa_proj_kernel,
        grid=(t // tm,),
        in_specs=[
            pl.BlockSpec((tm, D_MODEL), lambda i: (i, 0)),
            pl.BlockSpec(wk.shape, lambda i: (0, 0)),
            pl.BlockSpec(wt.shape, lambda i: (0, 0)),
            lane_tab, lane_tab, lane_tab, row_tab, row_tab,
        ],
        out_specs=[
            pl.BlockSpec((tm, nk), lambda i: (i, 0)),
            pl.BlockSpec((1, nq, tm), lambda i: (i // spt, 0, i % spt)),
            pl.BlockSpec((1, nv, tm), lambda i: (i // spt, 0, i % spt)),
        ],
        out_shape=[
            jax.ShapeDtypeStruct((t, nk), BF16),
            jax.ShapeDtypeStruct((b, nq, s), BF16),
            jax.ShapeDtypeStruct((b, nv, s), BF16),
        ],
        compiler_params=_cparams("parallel"),
        name="swa_proj",
    )(x2d, wk, wt, *tables)


def _swa_kernel(sink_ref, qt_ref, k_ref, vt_ref, o_ref, *, tq):
    qi = pl.program_id(1)
    q0 = qi * tq
    kw = tq + WINDOW
    kstart = pl.multiple_of(jnp.maximum(q0 - WINDOW, 0), WINDOW)
    kwin = k_ref[0, pl.ds(kstart, kw), :]
    vwin = vt_ref[0, :, pl.ds(kstart, kw)]
    key = kstart + lax.broadcasted_iota(jnp.int32, (kw, tq), 0)
    qry = q0 + lax.broadcasted_iota(jnp.int32, (kw, tq), 1)
    diff = qry - key
    keep = (diff >= 0) & (diff < WINDOW)
    sub = lax.broadcasted_iota(jnp.int32, (LANES, tq), 0)
    vsub = lax.broadcasted_iota(jnp.int32, (LANES, kw), 0)
    own_rows = [sub < HEAD_DIM, sub >= HEAD_DIM]
    own_vrows = [vsub < HEAD_DIM, vsub >= HEAD_DIM]
    group = N_HEADS_SWA // N_KV_SWA

    def scores(h):
        c, r = divmod(h, 2)
        g = h // group
        qblk = qt_ref[0, c * LANES:(c + 1) * LANES, :]
        return _dot(kwin[:, g * LANES:(g + 1) * LANES], jnp.where(own_rows[r], qblk, 0.0))

    def weights(h, st):
        st = jnp.where(keep, st, NEG_INF)
        sink = sink_ref[h] * LOG2E
        mx = jnp.maximum(jnp.max(st, axis=0, keepdims=True), sink)
        return jnp.exp2(st - mx).astype(BF16), jnp.exp2(sink - mx)

    def values(h, pt):
        r = h % 2
        g = h // group
        va = jnp.where(own_vrows[r], vwin[g * LANES:(g + 1) * LANES, :], 1.0)
        return _dot(va, pt)

    batches = [list(range(i, i + SWA_HEAD_BATCH)) for i in range(0, N_HEADS_SWA, SWA_HEAD_BATCH)]
    sts = [scores(h) for h in batches[0]]
    for bi, heads in enumerate(batches):
        nxt = [scores(h) for h in batches[bi + 1]] if bi + 1 < len(batches) else None
        pts = [weights(h, st) for h, st in zip(heads, sts)]
        accs = [values(h, pt) for h, (pt, _) in zip(heads, pts)]
        outs = []
        for h, acc, (_, esink) in zip(heads, accs, pts):
            r = h % 2
            den = acc[HEAD_DIM * (1 - r):HEAD_DIM * (1 - r) + 1, :] + esink
            outs.append(acc / den)
            if r == 1:
                c = h // 2
                ot = jnp.where(own_rows[0], outs[-2], outs[-1])
                o_ref[0, :, c * LANES:(c + 1) * LANES] = ot.T.astype(BF16)
        sts = nxt


def _swa_attention(qt3, k3, vt3, sinks, tq=256):
    b, nq, s = qt3.shape
    kern = functools.partial(_swa_kernel, tq=tq)
    return pl.pallas_call(
        kern,
        grid=(b, s // tq),
        in_specs=[
            pl.BlockSpec(memory_space=pltpu.SMEM),
            pl.BlockSpec((1, nq, tq), lambda bi, qi: (bi, 0, qi)),
            pl.BlockSpec((1, s, k3.shape[2]), lambda bi, qi: (bi, 0, 0)),
            pl.BlockSpec((1, vt3.shape[1], s), lambda bi, qi: (bi, 0, 0)),
        ],
        out_specs=pl.BlockSpec((1, tq, nq), lambda bi, qi: (bi, qi, 0)),
        out_shape=jax.ShapeDtypeStruct((b, s, nq), BF16),
        compiler_params=_cparams("parallel", "arbitrary"),
        name="swa_attn",
    )(sinks, qt3, k3, vt3)


def _layer_norm(z, g, b):
    mu = jnp.mean(z, axis=-1, keepdims=True)
    zc = z - mu
    var = jnp.mean(zc * zc, axis=-1, keepdims=True)
    return zc * lax.rsqrt(var + LN_EPS) * g + b


def _outln_kernel(*refs, n_in, routed):
    o_refs = refs[:n_in]
    w_ref, x_ref, g_ref, b_ref = refs[n_in:n_in + 4]
    rest = refs[n_in + 4:]
    h = None
    row0 = 0
    for o_ref in o_refs:
        n = o_ref.shape[1]
        part = _dot(o_ref[...], w_ref[row0:row0 + n, :])
        h = part if h is None else h + part
        row0 += n
    y = _layer_norm(DEEPNORM_ALPHA * x_ref[...] + h, g_ref[...], b_ref[...])
    if routed:
        rw_ref, rb_ref, y_ref, yb_ref, c_ref = rest
        c_ref[...] = _route(y, rw_ref[...], rb_ref[...])
    else:
        y_ref, yb_ref = rest
    y_ref[...] = y
    yb_ref[...] = y.astype(BF16)


def _out_proj_ln(os_, w, x2d, g, b, router=None, tm=512):
    t = x2d.shape[0]
    n_in = len(os_)
    row = lambda n: pl.BlockSpec((tm, n), lambda i: (i, 0))
    full = lambda a: pl.BlockSpec(a.shape, lambda i: (0, 0))
    in_specs = [row(o.shape[1]) for o in os_] + [full(w), row(D_MODEL), full(g), full(b)]
    out_specs = [row(D_MODEL)] * 2
    out_shape = [jax.ShapeDtypeStruct((t, D_MODEL), F32), jax.ShapeDtypeStruct((t, D_MODEL), BF16)]
    args = [*os_, w, x2d, g, b]
    if router is not None:
        in_specs += [full(a) for a in router]
        out_specs.append(row(LANES))
        out_shape.append(jax.ShapeDtypeStruct((t, LANES), F32))
        args += list(router)
    return pl.pallas_call(
        functools.partial(_outln_kernel, n_in=n_in, routed=router is not None),
        grid=(t // tm,),
        in_specs=in_specs,
        out_specs=out_specs,
        out_shape=out_shape,
        compiler_params=_cparams("parallel"),
        name="out_proj_ln",
    )(*args)


def _route(x, w, b):
    xh = x.astype(BF16)
    xl = (x - xh.astype(F32)).astype(BF16)
    wh = w.astype(BF16)
    wl = (w - wh.astype(F32)).astype(BF16)
    logits = _dot(xh, wh) + (_dot(xl, wh) + _dot(xh, wl)) + b
    lane = lax.broadcasted_iota(jnp.int32, logits.shape, 1)
    logits = jnp.where(lane < N_EXPERTS, logits, NEG_INF)
    m1 = jnp.max(logits, axis=1, keepdims=True)
    i1 = jnp.min(jnp.where(logits == m1, lane, LANES), axis=1, keepdims=True)
    rest = jnp.where(lane == i1, NEG_INF, logits)
    m2 = jnp.max(rest, axis=1, keepdims=True)
    i2 = jnp.min(jnp.where(rest == m2, lane, LANES), axis=1, keepdims=True)
    e2 = jnp.exp(m2 - m1)
    den = 1.0 + e2
    return jnp.where(lane == i1, 1.0 / den, 0.0) + jnp.where(lane == i2, e2 / den, 0.0)


def _ffn_kernel(eid_ref, valid_ref, x_ref, wg_ref, wu_ref, wd_ref, o_ref, acc_ref):
    i = pl.program_id(0)
    c = pl.program_id(1)
    last = c == pl.num_programs(1) - 1
    valid = valid_ref[i] != 0

    @pl.when(valid)
    def _():
        x = x_ref[...]
        gate = _dot(x, wg_ref[0])
        up = _dot(x, wu_ref[0])
        h = gate * jax.nn.sigmoid(gate) * up
        contrib = _dot(h.astype(BF16), wd_ref[0])

        @pl.when(c == 0)
        def _():
            acc_ref[...] = contrib

        @pl.when(c != 0)
        def _():
            acc_ref[...] += contrib

        @pl.when(last)
        def _():
            o_ref[...] = acc_ref[...].astype(o_ref.dtype)

    @pl.when(jnp.logical_not(valid) & last)
    def _():
        o_ref[...] = jnp.zeros(o_ref.shape, o_ref.dtype)


def _ffn(xb, w_gu, w_d, eids, valid, out_dtype, tm, fc, name):
    r = xb.shape[0]
    f = w_d.shape[1]
    ncf = f // fc
    in_specs = [
        pl.BlockSpec((tm, D_MODEL), lambda i, c, eid, ok: (i, 0)),
        pl.BlockSpec((1, D_MODEL, fc), lambda i, c, eid, ok: (eid[i], 0, c * ok[i])),
        pl.BlockSpec((1, D_MODEL, fc), lambda i, c, eid, ok: (eid[i], 0, ncf + c * ok[i])),
        pl.BlockSpec((1, fc, D_MODEL), lambda i, c, eid, ok: (eid[i], c * ok[i], 0)),
    ]
    return pl.pallas_call(
        _ffn_kernel,
        grid_spec=pltpu.PrefetchScalarGridSpec(
            num_scalar_prefetch=2,
            grid=(r // tm, ncf),
            in_specs=in_specs,
            out_specs=pl.BlockSpec((tm, D_MODEL), lambda i, c, eid, ok: (i, 0)),
            scratch_shapes=[pltpu.VMEM((tm, D_MODEL), F32)],
        ),
        out_shape=jax.ShapeDtypeStruct((r, D_MODEL), out_dtype),
        compiler_params=_cparams("parallel", "arbitrary"),
        name=name,
    )(eids, valid, xb, w_gu, w_gu, w_d)


def _moe_ffn_kernel(eid_ref, valid_ref, first_ref, x_ref, wgu_hbm, wd_hbm, o_ref,
                    cg_ref, cu_ref, cd_ref, sg_ref, su_ref, sd_ref, sem_ref, acc_ref, *, fc):
    i = pl.program_id(0)
    e = eid_ref[i]
    f = cd_ref.shape[0]
    n_chunks = f // fc

    def chunk_copies(c, slot):
        return (
            pltpu.make_async_copy(wgu_hbm.at[e, :, pl.ds(c * fc, fc)], sg_ref.at[slot], sem_ref.at[0, slot]),
            pltpu.make_async_copy(wgu_hbm.at[e, :, pl.ds(f + c * fc, fc)], su_ref.at[slot], sem_ref.at[1, slot]),
            pltpu.make_async_copy(wd_hbm.at[e, pl.ds(c * fc, fc), :], sd_ref.at[slot], sem_ref.at[2, slot]),
        )

    def compute(c):
        x = x_ref[...]
        gate = _dot(x, cg_ref[:, c * fc:(c + 1) * fc])
        up = _dot(x, cu_ref[:, c * fc:(c + 1) * fc])
        h = gate * jax.nn.sigmoid(gate) * up
        contrib = _dot(h.astype(BF16), cd_ref[c * fc:(c + 1) * fc, :])
        if c == 0:
            acc_ref[...] = contrib
        else:
            acc_ref[...] += contrib

    valid = valid_ref[i] != 0
    first = first_ref[i] != 0

    @pl.when(valid & first)
    def _():
        for cp in chunk_copies(0, 0):
            cp.start()
        for c in range(n_chunks):
            slot = c % 2
            if c + 1 < n_chunks:
                for cp in chunk_copies(c + 1, 1 - slot):
                    cp.start()
            for cp in chunk_copies(c, slot):
                cp.wait()
            cg_ref[:, c * fc:(c + 1) * fc] = sg_ref[slot].astype(BF16)
            cu_ref[:, c * fc:(c + 1) * fc] = su_ref[slot].astype(BF16)
            cd_ref[c * fc:(c + 1) * fc, :] = sd_ref[slot].astype(BF16)
            compute(c)
        o_ref[...] = acc_ref[...].astype(o_ref.dtype)

    @pl.when(valid & jnp.logical_not(first))
    def _():
        for c in range(n_chunks):
            compute(c)
        o_ref[...] = acc_ref[...].astype(o_ref.dtype)

    @pl.when(jnp.logical_not(valid))
    def _():
        o_ref[...] = jnp.zeros(o_ref.shape, o_ref.dtype)


def _moe_ffn(xs, w_gu, w_d, eids, valid, first, tm=MOE_TM, fc=MOE_FC):
    r = xs.shape[0]
    f = w_d.shape[1]
    return pl.pallas_call(
        functools.partial(_moe_ffn_kernel, fc=fc),
        grid_spec=pltpu.PrefetchScalarGridSpec(
            num_scalar_prefetch=3,
            grid=(r // tm,),
            in_specs=[
                pl.BlockSpec((tm, D_MODEL), lambda i, *_: (i, 0)),
                pl.BlockSpec(memory_space=pl.ANY),
                pl.BlockSpec(memory_space=pl.ANY),
            ],
            out_specs=pl.BlockSpec((tm, D_MODEL), lambda i, *_: (i, 0)),
            scratch_shapes=[
                pltpu.VMEM((D_MODEL, f), BF16),
                pltpu.VMEM((D_MODEL, f), BF16),
                pltpu.VMEM((f, D_MODEL), BF16),
                pltpu.VMEM((2, D_MODEL, fc), F32),
                pltpu.VMEM((2, D_MODEL, fc), F32),
                pltpu.VMEM((2, fc, D_MODEL), F32),
                pltpu.SemaphoreType.DMA((3, 2)),
                pltpu.VMEM((tm, D_MODEL), F32),
            ],
        ),
        out_shape=jax.ShapeDtypeStruct((r, D_MODEL), BF16),
        compiler_params=_cparams("arbitrary"),
        name="ffn_moe",
    )(eids, valid, first, xs, w_gu, w_d)


def _moe_gather_kernel(te_ref, clo_ref, nch_ref, s0_ref, ns_ref, x_ref, pos_ref, o_ref, acc_ref, *, tm, sr, n_chunks):
    r = pl.program_id(0)
    e = te_ref[r]
    sub_row = lax.broadcasted_iota(jnp.int32, (sr, tm), 0)
    acc_ref[...] = jnp.zeros(acc_ref.shape, F32)

    def body(i, carry):
        c = clo_ref[r] + i
        t0 = pl.multiple_of(c * tm, tm)
        p = pos_ref[pl.ds(e, 1), pl.ds(t0, tm)]

        def sub(j, carry2):
            r0 = pl.multiple_of((s0_ref[r * n_chunks + c] + j) * sr, sr)
            onehot = jnp.where(p == r * tm + r0 + sub_row, 1.0, 0.0).astype(BF16)
            acc_ref[pl.ds(r0, sr), :] += _dot(onehot, x_ref[pl.ds(t0, tm), :])
            return carry2

        lax.fori_loop(0, ns_ref[r * n_chunks + c], sub, 0)
        return carry

    lax.fori_loop(0, nch_ref[r], body, 0)
    o_ref[...] = acc_ref[...].astype(BF16)


def _moe_gather(xb, pos_t, tile_e, clo, nch, sub0, nsub, n_rows, tm=MOE_TM, sr=MOE_SR):
    t = xb.shape[0]
    return pl.pallas_call(
        functools.partial(_moe_gather_kernel, tm=tm, sr=sr, n_chunks=t // tm),
        grid_spec=pltpu.PrefetchScalarGridSpec(
            num_scalar_prefetch=5,
            grid=(n_rows // tm,),
            in_specs=[
                pl.BlockSpec((t, D_MODEL), lambda r, *_: (0, 0), pipeline_mode=pl.Buffered(1)),
                pl.BlockSpec(pos_t.shape, lambda r, *_: (0, 0)),
            ],
            out_specs=pl.BlockSpec((tm, D_MODEL), lambda r, *_: (r, 0)),
            scratch_shapes=[pltpu.VMEM((tm, D_MODEL), F32)],
        ),
        out_shape=jax.ShapeDtypeStruct((n_rows, D_MODEL), BF16),
        compiler_params=_cparams("arbitrary"),
        name="moe_gather",
    )(tile_e, clo, nch, sub0, nsub, xb, pos_t)


def _moe_combine_kernel(n_ref, cid_ref, eid_ref, ys_hbm, pos_ref, comb_ref, x_ref, *rest, tm, cs, n_slots, n_buf):
    *ple_refs, o_ref, buf_ref, sem_ref = rest
    i = pl.program_id(0)
    n = n_ref[i]
    base = i * n_slots

    def chunk_copy(k, slot):
        row0 = pl.multiple_of(cid_ref[base + k] * cs, cs)
        return pltpu.make_async_copy(ys_hbm.at[pl.ds(row0, cs), :], buf_ref.at[slot], sem_ref.at[slot])

    o_ref[...] = jnp.zeros(o_ref.shape, F32)

    for d in range(n_buf):

        @pl.when(d < n)
        def _():
            chunk_copy(d, d).start()

    def gathered(k, slot):
        e = eid_ref[base + k]
        lane = lax.broadcasted_iota(jnp.int32, pos_ref.shape, 1)
        pos_e = jnp.sum(jnp.where(lane == e, pos_ref[...], 0), axis=1, keepdims=True)
        lane_c = lax.broadcasted_iota(jnp.int32, comb_ref.shape, 1)
        g_e = jnp.sum(jnp.where(lane_c == e, comb_ref[...], 0.0), axis=1, keepdims=True)
        col = cid_ref[base + k] * cs + lax.broadcasted_iota(jnp.int32, (tm, cs), 1)
        onehot = jnp.where(pos_e == col, 1.0, 0.0).astype(BF16)
        return g_e * _dot(onehot, buf_ref[slot])

    def body(j, carry):
        k = 2 * j
        slot = k % n_buf
        chunk_copy(k, slot).wait()
        chunk_copy(k + 1, slot + 1).wait()
        o_ref[...] += gathered(k, slot) + gathered(k + 1, slot + 1)

        @pl.when(k + n_buf < n)
        def _():
            chunk_copy(k + n_buf, slot).start()
            chunk_copy(k + n_buf + 1, slot + 1).start()

        return carry

    lax.fori_loop(0, n // 2, body, 0)
    o_ref[...] = _ln_ple(x_ref[...], o_ref[...], ple_refs)


def _moe_combine(ys, pos, combine, n_need, cid, eid, x2d, ple_args, tm=MOE_TM, cs=MOE_CS):
    t = pos.shape[0]
    return pl.pallas_call(
        functools.partial(_moe_combine_kernel, tm=tm, cs=cs, n_slots=MOE_SLOTS, n_buf=MOE_RING),
        grid_spec=pltpu.PrefetchScalarGridSpec(
            num_scalar_prefetch=3,
            grid=(t // tm,),
            in_specs=[
                pl.BlockSpec(memory_space=pl.ANY),
                pl.BlockSpec((tm, N_EXPERTS), lambda i, *_: (i, 0)),
                pl.BlockSpec((tm, LANES), lambda i, *_: (i, 0)),
                pl.BlockSpec((tm, D_MODEL), lambda i, *_: (i, 0)),
            ] + _ple_specs(tm, ple_args),
            out_specs=pl.BlockSpec((tm, D_MODEL), lambda i, *_: (i, 0)),
            scratch_shapes=[pltpu.VMEM((MOE_RING, cs, D_MODEL), BF16), pltpu.SemaphoreType.DMA((MOE_RING,))],
        ),
        out_shape=jax.ShapeDtypeStruct((t, D_MODEL), F32),
        compiler_params=_cparams("arbitrary"),
        name="moe_combine",
    )(n_need, cid, eid, ys, pos, combine, x2d, *ple_args)


def _moe_plan(combine, tm=MOE_TM):
    t = combine.shape[0]
    n_tok_tiles = t // tm
    n_row_tiles = (2 * t) // tm + N_EXPERTS
    sel = (combine[:, :N_EXPERTS] != 0.0).astype(jnp.int32)
    csum = jnp.cumsum(sel, axis=0)
    count = csum[-1]
    ntile_e = (count + tm - 1) // tm
    tile_end = jnp.cumsum(ntile_e)
    gstart = (tile_end - ntile_e) * tm
    pos = jnp.where(sel != 0, gstart[None, :] + csum - sel, -1)
    rt = jnp.arange(n_row_tiles, dtype=jnp.int32)
    tile_e = jnp.minimum(jnp.sum(rt[:, None] >= tile_end[None, :], axis=1), N_EXPERTS - 1).astype(jnp.int32)
    valid = (rt < tile_end[-1]).astype(jnp.int32)
    prev_e = jnp.concatenate([jnp.full((1,), -1, jnp.int32), tile_e[:-1]])
    group_head = ((valid != 0) & (tile_e != prev_e)).astype(jnp.int32)
    cb = jnp.concatenate([jnp.zeros((1, N_EXPERTS), jnp.int32), csum[tm - 1::tm]], axis=0)
    off = rt * tm - gstart[tile_e]
    cb_t = cb[:, tile_e].T
    clo = jnp.sum(cb_t[:, 1:] <= off[:, None], axis=1).astype(jnp.int32)
    cend = jnp.sum(cb_t[:, :-1] < (off + tm)[:, None], axis=1).astype(jnp.int32)
    nch = jnp.where(valid != 0, jnp.maximum(cend - clo, 0), 0).astype(jnp.int32)
    clo = jnp.minimum(clo, n_tok_tiles - 1)
    r_lo = jnp.clip(cb_t[:, :-1] - off[:, None], 0, tm)
    r_hi = jnp.clip(cb_t[:, 1:] - off[:, None], 0, tm)
    sub0 = (r_lo // MOE_SR).astype(jnp.int32)
    nsub = jnp.where(r_hi > r_lo, (r_hi + MOE_SR - 1) // MOE_SR - sub0, 0).astype(jnp.int32)
    first = gstart[None, :] + cb[:-1]
    n_ie = cb[1:] - cb[:-1]
    per_e = MOE_SLOTS // N_EXPERTS
    k = jnp.arange(per_e, dtype=jnp.int32)
    c_lo = first // MOE_CS
    c_hi = (first + n_ie - 1) // MOE_CS
    cand = c_lo[..., None] + k
    need = ((n_ie[..., None] > 0) & (cand <= c_hi[..., None])).reshape(n_tok_tiles, MOE_SLOTS)
    cand = cand.reshape(n_tok_tiles, MOE_SLOTS)
    eids = jnp.broadcast_to(jnp.arange(N_EXPERTS, dtype=jnp.int32)[:, None], (N_EXPERTS, per_e)).reshape(-1)
    order = jnp.argsort(jnp.logical_not(need), axis=1, stable=True)
    cid = jnp.where(need, cand, 0)
    cid = jnp.take_along_axis(cid, order, axis=1).reshape(-1).astype(jnp.int32)
    need_sorted = jnp.take_along_axis(need, order, axis=1)
    eid = jnp.where(need_sorted, eids[order], -1).reshape(-1).astype(jnp.int32)
    n_need = jnp.sum(need, axis=1).astype(jnp.int32)
    n_need = n_need + n_need % 2
    return dict(pos=pos.astype(jnp.int32), tile_e=tile_e, valid=valid, first=group_head, clo=clo, nch=nch,
                sub0=sub0.reshape(-1), nsub=nsub.reshape(-1),
                n_need=n_need, cid=cid, eid=eid, n_rows=n_row_tiles * tm)


def _moe(xb, x2d, combine, w_gu, w_d, first_expert, ple_args):
    plan = _moe_plan(combine)
    xs = _moe_gather(xb, plan["pos"].T, plan["tile_e"], plan["clo"], plan["nch"], plan["sub0"], plan["nsub"],
                     plan["n_rows"])
    ys = _moe_ffn(xs, w_gu, w_d, plan["tile_e"] + first_expert, plan["valid"], plan["first"])
    return _moe_combine(ys, plan["pos"], combine, plan["n_need"], plan["cid"], plan["eid"], x2d, ple_args)


def _ln_ple(x, h, ple_refs):
    g_ref, b_ref, p_ref, wg_ref, wp_ref = ple_refs
    y = _layer_norm(DEEPNORM_ALPHA * x + h, g_ref[...], b_ref[...])
    gate = jax.nn.sigmoid(_dot(y.astype(BF16), wg_ref[...]))
    proj = _dot(p_ref[...].astype(BF16), wp_ref[...])
    return y + gate * proj


def _ple_specs(tm, ple_args):
    g, b, p2d, wg, wp = ple_args
    ign = lambda f: (lambda *idx: f(idx[0]))
    full = lambda a: pl.BlockSpec(a.shape, ign(lambda i: (0, 0)))
    return [full(g), full(b), pl.BlockSpec((tm, D_PLE), ign(lambda i: (i, 0))), full(wg), full(wp)]


def _ffn_ple_kernel(xb_ref, wgu_ref, wd_ref, x_ref, *rest, fc):
    *ple_refs, o_ref, acc_ref = rest
    f = wd_ref.shape[0]
    xb = xb_ref[...]
    for c in range(f // fc):
        gate = _dot(xb, wgu_ref[:, c * fc:(c + 1) * fc])
        up = _dot(xb, wgu_ref[:, f + c * fc:f + (c + 1) * fc])
        h = gate * jax.nn.sigmoid(gate) * up
        contrib = _dot(h.astype(BF16), wd_ref[c * fc:(c + 1) * fc, :])
        if c == 0:
            acc_ref[...] = contrib
        else:
            acc_ref[...] += contrib
    o_ref[...] = _ln_ple(x_ref[...], acc_ref[...], ple_refs)


def _ffn_ple(xb, x2d, w_gu, w_d, ple_args, tm=512, fc=1408):
    t = xb.shape[0]
    row = lambda n: pl.BlockSpec((tm, n), lambda i: (i, 0))
    resident = lambda a: pl.BlockSpec(a.shape, lambda i: (0, 0), pipeline_mode=pl.Buffered(1))
    return pl.pallas_call(
        functools.partial(_ffn_ple_kernel, fc=fc),
        grid=(t // tm,),
        in_specs=[row(D_MODEL), resident(w_gu), resident(w_d), row(D_MODEL)] + _ple_specs(tm, ple_args),
        out_specs=row(D_MODEL),
        out_shape=jax.ShapeDtypeStruct((t, D_MODEL), F32),
        scratch_shapes=[pltpu.VMEM((tm, D_MODEL), F32)],
        compiler_params=_cparams("parallel"),
        name="ffn_dense",
    )(xb, w_gu, w_d, x2d, *ple_args)


def _even_mixer(x2d, b, s, w_in, b_f, w_out):
    nf, ns = FOX_W, SB_W
    o_f = 3 * nf
    o_s = o_f + N_HEADS_FOX
    qa, ka, va = w_in[:, :nf], w_in[:, nf:2 * nf], w_in[:, 2 * nf:o_f]
    qs, ks, vs = w_in[:, o_s:o_s + ns], w_in[:, o_s + ns:o_s + 2 * ns], w_in[:, o_s + 2 * ns:]
    w_k = jnp.concatenate([ka, ks], axis=1).astype(BF16)
    w_t = jnp.concatenate([qa * Q_SCALE, va, qs * Q_SCALE, vs], axis=1).T.astype(BF16)
    w_f = jnp.pad(w_in[:, o_f:o_s], ((0, 0), (0, LANES - N_HEADS_FOX))).astype(BF16)
    bias_f = jnp.pad(b_f, (0, LANES - N_HEADS_FOX)).reshape(1, LANES)
    k2d, ht, lf = _ab_proj(x2d, w_k, w_t, w_f, bias_f, b, s)
    lf_t = lf.reshape(b, s, LANES)[:, :, :N_HEADS_FOX].transpose(0, 2, 1)
    crow, caug = _cumsum_seq(lf_t)
    k3 = k2d.reshape(b, s, -1)
    oa = _fox_attention(k3, ht, caug, crow)
    ob = _sb_attention(k3, ht)
    return [oa.reshape(b * s, nf), ob.reshape(b * s, ns)], w_out.astype(BF16)


def _odd_mixer(x2d, b, s, w_qkv, sinks, w_out, tables):
    nq = N_HEADS_SWA * HEAD_DIM
    wq = w_qkv[:, :nq]
    wk = [w_qkv[:, nq + i * HEAD_DIM: nq + (i + 1) * HEAD_DIM] for i in range(N_KV_SWA)]
    wv = [w_qkv[:, nq + (N_KV_SWA + i) * HEAD_DIM: nq + (N_KV_SWA + i + 1) * HEAD_DIM] for i in range(N_KV_SWA)]
    dup = lambda ws: [w for w in ws for _ in range(2)]
    w_k = jnp.concatenate(dup(wk), axis=1).astype(BF16)
    w_t = jnp.concatenate([wq * Q_SCALE] + dup(wv), axis=1).T.astype(BF16)
    k2d, qt, vt = _swa_proj(x2d, w_k, w_t, tables, b, s)
    o = _swa_attention(qt, k2d.reshape(b, s, -1), vt, sinks)
    return [o.reshape(b * s, nq)], w_out.astype(BF16)


def kernel(x, p, ln_mix_g, ln_mix_b, ln_ffn_g, ln_ffn_b, ab_w_in, ab_b_f, ab_w_out, c_w_qkv, c_sinks, c_w_out,
           ffn_w_gate_up, ffn_w_down, router_w, router_b, moe_w_gate_up, moe_w_down, ple_w_gate, ple_w_proj):
    b, s, d = x.shape
    t = b * s
    x2d = x.reshape(t, d)
    tables = _rope_tables(s)
    row = lambda v: v.reshape(1, -1)
    moe_gu = moe_w_gate_up.reshape((-1,) + moe_w_gate_up.shape[2:])
    moe_d = moe_w_down.reshape((-1,) + moe_w_down.shape[2:])
    for i in range(DEPTH):
        j = i // 2
        if i % 2 == 0:
            os_, ws = _even_mixer(x2d, b, s, ab_w_in[j], ab_b_f[j], ab_w_out[j])
        else:
            os_, ws = _odd_mixer(x2d, b, s, c_w_qkv[j], c_sinks[j], c_w_out[j], tables)
        ple_args = (row(ln_ffn_g[i]), row(ln_ffn_b[i]), p[i].reshape(t, D_PLE),
                    ple_w_gate[i].astype(BF16), ple_w_proj[i].astype(BF16))
        if i % 2 == 0:
            x2d, xb = _out_proj_ln(os_, ws, x2d, row(ln_mix_g[i]), row(ln_mix_b[i]))
            x2d = _ffn_ple(xb, x2d, ffn_w_gate_up[j].astype(BF16), ffn_w_down[j].astype(BF16), ple_args)
        else:
            rw = jnp.pad(router_w[j], ((0, 0), (0, LANES - N_EXPERTS)))
            rb = jnp.pad(router_b[j], (0, LANES - N_EXPERTS)).reshape(1, LANES)
            x2d, xb, combine = _out_proj_ln(os_, ws, x2d, row(ln_mix_g[i]), row(ln_mix_b[i]), router=(rw, rb))
            x2d = _moe(xb, x2d, combine, moe_gu, moe_d, j * N_EXPERTS, ple_args)
    return x2d.reshape(b, s, d)
```

```python
import functools
import math

import jax
import jax.numpy as jnp
from jax import lax
from jax.experimental import pallas as pl
from jax.experimental.pallas import tpu as pltpu

F32 = jnp.float32
BF16 = jnp.bfloat16

D_MODEL = 1024
HEAD_DIM = 64
LANES = 128
N_HEADS_FOX = 8
N_HEADS_SB = 8
N_HEADS_SWA = 16
N_KV_SWA = 2
WINDOW = 128
ROPE_THETA = 500000.0
ROPE_DIM = HEAD_DIM // 4
N_EXPERTS = 8
D_PLE = 256
LN_EPS = 1e-5
DEPTH = 4
DEEPNORM_ALPHA = (2.0 * DEPTH) ** 0.25
ATTN_SCALE = HEAD_DIM ** -0.5
FOX_W = N_HEADS_FOX * HEAD_DIM
SB_W = N_HEADS_SB * HEAD_DIM
VMEM_LIMIT = 56 * 1024 * 1024

NEG_INF = float("-inf")
LOG2E = math.log2(math.e)
Q_SCALE = ATTN_SCALE * LOG2E
SUM_ROWS = 16
SWA_HEAD_BATCH = 4
MOE_TM = 512
MOE_SR = 128
MOE_FC = 512
MOE_CS = 256
MOE_SLOTS = N_EXPERTS * (MOE_TM // MOE_CS + 1)
MOE_RING = 4


def _cparams(*sem):
    return pltpu.CompilerParams(dimension_semantics=sem, vmem_limit_bytes=VMEM_LIMIT)


def _dot(a, b):
    return jnp.dot(a, b, preferred_element_type=F32)


def _dot_nt(a, b):
    return lax.dot_general(a, b, (((1,), (1,)), ((), ())), preferred_element_type=F32)


def _log_sigmoid(x):
    return jnp.minimum(x, 0.0) - jnp.log1p(jnp.exp(-jnp.abs(x)))


def _head_masks(shape):
    lane = lax.broadcasted_iota(jnp.int32, shape, len(shape) - 1)
    return lane < HEAD_DIM, lane >= HEAD_DIM


def _ab_proj_kernel(x_ref, wk_ref, wt_ref, wf_ref, bf_ref, k_ref, ht_ref, lf_ref):
    xb = x_ref[...].astype(BF16)
    k_ref[...] = _dot(xb, wk_ref[...]).astype(BF16)
    ht_ref[0] = _dot_nt(wt_ref[...], xb).astype(BF16)
    lf_ref[...] = _log_sigmoid(_dot(xb, wf_ref[...]) + bf_ref[...])


def _ab_proj(x2d, wk, wt, wf, bf, b, s, tm=512):
    t = x2d.shape[0]
    nk = wk.shape[1]
    nt = wt.shape[0]
    spt = s // tm
    return pl.pallas_call(
        _ab_proj_kernel,
        grid=(t // tm,),
        in_specs=[
            pl.BlockSpec((tm, D_MODEL), lambda i: (i, 0)),
            pl.BlockSpec((D_MODEL, nk), lambda i: (0, 0)),
            pl.BlockSpec((nt, D_MODEL), lambda i: (0, 0)),
            pl.BlockSpec((D_MODEL, LANES), lambda i: (0, 0)),
            pl.BlockSpec((1, LANES), lambda i: (0, 0)),
        ],
        out_specs=[
            pl.BlockSpec((tm, nk), lambda i: (i, 0)),
            pl.BlockSpec((1, nt, tm), lambda i: (i // spt, 0, i % spt)),
            pl.BlockSpec((tm, LANES), lambda i: (i, 0)),
        ],
        out_shape=[
            jax.ShapeDtypeStruct((t, nk), BF16),
            jax.ShapeDtypeStruct((b, nt, s), BF16),
            jax.ShapeDtypeStruct((t, LANES), F32),
        ],
        compiler_params=_cparams("parallel"),
        name="ab_proj",
    )(x2d, wk, wt, wf, bf)


def _split3(v):
    hi = v.astype(BF16).astype(F32)
    r1 = v - hi
    mid = r1.astype(BF16).astype(F32)
    return hi, mid, r1 - mid


def _cumsum_kernel(x_ref, c_ref, aug_ref, st_ref):
    x = x_ref[0]
    s = x.shape[1]
    lane = lax.broadcasted_iota(jnp.int32, x.shape, 1) & (LANES - 1)
    sh = 1
    while sh < LANES:
        x = x + jnp.where(lane >= sh, pltpu.roll(x, sh, axis=1), 0.0)
        sh *= 2
    carry = jnp.zeros((x.shape[0], 1), F32)
    for c in range(s // LANES):
        blk = x[:, c * LANES:(c + 1) * LANES] + carry
        c_ref[0, :, c * LANES:(c + 1) * LANES] = blk
        carry = blk[:, LANES - 1:LANES]

    ones = jnp.ones((3, s), F32)
    st_ref[...] = jnp.zeros(st_ref.shape, F32)
    for p in range(x.shape[0] // 2):
        for which, base in ((0, HEAD_DIM), (1, 0)):
            negc = c_ref[0, 2 * p + which:2 * p + which + 1, :] * -LOG2E
            hi, mid, lo = _split3(negc)
            st_ref[base:base + 1, :] = hi
            st_ref[base + 1:base + 2, :] = mid
            st_ref[base + 2:base + 3, :] = lo
            st_ref[base + 3:base + 6, :] = ones
        for c in range(s // LANES):
            blk = st_ref[:, c * LANES:(c + 1) * LANES]
            aug_ref[0, p, c * LANES:(c + 1) * LANES, :] = blk.T.astype(BF16)


def _cumsum_seq(lf_t):
    b, h, s = lf_t.shape
    return pl.pallas_call(
        _cumsum_kernel,
        grid=(b,),
        in_specs=[pl.BlockSpec((1, h, s), lambda i: (i, 0, 0))],
        out_specs=[
            pl.BlockSpec((1, h, s), lambda i: (i, 0, 0)),
            pl.BlockSpec((1, h // 2, s, LANES), lambda i: (i, 0, 0, 0)),
        ],
        out_shape=[
            jax.ShapeDtypeStruct((b, h, s), F32),
            jax.ShapeDtypeStruct((b, h // 2, s, LANES), BF16),
        ],
        scratch_shapes=[pltpu.VMEM((LANES, s), F32)],
        compiler_params=_cparams("parallel"),
        name="fox_cumsum",
    )(lf_t)


def _fox_stages(qt_ref, k_ref, aug_ref, vt_ref, c_ref, o_ref, m_ref, acc_ref, sa_ref, sb_ref, *, tq, tk):
    hp = pl.program_id(1)
    qi = pl.program_id(2)
    q0 = qi * tq
    qt = qt_ref[0].astype(F32)
    sub = lax.broadcasted_iota(jnp.int32, (LANES, tq), 0)
    lane = lax.broadcasted_iota(jnp.int32, (tk, LANES), 1)
    vsub = lax.broadcasted_iota(jnp.int32, (LANES, tk), 0)
    own_rows = [sub < HEAD_DIM, sub >= HEAD_DIM]
    own_lanes = [lane < HEAD_DIM, lane >= HEAD_DIM]
    own_vrows = [vsub < HEAD_DIM, vsub >= HEAD_DIM]

    qaug = []
    for r in range(2):
        base = HEAD_DIM * (1 - r)
        ci = c_ref[0, pl.ds(2 * hp + r, 1), :]
        hi, mid, lo = _split3(ci * LOG2E)
        t = jnp.where(own_rows[r], qt, 0.0)
        t = jnp.where((sub >= base) & (sub < base + 3), 1.0, t)
        t = jnp.where(sub == base + 3, hi, t)
        t = jnp.where(sub == base + 4, mid, t)
        t = jnp.where(sub == base + 5, lo, t)
        qaug.append(t.astype(BF16))

    m_ref[...] = jnp.full(m_ref.shape, NEG_INF, F32)
    acc_ref[...] = jnp.zeros(acc_ref.shape, F32)

    def scores(k0, s_ref, lo=0):
        kt = k_ref[0, pl.ds(k0, tk), :]
        ca = aug_ref[0, 0, pl.ds(k0, tk), :]
        for r in range(2):
            ka = jnp.where(own_lanes[r], kt, ca)
            s_ref[r, :, lo:] = _dot(ka, qaug[r][:, lo:])

    def softmax_pv(k0, s_ref, masked, lo=0):
        vt = vt_ref[0, :, pl.ds(k0, tk)]
        if masked:
            key = k0 + lax.broadcasted_iota(jnp.int32, (tk, tq - lo), 0)
            qry = q0 + lo + lax.broadcasted_iota(jnp.int32, (tk, tq - lo), 1)
            keep = key <= qry
        for r in range(2):
            st = s_ref[r, :, lo:]
            if masked:
                st = jnp.where(keep, st, NEG_INF)
            m_prev = m_ref[r, :, lo:]
            m_new = jnp.maximum(m_prev, jnp.max(st, axis=0, keepdims=True))
            alpha = jnp.exp2(m_prev - m_new)
            pt = jnp.exp2(st - m_new).astype(BF16)
            va = jnp.where(own_vrows[r], vt, 1.0)
            acc_ref[r, :, lo:] = alpha * acc_ref[r, :, lo:] + _dot(va, pt)
            m_ref[r, :, lo:] = m_new

    tile = lambda j: pl.multiple_of(j * tk, tk)

    def start():
        scores(tile(0), sa_ref)

    def half_a(i):
        scores(tile(2 * i + 1), sb_ref)
        softmax_pv(tile(2 * i), sa_ref, False)

    def half_b(i):
        scores(tile(2 * i + 2), sa_ref)
        softmax_pv(tile(2 * i + 1), sb_ref, False)

    def finish():
        scores(tile(2 * qi + 1), sb_ref, lo=tk)
        softmax_pv(tile(2 * qi), sa_ref, True)
        softmax_pv(tile(2 * qi + 1), sb_ref, True, lo=tk)
        ot = jnp.zeros((LANES, tq), F32)
        for r in range(2):
            a = acc_ref[r]
            den = a[HEAD_DIM * (1 - r):HEAD_DIM * (1 - r) + 1, :]
            ot = ot + jnp.where(own_rows[r], a / den, 0.0)
        o_ref[0] = ot.T.astype(BF16)

    return start, half_a, half_b, finish


def _sb_stages(qt_ref, k_ref, vt_ref, o_ref, run_ref, acc_ref, za_ref, zb_ref, *, tq, cw):
    qi = pl.program_id(2)
    q0 = qi * tq
    qt = qt_ref[0]
    sub = lax.broadcasted_iota(jnp.int32, (LANES, tq), 0)
    own_rows = [sub < HEAD_DIM, sub >= HEAD_DIM]
    qm = [jnp.where(mk, qt, 0.0) for mk in own_rows]

    ss = lax.broadcasted_iota(jnp.int32, (cw, cw), 0)
    jj = lax.broadcasted_iota(jnp.int32, (cw, cw), 1)
    tri = jnp.where(jj > ss, 1.0, 0.0).astype(BF16)
    tri = jnp.concatenate([tri, jnp.ones((SUM_ROWS, cw), BF16)], axis=0)

    run_ref[...] = jnp.zeros(run_ref.shape, F32)
    acc_ref[...] = jnp.zeros(acc_ref.shape, F32)

    def start_of(c):
        return pl.multiple_of(jnp.maximum(q0 + tq - (c + 1) * cw, 0), cw)

    def logits(c, z_ref, lo=0):
        kt = k_ref[0, pl.ds(start_of(c), cw), :]
        for r in range(2):
            z_ref[r, :, lo:] = _dot(kt, qm[r][:, lo:])

    def weights_pv(c, z_ref, masked, lo=0):
        k0 = start_of(c)
        vt = vt_ref[0, :, pl.ds(k0, cw)]
        if masked:
            key = k0 + lax.broadcasted_iota(jnp.int32, (cw, tq - lo), 0)
            qry = q0 + lo + lax.broadcasted_iota(jnp.int32, (cw, tq - lo), 1)
            keep = key < qry
        zs, sps, laters, ws = [], [], [], []
        for r in range(2):
            z = z_ref[r, :, lo:]
            e = jnp.exp2(-jnp.abs(z))
            sp = jnp.maximum(z, 0.0) + jnp.log(1.0 + e) * LOG2E
            if masked:
                sp = jnp.where(keep, sp, 0.0)
            zs.append(z)
            sps.append(sp)
        for r in range(2):
            laters.append(_dot(tri, sps[r].astype(BF16)))
        for r in range(2):
            run = run_ref[r, :, lo:]
            a = jnp.exp2((zs[r] - sps[r]) - (laters[r][:cw] + run))
            if masked:
                a = jnp.where(keep, a, 0.0)
            ws.append(a.astype(BF16))
            run_ref[r, :, lo:] = run + laters[r][cw:cw + 1]
        for r in range(2):
            acc_ref[r, :, lo:] = acc_ref[r, :, lo:] + _dot(vt, ws[r])

    def start():
        logits(0, za_ref, lo=cw)
        logits(1, zb_ref)
        weights_pv(0, za_ref, True, lo=cw)
        logits(2, za_ref)
        weights_pv(1, zb_ref, True)

    def half_a(i):
        logits(2 * i + 1, zb_ref)
        weights_pv(2 * i, za_ref, False)

    def half_b(i):
        logits(2 * i + 2, za_ref)
        weights_pv(2 * i + 1, zb_ref, False)

    def finish():
        ot = jnp.where(own_rows[0], acc_ref[0], acc_ref[1])
        o_ref[0] = ot.T.astype(BF16)

    return start, half_a, half_b, finish


def _even_attn_kernel(fq_ref, fk_ref, aug_ref, fv_ref, c_ref, sq_ref, sk_ref, sv_ref, of_ref, os_ref,
                      m_ref, facc_ref, sa_ref, sb_ref, run_ref, sacc_ref, za_ref, zb_ref, *, tq, tk):
    qi = pl.program_id(2)
    f_start, f_a, f_b, f_finish = _fox_stages(fq_ref, fk_ref, aug_ref, fv_ref, c_ref, of_ref,
                                              m_ref, facc_ref, sa_ref, sb_ref, tq=tq, tk=tk)
    s_start, s_a, s_b, s_finish = _sb_stages(sq_ref, sk_ref, sv_ref, os_ref,
                                             run_ref, sacc_ref, za_ref, zb_ref, tq=tq, cw=tk)
    f_start()
    s_start()

    def body(i, carry):
        f_a(i)
        s_a(i + 1)
        f_b(i)
        s_b(i + 1)
        return carry

    lax.fori_loop(0, qi, body, 0)
    f_finish()
    s_finish()


def _even_attention(k3, ht, caug, crow, tq=512):
    b, s, _ = k3.shape
    nhp = N_HEADS_FOX // 2
    tk = tq // 2
    qt_spec = lambda blk: pl.BlockSpec((1, LANES, tq), lambda bi, hp, qi: (bi, blk * nhp + hp, qi))
    k_spec = lambda blk: pl.BlockSpec((1, s, LANES), lambda bi, hp, qi: (bi, 0, blk * nhp + hp))
    vt_spec = lambda blk: pl.BlockSpec((1, LANES, s), lambda bi, hp, qi: (bi, blk * nhp + hp, 0))
    out_spec = pl.BlockSpec((1, tq, LANES), lambda bi, hp, qi: (bi, qi, hp))
    return pl.pallas_call(
        functools.partial(_even_attn_kernel, tq=tq, tk=tk),
        grid=(b, nhp, s // tq),
        in_specs=[
            qt_spec(0), k_spec(0),
            pl.BlockSpec((1, 1, s, LANES), lambda bi, hp, qi: (bi, hp, 0, 0)),
            vt_spec(1),
            pl.BlockSpec((1, N_HEADS_FOX, tq), lambda bi, hp, qi: (bi, 0, qi)),
            qt_spec(2), k_spec(1), vt_spec(3),
        ],
        out_specs=[out_spec, out_spec],
        out_shape=[jax.ShapeDtypeStruct((b, s, FOX_W), BF16), jax.ShapeDtypeStruct((b, s, SB_W), BF16)],
        scratch_shapes=[
            pltpu.VMEM((2, 1, tq), F32), pltpu.VMEM((2, LANES, tq), F32),
            pltpu.VMEM((2, tk, tq), F32), pltpu.VMEM((2, tk, tq), F32),
            pltpu.VMEM((2, 1, tq), F32), pltpu.VMEM((2, LANES, tq), F32),
            pltpu.VMEM((2, tk, tq), F32), pltpu.VMEM((2, tk, tq), F32),
        ],
        compiler_params=_cparams("parallel", "parallel", "arbitrary"),
        name="even_attn",
    )(ht, k3, caug, ht, crow, ht, k3, ht)


def _swa_proj_kernel(x_ref, wk_ref, wt_ref, cos_ref, sa_ref, sb_ref, cost_ref, sint_ref, k_ref, qt_ref, vt_ref):
    xb = x_ref[...].astype(BF16)
    half = ROPE_DIM // 2
    yk = _dot(xb, wk_ref[...])
    for c in range(yk.shape[1] // LANES):
        blk = yk[:, c * LANES:(c + 1) * LANES]
        blk = (blk * cos_ref[...] + pltpu.roll(blk, half, axis=1) * sa_ref[...]
               + pltpu.roll(blk, LANES - half, axis=1) * sb_ref[...])
        k_ref[:, c * LANES:(c + 1) * LANES] = blk.astype(BF16)
    yt = _dot_nt(wt_ref[...], xb)
    cos = cost_ref[...]
    sin = sint_ref[...]
    nq = N_HEADS_SWA * HEAD_DIM
    for h in range(N_HEADS_SWA):
        base = h * HEAD_DIM
        x1 = yt[base:base + half]
        x2 = yt[base + half:base + ROPE_DIM]
        rot = jnp.concatenate([x1 * cos - x2 * sin, x2 * cos + x1 * sin], axis=0)
        qt_ref[0, base:base + ROPE_DIM, :] = rot.astype(BF16)
        qt_ref[0, base + ROPE_DIM:base + HEAD_DIM, :] = yt[base + ROPE_DIM:base + HEAD_DIM].astype(BF16)
    vt_ref[0] = yt[nq:].astype(BF16)


def _rope_tables(s):
    half = ROPE_DIM // 2
    inv = ROPE_THETA ** (-jnp.arange(half, dtype=F32) * 2.0 / ROPE_DIM)
    ang = jnp.arange(s, dtype=F32)[:, None] * inv[None, :]
    cos, sin = jnp.cos(ang), jnp.sin(ang)
    d = jnp.arange(LANES) % HEAD_DIM
    idx = d % half
    cos_t = jnp.where(d[None, :] < ROPE_DIM, cos[:, idx], 1.0)
    sin_a = jnp.where((d[None, :] >= half) & (d[None, :] < ROPE_DIM), sin[:, idx], 0.0)
    sin_b = jnp.where(d[None, :] < half, -sin[:, idx], 0.0)
    return cos_t.astype(F32), sin_a.astype(F32), sin_b.astype(F32), cos.T.astype(F32), sin.T.astype(F32)


def _swa_proj(x2d, wk, wt, tables, b, s, tm=512):
    t = x2d.shape[0]
    nk = wk.shape[1]
    nq = N_HEADS_SWA * HEAD_DIM
    nv = wt.shape[0] - nq
    spt = s // tm
    half = ROPE_DIM // 2
    lane_tab = pl.BlockSpec((tm, LANES), lambda i: (i % spt, 0))
    row_tab = pl.BlockSpec((half, tm), lambda i: (0, i % spt))
    return pl.pallas_call(
        _swa_proj_kernel,
        grid=(t // tm,),
        in_specs=[
            pl.BlockSpec((tm, D_MODEL), lambda i: (i, 0)),
            pl.BlockSpec(wk.shape, lambda i: (0, 0)),
            pl.BlockSpec(wt.shape, lambda i: (0, 0)),
            lane_tab, lane_tab, lane_tab, row_tab, row_tab,
        ],
        out_specs=[
            pl.BlockSpec((tm, nk), lambda i: (i, 0)),
            pl.BlockSpec((1, nq, tm), lambda i: (i // spt, 0, i % spt)),
            pl.BlockSpec((1, nv, tm), lambda i: (i // spt, 0, i % spt)),
        ],
        out_shape=[
            jax.ShapeDtypeStruct((t, nk), BF16),
            jax.ShapeDtypeStruct((b, nq, s), BF16),
            jax.ShapeDtypeStruct((b, nv, s), BF16),
        ],
        compiler_params=_cparams("parallel"),
        name="swa_proj",
    )(x2d, wk, wt, *tables)


def _swa_kernel(sink_ref, qt_ref, k_ref, vt_ref, o_ref, *, tq):
    qi = pl.program_id(1)
    q0 = qi * tq
    kw = tq + WINDOW
    kstart = pl.multiple_of(jnp.maximum(q0 - WINDOW, 0), WINDOW)
    kwin = k_ref[0, pl.ds(kstart, kw), :]
    vwin = vt_ref[0, :, pl.ds(kstart, kw)]
    key = kstart + lax.broadcasted_iota(jnp.int32, (kw, tq), 0)
    qry = q0 + lax.broadcasted_iota(jnp.int32, (kw, tq), 1)
    diff = qry - key
    keep = (diff >= 0) & (diff < WINDOW)
    sub = lax.broadcasted_iota(jnp.int32, (LANES, tq), 0)
    vsub = lax.broadcasted_iota(jnp.int32, (LANES, kw), 0)
    own_rows = [sub < HEAD_DIM, sub >= HEAD_DIM]
    own_vrows = [vsub < HEAD_DIM, vsub >= HEAD_DIM]
    group = N_HEADS_SWA // N_KV_SWA

    def scores(h):
        c, r = divmod(h, 2)
        g = h // group
        qblk = qt_ref[0, c * LANES:(c + 1) * LANES, :]
        return _dot(kwin[:, g * LANES:(g + 1) * LANES], jnp.where(own_rows[r], qblk, 0.0))

    def weights(h, st):
        st = jnp.where(keep, st, NEG_INF)
        sink = sink_ref[h] * LOG2E
        mx = jnp.maximum(jnp.max(st, axis=0, keepdims=True), sink)
        return jnp.exp2(st - mx).astype(BF16), jnp.exp2(sink - mx)

    def values(h, pt):
        r = h % 2
        g = h // group
        va = jnp.where(own_vrows[r], vwin[g * LANES:(g + 1) * LANES, :], 1.0)
        return _dot(va, pt)

    batches = [list(range(i, i + SWA_HEAD_BATCH)) for i in range(0, N_HEADS_SWA, SWA_HEAD_BATCH)]
    sts = [scores(h) for h in batches[0]]
    for bi, heads in enumerate(batches):
        nxt = [scores(h) for h in batches[bi + 1]] if bi + 1 < len(batches) else None
        pts = [weights(h, st) for h, st in zip(heads, sts)]
        accs = [values(h, pt) for h, (pt, _) in zip(heads, pts)]
        outs = []
        for h, acc, (_, esink) in zip(heads, accs, pts):
            r = h % 2
            den = acc[HEAD_DIM * (1 - r):HEAD_DIM * (1 - r) + 1, :] + esink
            outs.append(acc / den)
            if r == 1:
                c = h // 2
                ot = jnp.where(own_rows[0], outs[-2], outs[-1])
                o_ref[0, :, c * LANES:(c + 1) * LANES] = ot.T.astype(BF16)
        sts = nxt


def _swa_attention(qt3, k3, vt3, sinks, tq=256):
    b, nq, s = qt3.shape
    kern = functools.partial(_swa_kernel, tq=tq)
    return pl.pallas_call(
        kern,
        grid=(b, s // tq),
        in_specs=[
            pl.BlockSpec(memory_space=pltpu.SMEM),
            pl.BlockSpec((1, nq, tq), lambda bi, qi: (bi, 0, qi)),
            pl.BlockSpec((1, s, k3.shape[2]), lambda bi, qi: (bi, 0, 0)),
            pl.BlockSpec((1, vt3.shape[1], s), lambda bi, qi: (bi, 0, 0)),
        ],
        out_specs=pl.BlockSpec((1, tq, nq), lambda bi, qi: (bi, qi, 0)),
        out_shape=jax.ShapeDtypeStruct((b, s, nq), BF16),
        compiler_params=_cparams("parallel", "arbitrary"),
        name="swa_attn",
    )(sinks, qt3, k3, vt3)


def _layer_norm(z, g, b):
    mu = jnp.mean(z, axis=-1, keepdims=True)
    zc = z - mu
    var = jnp.mean(zc * zc, axis=-1, keepdims=True)
    return zc * lax.rsqrt(var + LN_EPS) * g + b


def _outln_kernel(*refs, n_in, routed):
    o_refs = refs[:n_in]
    w_ref, x_ref, g_ref, b_ref = refs[n_in:n_in + 4]
    rest = refs[n_in + 4:]
    h = None
    row0 = 0
    for o_ref in o_refs:
        n = o_ref.shape[1]
        part = _dot(o_ref[...], w_ref[row0:row0 + n, :])
        h = part if h is None else h + part
        row0 += n
    y = _layer_norm(DEEPNORM_ALPHA * x_ref[...] + h, g_ref[...], b_ref[...])
    if routed:
        rw_ref, rb_ref, y_ref, yb_ref, c_ref = rest
        c_ref[...] = _route(y, rw_ref[...], rb_ref[...])
    else:
        y_ref, yb_ref = rest
    y_ref[...] = y
    yb_ref[...] = y.astype(BF16)


def _out_proj_ln(os_, w, x2d, g, b, router=None, tm=512):
    t = x2d.shape[0]
    n_in = len(os_)
    row = lambda n: pl.BlockSpec((tm, n), lambda i: (i, 0))
    full = lambda a: pl.BlockSpec(a.shape, lambda i: (0, 0))
    in_specs = [row(o.shape[1]) for o in os_] + [full(w), row(D_MODEL), full(g), full(b)]
    out_specs = [row(D_MODEL)] * 2
    out_shape = [jax.ShapeDtypeStruct((t, D_MODEL), F32), jax.ShapeDtypeStruct((t, D_MODEL), BF16)]
    args = [*os_, w, x2d, g, b]
    if router is not None:
        in_specs += [full(a) for a in router]
        out_specs.append(row(LANES))
        out_shape.append(jax.ShapeDtypeStruct((t, LANES), F32))
        args += list(router)
    return pl.pallas_call(
        functools.partial(_outln_kernel, n_in=n_in, routed=router is not None),
        grid=(t // tm,),
        in_specs=in_specs,
        out_specs=out_specs,
        out_shape=out_shape,
        compiler_params=_cparams("parallel"),
        name="out_proj_ln",
    )(*args)


def _route(x, w, b):
    xh = x.astype(BF16)
    xl = (x - xh.astype(F32)).astype(BF16)
    wh = w.astype(BF16)
    wl = (w - wh.astype(F32)).astype(BF16)
    logits = _dot(xh, wh) + (_dot(xl, wh) + _dot(xh, wl)) + b
    lane = lax.broadcasted_iota(jnp.int32, logits.shape, 1)
    logits = jnp.where(lane < N_EXPERTS, logits, NEG_INF)
    m1 = jnp.max(logits, axis=1, keepdims=True)
    i1 = jnp.min(jnp.where(logits == m1, lane, LANES), axis=1, keepdims=True)
    rest = jnp.where(lane == i1, NEG_INF, logits)
    m2 = jnp.max(rest, axis=1, keepdims=True)
    i2 = jnp.min(jnp.where(rest == m2, lane, LANES), axis=1, keepdims=True)
    e2 = jnp.exp(m2 - m1)
    den = 1.0 + e2
    return jnp.where(lane == i1, 1.0 / den, 0.0) + jnp.where(lane == i2, e2 / den, 0.0)


def _ffn_kernel(eid_ref, valid_ref, x_ref, wg_ref, wu_ref, wd_ref, o_ref, acc_ref):
    i = pl.program_id(0)
    c = pl.program_id(1)
    last = c == pl.num_programs(1) - 1
    valid = valid_ref[i] != 0

    @pl.when(valid)
    def _():
        x = x_ref[...]
        gate = _dot(x, wg_ref[0])
        up = _dot(x, wu_ref[0])
        h = gate * jax.nn.sigmoid(gate) * up
        contrib = _dot(h.astype(BF16), wd_ref[0])

        @pl.when(c == 0)
        def _():
            acc_ref[...] = contrib

        @pl.when(c != 0)
        def _():
            acc_ref[...] += contrib

        @pl.when(last)
        def _():
            o_ref[...] = acc_ref[...].astype(o_ref.dtype)

    @pl.when(jnp.logical_not(valid) & last)
    def _():
        o_ref[...] = jnp.zeros(o_ref.shape, o_ref.dtype)


def _ffn(xb, w_gu, w_d, eids, valid, out_dtype, tm, fc, name):
    r = xb.shape[0]
    f = w_d.shape[1]
    ncf = f // fc
    in_specs = [
        pl.BlockSpec((tm, D_MODEL), lambda i, c, eid, ok: (i, 0)),
        pl.BlockSpec((1, D_MODEL, fc), lambda i, c, eid, ok: (eid[i], 0, c * ok[i])),
        pl.BlockSpec((1, D_MODEL, fc), lambda i, c, eid, ok: (eid[i], 0, ncf + c * ok[i])),
        pl.BlockSpec((1, fc, D_MODEL), lambda i, c, eid, ok: (eid[i], c * ok[i], 0)),
    ]
    return pl.pallas_call(
        _ffn_kernel,
        grid_spec=pltpu.PrefetchScalarGridSpec(
            num_scalar_prefetch=2,
            grid=(r // tm, ncf),
            in_specs=in_specs,
            out_specs=pl.BlockSpec((tm, D_MODEL), lambda i, c, eid, ok: (i, 0)),
            scratch_shapes=[pltpu.VMEM((tm, D_MODEL), F32)],
        ),
        out_shape=jax.ShapeDtypeStruct((r, D_MODEL), out_dtype),
        compiler_params=_cparams("parallel", "arbitrary"),
        name=name,
    )(eids, valid, xb, w_gu, w_gu, w_d)


def _moe_ffn_kernel(eid_ref, valid_ref, first_ref, x_ref, wgu_hbm, wd_hbm, o_ref,
                    cg_ref, cu_ref, cd_ref, sg_ref, su_ref, sd_ref, sem_ref, acc_ref, *, fc):
    i = pl.program_id(0)
    e = eid_ref[i]
    f = cd_ref.shape[0]
    n_chunks = f // fc

    def chunk_copies(c, slot):
        return (
            pltpu.make_async_copy(wgu_hbm.at[e, :, pl.ds(c * fc, fc)], sg_ref.at[slot], sem_ref.at[0, slot]),
            pltpu.make_async_copy(wgu_hbm.at[e, :, pl.ds(f + c * fc, fc)], su_ref.at[slot], sem_ref.at[1, slot]),
            pltpu.make_async_copy(wd_hbm.at[e, pl.ds(c * fc, fc), :], sd_ref.at[slot], sem_ref.at[2, slot]),
        )

    def compute(c):
        x = x_ref[...]
        gate = _dot(x, cg_ref[:, c * fc:(c + 1) * fc])
        up = _dot(x, cu_ref[:, c * fc:(c + 1) * fc])
        h = gate * jax.nn.sigmoid(gate) * up
        contrib = _dot(h.astype(BF16), cd_ref[c * fc:(c + 1) * fc, :])
        if c == 0:
            acc_ref[...] = contrib
        else:
            acc_ref[...] += contrib

    valid = valid_ref[i] != 0
    first = first_ref[i] != 0

    @pl.when(valid & first)
    def _():
        for cp in chunk_copies(0, 0):
            cp.start()
        for c in range(n_chunks):
            slot = c % 2
            if c + 1 < n_chunks:
                for cp in chunk_copies(c + 1, 1 - slot):
                    cp.start()
            for cp in chunk_copies(c, slot):
                cp.wait()
            cg_ref[:, c * fc:(c + 1) * fc] = sg_ref[slot].astype(BF16)
            cu_ref[:, c * fc:(c + 1) * fc] = su_ref[slot].astype(BF16)
            cd_ref[c * fc:(c + 1) * fc, :] = sd_ref[slot].astype(BF16)
            compute(c)
        o_ref[...] = acc_ref[...].astype(o_ref.dtype)

    @pl.when(valid & jnp.logical_not(first))
    def _():
        for c in range(n_chunks):
            compute(c)
        o_ref[...] = acc_ref[...].astype(o_ref.dtype)

    @pl.when(jnp.logical_not(valid))
    def _():
        o_ref[...] = jnp.zeros(o_ref.shape, o_ref.dtype)


def _moe_ffn(xs, w_gu, w_d, eids, valid, first, tm=MOE_TM, fc=MOE_FC):
    r = xs.shape[0]
    f = w_d.shape[1]
    return pl.pallas_call(
        functools.partial(_moe_ffn_kernel, fc=fc),
        grid_spec=pltpu.PrefetchScalarGridSpec(
            num_scalar_prefetch=3,
            grid=(r // tm,),
            in_specs=[
                pl.BlockSpec((tm, D_MODEL), lambda i, *_: (i, 0)),
                pl.BlockSpec(memory_space=pl.ANY),
                pl.BlockSpec(memory_space=pl.ANY),
            ],
            out_specs=pl.BlockSpec((tm, D_MODEL), lambda i, *_: (i, 0)),
            scratch_shapes=[
                pltpu.VMEM((D_MODEL, f), BF16),
                pltpu.VMEM((D_MODEL, f), BF16),
                pltpu.VMEM((f, D_MODEL), BF16),
                pltpu.VMEM((2, D_MODEL, fc), F32),
                pltpu.VMEM((2, D_MODEL, fc), F32),
                pltpu.VMEM((2, fc, D_MODEL), F32),
                pltpu.SemaphoreType.DMA((3, 2)),
                pltpu.VMEM((tm, D_MODEL), F32),
            ],
        ),
        out_shape=jax.ShapeDtypeStruct((r, D_MODEL), BF16),
        compiler_params=_cparams("arbitrary"),
        name="ffn_moe",
    )(eids, valid, first, xs, w_gu, w_d)


def _moe_gather_kernel(te_ref, clo_ref, nch_ref, s0_ref, ns_ref, x_ref, pos_ref, o_ref, acc_ref, *, tm, sr, n_chunks):
    r = pl.program_id(0)
    e = te_ref[r]
    sub_row = lax.broadcasted_iota(jnp.int32, (sr, tm), 0)
    acc_ref[...] = jnp.zeros(acc_ref.shape, F32)

    def body(i, carry):
        c = clo_ref[r] + i
        t0 = pl.multiple_of(c * tm, tm)
        p = pos_ref[pl.ds(e, 1), pl.ds(t0, tm)]

        def sub(j, carry2):
            r0 = pl.multiple_of((s0_ref[r * n_chunks + c] + j) * sr, sr)
            onehot = jnp.where(p == r * tm + r0 + sub_row, 1.0, 0.0).astype(BF16)
            acc_ref[pl.ds(r0, sr), :] += _dot(onehot, x_ref[pl.ds(t0, tm), :])
            return carry2

        lax.fori_loop(0, ns_ref[r * n_chunks + c], sub, 0)
        return carry

    lax.fori_loop(0, nch_ref[r], body, 0)
    o_ref[...] = acc_ref[...].astype(BF16)


def _moe_gather(xb, pos_t, tile_e, clo, nch, sub0, nsub, n_rows, tm=MOE_TM, sr=MOE_SR):
    t = xb.shape[0]
    return pl.pallas_call(
        functools.partial(_moe_gather_kernel, tm=tm, sr=sr, n_chunks=t // tm),
        grid_spec=pltpu.PrefetchScalarGridSpec(
            num_scalar_prefetch=5,
            grid=(n_rows // tm,),
            in_specs=[
                pl.BlockSpec((t, D_MODEL), lambda r, *_: (0, 0), pipeline_mode=pl.Buffered(1)),
                pl.BlockSpec(pos_t.shape, lambda r, *_: (0, 0)),
            ],
            out_specs=pl.BlockSpec((tm, D_MODEL), lambda r, *_: (r, 0)),
            scratch_shapes=[pltpu.VMEM((tm, D_MODEL), F32)],
        ),
        out_shape=jax.ShapeDtypeStruct((n_rows, D_MODEL), BF16),
        compiler_params=_cparams("arbitrary"),
        name="moe_gather",
    )(tile_e, clo, nch, sub0, nsub, xb, pos_t)


def _moe_combine_kernel(n_ref, cid_ref, eid_ref, ys_hbm, pos_ref, comb_ref, x_ref, *rest, tm, cs, n_slots, n_buf):
    *ple_refs, o_ref, buf_ref, sem_ref = rest
    i = pl.program_id(0)
    n = n_ref[i]
    base = i * n_slots

    def chunk_copy(k, slot):
        row0 = pl.multiple_of(cid_ref[base + k] * cs, cs)
        return pltpu.make_async_copy(ys_hbm.at[pl.ds(row0, cs), :], buf_ref.at[slot], sem_ref.at[slot])

    o_ref[...] = jnp.zeros(o_ref.shape, F32)

    for d in range(n_buf):

        @pl.when(d < n)
        def _():
            chunk_copy(d, d).start()

    def gathered(k, slot):
        e = eid_ref[base + k]
        lane = lax.broadcasted_iota(jnp.int32, pos_ref.shape, 1)
        pos_e = jnp.sum(jnp.where(lane == e, pos_ref[...], 0), axis=1, keepdims=True)
        lane_c = lax.broadcasted_iota(jnp.int32, comb_ref.shape, 1)
        g_e = jnp.sum(jnp.where(lane_c == e, comb_ref[...], 0.0), axis=1, keepdims=True)
        col = cid_ref[base + k] * cs + lax.broadcasted_iota(jnp.int32, (tm, cs), 1)
        onehot = jnp.where(pos_e == col, 1.0, 0.0).astype(BF16)
        return g_e * _dot(onehot, buf_ref[slot])

    def body(j, carry):
        k = 2 * j
        slot = k % n_buf
        chunk_copy(k, slot).wait()
        chunk_copy(k + 1, slot + 1).wait()
        o_ref[...] += gathered(k, slot) + gathered(k + 1, slot + 1)

        @pl.when(k + n_buf < n)
        def _():
            chunk_copy(k + n_buf, slot).start()
            chunk_copy(k + n_buf + 1, slot + 1).start()

        return carry

    lax.fori_loop(0, n // 2, body, 0)
    o_ref[...] = _ln_ple(x_ref[...], o_ref[...], ple_refs)


def _moe_combine(ys, pos, combine, n_need, cid, eid, x2d, ple_args, tm=MOE_TM, cs=MOE_CS):
    t = pos.shape[0]
    return pl.pallas_call(
        functools.partial(_moe_combine_kernel, tm=tm, cs=cs, n_slots=MOE_SLOTS, n_buf=MOE_RING),
        grid_spec=pltpu.PrefetchScalarGridSpec(
            num_scalar_prefetch=3,
            grid=(t // tm,),
            in_specs=[
                pl.BlockSpec(memory_space=pl.ANY),
                pl.BlockSpec((tm, N_EXPERTS), lambda i, *_: (i, 0)),
                pl.BlockSpec((tm, LANES), lambda i, *_: (i, 0)),
                pl.BlockSpec((tm, D_MODEL), lambda i, *_: (i, 0)),
            ] + _ple_specs(tm, ple_args),
            out_specs=pl.BlockSpec((tm, D_MODEL), lambda i, *_: (i, 0)),
            scratch_shapes=[pltpu.VMEM((MOE_RING, cs, D_MODEL), BF16), pltpu.SemaphoreType.DMA((MOE_RING,))],
        ),
        out_shape=jax.ShapeDtypeStruct((t, D_MODEL), F32),
        compiler_params=_cparams("arbitrary"),
        name="moe_combine",
    )(n_need, cid, eid, ys, pos, combine, x2d, *ple_args)


def _moe_plan(combine, tm=MOE_TM):
    t = combine.shape[0]
    n_tok_tiles = t // tm
    n_row_tiles = (2 * t) // tm + N_EXPERTS
    sel = (combine[:, :N_EXPERTS] != 0.0).astype(jnp.int32)
    csum = jnp.cumsum(sel, axis=0)
    count = csum[-1]
    ntile_e = (count + tm - 1) // tm
    tile_end = jnp.cumsum(ntile_e)
    gstart = (tile_end - ntile_e) * tm
    pos = jnp.where(sel != 0, gstart[None, :] + csum - sel, -1)
    rt = jnp.arange(n_row_tiles, dtype=jnp.int32)
    tile_e = jnp.minimum(jnp.sum(rt[:, None] >= tile_end[None, :], axis=1), N_EXPERTS - 1).astype(jnp.int32)
    valid = (rt < tile_end[-1]).astype(jnp.int32)
    prev_e = jnp.concatenate([jnp.full((1,), -1, jnp.int32), tile_e[:-1]])
    group_head = ((valid != 0) & (tile_e != prev_e)).astype(jnp.int32)
    cb = jnp.concatenate([jnp.zeros((1, N_EXPERTS), jnp.int32), csum[tm - 1::tm]], axis=0)
    off = rt * tm - gstart[tile_e]
    cb_t = cb[:, tile_e].T
    clo = jnp.sum(cb_t[:, 1:] <= off[:, None], axis=1).astype(jnp.int32)
    cend = jnp.sum(cb_t[:, :-1] < (off + tm)[:, None], axis=1).astype(jnp.int32)
    nch = jnp.where(valid != 0, jnp.maximum(cend - clo, 0), 0).astype(jnp.int32)
    clo = jnp.minimum(clo, n_tok_tiles - 1)
    r_lo = jnp.clip(cb_t[:, :-1] - off[:, None], 0, tm)
    r_hi = jnp.clip(cb_t[:, 1:] - off[:, None], 0, tm)
    sub0 = (r_lo // MOE_SR).astype(jnp.int32)
    nsub = jnp.where(r_hi > r_lo, (r_hi + MOE_SR - 1) // MOE_SR - sub0, 0).astype(jnp.int32)
    first = gstart[None, :] + cb[:-1]
    n_ie = cb[1:] - cb[:-1]
    per_e = MOE_SLOTS // N_EXPERTS
    k = jnp.arange(per_e, dtype=jnp.int32)
    c_lo = first // MOE_CS
    c_hi = (first + n_ie - 1) // MOE_CS
    cand = c_lo[..., None] + k
    need = ((n_ie[..., None] > 0) & (cand <= c_hi[..., None])).reshape(n_tok_tiles, MOE_SLOTS)
    cand = cand.reshape(n_tok_tiles, MOE_SLOTS)
    eids = jnp.broadcast_to(jnp.arange(N_EXPERTS, dtype=jnp.int32)[:, None], (N_EXPERTS, per_e)).reshape(-1)
    order = jnp.argsort(jnp.logical_not(need), axis=1, stable=True)
    cid = jnp.where(need, cand, 0)
    cid = jnp.take_along_axis(cid, order, axis=1).reshape(-1).astype(jnp.int32)
    need_sorted = jnp.take_along_axis(need, order, axis=1)
    eid = jnp.where(need_sorted, eids[order], -1).reshape(-1).astype(jnp.int32)
    n_need = jnp.sum(need, axis=1).astype(jnp.int32)
    n_need = n_need + n_need % 2
    return dict(pos=pos.astype(jnp.int32), tile_e=tile_e, valid=valid, first=group_head, clo=clo, nch=nch,
                sub0=sub0.reshape(-1), nsub=nsub.reshape(-1),
                n_need=n_need, cid=cid, eid=eid, n_rows=n_row_tiles * tm)


def _moe(xb, x2d, combine, w_gu, w_d, first_expert, ple_args):
    plan = _moe_plan(combine)
    xs = _moe_gather(xb, plan["pos"].T, plan["tile_e"], plan["clo"], plan["nch"], plan["sub0"], plan["nsub"],
                     plan["n_rows"])
    ys = _moe_ffn(xs, w_gu, w_d, plan["tile_e"] + first_expert, plan["valid"], plan["first"])
    return _moe_combine(ys, plan["pos"], combine, plan["n_need"], plan["cid"], plan["eid"], x2d, ple_args)


def _ln_ple(x, h, ple_refs):
    g_ref, b_ref, p_ref, wg_ref, wp_ref = ple_refs
    y = _layer_norm(DEEPNORM_ALPHA * x + h, g_ref[...], b_ref[...])
    gate = jax.nn.sigmoid(_dot(y.astype(BF16), wg_ref[...]))
    proj = _dot(p_ref[...].astype(BF16), wp_ref[...])
    return y + gate * proj


def _ple_specs(tm, ple_args):
    g, b, p2d, wg, wp = ple_args
    ign = lambda f: (lambda *idx: f(idx[0]))
    full = lambda a: pl.BlockSpec(a.shape, ign(lambda i: (0, 0)))
    return [full(g), full(b), pl.BlockSpec((tm, D_PLE), ign(lambda i: (i, 0))), full(wg), full(wp)]


def _ffn_ple_kernel(xb_ref, wgu_ref, wd_ref, x_ref, *rest, fc):
    *ple_refs, o_ref, acc_ref = rest
    f = wd_ref.shape[0]
    xb = xb_ref[...]
    for c in range(f // fc):
        gate = _dot(xb, wgu_ref[:, c * fc:(c + 1) * fc])
        up = _dot(xb, wgu_ref[:, f + c * fc:f + (c + 1) * fc])
        h = gate * jax.nn.sigmoid(gate) * up
        contrib = _dot(h.astype(BF16), wd_ref[c * fc:(c + 1) * fc, :])
        if c == 0:
            acc_ref[...] = contrib
        else:
            acc_ref[...] += contrib
    o_ref[...] = _ln_ple(x_ref[...], acc_ref[...], ple_refs)


def _ffn_ple(xb, x2d, w_gu, w_d, ple_args, tm=512, fc=1408):
    t = xb.shape[0]
    row = lambda n: pl.BlockSpec((tm, n), lambda i: (i, 0))
    resident = lambda a: pl.BlockSpec(a.shape, lambda i: (0, 0), pipeline_mode=pl.Buffered(1))
    return pl.pallas_call(
        functools.partial(_ffn_ple_kernel, fc=fc),
        grid=(t // tm,),
        in_specs=[row(D_MODEL), resident(w_gu), resident(w_d), row(D_MODEL)] + _ple_specs(tm, ple_args),
        out_specs=row(D_MODEL),
        out_shape=jax.ShapeDtypeStruct((t, D_MODEL), F32),
        scratch_shapes=[pltpu.VMEM((tm, D_MODEL), F32)],
        compiler_params=_cparams("parallel"),
        name="ffn_dense",
    )(xb, w_gu, w_d, x2d, *ple_args)


def _even_mixer(x2d, b, s, w_in, b_f, w_out):
    nf, ns = FOX_W, SB_W
    o_f = 3 * nf
    o_s = o_f + N_HEADS_FOX
    qa, ka, va = w_in[:, :nf], w_in[:, nf:2 * nf], w_in[:, 2 * nf:o_f]
    qs, ks, vs = w_in[:, o_s:o_s + ns], w_in[:, o_s + ns:o_s + 2 * ns], w_in[:, o_s + 2 * ns:]
    w_k = jnp.concatenate([ka, ks], axis=1).astype(BF16)
    w_t = jnp.concatenate([qa * Q_SCALE, va, qs * Q_SCALE, vs], axis=1).T.astype(BF16)
    w_f = jnp.pad(w_in[:, o_f:o_s], ((0, 0), (0, LANES - N_HEADS_FOX))).astype(BF16)
    bias_f = jnp.pad(b_f, (0, LANES - N_HEADS_FOX)).reshape(1, LANES)
    k2d, ht, lf = _ab_proj(x2d, w_k, w_t, w_f, bias_f, b, s)
    lf_t = lf.reshape(b, s, LANES)[:, :, :N_HEADS_FOX].transpose(0, 2, 1)
    crow, caug = _cumsum_seq(lf_t)
    k3 = k2d.reshape(b, s, -1)
    oa, ob = _even_attention(k3, ht, caug, crow)
    return [oa.reshape(b * s, nf), ob.reshape(b * s, ns)], w_out.astype(BF16)


def _odd_mixer(x2d, b, s, w_qkv, sinks, w_out, tables):
    nq = N_HEADS_SWA * HEAD_DIM
    wq = w_qkv[:, :nq]
    wk = [w_qkv[:, nq + i * HEAD_DIM: nq + (i + 1) * HEAD_DIM] for i in range(N_KV_SWA)]
    wv = [w_qkv[:, nq + (N_KV_SWA + i) * HEAD_DIM: nq + (N_KV_SWA + i + 1) * HEAD_DIM] for i in range(N_KV_SWA)]
    dup = lambda ws: [w for w in ws for _ in range(2)]
    w_k = jnp.concatenate(dup(wk), axis=1).astype(BF16)
    w_t = jnp.concatenate([wq * Q_SCALE] + dup(wv), axis=1).T.astype(BF16)
    k2d, qt, vt = _swa_proj(x2d, w_k, w_t, tables, b, s)
    o = _swa_attention(qt, k2d.reshape(b, s, -1), vt, sinks)
    return [o.reshape(b * s, nq)], w_out.astype(BF16)


def kernel(x, p, ln_mix_g, ln_mix_b, ln_ffn_g, ln_ffn_b, ab_w_in, ab_b_f, ab_w_out, c_w_qkv, c_sinks, c_w_out,
           ffn_w_gate_up, ffn_w_down, router_w, router_b, moe_w_gate_up, moe_w_down, ple_w_gate, ple_w_proj):
    b, s, d = x.shape
    t = b * s
    x2d = x.reshape(t, d)
    tables = _rope_tables(s)
    row = lambda v: v.reshape(1, -1)
    moe_gu = moe_w_gate_up.reshape((-1,) + moe_w_gate_up.shape[2:])
    moe_d = moe_w_down.reshape((-1,) + moe_w_down.shape[2:])
    for i in range(DEPTH):
        j = i // 2
        if i % 2 == 0:
            os_, ws = _even_mixer(x2d, b, s, ab_w_in[j], ab_b_f[j], ab_w_out[j])
        else:
            os_, ws = _odd_mixer(x2d, b, s, c_w_qkv[j], c_sinks[j], c_w_out[j], tables)
        ple_args = (row(ln_ffn_g[i]), row(ln_ffn_b[i]), p[i].reshape(t, D_PLE),
                    ple_w_gate[i].astype(BF16), ple_w_proj[i].astype(BF16))
        if i % 2 == 0:
            x2d, xb = _out_proj_ln(os_, ws, x2d, row(ln_mix_g[i]), row(ln_mix_b[i]))
            x2d = _ffn_ple(xb, x2d, ffn_w_gate_up[j].astype(BF16), ffn_w_down[j].astype(BF16), ple_args)
        else:
            rw = jnp.pad(router_w[j], ((0, 0), (0, LANES - N_EXPERTS)))
            rb = jnp.pad(router_b[j], (0, LANES - N_EXPERTS)).reshape(1, LANES)
            x2d, xb, combine = _out_proj_ln(os_, ws, x2d, row(ln_mix_g[i]), row(ln_mix_b[i]), router=(rw, rb))
            x2d = _moe(xb, x2d, combine, moe_gu, moe_d, j * N_EXPERTS, ple_args)
    return x2d.reshape(b, s, d)
```

```python
import functools
import math

import jax
import jax.numpy as jnp
from jax import lax
from jax.experimental import pallas as pl
from jax.experimental.pallas import tpu as pltpu

F32 = jnp.float32
BF16 = jnp.bfloat16

D_MODEL = 1024
HEAD_DIM = 64
LANES = 128
N_HEADS_FOX = 8
N_HEADS_SB = 8
N_HEADS_SWA = 16
N_KV_SWA = 2
WINDOW = 128
ROPE_THETA = 500000.0
ROPE_DIM = HEAD_DIM // 4
N_EXPERTS = 8
D_PLE = 256
LN_EPS = 1e-5
DEPTH = 4
DEEPNORM_ALPHA = (2.0 * DEPTH) ** 0.25
ATTN_SCALE = HEAD_DIM ** -0.5
FOX_W = N_HEADS_FOX * HEAD_DIM
SB_W = N_HEADS_SB * HEAD_DIM
VMEM_LIMIT = 56 * 1024 * 1024

NEG_INF = float("-inf")
LOG2E = math.log2(math.e)
Q_SCALE = ATTN_SCALE * LOG2E
SUM_ROWS = 16
SWA_HEAD_BATCH = 4
MOE_TM = 512
MOE_SR = 128
MOE_FC = 512
MOE_CS = 256
MOE_SLOTS = N_EXPERTS * (MOE_TM // MOE_CS + 1)
MOE_RING = 8


def _cparams(*sem):
    return pltpu.CompilerParams(dimension_semantics=sem, vmem_limit_bytes=VMEM_LIMIT)


def _dot(a, b):
    return jnp.dot(a, b, preferred_element_type=F32)


def _dot_nt(a, b):
    return lax.dot_general(a, b, (((1,), (1,)), ((), ())), preferred_element_type=F32)


def _log_sigmoid(x):
    return jnp.minimum(x, 0.0) - jnp.log1p(jnp.exp(-jnp.abs(x)))


def _head_masks(shape):
    lane = lax.broadcasted_iota(jnp.int32, shape, len(shape) - 1)
    return lane < HEAD_DIM, lane >= HEAD_DIM


def _ab_proj_kernel(x_ref, wk_ref, wt_ref, wf_ref, bf_ref, k_ref, ht_ref, lf_ref):
    xb = x_ref[...].astype(BF16)
    k_ref[...] = _dot(xb, wk_ref[...]).astype(BF16)
    ht_ref[0] = _dot_nt(wt_ref[...], xb).astype(BF16)
    lf_ref[...] = _log_sigmoid(_dot(xb, wf_ref[...]) + bf_ref[...])


def _ab_proj(x2d, wk, wt, wf, bf, b, s, tm=512):
    t = x2d.shape[0]
    nk = wk.shape[1]
    nt = wt.shape[0]
    spt = s // tm
    return pl.pallas_call(
        _ab_proj_kernel,
        grid=(t // tm,),
        in_specs=[
            pl.BlockSpec((tm, D_MODEL), lambda i: (i, 0)),
            pl.BlockSpec((D_MODEL, nk), lambda i: (0, 0)),
            pl.BlockSpec((nt, D_MODEL), lambda i: (0, 0)),
            pl.BlockSpec((D_MODEL, LANES), lambda i: (0, 0)),
            pl.BlockSpec((1, LANES), lambda i: (0, 0)),
        ],
        out_specs=[
            pl.BlockSpec((tm, nk), lambda i: (i, 0)),
            pl.BlockSpec((1, nt, tm), lambda i: (i // spt, 0, i % spt)),
            pl.BlockSpec((tm, LANES), lambda i: (i, 0)),
        ],
        out_shape=[
            jax.ShapeDtypeStruct((t, nk), BF16),
            jax.ShapeDtypeStruct((b, nt, s), BF16),
            jax.ShapeDtypeStruct((t, LANES), F32),
        ],
        compiler_params=_cparams("parallel"),
        name="ab_proj",
    )(x2d, wk, wt, wf, bf)


def _split3(v):
    hi = v.astype(BF16).astype(F32)
    r1 = v - hi
    mid = r1.astype(BF16).astype(F32)
    return hi, mid, r1 - mid


def _cumsum_kernel(x_ref, c_ref, aug_ref, st_ref):
    x = x_ref[0]
    s = x.shape[1]
    lane = lax.broadcasted_iota(jnp.int32, x.shape, 1) & (LANES - 1)
    sh = 1
    while sh < LANES:
        x = x + jnp.where(lane >= sh, pltpu.roll(x, sh, axis=1), 0.0)
        sh *= 2
    carry = jnp.zeros((x.shape[0], 1), F32)
    for c in range(s // LANES):
        blk = x[:, c * LANES:(c + 1) * LANES] + carry
        c_ref[0, :, c * LANES:(c + 1) * LANES] = blk
        carry = blk[:, LANES - 1:LANES]

    ones = jnp.ones((3, s), F32)
    st_ref[...] = jnp.zeros(st_ref.shape, F32)
    for p in range(x.shape[0] // 2):
        for which, base in ((0, HEAD_DIM), (1, 0)):
            negc = c_ref[0, 2 * p + which:2 * p + which + 1, :] * -LOG2E
            hi, mid, lo = _split3(negc)
            st_ref[base:base + 1, :] = hi
            st_ref[base + 1:base + 2, :] = mid
            st_ref[base + 2:base + 3, :] = lo
            st_ref[base + 3:base + 6, :] = ones
        for c in range(s // LANES):
            blk = st_ref[:, c * LANES:(c + 1) * LANES]
            aug_ref[0, p, c * LANES:(c + 1) * LANES, :] = blk.T.astype(BF16)


def _cumsum_seq(lf_t):
    b, h, s = lf_t.shape
    return pl.pallas_call(
        _cumsum_kernel,
        grid=(b,),
        in_specs=[pl.BlockSpec((1, h, s), lambda i: (i, 0, 0))],
        out_specs=[
            pl.BlockSpec((1, h, s), lambda i: (i, 0, 0)),
            pl.BlockSpec((1, h // 2, s, LANES), lambda i: (i, 0, 0, 0)),
        ],
        out_shape=[
            jax.ShapeDtypeStruct((b, h, s), F32),
            jax.ShapeDtypeStruct((b, h // 2, s, LANES), BF16),
        ],
        scratch_shapes=[pltpu.VMEM((LANES, s), F32)],
        compiler_params=_cparams("parallel"),
        name="fox_cumsum",
    )(lf_t)


def _fox_stages(qt_ref, k_ref, aug_ref, vt_ref, c_ref, o_ref, m_ref, acc_ref, sa_ref, sb_ref, *, tq, tk):
    hp = pl.program_id(1)
    qi = pl.program_id(2)
    q0 = qi * tq
    qt = qt_ref[0].astype(F32)
    sub = lax.broadcasted_iota(jnp.int32, (LANES, tq), 0)
    lane = lax.broadcasted_iota(jnp.int32, (tk, LANES), 1)
    vsub = lax.broadcasted_iota(jnp.int32, (LANES, tk), 0)
    own_rows = [sub < HEAD_DIM, sub >= HEAD_DIM]
    own_lanes = [lane < HEAD_DIM, lane >= HEAD_DIM]
    own_vrows = [vsub < HEAD_DIM, vsub >= HEAD_DIM]

    qaug = []
    for r in range(2):
        base = HEAD_DIM * (1 - r)
        ci = c_ref[0, pl.ds(2 * hp + r, 1), :]
        hi, mid, lo = _split3(ci * LOG2E)
        t = jnp.where(own_rows[r], qt, 0.0)
        t = jnp.where((sub >= base) & (sub < base + 3), 1.0, t)
        t = jnp.where(sub == base + 3, hi, t)
        t = jnp.where(sub == base + 4, mid, t)
        t = jnp.where(sub == base + 5, lo, t)
        qaug.append(t.astype(BF16))

    m_ref[...] = jnp.full(m_ref.shape, NEG_INF, F32)
    acc_ref[...] = jnp.zeros(acc_ref.shape, F32)

    def scores(k0, s_ref, lo=0):
        kt = k_ref[0, pl.ds(k0, tk), :]
        ca = aug_ref[0, 0, pl.ds(k0, tk), :]
        for r in range(2):
            ka = jnp.where(own_lanes[r], kt, ca)
            s_ref[r, :, lo:] = _dot(ka, qaug[r][:, lo:])

    def softmax_pv(k0, s_ref, masked, lo=0):
        vt = vt_ref[0, :, pl.ds(k0, tk)]
        if masked:
            key = k0 + lax.broadcasted_iota(jnp.int32, (tk, tq - lo), 0)
            qry = q0 + lo + lax.broadcasted_iota(jnp.int32, (tk, tq - lo), 1)
            keep = key <= qry
        for r in range(2):
            st = s_ref[r, :, lo:]
            if masked:
                st = jnp.where(keep, st, NEG_INF)
            m_prev = m_ref[r, :, lo:]
            m_new = jnp.maximum(m_prev, jnp.max(st, axis=0, keepdims=True))
            alpha = jnp.exp2(m_prev - m_new)
            pt = jnp.exp2(st - m_new).astype(BF16)
            va = jnp.where(own_vrows[r], vt, 1.0)
            acc_ref[r, :, lo:] = alpha * acc_ref[r, :, lo:] + _dot(va, pt)
            m_ref[r, :, lo:] = m_new

    tile = lambda j: pl.multiple_of(j * tk, tk)

    def start():
        scores(tile(0), sa_ref)

    def half_a(i):
        scores(tile(2 * i + 1), sb_ref)
        softmax_pv(tile(2 * i), sa_ref, False)

    def half_b(i):
        scores(tile(2 * i + 2), sa_ref)
        softmax_pv(tile(2 * i + 1), sb_ref, False)

    def finish():
        scores(tile(2 * qi + 1), sb_ref, lo=tk)
        softmax_pv(tile(2 * qi), sa_ref, True)
        softmax_pv(tile(2 * qi + 1), sb_ref, True, lo=tk)
        ot = jnp.zeros((LANES, tq), F32)
        for r in range(2):
            a = acc_ref[r]
            den = a[HEAD_DIM * (1 - r):HEAD_DIM * (1 - r) + 1, :]
            ot = ot + jnp.where(own_rows[r], a / den, 0.0)
        o_ref[0] = ot.T.astype(BF16)

    return start, half_a, half_b, finish


def _sb_stages(qt_ref, k_ref, vt_ref, o_ref, run_ref, acc_ref, za_ref, zb_ref, *, tq, cw):
    qi = pl.program_id(2)
    q0 = qi * tq
    qt = qt_ref[0]
    sub = lax.broadcasted_iota(jnp.int32, (LANES, tq), 0)
    own_rows = [sub < HEAD_DIM, sub >= HEAD_DIM]
    qm = [jnp.where(mk, qt, 0.0) for mk in own_rows]

    ss = lax.broadcasted_iota(jnp.int32, (cw, cw), 0)
    jj = lax.broadcasted_iota(jnp.int32, (cw, cw), 1)
    tri = jnp.where(jj > ss, 1.0, 0.0).astype(BF16)
    tri = jnp.concatenate([tri, jnp.ones((SUM_ROWS, cw), BF16)], axis=0)

    run_ref[...] = jnp.zeros(run_ref.shape, F32)
    acc_ref[...] = jnp.zeros(acc_ref.shape, F32)

    def start_of(c):
        return pl.multiple_of(jnp.maximum(q0 + tq - (c + 1) * cw, 0), cw)

    def logits(c, z_ref, lo=0):
        kt = k_ref[0, pl.ds(start_of(c), cw), :]
        for r in range(2):
            z_ref[r, :, lo:] = _dot(kt, qm[r][:, lo:])

    def weights_pv(c, z_ref, masked, lo=0):
        k0 = start_of(c)
        vt = vt_ref[0, :, pl.ds(k0, cw)]
        if masked:
            key = k0 + lax.broadcasted_iota(jnp.int32, (cw, tq - lo), 0)
            qry = q0 + lo + lax.broadcasted_iota(jnp.int32, (cw, tq - lo), 1)
            keep = key < qry
        zs, sps, laters, ws = [], [], [], []
        for r in range(2):
            z = z_ref[r, :, lo:]
            neg_abs = pltpu.bitcast(pltpu.bitcast(z, jnp.uint32) | jnp.uint32(0x80000000), F32)
            e = jnp.exp2(neg_abs)
            sp = jnp.maximum(z, 0.0) + jnp.log(1.0 + e) * LOG2E
            if masked:
                sp = jnp.where(keep, sp, 0.0)
            zs.append(z)
            sps.append(sp)
        for r in range(2):
            laters.append(_dot(tri, sps[r].astype(BF16)))
        for r in range(2):
            run = run_ref[r, :, lo:]
            a = jnp.exp2((zs[r] - sps[r]) - (laters[r][:cw] + run))
            if masked:
                a = jnp.where(keep, a, 0.0)
            ws.append(a.astype(BF16))
            run_ref[r, :, lo:] = run + laters[r][cw:cw + 1]
        for r in range(2):
            acc_ref[r, :, lo:] = acc_ref[r, :, lo:] + _dot(vt, ws[r])

    def start():
        logits(0, za_ref, lo=cw)
        logits(1, zb_ref)
        weights_pv(0, za_ref, True, lo=cw)
        logits(2, za_ref)
        weights_pv(1, zb_ref, True)

    def half_a(i):
        logits(2 * i + 1, zb_ref)
        weights_pv(2 * i, za_ref, False)

    def half_b(i):
        logits(2 * i + 2, za_ref)
        weights_pv(2 * i + 1, zb_ref, False)

    def finish():
        ot = jnp.where(own_rows[0], acc_ref[0], acc_ref[1])
        o_ref[0] = ot.T.astype(BF16)

    return start, half_a, half_b, finish


def _even_attn_kernel(fq_ref, fk_ref, aug_ref, fv_ref, c_ref, sq_ref, sk_ref, sv_ref, of_ref, os_ref,
                      m_ref, facc_ref, sa_ref, sb_ref, run_ref, sacc_ref, za_ref, zb_ref, *, tq, tk):
    qi = pl.program_id(2)
    f_start, f_a, f_b, f_finish = _fox_stages(fq_ref, fk_ref, aug_ref, fv_ref, c_ref, of_ref,
                                              m_ref, facc_ref, sa_ref, sb_ref, tq=tq, tk=tk)
    s_start, s_a, s_b, s_finish = _sb_stages(sq_ref, sk_ref, sv_ref, os_ref,
                                             run_ref, sacc_ref, za_ref, zb_ref, tq=tq, cw=tk)
    f_start()
    s_start()

    def body(i, carry):
        f_a(i)
        s_a(i + 1)
        f_b(i)
        s_b(i + 1)
        return carry

    lax.fori_loop(0, qi, body, 0)
    f_finish()
    s_finish()


def _even_attention(k3, ht, caug, crow, tq=512):
    b, s, _ = k3.shape
    nhp = N_HEADS_FOX // 2
    tk = tq // 2
    qt_spec = lambda blk: pl.BlockSpec((1, LANES, tq), lambda bi, hp, qi: (bi, blk * nhp + hp, qi))
    k_spec = lambda blk: pl.BlockSpec((1, s, LANES), lambda bi, hp, qi: (bi, 0, blk * nhp + hp))
    vt_spec = lambda blk: pl.BlockSpec((1, LANES, s), lambda bi, hp, qi: (bi, blk * nhp + hp, 0))
    out_spec = pl.BlockSpec((1, tq, LANES), lambda bi, hp, qi: (bi, qi, hp))
    return pl.pallas_call(
        functools.partial(_even_attn_kernel, tq=tq, tk=tk),
        grid=(b, nhp, s // tq),
        in_specs=[
            qt_spec(0), k_spec(0),
            pl.BlockSpec((1, 1, s, LANES), lambda bi, hp, qi: (bi, hp, 0, 0)),
            vt_spec(1),
            pl.BlockSpec((1, N_HEADS_FOX, tq), lambda bi, hp, qi: (bi, 0, qi)),
            qt_spec(2), k_spec(1), vt_spec(3),
        ],
        out_specs=[out_spec, out_spec],
        out_shape=[jax.ShapeDtypeStruct((b, s, FOX_W), BF16), jax.ShapeDtypeStruct((b, s, SB_W), BF16)],
        scratch_shapes=[
            pltpu.VMEM((2, 1, tq), F32), pltpu.VMEM((2, LANES, tq), F32),
            pltpu.VMEM((2, tk, tq), F32), pltpu.VMEM((2, tk, tq), F32),
            pltpu.VMEM((2, 1, tq), F32), pltpu.VMEM((2, LANES, tq), F32),
            pltpu.VMEM((2, tk, tq), F32), pltpu.VMEM((2, tk, tq), F32),
        ],
        compiler_params=_cparams("parallel", "parallel", "arbitrary"),
        name="even_attn",
    )(ht, k3, caug, ht, crow, ht, k3, ht)


def _swa_proj_kernel(x_ref, wk_ref, wt_ref, cos_ref, sa_ref, sb_ref, cost_ref, sint_ref, k_ref, qt_ref, vt_ref):
    xb = x_ref[...].astype(BF16)
    half = ROPE_DIM // 2
    yk = _dot(xb, wk_ref[...])
    for c in range(yk.shape[1] // LANES):
        blk = yk[:, c * LANES:(c + 1) * LANES]
        blk = (blk * cos_ref[...] + pltpu.roll(blk, half, axis=1) * sa_ref[...]
               + pltpu.roll(blk, LANES - half, axis=1) * sb_ref[...])
        k_ref[:, c * LANES:(c + 1) * LANES] = blk.astype(BF16)
    yt = _dot_nt(wt_ref[...], xb)
    cos = cost_ref[...]
    sin = sint_ref[...]
    nq = N_HEADS_SWA * HEAD_DIM
    for h in range(N_HEADS_SWA):
        base = h * HEAD_DIM
        x1 = yt[base:base + half]
        x2 = yt[base + half:base + ROPE_DIM]
        rot = jnp.concatenate([x1 * cos - x2 * sin, x2 * cos + x1 * sin], axis=0)
        qt_ref[0, base:base + ROPE_DIM, :] = rot.astype(BF16)
        qt_ref[0, base + ROPE_DIM:base + HEAD_DIM, :] = yt[base + ROPE_DIM:base + HEAD_DIM].astype(BF16)
    vt_ref[0] = yt[nq:].astype(BF16)


def _rope_tables(s):
    half = ROPE_DIM // 2
    inv = ROPE_THETA ** (-jnp.arange(half, dtype=F32) * 2.0 / ROPE_DIM)
    ang = jnp.arange(s, dtype=F32)[:, None] * inv[None, :]
    cos, sin = jnp.cos(ang), jnp.sin(ang)
    d = jnp.arange(LANES) % HEAD_DIM
    idx = d % half
    cos_t = jnp.where(d[None, :] < ROPE_DIM, cos[:, idx], 1.0)
    sin_a = jnp.where((d[None, :] >= half) & (d[None, :] < ROPE_DIM), sin[:, idx], 0.0)
    sin_b = jnp.where(d[None, :] < half, -sin[:, idx], 0.0)
    return cos_t.astype(F32), sin_a.astype(F32), sin_b.astype(F32), cos.T.astype(F32), sin.T.astype(F32)


def _swa_proj(x2d, wk, wt, tables, b, s, tm=512):
    t = x2d.shape[0]
    nk = wk.shape[1]
    nq = N_HEADS_SWA * HEAD_DIM
    nv = wt.shape[0] - nq
    spt = s // tm
    half = ROPE_DIM // 2
    lane_tab = pl.BlockSpec((tm, LANES), lambda i: (i % spt, 0))
    row_tab = pl.BlockSpec((half, tm), lambda i: (0, i % spt))
    return pl.pallas_call(
        _swa_proj_kernel,
        grid=(t // tm,),
        in_specs=[
            pl.BlockSpec((tm, D_MODEL), lambda i: (i, 0)),
            pl.BlockSpec(wk.shape, lambda i: (0, 0)),
            pl.BlockSpec(wt.shape, lambda i: (0, 0)),
            lane_tab, lane_tab, lane_tab, row_tab, row_tab,
        ],
        out_specs=[
            pl.BlockSpec((tm, nk), lambda i: (i, 0)),
            pl.BlockSpec((1, nq, tm), lambda i: (i // spt, 0, i % spt)),
            pl.BlockSpec((1, nv, tm), lambda i: (i // spt, 0, i % spt)),
        ],
        out_shape=[
            jax.ShapeDtypeStruct((t, nk), BF16),
            jax.ShapeDtypeStruct((b, nq, s), BF16),
            jax.ShapeDtypeStruct((b, nv, s), BF16),
        ],
        compiler_params=_cparams("parallel"),
        name="swa_proj",
    )(x2d, wk, wt, *tables)


def _swa_kernel(sink_ref, qt_ref, k_ref, vt_ref, o_ref, *, tq):
    qi = pl.program_id(1)
    q0 = qi * tq
    kw = tq + WINDOW
    kstart = pl.multiple_of(jnp.maximum(q0 - WINDOW, 0), WINDOW)
    kwin = k_ref[0, pl.ds(kstart, kw), :]
    vwin = vt_ref[0, :, pl.ds(kstart, kw)]
    key = kstart + lax.broadcasted_iota(jnp.int32, (kw, tq), 0)
    qry = q0 + lax.broadcasted_iota(jnp.int32, (kw, tq), 1)
    diff = qry - key
    keep = (diff >= 0) & (diff < WINDOW)
    sub = lax.broadcasted_iota(jnp.int32, (LANES, tq), 0)
    vsub = lax.broadcasted_iota(jnp.int32, (LANES, kw), 0)
    own_rows = [sub < HEAD_DIM, sub >= HEAD_DIM]
    own_vrows = [vsub < HEAD_DIM, vsub >= HEAD_DIM]
    group = N_HEADS_SWA // N_KV_SWA

    def scores(h):
        c, r = divmod(h, 2)
        g = h // group
        qblk = qt_ref[0, c * LANES:(c + 1) * LANES, :]
        return _dot(kwin[:, g * LANES:(g + 1) * LANES], jnp.where(own_rows[r], qblk, 0.0))

    def weights(h, st):
        st = jnp.where(keep, st, NEG_INF)
        sink = sink_ref[h] * LOG2E
        mx = jnp.maximum(jnp.max(st, axis=0, keepdims=True), sink)
        return jnp.exp2(st - mx).astype(BF16), jnp.exp2(sink - mx)

    def values(h, pt):
        r = h % 2
        g = h // group
        va = jnp.where(own_vrows[r], vwin[g * LANES:(g + 1) * LANES, :], 1.0)
        return _dot(va, pt)

    batches = [list(range(i, i + SWA_HEAD_BATCH)) for i in range(0, N_HEADS_SWA, SWA_HEAD_BATCH)]
    sts = [scores(h) for h in batches[0]]
    for bi, heads in enumerate(batches):
        nxt = [scores(h) for h in batches[bi + 1]] if bi + 1 < len(batches) else None
        pts = [weights(h, st) for h, st in zip(heads, sts)]
        accs = [values(h, pt) for h, (pt, _) in zip(heads, pts)]
        outs = []
        for h, acc, (_, esink) in zip(heads, accs, pts):
            r = h % 2
            den = acc[HEAD_DIM * (1 - r):HEAD_DIM * (1 - r) + 1, :] + esink
            outs.append(acc / den)
            if r == 1:
                c = h // 2
                ot = jnp.where(own_rows[0], outs[-2], outs[-1])
                o_ref[0, :, c * LANES:(c + 1) * LANES] = ot.T.astype(BF16)
        sts = nxt


def _swa_attention(qt3, k3, vt3, sinks, tq=256):
    b, nq, s = qt3.shape
    kern = functools.partial(_swa_kernel, tq=tq)
    return pl.pallas_call(
        kern,
        grid=(b, s // tq),
        in_specs=[
            pl.BlockSpec(memory_space=pltpu.SMEM),
            pl.BlockSpec((1, nq, tq), lambda bi, qi: (bi, 0, qi)),
            pl.BlockSpec((1, s, k3.shape[2]), lambda bi, qi: (bi, 0, 0)),
            pl.BlockSpec((1, vt3.shape[1], s), lambda bi, qi: (bi, 0, 0)),
        ],
        out_specs=pl.BlockSpec((1, tq, nq), lambda bi, qi: (bi, qi, 0)),
        out_shape=jax.ShapeDtypeStruct((b, s, nq), BF16),
        compiler_params=_cparams("parallel", "arbitrary"),
        name="swa_attn",
    )(sinks, qt3, k3, vt3)


def _layer_norm(z, g, b):
    mu = jnp.mean(z, axis=-1, keepdims=True)
    zc = z - mu
    var = jnp.mean(zc * zc, axis=-1, keepdims=True)
    return zc * lax.rsqrt(var + LN_EPS) * g + b


def _outln_kernel(*refs, n_in, routed):
    o_refs = refs[:n_in]
    w_ref, x_ref, g_ref, b_ref = refs[n_in:n_in + 4]
    rest = refs[n_in + 4:]
    h = None
    row0 = 0
    for o_ref in o_refs:
        n = o_ref.shape[1]
        part = _dot(o_ref[...], w_ref[row0:row0 + n, :])
        h = part if h is None else h + part
        row0 += n
    y = _layer_norm(DEEPNORM_ALPHA * x_ref[...] + h, g_ref[...], b_ref[...])
    if routed:
        rw_ref, rb_ref, y_ref, yb_ref, c_ref = rest
        c_ref[...] = _route(y, rw_ref[...], rb_ref[...])
    else:
        y_ref, yb_ref = rest
    y_ref[...] = y
    yb_ref[...] = y.astype(BF16)


def _out_proj_ln(os_, w, x2d, g, b, router=None, tm=512):
    t = x2d.shape[0]
    n_in = len(os_)
    row = lambda n: pl.BlockSpec((tm, n), lambda i: (i, 0))
    full = lambda a: pl.BlockSpec(a.shape, lambda i: (0, 0))
    in_specs = [row(o.shape[1]) for o in os_] + [full(w), row(D_MODEL), full(g), full(b)]
    out_specs = [row(D_MODEL)] * 2
    out_shape = [jax.ShapeDtypeStruct((t, D_MODEL), F32), jax.ShapeDtypeStruct((t, D_MODEL), BF16)]
    args = [*os_, w, x2d, g, b]
    if router is not None:
        in_specs += [full(a) for a in router]
        out_specs.append(row(LANES))
        out_shape.append(jax.ShapeDtypeStruct((t, LANES), F32))
        args += list(router)
    return pl.pallas_call(
        functools.partial(_outln_kernel, n_in=n_in, routed=router is not None),
        grid=(t // tm,),
        in_specs=in_specs,
        out_specs=out_specs,
        out_shape=out_shape,
        compiler_params=_cparams("parallel"),
        name="out_proj_ln",
    )(*args)


def _route(x, w, b):
    xh = x.astype(BF16)
    xl = (x - xh.astype(F32)).astype(BF16)
    wh = w.astype(BF16)
    wl = (w - wh.astype(F32)).astype(BF16)
    logits = _dot(xh, wh) + (_dot(xl, wh) + _dot(xh, wl)) + b
    lane = lax.broadcasted_iota(jnp.int32, logits.shape, 1)
    logits = jnp.where(lane < N_EXPERTS, logits, NEG_INF)
    m1 = jnp.max(logits, axis=1, keepdims=True)
    i1 = jnp.min(jnp.where(logits == m1, lane, LANES), axis=1, keepdims=True)
    rest = jnp.where(lane == i1, NEG_INF, logits)
    m2 = jnp.max(rest, axis=1, keepdims=True)
    i2 = jnp.min(jnp.where(rest == m2, lane, LANES), axis=1, keepdims=True)
    e2 = jnp.exp(m2 - m1)
    den = 1.0 + e2
    return jnp.where(lane == i1, 1.0 / den, 0.0) + jnp.where(lane == i2, e2 / den, 0.0)


def _ffn_kernel(eid_ref, valid_ref, x_ref, wg_ref, wu_ref, wd_ref, o_ref, acc_ref):
    i = pl.program_id(0)
    c = pl.program_id(1)
    last = c == pl.num_programs(1) - 1
    valid = valid_ref[i] != 0

    @pl.when(valid)
    def _():
        x = x_ref[...]
        gate = _dot(x, wg_ref[0])
        up = _dot(x, wu_ref[0])
        h = gate * jax.nn.sigmoid(gate) * up
        contrib = _dot(h.astype(BF16), wd_ref[0])

        @pl.when(c == 0)
        def _():
            acc_ref[...] = contrib

        @pl.when(c != 0)
        def _():
            acc_ref[...] += contrib

        @pl.when(last)
        def _():
            o_ref[...] = acc_ref[...].astype(o_ref.dtype)

    @pl.when(jnp.logical_not(valid) & last)
    def _():
        o_ref[...] = jnp.zeros(o_ref.shape, o_ref.dtype)


def _ffn(xb, w_gu, w_d, eids, valid, out_dtype, tm, fc, name):
    r = xb.shape[0]
    f = w_d.shape[1]
    ncf = f // fc
    in_specs = [
        pl.BlockSpec((tm, D_MODEL), lambda i, c, eid, ok: (i, 0)),
        pl.BlockSpec((1, D_MODEL, fc), lambda i, c, eid, ok: (eid[i], 0, c * ok[i])),
        pl.BlockSpec((1, D_MODEL, fc), lambda i, c, eid, ok: (eid[i], 0, ncf + c * ok[i])),
        pl.BlockSpec((1, fc, D_MODEL), lambda i, c, eid, ok: (eid[i], c * ok[i], 0)),
    ]
    return pl.pallas_call(
        _ffn_kernel,
        grid_spec=pltpu.PrefetchScalarGridSpec(
            num_scalar_prefetch=2,
            grid=(r // tm, ncf),
            in_specs=in_specs,
            out_specs=pl.BlockSpec((tm, D_MODEL), lambda i, c, eid, ok: (i, 0)),
            scratch_shapes=[pltpu.VMEM((tm, D_MODEL), F32)],
        ),
        out_shape=jax.ShapeDtypeStruct((r, D_MODEL), out_dtype),
        compiler_params=_cparams("parallel", "arbitrary"),
        name=name,
    )(eids, valid, xb, w_gu, w_gu, w_d)


def _moe_ffn_kernel(eid_ref, valid_ref, first_ref, x_ref, wgu_hbm, wd_hbm, o_ref,
                    cg_ref, cu_ref, cd_ref, sg_ref, su_ref, sd_ref, sem_ref, acc_ref, *, fc):
    i = pl.program_id(0)
    e = eid_ref[i]
    f = cd_ref.shape[0]
    n_chunks = f // fc

    def chunk_copies(c, slot):
        return (
            pltpu.make_async_copy(wgu_hbm.at[e, :, pl.ds(c * fc, fc)], sg_ref.at[slot], sem_ref.at[0, slot]),
            pltpu.make_async_copy(wgu_hbm.at[e, :, pl.ds(f + c * fc, fc)], su_ref.at[slot], sem_ref.at[1, slot]),
            pltpu.make_async_copy(wd_hbm.at[e, pl.ds(c * fc, fc), :], sd_ref.at[slot], sem_ref.at[2, slot]),
        )

    def compute(c):
        x = x_ref[...]
        gate = _dot(x, cg_ref[:, c * fc:(c + 1) * fc])
        up = _dot(x, cu_ref[:, c * fc:(c + 1) * fc])
        h = gate * jax.nn.sigmoid(gate) * up
        contrib = _dot(h.astype(BF16), cd_ref[c * fc:(c + 1) * fc, :])
        if c == 0:
            acc_ref[...] = contrib
        else:
            acc_ref[...] += contrib

    valid = valid_ref[i] != 0
    first = first_ref[i] != 0

    @pl.when(valid & first)
    def _():
        for cp in chunk_copies(0, 0):
            cp.start()
        for c in range(n_chunks):
            slot = c % 2
            if c + 1 < n_chunks:
                for cp in chunk_copies(c + 1, 1 - slot):
                    cp.start()
            for cp in chunk_copies(c, slot):
                cp.wait()
            cg_ref[:, c * fc:(c + 1) * fc] = sg_ref[slot].astype(BF16)
            cu_ref[:, c * fc:(c + 1) * fc] = su_ref[slot].astype(BF16)
            cd_ref[c * fc:(c + 1) * fc, :] = sd_ref[slot].astype(BF16)
            compute(c)
        o_ref[...] = acc_ref[...].astype(o_ref.dtype)

    @pl.when(valid & jnp.logical_not(first))
    def _():
        for c in range(n_chunks):
            compute(c)
        o_ref[...] = acc_ref[...].astype(o_ref.dtype)

    @pl.when(jnp.logical_not(valid))
    def _():
        o_ref[...] = jnp.zeros(o_ref.shape, o_ref.dtype)


def _moe_ffn(xs, w_gu, w_d, eids, valid, first, tm=MOE_TM, fc=MOE_FC):
    r = xs.shape[0]
    f = w_d.shape[1]
    return pl.pallas_call(
        functools.partial(_moe_ffn_kernel, fc=fc),
        grid_spec=pltpu.PrefetchScalarGridSpec(
            num_scalar_prefetch=3,
            grid=(r // tm,),
            in_specs=[
                pl.BlockSpec((tm, D_MODEL), lambda i, *_: (i, 0)),
                pl.BlockSpec(memory_space=pl.ANY),
                pl.BlockSpec(memory_space=pl.ANY),
            ],
            out_specs=pl.BlockSpec((tm, D_MODEL), lambda i, *_: (i, 0)),
            scratch_shapes=[
                pltpu.VMEM((D_MODEL, f), BF16),
                pltpu.VMEM((D_MODEL, f), BF16),
                pltpu.VMEM((f, D_MODEL), BF16),
                pltpu.VMEM((2, D_MODEL, fc), F32),
                pltpu.VMEM((2, D_MODEL, fc), F32),
                pltpu.VMEM((2, fc, D_MODEL), F32),
                pltpu.SemaphoreType.DMA((3, 2)),
                pltpu.VMEM((tm, D_MODEL), F32),
            ],
        ),
        out_shape=jax.ShapeDtypeStruct((r, D_MODEL), BF16),
        compiler_params=_cparams("arbitrary"),
        name="ffn_moe",
    )(eids, valid, first, xs, w_gu, w_d)


def _moe_gather_kernel(te_ref, clo_ref, nch_ref, s0_ref, ns_ref, x_ref, pos_ref, o_ref, acc_ref, *, tm, sr, n_chunks):
    r = pl.program_id(0)
    e = te_ref[r]
    sub_row = lax.broadcasted_iota(jnp.int32, (sr, tm), 0)
    acc_ref[...] = jnp.zeros(acc_ref.shape, F32)

    def body(i, carry):
        c = clo_ref[r] + i
        t0 = pl.multiple_of(c * tm, tm)
        p = pos_ref[pl.ds(e, 1), pl.ds(t0, tm)]

        def sub(j, carry2):
            r0 = pl.multiple_of((s0_ref[r * n_chunks + c] + j) * sr, sr)
            onehot = jnp.where(p == r * tm + r0 + sub_row, 1.0, 0.0).astype(BF16)
            acc_ref[pl.ds(r0, sr), :] += _dot(onehot, x_ref[pl.ds(t0, tm), :])
            return carry2

        lax.fori_loop(0, ns_ref[r * n_chunks + c], sub, 0)
        return carry

    lax.fori_loop(0, nch_ref[r], body, 0)
    o_ref[...] = acc_ref[...].astype(BF16)


def _moe_gather(xb, pos_t, tile_e, clo, nch, sub0, nsub, n_rows, tm=MOE_TM, sr=MOE_SR):
    t = xb.shape[0]
    return pl.pallas_call(
        functools.partial(_moe_gather_kernel, tm=tm, sr=sr, n_chunks=t // tm),
        grid_spec=pltpu.PrefetchScalarGridSpec(
            num_scalar_prefetch=5,
            grid=(n_rows // tm,),
            in_specs=[
                pl.BlockSpec((t, D_MODEL), lambda r, *_: (0, 0), pipeline_mode=pl.Buffered(1)),
                pl.BlockSpec(pos_t.shape, lambda r, *_: (0, 0)),
            ],
            out_specs=pl.BlockSpec((tm, D_MODEL), lambda r, *_: (r, 0)),
            scratch_shapes=[pltpu.VMEM((tm, D_MODEL), F32)],
        ),
        out_shape=jax.ShapeDtypeStruct((n_rows, D_MODEL), BF16),
        compiler_params=_cparams("arbitrary"),
        name="moe_gather",
    )(tile_e, clo, nch, sub0, nsub, xb, pos_t)


def _moe_combine_kernel(n_ref, cid_ref, eid_ref, ys_hbm, pos_ref, comb_ref, x_ref, *rest, tm, cs, n_slots, n_buf):
    *ple_refs, o_ref, buf_ref, sem_ref = rest
    i = pl.program_id(0)
    n = n_ref[i]
    base = i * n_slots

    def chunk_copy(k, slot):
        row0 = pl.multiple_of(cid_ref[base + k] * cs, cs)
        return pltpu.make_async_copy(ys_hbm.at[pl.ds(row0, cs), :], buf_ref.at[slot], sem_ref.at[slot])

    o_ref[...] = jnp.zeros(o_ref.shape, F32)

    for d in range(n_buf):

        @pl.when(d < n)
        def _():
            chunk_copy(d, d).start()

    def gathered(k, slot):
        e = eid_ref[base + k]
        lane = lax.broadcasted_iota(jnp.int32, pos_ref.shape, 1)
        pos_e = jnp.sum(jnp.where(lane == e, pos_ref[...], 0), axis=1, keepdims=True)
        lane_c = lax.broadcasted_iota(jnp.int32, comb_ref.shape, 1)
        g_e = jnp.sum(jnp.where(lane_c == e, comb_ref[...], 0.0), axis=1, keepdims=True)
        col = cid_ref[base + k] * cs + lax.broadcasted_iota(jnp.int32, (tm, cs), 1)
        onehot = jnp.where(pos_e == col, 1.0, 0.0).astype(BF16)
        return g_e * _dot(onehot, buf_ref[slot])

    def body(j, carry):
        k = 2 * j
        slot = k % n_buf
        chunk_copy(k, slot).wait()
        chunk_copy(k + 1, slot + 1).wait()
        o_ref[...] += gathered(k, slot) + gathered(k + 1, slot + 1)

        @pl.when(k + n_buf < n)
        def _():
            chunk_copy(k + n_buf, slot).start()
            chunk_copy(k + n_buf + 1, slot + 1).start()

        return carry

    lax.fori_loop(0, n // 2, body, 0)
    o_ref[...] = _ln_ple(x_ref[...], o_ref[...], ple_refs)


def _moe_combine(ys, pos, combine, n_need, cid, eid, x2d, ple_args, tm=MOE_TM, cs=MOE_CS):
    t = pos.shape[0]
    return pl.pallas_call(
        functools.partial(_moe_combine_kernel, tm=tm, cs=cs, n_slots=MOE_SLOTS, n_buf=MOE_RING),
        grid_spec=pltpu.PrefetchScalarGridSpec(
            num_scalar_prefetch=3,
            grid=(t // tm,),
            in_specs=[
                pl.BlockSpec(memory_space=pl.ANY),
                pl.BlockSpec((tm, N_EXPERTS), lambda i, *_: (i, 0)),
                pl.BlockSpec((tm, LANES), lambda i, *_: (i, 0)),
                pl.BlockSpec((tm, D_MODEL), lambda i, *_: (i, 0)),
            ] + _ple_specs(tm, ple_args),
            out_specs=pl.BlockSpec((tm, D_MODEL), lambda i, *_: (i, 0)),
            scratch_shapes=[pltpu.VMEM((MOE_RING, cs, D_MODEL), BF16), pltpu.SemaphoreType.DMA((MOE_RING,))],
        ),
        out_shape=jax.ShapeDtypeStruct((t, D_MODEL), F32),
        compiler_params=_cparams("arbitrary"),
        name="moe_combine",
    )(n_need, cid, eid, ys, pos, combine, x2d, *ple_args)


def _moe_plan(combine, tm=MOE_TM):
    t = combine.shape[0]
    n_tok_tiles = t // tm
    n_row_tiles = (2 * t) // tm + N_EXPERTS
    sel = (combine[:, :N_EXPERTS] != 0.0).astype(jnp.int32)
    csum = jnp.cumsum(sel, axis=0)
    count = csum[-1]
    ntile_e = (count + tm - 1) // tm
    tile_end = jnp.cumsum(ntile_e)
    gstart = (tile_end - ntile_e) * tm
    pos = jnp.where(sel != 0, gstart[None, :] + csum - sel, -1)
    rt = jnp.arange(n_row_tiles, dtype=jnp.int32)
    tile_e = jnp.minimum(jnp.sum(rt[:, None] >= tile_end[None, :], axis=1), N_EXPERTS - 1).astype(jnp.int32)
    valid = (rt < tile_end[-1]).astype(jnp.int32)
    prev_e = jnp.concatenate([jnp.full((1,), -1, jnp.int32), tile_e[:-1]])
    group_head = ((valid != 0) & (tile_e != prev_e)).astype(jnp.int32)
    cb = jnp.concatenate([jnp.zeros((1, N_EXPERTS), jnp.int32), csum[tm - 1::tm]], axis=0)
    off = rt * tm - gstart[tile_e]
    cb_t = cb[:, tile_e].T
    clo = jnp.sum(cb_t[:, 1:] <= off[:, None], axis=1).astype(jnp.int32)
    cend = jnp.sum(cb_t[:, :-1] < (off + tm)[:, None], axis=1).astype(jnp.int32)
    nch = jnp.where(valid != 0, jnp.maximum(cend - clo, 0), 0).astype(jnp.int32)
    clo = jnp.minimum(clo, n_tok_tiles - 1)
    r_lo = jnp.clip(cb_t[:, :-1] - off[:, None], 0, tm)
    r_hi = jnp.clip(cb_t[:, 1:] - off[:, None], 0, tm)
    sub0 = (r_lo // MOE_SR).astype(jnp.int32)
    nsub = jnp.where(r_hi > r_lo, (r_hi + MOE_SR - 1) // MOE_SR - sub0, 0).astype(jnp.int32)
    first = gstart[None, :] + cb[:-1]
    n_ie = cb[1:] - cb[:-1]
    per_e = MOE_SLOTS // N_EXPERTS
    k = jnp.arange(per_e, dtype=jnp.int32)
    c_lo = first // MOE_CS
    c_hi = (first + n_ie - 1) // MOE_CS
    cand = c_lo[..., None] + k
    need = ((n_ie[..., None] > 0) & (cand <= c_hi[..., None])).reshape(n_tok_tiles, MOE_SLOTS)
    cand = cand.reshape(n_tok_tiles, MOE_SLOTS)
    eids = jnp.broadcast_to(jnp.arange(N_EXPERTS, dtype=jnp.int32)[:, None], (N_EXPERTS, per_e)).reshape(-1)
    order = jnp.argsort(jnp.logical_not(need), axis=1, stable=True)
    cid = jnp.where(need, cand, 0)
    cid = jnp.take_along_axis(cid, order, axis=1).reshape(-1).astype(jnp.int32)
    need_sorted = jnp.take_along_axis(need, order, axis=1)
    eid = jnp.where(need_sorted, eids[order], -1).reshape(-1).astype(jnp.int32)
    n_need = jnp.sum(need, axis=1).astype(jnp.int32)
    n_need = n_need + n_need % 2
    return dict(pos=pos.astype(jnp.int32), tile_e=tile_e, valid=valid, first=group_head, clo=clo, nch=nch,
                sub0=sub0.reshape(-1), nsub=nsub.reshape(-1),
                n_need=n_need, cid=cid, eid=eid, n_rows=n_row_tiles * tm)


def _moe(xb, x2d, combine, w_gu, w_d, first_expert, ple_args):
    plan = _moe_plan(combine)
    xs = _moe_gather(xb, plan["pos"].T, plan["tile_e"], plan["clo"], plan["nch"], plan["sub0"], plan["nsub"],
                     plan["n_rows"])
    ys = _moe_ffn(xs, w_gu, w_d, plan["tile_e"] + first_expert, plan["valid"], plan["first"])
    return _moe_combine(ys, plan["pos"], combine, plan["n_need"], plan["cid"], plan["eid"], x2d, ple_args)


def _ln_ple(x, h, ple_refs):
    g_ref, b_ref, p_ref, wg_ref, wp_ref = ple_refs
    y = _layer_norm(DEEPNORM_ALPHA * x + h, g_ref[...], b_ref[...])
    gate = jax.nn.sigmoid(_dot(y.astype(BF16), wg_ref[...]))
    proj = _dot(p_ref[...].astype(BF16), wp_ref[...])
    return y + gate * proj


def _ple_specs(tm, ple_args):
    g, b, p2d, wg, wp = ple_args
    ign = lambda f: (lambda *idx: f(idx[0]))
    full = lambda a: pl.BlockSpec(a.shape, ign(lambda i: (0, 0)))
    return [full(g), full(b), pl.BlockSpec((tm, D_PLE), ign(lambda i: (i, 0))), full(wg), full(wp)]


def _ffn_ple_kernel(xb_ref, wgu_ref, wd_ref, x_ref, *rest, fc):
    *ple_refs, o_ref, acc_ref = rest
    f = wd_ref.shape[0]
    xb = xb_ref[...]
    for c in range(f // fc):
        gate = _dot(xb, wgu_ref[:, c * fc:(c + 1) * fc])
        up = _dot(xb, wgu_ref[:, f + c * fc:f + (c + 1) * fc])
        h = gate * jax.nn.sigmoid(gate) * up
        contrib = _dot(h.astype(BF16), wd_ref[c * fc:(c + 1) * fc, :])
        if c == 0:
            acc_ref[...] = contrib
        else:
            acc_ref[...] += contrib
    o_ref[...] = _ln_ple(x_ref[...], acc_ref[...], ple_refs)


def _ffn_ple(xb, x2d, w_gu, w_d, ple_args, tm=512, fc=1408):
    t = xb.shape[0]
    row = lambda n: pl.BlockSpec((tm, n), lambda i: (i, 0))
    resident = lambda a: pl.BlockSpec(a.shape, lambda i: (0, 0), pipeline_mode=pl.Buffered(1))
    return pl.pallas_call(
        functools.partial(_ffn_ple_kernel, fc=fc),
        grid=(t // tm,),
        in_specs=[row(D_MODEL), resident(w_gu), resident(w_d), row(D_MODEL)] + _ple_specs(tm, ple_args),
        out_specs=row(D_MODEL),
        out_shape=jax.ShapeDtypeStruct((t, D_MODEL), F32),
        scratch_shapes=[pltpu.VMEM((tm, D_MODEL), F32)],
        compiler_params=_cparams("parallel"),
        name="ffn_dense",
    )(xb, w_gu, w_d, x2d, *ple_args)


def _even_mixer(x2d, b, s, w_in, b_f, w_out):
    nf, ns = FOX_W, SB_W
    o_f = 3 * nf
    o_s = o_f + N_HEADS_FOX
    qa, ka, va = w_in[:, :nf], w_in[:, nf:2 * nf], w_in[:, 2 * nf:o_f]
    qs, ks, vs = w_in[:, o_s:o_s + ns], w_in[:, o_s + ns:o_s + 2 * ns], w_in[:, o_s + 2 * ns:]
    w_k = jnp.concatenate([ka, ks], axis=1).astype(BF16)
    w_t = jnp.concatenate([qa * Q_SCALE, va, qs * Q_SCALE, vs], axis=1).T.astype(BF16)
    w_f = jnp.pad(w_in[:, o_f:o_s], ((0, 0), (0, LANES - N_HEADS_FOX))).astype(BF16)
    bias_f = jnp.pad(b_f, (0, LANES - N_HEADS_FOX)).reshape(1, LANES)
    k2d, ht, lf = _ab_proj(x2d, w_k, w_t, w_f, bias_f, b, s)
    lf_t = lf.reshape(b, s, LANES)[:, :, :N_HEADS_FOX].transpose(0, 2, 1)
    crow, caug = _cumsum_seq(lf_t)
    k3 = k2d.reshape(b, s, -1)
    oa, ob = _even_attention(k3, ht, caug, crow)
    return [oa.reshape(b * s, nf), ob.reshape(b * s, ns)], w_out.astype(BF16)


def _odd_mixer(x2d, b, s, w_qkv, sinks, w_out, tables):
    nq = N_HEADS_SWA * HEAD_DIM
    wq = w_qkv[:, :nq]
    wk = [w_qkv[:, nq + i * HEAD_DIM: nq + (i + 1) * HEAD_DIM] for i in range(N_KV_SWA)]
    wv = [w_qkv[:, nq + (N_KV_SWA + i) * HEAD_DIM: nq + (N_KV_SWA + i + 1) * HEAD_DIM] for i in range(N_KV_SWA)]
    dup = lambda ws: [w for w in ws for _ in range(2)]
    w_k = jnp.concatenate(dup(wk), axis=1).astype(BF16)
    w_t = jnp.concatenate([wq * Q_SCALE] + dup(wv), axis=1).T.astype(BF16)
    k2d, qt, vt = _swa_proj(x2d, w_k, w_t, tables, b, s)
    o = _swa_attention(qt, k2d.reshape(b, s, -1), vt, sinks)
    return [o.reshape(b * s, nq)], w_out.astype(BF16)


def kernel(x, p, ln_mix_g, ln_mix_b, ln_ffn_g, ln_ffn_b, ab_w_in, ab_b_f, ab_w_out, c_w_qkv, c_sinks, c_w_out,
           ffn_w_gate_up, ffn_w_down, router_w, router_b, moe_w_gate_up, moe_w_down, ple_w_gate, ple_w_proj):
    b, s, d = x.shape
    t = b * s
    x2d = x.reshape(t, d)
    tables = _rope_tables(s)
    row = lambda v: v.reshape(1, -1)
    moe_gu = moe_w_gate_up.reshape((-1,) + moe_w_gate_up.shape[2:])
    moe_d = moe_w_down.reshape((-1,) + moe_w_down.shape[2:])
    for i in range(DEPTH):
        j = i // 2
        if i % 2 == 0:
            os_, ws = _even_mixer(x2d, b, s, ab_w_in[j], ab_b_f[j], ab_w_out[j])
        else:
            os_, ws = _odd_mixer(x2d, b, s, c_w_qkv[j], c_sinks[j], c_w_out[j], tables)
        ple_args = (row(ln_ffn_g[i]), row(ln_ffn_b[i]), p[i].reshape(t, D_PLE),
                    ple_w_gate[i].astype(BF16), ple_w_proj[i].astype(BF16))
        if i % 2 == 0:
            x2d, xb = _out_proj_ln(os_, ws, x2d, row(ln_mix_g[i]), row(ln_mix_b[i]))
            x2d = _ffn_ple(xb, x2d, ffn_w_gate_up[j].astype(BF16), ffn_w_down[j].astype(BF16), ple_args)
        else:
            rw = jnp.pad(router_w[j], ((0, 0), (0, LANES - N_EXPERTS)))
            rb = jnp.pad(router_b[j], (0, LANES - N_EXPERTS)).reshape(1, LANES)
            x2d, xb, combine = _out_proj_ln(os_, ws, x2d, row(ln_mix_g[i]), row(ln_mix_b[i]), router=(rw, rb))
            x2d = _moe(xb, x2d, combine, moe_gu, moe_d, j * N_EXPERTS, ple_args)
    return x2d.reshape(b, s, d)
```

```python
import functools
import math

import jax
import jax.numpy as jnp
from jax import lax
from jax.experimental import pallas as pl
from jax.experimental.pallas import tpu as pltpu

F32 = jnp.float32
BF16 = jnp.bfloat16

D_MODEL = 1024
HEAD_DIM = 64
LANES = 128
N_HEADS_FOX = 8
N_HEADS_SB = 8
N_HEADS_SWA = 16
N_KV_SWA = 2
WINDOW = 128
ROPE_THETA = 500000.0
ROPE_DIM = HEAD_DIM // 4
N_EXPERTS = 8
D_PLE = 256
LN_EPS = 1e-5
DEPTH = 4
DEEPNORM_ALPHA = (2.0 * DEPTH) ** 0.25
ATTN_SCALE = HEAD_DIM ** -0.5
FOX_W = N_HEADS_FOX * HEAD_DIM
SB_W = N_HEADS_SB * HEAD_DIM
VMEM_LIMIT = 56 * 1024 * 1024

NEG_INF = float("-inf")
LOG2E = math.log2(math.e)
Q_SCALE = ATTN_SCALE * LOG2E
SUM_ROWS = 16
SWA_HEAD_BATCH = 4
MOE_TM = 512
MOE_SR = 128
MOE_FC = 512
MOE_CS = 256
MOE_SLOTS = N_EXPERTS * (MOE_TM // MOE_CS + 1)
MOE_RING = 8


def _cparams(*sem):
    return pltpu.CompilerParams(dimension_semantics=sem, vmem_limit_bytes=VMEM_LIMIT)


def _dot(a, b):
    return jnp.dot(a, b, preferred_element_type=F32)


def _dot_nt(a, b):
    return lax.dot_general(a, b, (((1,), (1,)), ((), ())), preferred_element_type=F32)


def _log_sigmoid(x):
    return jnp.minimum(x, 0.0) - jnp.log1p(jnp.exp(-jnp.abs(x)))


def _head_masks(shape):
    lane = lax.broadcasted_iota(jnp.int32, shape, len(shape) - 1)
    return lane < HEAD_DIM, lane >= HEAD_DIM


def _ab_proj_kernel(x_ref, wk_ref, wt_ref, wf_ref, bf_ref, k_ref, ht_ref, lf_ref):
    xb = x_ref[...].astype(BF16)
    k_ref[...] = _dot(xb, wk_ref[...]).astype(BF16)
    ht_ref[0] = _dot_nt(wt_ref[...], xb).astype(BF16)
    lf_ref[...] = _log_sigmoid(_dot(xb, wf_ref[...]) + bf_ref[...])


def _ab_proj(x2d, wk, wt, wf, bf, b, s, tm=512):
    t = x2d.shape[0]
    nk = wk.shape[1]
    nt = wt.shape[0]
    spt = s // tm
    return pl.pallas_call(
        _ab_proj_kernel,
        grid=(t // tm,),
        in_specs=[
            pl.BlockSpec((tm, D_MODEL), lambda i: (i, 0)),
            pl.BlockSpec((D_MODEL, nk), lambda i: (0, 0)),
            pl.BlockSpec((nt, D_MODEL), lambda i: (0, 0)),
            pl.BlockSpec((D_MODEL, LANES), lambda i: (0, 0)),
            pl.BlockSpec((1, LANES), lambda i: (0, 0)),
        ],
        out_specs=[
            pl.BlockSpec((tm, nk), lambda i: (i, 0)),
            pl.BlockSpec((1, nt, tm), lambda i: (i // spt, 0, i % spt)),
            pl.BlockSpec((tm, LANES), lambda i: (i, 0)),
        ],
        out_shape=[
            jax.ShapeDtypeStruct((t, nk), BF16),
            jax.ShapeDtypeStruct((b, nt, s), BF16),
            jax.ShapeDtypeStruct((t, LANES), F32),
        ],
        compiler_params=_cparams("parallel"),
        name="ab_proj",
    )(x2d, wk, wt, wf, bf)


def _split3(v):
    hi = v.astype(BF16).astype(F32)
    r1 = v - hi
    mid = r1.astype(BF16).astype(F32)
    return hi, mid, r1 - mid


def _cumsum_kernel(x_ref, c_ref, aug_ref, st_ref):
    x = x_ref[0]
    s = x.shape[1]
    lane = lax.broadcasted_iota(jnp.int32, x.shape, 1) & (LANES - 1)
    sh = 1
    while sh < LANES:
        x = x + jnp.where(lane >= sh, pltpu.roll(x, sh, axis=1), 0.0)
        sh *= 2
    carry = jnp.zeros((x.shape[0], 1), F32)
    for c in range(s // LANES):
        blk = x[:, c * LANES:(c + 1) * LANES] + carry
        c_ref[0, :, c * LANES:(c + 1) * LANES] = blk
        carry = blk[:, LANES - 1:LANES]

    ones = jnp.ones((3, s), F32)
    st_ref[...] = jnp.zeros(st_ref.shape, F32)
    for p in range(x.shape[0] // 2):
        for which, base in ((0, HEAD_DIM), (1, 0)):
            negc = c_ref[0, 2 * p + which:2 * p + which + 1, :] * -LOG2E
            hi, mid, lo = _split3(negc)
            st_ref[base:base + 1, :] = hi
            st_ref[base + 1:base + 2, :] = mid
            st_ref[base + 2:base + 3, :] = lo
            st_ref[base + 3:base + 6, :] = ones
        for c in range(s // LANES):
            blk = st_ref[:, c * LANES:(c + 1) * LANES]
            aug_ref[0, p, c * LANES:(c + 1) * LANES, :] = blk.T.astype(BF16)


def _cumsum_seq(lf_t):
    b, h, s = lf_t.shape
    return pl.pallas_call(
        _cumsum_kernel,
        grid=(b,),
        in_specs=[pl.BlockSpec((1, h, s), lambda i: (i, 0, 0))],
        out_specs=[
            pl.BlockSpec((1, h, s), lambda i: (i, 0, 0)),
            pl.BlockSpec((1, h // 2, s, LANES), lambda i: (i, 0, 0, 0)),
        ],
        out_shape=[
            jax.ShapeDtypeStruct((b, h, s), F32),
            jax.ShapeDtypeStruct((b, h // 2, s, LANES), BF16),
        ],
        scratch_shapes=[pltpu.VMEM((LANES, s), F32)],
        compiler_params=_cparams("parallel"),
        name="fox_cumsum",
    )(lf_t)


def _fox_stages(qt_ref, k_ref, aug_ref, vt_ref, c_ref, o_ref, m_ref, acc_ref, sa_ref, sb_ref, *, tq, tk):
    hp = pl.program_id(1)
    qi = pl.program_id(2)
    q0 = qi * tq
    qt = qt_ref[0].astype(F32)
    sub = lax.broadcasted_iota(jnp.int32, (LANES, tq), 0)
    lane = lax.broadcasted_iota(jnp.int32, (tk, LANES), 1)
    vsub = lax.broadcasted_iota(jnp.int32, (LANES, tk), 0)
    own_rows = [sub < HEAD_DIM, sub >= HEAD_DIM]
    own_lanes = [lane < HEAD_DIM, lane >= HEAD_DIM]
    own_vrows = [vsub < HEAD_DIM, vsub >= HEAD_DIM]

    qaug = []
    for r in range(2):
        base = HEAD_DIM * (1 - r)
        ci = c_ref[0, pl.ds(2 * hp + r, 1), :]
        hi, mid, lo = _split3(ci * LOG2E)
        t = jnp.where(own_rows[r], qt, 0.0)
        t = jnp.where((sub >= base) & (sub < base + 3), 1.0, t)
        t = jnp.where(sub == base + 3, hi, t)
        t = jnp.where(sub == base + 4, mid, t)
        t = jnp.where(sub == base + 5, lo, t)
        qaug.append(t.astype(BF16))

    m_ref[...] = jnp.full(m_ref.shape, NEG_INF, F32)
    acc_ref[...] = jnp.zeros(acc_ref.shape, F32)

    def scores(k0, s_ref, lo=0):
        kt = k_ref[0, pl.ds(k0, tk), :]
        ca = aug_ref[0, 0, pl.ds(k0, tk), :]
        for r in range(2):
            ka = jnp.where(own_lanes[r], kt, ca)
            s_ref[r, :, lo:] = _dot(ka, qaug[r][:, lo:])

    def softmax_pv(k0, s_ref, masked, lo=0):
        vt = vt_ref[0, :, pl.ds(k0, tk)]
        if masked:
            key = k0 + lax.broadcasted_iota(jnp.int32, (tk, tq - lo), 0)
            qry = q0 + lo + lax.broadcasted_iota(jnp.int32, (tk, tq - lo), 1)
            keep = key <= qry
        for r in range(2):
            st = s_ref[r, :, lo:]
            if masked:
                st = jnp.where(keep, st, NEG_INF)
            m_prev = m_ref[r, :, lo:]
            m_new = jnp.maximum(m_prev, jnp.max(st, axis=0, keepdims=True))
            alpha = jnp.exp2(m_prev - m_new)
            pt = jnp.exp2(st - m_new).astype(BF16)
            va = jnp.where(own_vrows[r], vt, 1.0)
            acc_ref[r, :, lo:] = alpha * acc_ref[r, :, lo:] + _dot(va, pt)
            m_ref[r, :, lo:] = m_new

    tile = lambda j: pl.multiple_of(j * tk, tk)

    def start():
        scores(tile(0), sa_ref)

    def half_a(i):
        scores(tile(2 * i + 1), sb_ref)
        softmax_pv(tile(2 * i), sa_ref, False)

    def half_b(i):
        scores(tile(2 * i + 2), sa_ref)
        softmax_pv(tile(2 * i + 1), sb_ref, False)

    def finish():
        scores(tile(2 * qi + 1), sb_ref, lo=tk)
        softmax_pv(tile(2 * qi), sa_ref, True)
        softmax_pv(tile(2 * qi + 1), sb_ref, True, lo=tk)
        ot = jnp.zeros((LANES, tq), F32)
        for r in range(2):
            a = acc_ref[r]
            den = a[HEAD_DIM * (1 - r):HEAD_DIM * (1 - r) + 1, :]
            ot = ot + jnp.where(own_rows[r], a / den, 0.0)
        o_ref[0] = ot.T.astype(BF16)

    return start, half_a, half_b, finish


def _sb_stages(qt_ref, k_ref, vt_ref, o_ref, run_ref, acc_ref, za_ref, zb_ref, *, tq, cw):
    qi = pl.program_id(2)
    q0 = qi * tq
    qt = qt_ref[0]
    sub = lax.broadcasted_iota(jnp.int32, (LANES, tq), 0)
    own_rows = [sub < HEAD_DIM, sub >= HEAD_DIM]
    qm = [jnp.where(mk, qt, 0.0) for mk in own_rows]

    ss = lax.broadcasted_iota(jnp.int32, (cw, cw), 0)
    jj = lax.broadcasted_iota(jnp.int32, (cw, cw), 1)
    tri = jnp.where(jj > ss, 1.0, 0.0).astype(BF16)
    tri = jnp.concatenate([tri, jnp.ones((SUM_ROWS, cw), BF16)], axis=0)

    run_ref[...] = jnp.zeros(run_ref.shape, F32)
    acc_ref[...] = jnp.zeros(acc_ref.shape, F32)

    def start_of(c):
        return pl.multiple_of(jnp.maximum(q0 + tq - (c + 1) * cw, 0), cw)

    def logits(c, z_ref, lo=0):
        kt = k_ref[0, pl.ds(start_of(c), cw), :]
        for r in range(2):
            z_ref[r, :, lo:] = _dot(kt, qm[r][:, lo:])

    def weights_pv(c, z_ref, masked, lo=0):
        k0 = start_of(c)
        vt = vt_ref[0, :, pl.ds(k0, cw)]
        if masked:
            key = k0 + lax.broadcasted_iota(jnp.int32, (cw, tq - lo), 0)
            qry = q0 + lo + lax.broadcasted_iota(jnp.int32, (cw, tq - lo), 1)
            keep = key < qry
        zs, sps, laters, ws = [], [], [], []
        for r in range(2):
            z = z_ref[r, :, lo:]
            e = jnp.exp2(-jnp.abs(z))
            sp = jnp.maximum(z, 0.0) + jnp.log(1.0 + e) * LOG2E
            if masked:
                sp = jnp.where(keep, sp, 0.0)
            zs.append(z)
            sps.append(sp)
        for r in range(2):
            laters.append(_dot(tri, sps[r].astype(BF16)))
        for r in range(2):
            run = run_ref[r, :, lo:]
            a = jnp.exp2((zs[r] - sps[r]) - (laters[r][:cw] + run))
            if masked:
                a = jnp.where(keep, a, 0.0)
            ws.append(a.astype(BF16))
            run_ref[r, :, lo:] = run + laters[r][cw:cw + 1]
        for r in range(2):
            acc_ref[r, :, lo:] = acc_ref[r, :, lo:] + _dot(vt, ws[r])

    def start():
        logits(0, za_ref, lo=cw)
        logits(1, zb_ref)
        weights_pv(0, za_ref, True, lo=cw)
        logits(2, za_ref)
        weights_pv(1, zb_ref, True)

    def half_a(i):
        logits(2 * i + 1, zb_ref)
        weights_pv(2 * i, za_ref, False)

    def half_b(i):
        logits(2 * i + 2, za_ref)
        weights_pv(2 * i + 1, zb_ref, False)

    def finish():
        ot = jnp.where(own_rows[0], acc_ref[0], acc_ref[1])
        o_ref[0] = ot.T.astype(BF16)

    return start, half_a, half_b, finish


def _even_attn_kernel(fq_ref, fk_ref, aug_ref, fv_ref, c_ref, sq_ref, sk_ref, sv_ref, of_ref, os_ref,
                      m_ref, facc_ref, sa_ref, sb_ref, run_ref, sacc_ref, za_ref, zb_ref, *, tq, tk):
    qi = pl.program_id(2)
    f_start, f_a, f_b, f_finish = _fox_stages(fq_ref, fk_ref, aug_ref, fv_ref, c_ref, of_ref,
                                              m_ref, facc_ref, sa_ref, sb_ref, tq=tq, tk=tk)
    s_start, s_a, s_b, s_finish = _sb_stages(sq_ref, sk_ref, sv_ref, os_ref,
                                             run_ref, sacc_ref, za_ref, zb_ref, tq=tq, cw=tk)
    f_start()
    s_start()

    def body(i, carry):
        f_a(i)
        s_a(i + 1)
        f_b(i)
        s_b(i + 1)
        return carry

    lax.fori_loop(0, qi, body, 0)
    f_finish()
    s_finish()


def _even_attention(k3, ht, caug, crow, tq=512):
    b, s, _ = k3.shape
    nhp = N_HEADS_FOX // 2
    tk = tq // 2
    qt_spec = lambda blk: pl.BlockSpec((1, LANES, tq), lambda bi, hp, qi: (bi, blk * nhp + hp, qi))
    k_spec = lambda blk: pl.BlockSpec((1, s, LANES), lambda bi, hp, qi: (bi, 0, blk * nhp + hp))
    vt_spec = lambda blk: pl.BlockSpec((1, LANES, s), lambda bi, hp, qi: (bi, blk * nhp + hp, 0))
    out_spec = pl.BlockSpec((1, tq, LANES), lambda bi, hp, qi: (bi, qi, hp))
    return pl.pallas_call(
        functools.partial(_even_attn_kernel, tq=tq, tk=tk),
        grid=(b, nhp, s // tq),
        in_specs=[
            qt_spec(0), k_spec(0),
            pl.BlockSpec((1, 1, s, LANES), lambda bi, hp, qi: (bi, hp, 0, 0)),
            vt_spec(1),
            pl.BlockSpec((1, N_HEADS_FOX, tq), lambda bi, hp, qi: (bi, 0, qi)),
            qt_spec(2), k_spec(1), vt_spec(3),
        ],
        out_specs=[out_spec, out_spec],
        out_shape=[jax.ShapeDtypeStruct((b, s, FOX_W), BF16), jax.ShapeDtypeStruct((b, s, SB_W), BF16)],
        scratch_shapes=[
            pltpu.VMEM((2, 1, tq), F32), pltpu.VMEM((2, LANES, tq), F32),
            pltpu.VMEM((2, tk, tq), F32), pltpu.VMEM((2, tk, tq), F32),
            pltpu.VMEM((2, 1, tq), F32), pltpu.VMEM((2, LANES, tq), F32),
            pltpu.VMEM((2, tk, tq), F32), pltpu.VMEM((2, tk, tq), F32),
        ],
        compiler_params=_cparams("parallel", "parallel", "arbitrary"),
        name="even_attn",
    )(ht, k3, caug, ht, crow, ht, k3, ht)


def _swa_proj_kernel(x_ref, wk_ref, wt_ref, cos_ref, sa_ref, sb_ref, cost_ref, sint_ref, k_ref, qt_ref, vt_ref):
    xb = x_ref[...].astype(BF16)
    half = ROPE_DIM // 2
    yk = _dot(xb, wk_ref[...])
    for c in range(yk.shape[1] // LANES):
        blk = yk[:, c * LANES:(c + 1) * LANES]
        blk = (blk * cos_ref[...] + pltpu.roll(blk, half, axis=1) * sa_ref[...]
               + pltpu.roll(blk, LANES - half, axis=1) * sb_ref[...])
        k_ref[:, c * LANES:(c + 1) * LANES] = blk.astype(BF16)
    yt = _dot_nt(wt_ref[...], xb)
    cos = cost_ref[...]
    sin = sint_ref[...]
    nq = N_HEADS_SWA * HEAD_DIM
    for h in range(N_HEADS_SWA):
        base = h * HEAD_DIM
        x1 = yt[base:base + half]
        x2 = yt[base + half:base + ROPE_DIM]
        rot = jnp.concatenate([x1 * cos - x2 * sin, x2 * cos + x1 * sin], axis=0)
        qt_ref[0, base:base + ROPE_DIM, :] = rot.astype(BF16)
        qt_ref[0, base + ROPE_DIM:base + HEAD_DIM, :] = yt[base + ROPE_DIM:base + HEAD_DIM].astype(BF16)
    vt_ref[0] = yt[nq:].astype(BF16)


def _rope_tables(s):
    half = ROPE_DIM // 2
    inv = ROPE_THETA ** (-jnp.arange(half, dtype=F32) * 2.0 / ROPE_DIM)
    ang = jnp.arange(s, dtype=F32)[:, None] * inv[None, :]
    cos, sin = jnp.cos(ang), jnp.sin(ang)
    d = jnp.arange(LANES) % HEAD_DIM
    idx = d % half
    cos_t = jnp.where(d[None, :] < ROPE_DIM, cos[:, idx], 1.0)
    sin_a = jnp.where((d[None, :] >= half) & (d[None, :] < ROPE_DIM), sin[:, idx], 0.0)
    sin_b = jnp.where(d[None, :] < half, -sin[:, idx], 0.0)
    return cos_t.astype(F32), sin_a.astype(F32), sin_b.astype(F32), cos.T.astype(F32), sin.T.astype(F32)


def _swa_proj(x2d, wk, wt, tables, b, s, tm=512):
    t = x2d.shape[0]
    nk = wk.shape[1]
    nq = N_HEADS_SWA * HEAD_DIM
    nv = wt.shape[0] - nq
    spt = s // tm
    half = ROPE_DIM // 2
    lane_tab = pl.BlockSpec((tm, LANES), lambda i: (i % spt, 0))
    row_tab = pl.BlockSpec((half, tm), lambda i: (0, i % spt))
    return pl.pallas_call(
        _swa_proj_kernel,
        grid=(t // tm,),
        in_specs=[
            pl.BlockSpec((tm, D_MODEL), lambda i: (i, 0)),
            pl.BlockSpec(wk.shape, lambda i: (0, 0)),
            pl.BlockSpec(wt.shape, lambda i: (0, 0)),
            lane_tab, lane_tab, lane_tab, row_tab, row_tab,
        ],
        out_specs=[
            pl.BlockSpec((tm, nk), lambda i: (i, 0)),
            pl.BlockSpec((1, nq, tm), lambda i: (i // spt, 0, i % spt)),
            pl.BlockSpec((1, nv, tm), lambda i: (i // spt, 0, i % spt)),
        ],
        out_shape=[
            jax.ShapeDtypeStruct((t, nk), BF16),
            jax.ShapeDtypeStruct((b, nq, s), BF16),
            jax.ShapeDtypeStruct((b, nv, s), BF16),
        ],
        compiler_params=_cparams("parallel"),
        name="swa_proj",
    )(x2d, wk, wt, *tables)


def _swa_kernel(sink_ref, qt_ref, k_ref, vt_ref, o_ref, *, tq):
    qi = pl.program_id(1)
    q0 = qi * tq
    kw = tq + WINDOW
    kstart = pl.multiple_of(jnp.maximum(q0 - WINDOW, 0), WINDOW)
    kwin = k_ref[0, pl.ds(kstart, kw), :]
    vwin = vt_ref[0, :, pl.ds(kstart, kw)]
    key = kstart + lax.broadcasted_iota(jnp.int32, (kw, tq), 0)
    qry = q0 + lax.broadcasted_iota(jnp.int32, (kw, tq), 1)
    diff = qry - key
    keep = (diff >= 0) & (diff < WINDOW)
    sub = lax.broadcasted_iota(jnp.int32, (LANES, tq), 0)
    vsub = lax.broadcasted_iota(jnp.int32, (LANES, kw), 0)
    own_rows = [sub < HEAD_DIM, sub >= HEAD_DIM]
    own_vrows = [vsub < HEAD_DIM, vsub >= HEAD_DIM]
    group = N_HEADS_SWA // N_KV_SWA

    def scores(h):
        c, r = divmod(h, 2)
        g = h // group
        qblk = qt_ref[0, c * LANES:(c + 1) * LANES, :]
        return _dot(kwin[:, g * LANES:(g + 1) * LANES], jnp.where(own_rows[r], qblk, 0.0))

    def weights(h, st):
        st = jnp.where(keep, st, NEG_INF)
        sink = sink_ref[h] * LOG2E
        mx = jnp.maximum(jnp.max(st, axis=0, keepdims=True), sink)
        return jnp.exp2(st - mx).astype(BF16), jnp.exp2(sink - mx)

    def values(h, pt):
        r = h % 2
        g = h // group
        va = jnp.where(own_vrows[r], vwin[g * LANES:(g + 1) * LANES, :], 1.0)
        return _dot(va, pt)

    batches = [list(range(i, i + SWA_HEAD_BATCH)) for i in range(0, N_HEADS_SWA, SWA_HEAD_BATCH)]
    sts = [scores(h) for h in batches[0]]
    for bi, heads in enumerate(batches):
        nxt = [scores(h) for h in batches[bi + 1]] if bi + 1 < len(batches) else None
        pts = [weights(h, st) for h, st in zip(heads, sts)]
        accs = [values(h, pt) for h, (pt, _) in zip(heads, pts)]
        outs = []
        for h, acc, (_, esink) in zip(heads, accs, pts):
            r = h % 2
            den = acc[HEAD_DIM * (1 - r):HEAD_DIM * (1 - r) + 1, :] + esink
            outs.append(acc / den)
            if r == 1:
                c = h // 2
                ot = jnp.where(own_rows[0], outs[-2], outs[-1])
                o_ref[0, :, c * LANES:(c + 1) * LANES] = ot.T.astype(BF16)
        sts = nxt


def _swa_attention(qt3, k3, vt3, sinks, tq=256):
    b, nq, s = qt3.shape
    kern = functools.partial(_swa_kernel, tq=tq)
    return pl.pallas_call(
        kern,
        grid=(b, s // tq),
        in_specs=[
            pl.BlockSpec(memory_space=pltpu.SMEM),
            pl.BlockSpec((1, nq, tq), lambda bi, qi: (bi, 0, qi)),
            pl.BlockSpec((1, s, k3.shape[2]), lambda bi, qi: (bi, 0, 0)),
            pl.BlockSpec((1, vt3.shape[1], s), lambda bi, qi: (bi, 0, 0)),
        ],
        out_specs=pl.BlockSpec((1, tq, nq), lambda bi, qi: (bi, qi, 0)),
        out_shape=jax.ShapeDtypeStruct((b, s, nq), BF16),
        compiler_params=_cparams("parallel", "arbitrary"),
        name="swa_attn",
    )(sinks, qt3, k3, vt3)


def _layer_norm(z, g, b):
    mu = jnp.mean(z, axis=-1, keepdims=True)
    zc = z - mu
    var = jnp.mean(zc * zc, axis=-1, keepdims=True)
    return zc * lax.rsqrt(var + LN_EPS) * g + b


def _outln_kernel(*refs, n_in, routed):
    o_refs = refs[:n_in]
    w_ref, x_ref, g_ref, b_ref = refs[n_in:n_in + 4]
    rest = refs[n_in + 4:]
    h = None
    row0 = 0
    for o_ref in o_refs:
        n = o_ref.shape[1]
        part = _dot(o_ref[...], w_ref[row0:row0 + n, :])
        h = part if h is None else h + part
        row0 += n
    y = _layer_norm(DEEPNORM_ALPHA * x_ref[...] + h, g_ref[...], b_ref[...])
    if routed:
        rw_ref, rb_ref, y_ref, yb_ref, c_ref = rest
        c_ref[...] = _route(y, rw_ref[...], rb_ref[...])
    else:
        y_ref, yb_ref = rest
    y_ref[...] = y
    yb_ref[...] = y.astype(BF16)


def _out_proj_ln(os_, w, x2d, g, b, router=None, tm=512):
    t = x2d.shape[0]
    n_in = len(os_)
    row = lambda n: pl.BlockSpec((tm, n), lambda i: (i, 0))
    full = lambda a: pl.BlockSpec(a.shape, lambda i: (0, 0))
    in_specs = [row(o.shape[1]) for o in os_] + [full(w), row(D_MODEL), full(g), full(b)]
    out_specs = [row(D_MODEL)] * 2
    out_shape = [jax.ShapeDtypeStruct((t, D_MODEL), F32), jax.ShapeDtypeStruct((t, D_MODEL), BF16)]
    args = [*os_, w, x2d, g, b]
    if router is not None:
        in_specs += [full(a) for a in router]
        out_specs.append(row(LANES))
        out_shape.append(jax.ShapeDtypeStruct((t, LANES), F32))
        args += list(router)
    return pl.pallas_call(
        functools.partial(_outln_kernel, n_in=n_in, routed=router is not None),
        grid=(t // tm,),
        in_specs=in_specs,
        out_specs=out_specs,
        out_shape=out_shape,
        compiler_params=_cparams("parallel"),
        name="out_proj_ln",
    )(*args)


def _route(x, w, b):
    xh = x.astype(BF16)
    xl = (x - xh.astype(F32)).astype(BF16)
    wh = w.astype(BF16)
    wl = (w - wh.astype(F32)).astype(BF16)
    logits = _dot(xh, wh) + (_dot(xl, wh) + _dot(xh, wl)) + b
    lane = lax.broadcasted_iota(jnp.int32, logits.shape, 1)
    logits = jnp.where(lane < N_EXPERTS, logits, NEG_INF)
    m1 = jnp.max(logits, axis=1, keepdims=True)
    i1 = jnp.min(jnp.where(logits == m1, lane, LANES), axis=1, keepdims=True)
    rest = jnp.where(lane == i1, NEG_INF, logits)
    m2 = jnp.max(rest, axis=1, keepdims=True)
    i2 = jnp.min(jnp.where(rest == m2, lane, LANES), axis=1, keepdims=True)
    e2 = jnp.exp(m2 - m1)
    den = 1.0 + e2
    return jnp.where(lane == i1, 1.0 / den, 0.0) + jnp.where(lane == i2, e2 / den, 0.0)


def _ffn_kernel(eid_ref, valid_ref, x_ref, wg_ref, wu_ref, wd_ref, o_ref, acc_ref):
    i = pl.program_id(0)
    c = pl.program_id(1)
    last = c == pl.num_programs(1) - 1
    valid = valid_ref[i] != 0

    @pl.when(valid)
    def _():
        x = x_ref[...]
        gate = _dot(x, wg_ref[0])
        up = _dot(x, wu_ref[0])
        h = gate * jax.nn.sigmoid(gate) * up
        contrib = _dot(h.astype(BF16), wd_ref[0])

        @pl.when(c == 0)
        def _():
            acc_ref[...] = contrib

        @pl.when(c != 0)
        def _():
            acc_ref[...] += contrib

        @pl.when(last)
        def _():
            o_ref[...] = acc_ref[...].astype(o_ref.dtype)

    @pl.when(jnp.logical_not(valid) & last)
    def _():
        o_ref[...] = jnp.zeros(o_ref.shape, o_ref.dtype)


def _ffn(xb, w_gu, w_d, eids, valid, out_dtype, tm, fc, name):
    r = xb.shape[0]
    f = w_d.shape[1]
    ncf = f // fc
    in_specs = [
        pl.BlockSpec((tm, D_MODEL), lambda i, c, eid, ok: (i, 0)),
        pl.BlockSpec((1, D_MODEL, fc), lambda i, c, eid, ok: (eid[i], 0, c * ok[i])),
        pl.BlockSpec((1, D_MODEL, fc), lambda i, c, eid, ok: (eid[i], 0, ncf + c * ok[i])),
        pl.BlockSpec((1, fc, D_MODEL), lambda i, c, eid, ok: (eid[i], c * ok[i], 0)),
    ]
    return pl.pallas_call(
        _ffn_kernel,
        grid_spec=pltpu.PrefetchScalarGridSpec(
            num_scalar_prefetch=2,
            grid=(r // tm, ncf),
            in_specs=in_specs,
            out_specs=pl.BlockSpec((tm, D_MODEL), lambda i, c, eid, ok: (i, 0)),
            scratch_shapes=[pltpu.VMEM((tm, D_MODEL), F32)],
        ),
        out_shape=jax.ShapeDtypeStruct((r, D_MODEL), out_dtype),
        compiler_params=_cparams("parallel", "arbitrary"),
        name=name,
    )(eids, valid, xb, w_gu, w_gu, w_d)


def _moe_ffn_kernel(eid_ref, valid_ref, first_ref, x_ref, wgu_hbm, wd_hbm, o_ref,
                    cg_ref, cu_ref, cd_ref, sg_ref, su_ref, sd_ref, sem_ref, acc_ref, *, fc):
    i = pl.program_id(0)
    e = eid_ref[i]
    f = cd_ref.shape[0]
    n_chunks = f // fc

    def chunk_copies(c, slot):
        return (
            pltpu.make_async_copy(wgu_hbm.at[e, :, pl.ds(c * fc, fc)], sg_ref.at[slot], sem_ref.at[0, slot]),
            pltpu.make_async_copy(wgu_hbm.at[e, :, pl.ds(f + c * fc, fc)], su_ref.at[slot], sem_ref.at[1, slot]),
            pltpu.make_async_copy(wd_hbm.at[e, pl.ds(c * fc, fc), :], sd_ref.at[slot], sem_ref.at[2, slot]),
        )

    def compute(c):
        x = x_ref[...]
        gate = _dot(x, cg_ref[:, c * fc:(c + 1) * fc])
        up = _dot(x, cu_ref[:, c * fc:(c + 1) * fc])
        h = gate * jax.nn.sigmoid(gate) * up
        contrib = _dot(h.astype(BF16), cd_ref[c * fc:(c + 1) * fc, :])
        if c == 0:
            acc_ref[...] = contrib
        else:
            acc_ref[...] += contrib

    valid = valid_ref[i] != 0
    first = first_ref[i] != 0

    @pl.when(valid & first)
    def _():
        for cp in chunk_copies(0, 0):
            cp.start()
        for c in range(n_chunks):
            slot = c % 2
            if c + 1 < n_chunks:
                for cp in chunk_copies(c + 1, 1 - slot):
                    cp.start()
            for cp in chunk_copies(c, slot):
                cp.wait()
            cg_ref[:, c * fc:(c + 1) * fc] = sg_ref[slot].astype(BF16)
            cu_ref[:, c * fc:(c + 1) * fc] = su_ref[slot].astype(BF16)
            cd_ref[c * fc:(c + 1) * fc, :] = sd_ref[slot].astype(BF16)
            compute(c)
        o_ref[...] = acc_ref[...].astype(o_ref.dtype)

    @pl.when(valid & jnp.logical_not(first))
    def _():
        for c in range(n_chunks):
            compute(c)
        o_ref[...] = acc_ref[...].astype(o_ref.dtype)

    @pl.when(jnp.logical_not(valid))
    def _():
        o_ref[...] = jnp.zeros(o_ref.shape, o_ref.dtype)


def _moe_ffn(xs, w_gu, w_d, eids, valid, first, tm=MOE_TM, fc=MOE_FC):
    r = xs.shape[0]
    f = w_d.shape[1]
    return pl.pallas_call(
        functools.partial(_moe_ffn_kernel, fc=fc),
        grid_spec=pltpu.PrefetchScalarGridSpec(
            num_scalar_prefetch=3,
            grid=(r // tm,),
            in_specs=[
                pl.BlockSpec((tm, D_MODEL), lambda i, *_: (i, 0)),
                pl.BlockSpec(memory_space=pl.ANY),
                pl.BlockSpec(memory_space=pl.ANY),
            ],
            out_specs=pl.BlockSpec((tm, D_MODEL), lambda i, *_: (i, 0)),
            scratch_shapes=[
                pltpu.VMEM((D_MODEL, f), BF16),
                pltpu.VMEM((D_MODEL, f), BF16),
                pltpu.VMEM((f, D_MODEL), BF16),
                pltpu.VMEM((2, D_MODEL, fc), F32),
                pltpu.VMEM((2, D_MODEL, fc), F32),
                pltpu.VMEM((2, fc, D_MODEL), F32),
                pltpu.SemaphoreType.DMA((3, 2)),
                pltpu.VMEM((tm, D_MODEL), F32),
            ],
        ),
        out_shape=jax.ShapeDtypeStruct((r, D_MODEL), BF16),
        compiler_params=_cparams("arbitrary"),
        name="ffn_moe",
    )(eids, valid, first, xs, w_gu, w_d)


def _moe_gather_kernel(te_ref, clo_ref, nch_ref, s0_ref, ns_ref, x_ref, pos_ref, o_ref, acc_ref, *, tm, sr, n_chunks):
    r = pl.program_id(0)
    e = te_ref[r]
    sub_row = lax.broadcasted_iota(jnp.int32, (sr, tm), 0)
    acc_ref[...] = jnp.zeros(acc_ref.shape, F32)

    def body(i, carry):
        c = clo_ref[r] + i
        t0 = pl.multiple_of(c * tm, tm)
        p = pos_ref[pl.ds(e, 1), pl.ds(t0, tm)]

        def sub(j, carry2):
            r0 = pl.multiple_of((s0_ref[r * n_chunks + c] + j) * sr, sr)
            onehot = jnp.where(p == r * tm + r0 + sub_row, 1.0, 0.0).astype(BF16)
            acc_ref[pl.ds(r0, sr), :] += _dot(onehot, x_ref[pl.ds(t0, tm), :])
            return carry2

        lax.fori_loop(0, ns_ref[r * n_chunks + c], sub, 0)
        return carry

    lax.fori_loop(0, nch_ref[r], body, 0)
    o_ref[...] = acc_ref[...].astype(BF16)


def _moe_gather(xb, pos_t, tile_e, clo, nch, sub0, nsub, n_rows, tm=MOE_TM, sr=MOE_SR):
    t = xb.shape[0]
    return pl.pallas_call(
        functools.partial(_moe_gather_kernel, tm=tm, sr=sr, n_chunks=t // tm),
        grid_spec=pltpu.PrefetchScalarGridSpec(
            num_scalar_prefetch=5,
            grid=(n_rows // tm,),
            in_specs=[
                pl.BlockSpec((t, D_MODEL), lambda r, *_: (0, 0), pipeline_mode=pl.Buffered(1)),
                pl.BlockSpec(pos_t.shape, lambda r, *_: (0, 0)),
            ],
            out_specs=pl.BlockSpec((tm, D_MODEL), lambda r, *_: (r, 0)),
            scratch_shapes=[pltpu.VMEM((tm, D_MODEL), F32)],
        ),
        out_shape=jax.ShapeDtypeStruct((n_rows, D_MODEL), BF16),
        compiler_params=_cparams("arbitrary"),
        name="moe_gather",
    )(tile_e, clo, nch, sub0, nsub, xb, pos_t)


def _moe_combine_kernel(n_ref, cid_ref, eid_ref, ys_hbm, pos_ref, comb_ref, x_ref, *rest, tm, cs, n_slots, n_buf):
    *ple_refs, o_ref, buf_ref, sem_ref = rest
    i = pl.program_id(0)
    n = n_ref[i]
    base = i * n_slots

    def chunk_copy(k, slot):
        row0 = pl.multiple_of(cid_ref[base + k] * cs, cs)
        return pltpu.make_async_copy(ys_hbm.at[pl.ds(row0, cs), :], buf_ref.at[slot], sem_ref.at[slot])

    o_ref[...] = jnp.zeros(o_ref.shape, F32)

    for d in range(n_buf):

        @pl.when(d < n)
        def _():
            chunk_copy(d, d).start()

    def gathered(k, slot):
        e = eid_ref[base + k]
        lane = lax.broadcasted_iota(jnp.int32, pos_ref.shape, 1)
        pos_e = jnp.sum(jnp.where(lane == e, pos_ref[...], 0), axis=1, keepdims=True)
        lane_c = lax.broadcasted_iota(jnp.int32, comb_ref.shape, 1)
        g_e = jnp.sum(jnp.where(lane_c == e, comb_ref[...], 0.0), axis=1, keepdims=True)
        col = cid_ref[base + k] * cs + lax.broadcasted_iota(jnp.int32, (tm, cs), 1)
        onehot = jnp.where(pos_e == col, 1.0, 0.0).astype(BF16)
        return g_e * _dot(onehot, buf_ref[slot])

    def body(j, carry):
        k = 2 * j
        slot = k % n_buf
        chunk_copy(k, slot).wait()
        chunk_copy(k + 1, slot + 1).wait()
        o_ref[...] += gathered(k, slot) + gathered(k + 1, slot + 1)

        @pl.when(k + n_buf < n)
        def _():
            chunk_copy(k + n_buf, slot).start()
            chunk_copy(k + n_buf + 1, slot + 1).start()

        return carry

    lax.fori_loop(0, n // 2, body, 0)
    o_ref[...] = _ln_ple(x_ref[...], o_ref[...], ple_refs)


def _moe_combine(ys, pos, combine, n_need, cid, eid, x2d, ple_args, tm=MOE_TM, cs=MOE_CS):
    t = pos.shape[0]
    return pl.pallas_call(
        functools.partial(_moe_combine_kernel, tm=tm, cs=cs, n_slots=MOE_SLOTS, n_buf=MOE_RING),
        grid_spec=pltpu.PrefetchScalarGridSpec(
            num_scalar_prefetch=3,
            grid=(t // tm,),
            in_specs=[
                pl.BlockSpec(memory_space=pl.ANY),
                pl.BlockSpec((tm, N_EXPERTS), lambda i, *_: (i, 0)),
                pl.BlockSpec((tm, LANES), lambda i, *_: (i, 0)),
                pl.BlockSpec((tm, D_MODEL), lambda i, *_: (i, 0)),
            ] + _ple_specs(tm, ple_args),
            out_specs=pl.BlockSpec((tm, D_MODEL), lambda i, *_: (i, 0)),
            scratch_shapes=[pltpu.VMEM((MOE_RING, cs, D_MODEL), BF16), pltpu.SemaphoreType.DMA((MOE_RING,))],
        ),
        out_shape=jax.ShapeDtypeStruct((t, D_MODEL), F32),
        compiler_params=_cparams("arbitrary"),
        name="moe_combine",
    )(n_need, cid, eid, ys, pos, combine, x2d, *ple_args)


def _moe_plan(combine, tm=MOE_TM):
    t = combine.shape[0]
    n_tok_tiles = t // tm
    n_row_tiles = (2 * t) // tm + N_EXPERTS
    sel = (combine[:, :N_EXPERTS] != 0.0).astype(jnp.int32)
    csum = jnp.cumsum(sel, axis=0)
    count = csum[-1]
    ntile_e = (count + tm - 1) // tm
    tile_end = jnp.cumsum(ntile_e)
    gstart = (tile_end - ntile_e) * tm
    pos = jnp.where(sel != 0, gstart[None, :] + csum - sel, -1)
    rt = jnp.arange(n_row_tiles, dtype=jnp.int32)
    tile_e = jnp.minimum(jnp.sum(rt[:, None] >= tile_end[None, :], axis=1), N_EXPERTS - 1).astype(jnp.int32)
    valid = (rt < tile_end[-1]).astype(jnp.int32)
    prev_e = jnp.concatenate([jnp.full((1,), -1, jnp.int32), tile_e[:-1]])
    group_head = ((valid != 0) & (tile_e != prev_e)).astype(jnp.int32)
    cb = jnp.concatenate([jnp.zeros((1, N_EXPERTS), jnp.int32), csum[tm - 1::tm]], axis=0)
    off = rt * tm - gstart[tile_e]
    cb_t = cb[:, tile_e].T
    clo = jnp.sum(cb_t[:, 1:] <= off[:, None], axis=1).astype(jnp.int32)
    cend = jnp.sum(cb_t[:, :-1] < (off + tm)[:, None], axis=1).astype(jnp.int32)
    nch = jnp.where(valid != 0, jnp.maximum(cend - clo, 0), 0).astype(jnp.int32)
    clo = jnp.minimum(clo, n_tok_tiles - 1)
    r_lo = jnp.clip(cb_t[:, :-1] - off[:, None], 0, tm)
    r_hi = jnp.clip(cb_t[:, 1:] - off[:, None], 0, tm)
    sub0 = (r_lo // MOE_SR).astype(jnp.int32)
    nsub = jnp.where(r_hi > r_lo, (r_hi + MOE_SR - 1) // MOE_SR - sub0, 0).astype(jnp.int32)
    first = gstart[None, :] + cb[:-1]
    n_ie = cb[1:] - cb[:-1]
    per_e = MOE_SLOTS // N_EXPERTS
    k = jnp.arange(per_e, dtype=jnp.int32)
    c_lo = first // MOE_CS
    c_hi = (first + n_ie - 1) // MOE_CS
    cand = c_lo[..., None] + k
    need = ((n_ie[..., None] > 0) & (cand <= c_hi[..., None])).reshape(n_tok_tiles, MOE_SLOTS)
    cand = cand.reshape(n_tok_tiles, MOE_SLOTS)
    eids = jnp.broadcast_to(jnp.arange(N_EXPERTS, dtype=jnp.int32)[:, None], (N_EXPERTS, per_e)).reshape(-1)
    order = jnp.argsort(jnp.logical_not(need), axis=1, stable=True)
    cid = jnp.where(need, cand, 0)
    cid = jnp.take_along_axis(cid, order, axis=1).reshape(-1).astype(jnp.int32)
    need_sorted = jnp.take_along_axis(need, order, axis=1)
    eid = jnp.where(need_sorted, eids[order], -1).reshape(-1).astype(jnp.int32)
    n_need = jnp.sum(need, axis=1).astype(jnp.int32)
    n_need = n_need + n_need % 2
    return dict(pos=pos.astype(jnp.int32), tile_e=tile_e, valid=valid, first=group_head, clo=clo, nch=nch,
                sub0=sub0.reshape(-1), nsub=nsub.reshape(-1),
                n_need=n_need, cid=cid, eid=eid, n_rows=n_row_tiles * tm)


def _moe(xb, x2d, combine, w_gu, w_d, first_expert, ple_args):
    plan = _moe_plan(combine)
    xs = _moe_gather(xb, plan["pos"].T, plan["tile_e"], plan["clo"], plan["nch"], plan["sub0"], plan["nsub"],
                     plan["n_rows"])
    ys = _moe_ffn(xs, w_gu, w_d, plan["tile_e"] + first_expert, plan["valid"], plan["first"])
    return _moe_combine(ys, plan["pos"], combine, plan["n_need"], plan["cid"], plan["eid"], x2d, ple_args)


def _ln_ple(x, h, ple_refs):
    g_ref, b_ref, p_ref, wg_ref, wp_ref = ple_refs
    y = _layer_norm(DEEPNORM_ALPHA * x + h, g_ref[...], b_ref[...])
    gate = jax.nn.sigmoid(_dot(y.astype(BF16), wg_ref[...]))
    proj = _dot(p_ref[...].astype(BF16), wp_ref[...])
    return y + gate * proj


def _ple_specs(tm, ple_args):
    g, b, p2d, wg, wp = ple_args
    ign = lambda f: (lambda *idx: f(idx[0]))
    full = lambda a: pl.BlockSpec(a.shape, ign(lambda i: (0, 0)))
    return [full(g), full(b), pl.BlockSpec((tm, D_PLE), ign(lambda i: (i, 0))), full(wg), full(wp)]


def _ffn_ple_kernel(xb_ref, wgu_ref, wd_ref, x_ref, *rest, fc):
    *ple_refs, o_ref, acc_ref = rest
    f = wd_ref.shape[0]
    xb = xb_ref[...]
    for c in range(f // fc):
        gate = _dot(xb, wgu_ref[:, c * fc:(c + 1) * fc])
        up = _dot(xb, wgu_ref[:, f + c * fc:f + (c + 1) * fc])
        h = gate * jax.nn.sigmoid(gate) * up
        contrib = _dot(h.astype(BF16), wd_ref[c * fc:(c + 1) * fc, :])
        if c == 0:
            acc_ref[...] = contrib
        else:
            acc_ref[...] += contrib
    o_ref[...] = _ln_ple(x_ref[...], acc_ref[...], ple_refs)


def _ffn_ple(xb, x2d, w_gu, w_d, ple_args, tm=512, fc=1408):
    t = xb.shape[0]
    row = lambda n: pl.BlockSpec((tm, n), lambda i: (i, 0))
    resident = lambda a: pl.BlockSpec(a.shape, lambda i: (0, 0), pipeline_mode=pl.Buffered(1))
    return pl.pallas_call(
        functools.partial(_ffn_ple_kernel, fc=fc),
        grid=(t // tm,),
        in_specs=[row(D_MODEL), resident(w_gu), resident(w_d), row(D_MODEL)] + _ple_specs(tm, ple_args),
        out_specs=row(D_MODEL),
        out_shape=jax.ShapeDtypeStruct((t, D_MODEL), F32),
        scratch_shapes=[pltpu.VMEM((tm, D_MODEL), F32)],
        compiler_params=_cparams("parallel"),
        name="ffn_dense",
    )(xb, w_gu, w_d, x2d, *ple_args)


def _even_mixer(x2d, b, s, w_in, b_f, w_out):
    nf, ns = FOX_W, SB_W
    o_f = 3 * nf
    o_s = o_f + N_HEADS_FOX
    qa, ka, va = w_in[:, :nf], w_in[:, nf:2 * nf], w_in[:, 2 * nf:o_f]
    qs, ks, vs = w_in[:, o_s:o_s + ns], w_in[:, o_s + ns:o_s + 2 * ns], w_in[:, o_s + 2 * ns:]
    w_k = jnp.concatenate([ka, ks], axis=1).astype(BF16)
    w_t = jnp.concatenate([qa * Q_SCALE, va, qs * Q_SCALE, vs], axis=1).T.astype(BF16)
    w_f = jnp.pad(w_in[:, o_f:o_s], ((0, 0), (0, LANES - N_HEADS_FOX))).astype(BF16)
    bias_f = jnp.pad(b_f, (0, LANES - N_HEADS_FOX)).reshape(1, LANES)
    k2d, ht, lf = _ab_proj(x2d, w_k, w_t, w_f, bias_f, b, s)
    lf_t = lf.reshape(b, s, LANES)[:, :, :N_HEADS_FOX].transpose(0, 2, 1)
    crow, caug = _cumsum_seq(lf_t)
    k3 = k2d.reshape(b, s, -1)
    oa, ob = _even_attention(k3, ht, caug, crow)
    return [oa.reshape(b * s, nf), ob.reshape(b * s, ns)], w_out.astype(BF16)


def _odd_mixer(x2d, b, s, w_qkv, sinks, w_out, tables):
    nq = N_HEADS_SWA * HEAD_DIM
    wq = w_qkv[:, :nq]
    wk = [w_qkv[:, nq + i * HEAD_DIM: nq + (i + 1) * HEAD_DIM] for i in range(N_KV_SWA)]
    wv = [w_qkv[:, nq + (N_KV_SWA + i) * HEAD_DIM: nq + (N_KV_SWA + i + 1) * HEAD_DIM] for i in range(N_KV_SWA)]
    dup = lambda ws: [w for w in ws for _ in range(2)]
    w_k = jnp.concatenate(dup(wk), axis=1).astype(BF16)
    w_t = jnp.concatenate([wq * Q_SCALE] + dup(wv), axis=1).T.astype(BF16)
    k2d, qt, vt = _swa_proj(x2d, w_k, w_t, tables, b, s)
    o = _swa_attention(qt, k2d.reshape(b, s, -1), vt, sinks)
    return [o.reshape(b * s, nq)], w_out.astype(BF16)


def kernel(x, p, ln_mix_g, ln_mix_b, ln_ffn_g, ln_ffn_b, ab_w_in, ab_b_f, ab_w_out, c_w_qkv, c_sinks, c_w_out,
           ffn_w_gate_up, ffn_w_down, router_w, router_b, moe_w_gate_up, moe_w_down, ple_w_gate, ple_w_proj):
    b, s, d = x.shape
    t = b * s
    x2d = x.reshape(t, d)
    tables = _rope_tables(s)
    row = lambda v: v.reshape(1, -1)
    moe_gu = moe_w_gate_up.reshape((-1,) + moe_w_gate_up.shape[2:])
    moe_d = moe_w_down.reshape((-1,) + moe_w_down.shape[2:])
    for i in range(DEPTH):
        j = i // 2
        if i % 2 == 0:
            os_, ws = _even_mixer(x2d, b, s, ab_w_in[j], ab_b_f[j], ab_w_out[j])
        else:
            os_, ws = _odd_mixer(x2d, b, s, c_w_qkv[j], c_sinks[j], c_w_out[j], tables)
        ple_args = (row(ln_ffn_g[i]), row(ln_ffn_b[i]), p[i].reshape(t, D_PLE),
                    ple_w_gate[i].astype(BF16), ple_w_proj[i].astype(BF16))
        if i % 2 == 0:
            x2d, xb = _out_proj_ln(os_, ws, x2d, row(ln_mix_g[i]), row(ln_mix_b[i]))
            x2d = _ffn_ple(xb, x2d, ffn_w_gate_up[j].astype(BF16), ffn_w_down[j].astype(BF16), ple_args)
        else:
            rw = jnp.pad(router_w[j], ((0, 0), (0, LANES - N_EXPERTS)))
            rb = jnp.pad(router_b[j], (0, LANES - N_EXPERTS)).reshape(1, LANES)
            x2d, xb, combine = _out_proj_ln(os_, ws, x2d, row(ln_mix_g[i]), row(ln_mix_b[i]), router=(rw, rb))
            x2d = _moe(xb, x2d, combine, moe_gu, moe_d, j * N_EXPERTS, ple_args)
    return x2d.reshape(b, s, d)
```

```python
import functools
import math

import jax
import jax.numpy as jnp
from jax import lax
from jax.experimental import pallas as pl
from jax.experimental.pallas import tpu as pltpu

F32 = jnp.float32
BF16 = jnp.bfloat16

D_MODEL = 1024
HEAD_DIM = 64
LANES = 128
N_HEADS_FOX = 8
N_HEADS_SB = 8
N_HEADS_SWA = 16
N_KV_SWA = 2
WINDOW = 128
ROPE_THETA = 500000.0
ROPE_DIM = HEAD_DIM // 4
N_EXPERTS = 8
D_PLE = 256
LN_EPS = 1e-5
DEPTH = 4
DEEPNORM_ALPHA = (2.0 * DEPTH) ** 0.25
ATTN_SCALE = HEAD_DIM ** -0.5
FOX_W = N_HEADS_FOX * HEAD_DIM
SB_W = N_HEADS_SB * HEAD_DIM
VMEM_LIMIT = 56 * 1024 * 1024

NEG_INF = float("-inf")
LOG2E = math.log2(math.e)
Q_SCALE = ATTN_SCALE * LOG2E
SUM_ROWS = 16
SWA_HEAD_BATCH = 4
MOE_TM = 512
MOE_SR = 128
MOE_FC = 512
MOE_CS = 256
MOE_SLOTS = N_EXPERTS * (MOE_TM // MOE_CS + 1)
MOE_RING = 8


def _cparams(*sem):
    return pltpu.CompilerParams(dimension_semantics=sem, vmem_limit_bytes=VMEM_LIMIT)


def _dot(a, b):
    return jnp.dot(a, b, preferred_element_type=F32)


def _dot_nt(a, b):
    return lax.dot_general(a, b, (((1,), (1,)), ((), ())), preferred_element_type=F32)


def _log_sigmoid(x):
    return jnp.minimum(x, 0.0) - jnp.log1p(jnp.exp(-jnp.abs(x)))


def _ab_proj_kernel(x_ref, wk_ref, wt_ref, wf_ref, bf_ref, k_ref, ht_ref, lf_ref):
    xb = x_ref[...].astype(BF16)
    k_ref[...] = _dot(xb, wk_ref[...]).astype(BF16)
    ht_ref[0] = _dot_nt(wt_ref[...], xb).astype(BF16)
    lf_ref[...] = _log_sigmoid(_dot(xb, wf_ref[...]) + bf_ref[...])


def _ab_proj(x2d, wk, wt, wf, bf, b, s, tm=512):
    t = x2d.shape[0]
    nk = wk.shape[1]
    nt = wt.shape[0]
    spt = s // tm
    return pl.pallas_call(
        _ab_proj_kernel,
        grid=(t // tm,),
        in_specs=[
            pl.BlockSpec((tm, D_MODEL), lambda i: (i, 0)),
            pl.BlockSpec((D_MODEL, nk), lambda i: (0, 0)),
            pl.BlockSpec((nt, D_MODEL), lambda i: (0, 0)),
            pl.BlockSpec((D_MODEL, LANES), lambda i: (0, 0)),
            pl.BlockSpec((1, LANES), lambda i: (0, 0)),
        ],
        out_specs=[
            pl.BlockSpec((tm, nk), lambda i: (i, 0)),
            pl.BlockSpec((1, nt, tm), lambda i: (i // spt, 0, i % spt)),
            pl.BlockSpec((tm, LANES), lambda i: (i, 0)),
        ],
        out_shape=[
            jax.ShapeDtypeStruct((t, nk), BF16),
            jax.ShapeDtypeStruct((b, nt, s), BF16),
            jax.ShapeDtypeStruct((t, LANES), F32),
        ],
        compiler_params=_cparams("parallel"),
        name="ab_proj",
    )(x2d, wk, wt, wf, bf)


def _split3(v):
    hi = v.astype(BF16).astype(F32)
    r1 = v - hi
    mid = r1.astype(BF16).astype(F32)
    return hi, mid, r1 - mid


def _cumsum_kernel(x_ref, c_ref, aug_ref, st_ref):
    x = x_ref[0]
    s = x.shape[1]
    lane = lax.broadcasted_iota(jnp.int32, x.shape, 1) & (LANES - 1)
    sh = 1
    while sh < LANES:
        x = x + jnp.where(lane >= sh, pltpu.roll(x, sh, axis=1), 0.0)
        sh *= 2
    carry = jnp.zeros((x.shape[0], 1), F32)
    for c in range(s // LANES):
        blk = x[:, c * LANES:(c + 1) * LANES] + carry
        c_ref[0, :, c * LANES:(c + 1) * LANES] = blk
        carry = blk[:, LANES - 1:LANES]

    ones = jnp.ones((3, s), F32)
    st_ref[...] = jnp.zeros(st_ref.shape, F32)
    for p in range(x.shape[0] // 2):
        for which, base in ((0, HEAD_DIM), (1, 0)):
            negc = c_ref[0, 2 * p + which:2 * p + which + 1, :] * -LOG2E
            hi, mid, lo = _split3(negc)
            st_ref[base:base + 1, :] = hi
            st_ref[base + 1:base + 2, :] = mid
            st_ref[base + 2:base + 3, :] = lo
            st_ref[base + 3:base + 6, :] = ones
        for c in range(s // LANES):
            blk = st_ref[:, c * LANES:(c + 1) * LANES]
            aug_ref[0, p, c * LANES:(c + 1) * LANES, :] = blk.T.astype(BF16)


def _cumsum_seq(lf_t):
    b, h, s = lf_t.shape
    return pl.pallas_call(
        _cumsum_kernel,
        grid=(b,),
        in_specs=[pl.BlockSpec((1, h, s), lambda i: (i, 0, 0))],
        out_specs=[
            pl.BlockSpec((1, h, s), lambda i: (i, 0, 0)),
            pl.BlockSpec((1, h // 2, s, LANES), lambda i: (i, 0, 0, 0)),
        ],
        out_shape=[
            jax.ShapeDtypeStruct((b, h, s), F32),
            jax.ShapeDtypeStruct((b, h // 2, s, LANES), BF16),
        ],
        scratch_shapes=[pltpu.VMEM((LANES, s), F32)],
        compiler_params=_cparams("parallel"),
        name="fox_cumsum",
    )(lf_t)


def _fox_stages(qt_ref, k_ref, aug_ref, vt_ref, c_ref, o_ref, m_ref, acc_ref, sa_ref, sb_ref, *, tq, tk):
    hp = pl.program_id(1)
    qi = pl.program_id(2)
    q0 = qi * tq
    qt = qt_ref[0].astype(F32)
    sub = lax.broadcasted_iota(jnp.int32, (LANES, tq), 0)
    lane = lax.broadcasted_iota(jnp.int32, (tk, LANES), 1)
    vsub = lax.broadcasted_iota(jnp.int32, (LANES, tk), 0)
    own_rows = [sub < HEAD_DIM, sub >= HEAD_DIM]
    own_lanes = [lane < HEAD_DIM, lane >= HEAD_DIM]
    own_vrows = [vsub < HEAD_DIM, vsub >= HEAD_DIM]

    qaug = []
    for r in range(2):
        base = HEAD_DIM * (1 - r)
        ci = c_ref[0, pl.ds(2 * hp + r, 1), :]
        hi, mid, lo = _split3(ci * LOG2E)
        t = jnp.where(own_rows[r], qt, 0.0)
        t = jnp.where((sub >= base) & (sub < base + 3), 1.0, t)
        t = jnp.where(sub == base + 3, hi, t)
        t = jnp.where(sub == base + 4, mid, t)
        t = jnp.where(sub == base + 5, lo, t)
        qaug.append(t.astype(BF16))

    m_ref[...] = jnp.full(m_ref.shape, NEG_INF, F32)
    acc_ref[...] = jnp.zeros(acc_ref.shape, F32)

    def scores(k0, s_ref, lo=0):
        kt = k_ref[0, pl.ds(k0, tk), :]
        ca = aug_ref[0, 0, pl.ds(k0, tk), :]
        for r in range(2):
            ka = jnp.where(own_lanes[r], kt, ca)
            s_ref[r, :, lo:] = _dot(ka, qaug[r][:, lo:])

    def softmax_pv(k0, s_ref, masked, lo=0):
        vt = vt_ref[0, :, pl.ds(k0, tk)]
        if masked:
            key = k0 + lax.broadcasted_iota(jnp.int32, (tk, tq - lo), 0)
            qry = q0 + lo + lax.broadcasted_iota(jnp.int32, (tk, tq - lo), 1)
            keep = key <= qry
        for r in range(2):
            st = s_ref[r, :, lo:]
            if masked:
                st = jnp.where(keep, st, NEG_INF)
            m_prev = m_ref[r, :, lo:]
            m_new = jnp.maximum(m_prev, jnp.max(st, axis=0, keepdims=True))
            alpha = jnp.exp2(m_prev - m_new)
            pt = jnp.exp2(st - m_new).astype(BF16)
            va = jnp.where(own_vrows[r], vt, 1.0)
            acc_ref[r, :, lo:] = alpha * acc_ref[r, :, lo:] + _dot(va, pt)
            m_ref[r, :, lo:] = m_new

    tile = lambda j: pl.multiple_of(j * tk, tk)

    def start():
        scores(tile(0), sa_ref)

    def half_a(i):
        scores(tile(2 * i + 1), sb_ref)
        softmax_pv(tile(2 * i), sa_ref, False)

    def half_b(i):
        scores(tile(2 * i + 2), sa_ref)
        softmax_pv(tile(2 * i + 1), sb_ref, False)

    def finish():
        scores(tile(2 * qi + 1), sb_ref, lo=tk)
        softmax_pv(tile(2 * qi), sa_ref, True)
        softmax_pv(tile(2 * qi + 1), sb_ref, True, lo=tk)
        ot = jnp.zeros((LANES, tq), F32)
        for r in range(2):
            a = acc_ref[r]
            den = a[HEAD_DIM * (1 - r):HEAD_DIM * (1 - r) + 1, :]
            ot = ot + jnp.where(own_rows[r], a / den, 0.0)
        o_ref[0] = ot.T.astype(BF16)

    return start, half_a, half_b, finish


def _sb_stages(qt_ref, k_ref, vt_ref, o_ref, run_ref, acc_ref, za_ref, zb_ref, *, tq, cw):
    qi = pl.program_id(2)
    q0 = qi * tq
    qt = qt_ref[0]
    sub = lax.broadcasted_iota(jnp.int32, (LANES, tq), 0)
    own_rows = [sub < HEAD_DIM, sub >= HEAD_DIM]
    qm = [jnp.where(mk, qt, 0.0) for mk in own_rows]

    ss = lax.broadcasted_iota(jnp.int32, (cw, cw), 0)
    jj = lax.broadcasted_iota(jnp.int32, (cw, cw), 1)
    tri = jnp.where(jj > ss, 1.0, 0.0).astype(BF16)
    tri = jnp.concatenate([tri, jnp.ones((SUM_ROWS, cw), BF16)], axis=0)

    run_ref[...] = jnp.zeros(run_ref.shape, F32)
    acc_ref[...] = jnp.zeros(acc_ref.shape, F32)

    def start_of(c):
        return pl.multiple_of(jnp.maximum(q0 + tq - (c + 1) * cw, 0), cw)

    def logits(c, z_ref, lo=0):
        kt = k_ref[0, pl.ds(start_of(c), cw), :]
        for r in range(2):
            z_ref[r, :, lo:] = _dot(kt, qm[r][:, lo:])

    def weights_pv(c, z_ref, masked, lo=0):
        k0 = start_of(c)
        vt = vt_ref[0, :, pl.ds(k0, cw)]
        if masked:
            key = k0 + lax.broadcasted_iota(jnp.int32, (cw, tq - lo), 0)
            qry = q0 + lo + lax.broadcasted_iota(jnp.int32, (cw, tq - lo), 1)
            keep = key < qry
        zs, sps, laters, ws = [], [], [], []
        for r in range(2):
            z = z_ref[r, :, lo:]
            e = jnp.exp2(-jnp.abs(z))
            sp = jnp.maximum(z, 0.0) + jnp.log(1.0 + e) * LOG2E
            if masked:
                sp = jnp.where(keep, sp, 0.0)
            zs.append(z)
            sps.append(sp)
        for r in range(2):
            laters.append(_dot(tri, sps[r].astype(BF16)))
        for r in range(2):
            run = run_ref[r, :, lo:]
            a = jnp.exp2((zs[r] - sps[r]) - (laters[r][:cw] + run))
            if masked:
                a = jnp.where(keep, a, 0.0)
            ws.append(a.astype(BF16))
            run_ref[r, :, lo:] = run + laters[r][cw:cw + 1]
        for r in range(2):
            acc_ref[r, :, lo:] = acc_ref[r, :, lo:] + _dot(vt, ws[r])

    def start():
        logits(0, za_ref, lo=cw)
        logits(1, zb_ref)
        weights_pv(0, za_ref, True, lo=cw)
        logits(2, za_ref)
        weights_pv(1, zb_ref, True)

    def half_a(i):
        logits(2 * i + 1, zb_ref)
        weights_pv(2 * i, za_ref, False)

    def half_b(i):
        logits(2 * i + 2, za_ref)
        weights_pv(2 * i + 1, zb_ref, False)

    def finish():
        ot = jnp.where(own_rows[0], acc_ref[0], acc_ref[1])
        o_ref[0] = ot.T.astype(BF16)

    return start, half_a, half_b, finish


def _even_attn_kernel(fq_ref, fk_ref, aug_ref, fv_ref, c_ref, sq_ref, sk_ref, sv_ref, of_ref, os_ref,
                      m_ref, facc_ref, sa_ref, sb_ref, run_ref, sacc_ref, za_ref, zb_ref, *, tq, tk):
    qi = pl.program_id(2)
    f_start, f_a, f_b, f_finish = _fox_stages(fq_ref, fk_ref, aug_ref, fv_ref, c_ref, of_ref,
                                              m_ref, facc_ref, sa_ref, sb_ref, tq=tq, tk=tk)
    s_start, s_a, s_b, s_finish = _sb_stages(sq_ref, sk_ref, sv_ref, os_ref,
                                             run_ref, sacc_ref, za_ref, zb_ref, tq=tq, cw=tk)
    f_start()
    s_start()

    def body(i, carry):
        f_a(i)
        s_a(i + 1)
        f_b(i)
        s_b(i + 1)
        return carry

    lax.fori_loop(0, qi, body, 0)
    f_finish()
    s_finish()


def _even_attention(k3, ht, caug, crow, tq=512):
    b, s, _ = k3.shape
    nhp = N_HEADS_FOX // 2
    tk = tq // 2
    qt_spec = lambda blk: pl.BlockSpec((1, LANES, tq), lambda bi, hp, qi: (bi, blk * nhp + hp, qi))
    k_spec = lambda blk: pl.BlockSpec((1, s, LANES), lambda bi, hp, qi: (bi, 0, blk * nhp + hp))
    vt_spec = lambda blk: pl.BlockSpec((1, LANES, s), lambda bi, hp, qi: (bi, blk * nhp + hp, 0))
    out_spec = pl.BlockSpec((1, tq, LANES), lambda bi, hp, qi: (bi, qi, hp))
    return pl.pallas_call(
        functools.partial(_even_attn_kernel, tq=tq, tk=tk),
        grid=(b, nhp, s // tq),
        in_specs=[
            qt_spec(0), k_spec(0),
            pl.BlockSpec((1, 1, s, LANES), lambda bi, hp, qi: (bi, hp, 0, 0)),
            vt_spec(1),
            pl.BlockSpec((1, N_HEADS_FOX, tq), lambda bi, hp, qi: (bi, 0, qi)),
            qt_spec(2), k_spec(1), vt_spec(3),
        ],
        out_specs=[out_spec, out_spec],
        out_shape=[jax.ShapeDtypeStruct((b, s, FOX_W), BF16), jax.ShapeDtypeStruct((b, s, SB_W), BF16)],
        scratch_shapes=[
            pltpu.VMEM((2, 1, tq), F32), pltpu.VMEM((2, LANES, tq), F32),
            pltpu.VMEM((2, tk, tq), F32), pltpu.VMEM((2, tk, tq), F32),
            pltpu.VMEM((2, 1, tq), F32), pltpu.VMEM((2, LANES, tq), F32),
            pltpu.VMEM((2, tk, tq), F32), pltpu.VMEM((2, tk, tq), F32),
        ],
        compiler_params=_cparams("parallel", "parallel", "arbitrary"),
        name="even_attn",
    )(ht, k3, caug, ht, crow, ht, k3, ht)


def _swa_proj_kernel(x_ref, wk_ref, wt_ref, cos_ref, sa_ref, sb_ref, cost_ref, sint_ref, k_ref, qt_ref, vt_ref):
    xb = x_ref[...].astype(BF16)
    half = ROPE_DIM // 2
    yk = _dot(xb, wk_ref[...])
    for c in range(yk.shape[1] // LANES):
        blk = yk[:, c * LANES:(c + 1) * LANES]
        blk = (blk * cos_ref[...] + pltpu.roll(blk, half, axis=1) * sa_ref[...]
               + pltpu.roll(blk, LANES - half, axis=1) * sb_ref[...])
        k_ref[:, c * LANES:(c + 1) * LANES] = blk.astype(BF16)
    yt = _dot_nt(wt_ref[...], xb)
    cos = cost_ref[...]
    sin = sint_ref[...]
    nq = N_HEADS_SWA * HEAD_DIM
    for h in range(N_HEADS_SWA):
        base = h * HEAD_DIM
        x1 = yt[base:base + half]
        x2 = yt[base + half:base + ROPE_DIM]
        rot = jnp.concatenate([x1 * cos - x2 * sin, x2 * cos + x1 * sin], axis=0)
        qt_ref[0, base:base + ROPE_DIM, :] = rot.astype(BF16)
        qt_ref[0, base + ROPE_DIM:base + HEAD_DIM, :] = yt[base + ROPE_DIM:base + HEAD_DIM].astype(BF16)
    vt_ref[0] = yt[nq:].astype(BF16)


def _rope_tables(s):
    half = ROPE_DIM // 2
    inv = ROPE_THETA ** (-jnp.arange(half, dtype=F32) * 2.0 / ROPE_DIM)
    ang = jnp.arange(s, dtype=F32)[:, None] * inv[None, :]
    cos, sin = jnp.cos(ang), jnp.sin(ang)
    d = jnp.arange(LANES) % HEAD_DIM
    idx = d % half
    cos_t = jnp.where(d[None, :] < ROPE_DIM, cos[:, idx], 1.0)
    sin_a = jnp.where((d[None, :] >= half) & (d[None, :] < ROPE_DIM), sin[:, idx], 0.0)
    sin_b = jnp.where(d[None, :] < half, -sin[:, idx], 0.0)
    return cos_t.astype(F32), sin_a.astype(F32), sin_b.astype(F32), cos.T.astype(F32), sin.T.astype(F32)


def _swa_proj(x2d, wk, wt, tables, b, s, tm=512):
    t = x2d.shape[0]
    nk = wk.shape[1]
    nq = N_HEADS_SWA * HEAD_DIM
    nv = wt.shape[0] - nq
    spt = s // tm
    half = ROPE_DIM // 2
    lane_tab = pl.BlockSpec((tm, LANES), lambda i: (i % spt, 0))
    row_tab = pl.BlockSpec((half, tm), lambda i: (0, i % spt))
    return pl.pallas_call(
        _swa_proj_kernel,
        grid=(t // tm,),
        in_specs=[
            pl.BlockSpec((tm, D_MODEL), lambda i: (i, 0)),
            pl.BlockSpec(wk.shape, lambda i: (0, 0)),
            pl.BlockSpec(wt.shape, lambda i: (0, 0)),
            lane_tab, lane_tab, lane_tab, row_tab, row_tab,
        ],
        out_specs=[
            pl.BlockSpec((tm, nk), lambda i: (i, 0)),
            pl.BlockSpec((1, nq, tm), lambda i: (i // spt, 0, i % spt)),
            pl.BlockSpec((1, nv, tm), lambda i: (i // spt, 0, i % spt)),
        ],
        out_shape=[
            jax.ShapeDtypeStruct((t, nk), BF16),
            jax.ShapeDtypeStruct((b, nq, s), BF16),
            jax.ShapeDtypeStruct((b, nv, s), BF16),
        ],
        compiler_params=_cparams("parallel"),
        name="swa_proj",
    )(x2d, wk, wt, *tables)


def _swa_kernel(sink_ref, qt_ref, k_ref, vt_ref, o_ref, *, tq):
    qi = pl.program_id(1)
    q0 = qi * tq
    kw = tq + WINDOW
    kstart = pl.multiple_of(jnp.maximum(q0 - WINDOW, 0), WINDOW)
    kwin = k_ref[0, pl.ds(kstart, kw), :]
    vwin = vt_ref[0, :, pl.ds(kstart, kw)]
    key = kstart + lax.broadcasted_iota(jnp.int32, (kw, tq), 0)
    qry = q0 + lax.broadcasted_iota(jnp.int32, (kw, tq), 1)
    diff = qry - key
    keep = (diff >= 0) & (diff < WINDOW)
    sub = lax.broadcasted_iota(jnp.int32, (LANES, tq), 0)
    vsub = lax.broadcasted_iota(jnp.int32, (LANES, kw), 0)
    own_rows = [sub < HEAD_DIM, sub >= HEAD_DIM]
    own_vrows = [vsub < HEAD_DIM, vsub >= HEAD_DIM]
    group = N_HEADS_SWA // N_KV_SWA

    def scores(h):
        c, r = divmod(h, 2)
        g = h // group
        qblk = qt_ref[0, c * LANES:(c + 1) * LANES, :]
        return _dot(kwin[:, g * LANES:(g + 1) * LANES], jnp.where(own_rows[r], qblk, 0.0))

    def weights(h, st):
        st = jnp.where(keep, st, NEG_INF)
        sink = sink_ref[h] * LOG2E
        mx = jnp.maximum(jnp.max(st, axis=0, keepdims=True), sink)
        return jnp.exp2(st - mx).astype(BF16), jnp.exp2(sink - mx)

    def values(h, pt):
        r = h % 2
        g = h // group
        va = jnp.where(own_vrows[r], vwin[g * LANES:(g + 1) * LANES, :], 1.0)
        return _dot(va, pt)

    batches = [list(range(i, i + SWA_HEAD_BATCH)) for i in range(0, N_HEADS_SWA, SWA_HEAD_BATCH)]
    sts = [scores(h) for h in batches[0]]
    for bi, heads in enumerate(batches):
        nxt = [scores(h) for h in batches[bi + 1]] if bi + 1 < len(batches) else None
        pts = [weights(h, st) for h, st in zip(heads, sts)]
        accs = [values(h, pt) for h, (pt, _) in zip(heads, pts)]
        outs = []
        for h, acc, (_, esink) in zip(heads, accs, pts):
            r = h % 2
            den = acc[HEAD_DIM * (1 - r):HEAD_DIM * (1 - r) + 1, :] + esink
            outs.append(acc / den)
            if r == 1:
                c = h // 2
                ot = jnp.where(own_rows[0], outs[-2], outs[-1])
                o_ref[0, :, c * LANES:(c + 1) * LANES] = ot.T.astype(BF16)
        sts = nxt


def _swa_attention(qt3, k3, vt3, sinks, tq=256):
    b, nq, s = qt3.shape
    kern = functools.partial(_swa_kernel, tq=tq)
    return pl.pallas_call(
        kern,
        grid=(b, s // tq),
        in_specs=[
            pl.BlockSpec(memory_space=pltpu.SMEM),
            pl.BlockSpec((1, nq, tq), lambda bi, qi: (bi, 0, qi)),
            pl.BlockSpec((1, s, k3.shape[2]), lambda bi, qi: (bi, 0, 0)),
            pl.BlockSpec((1, vt3.shape[1], s), lambda bi, qi: (bi, 0, 0)),
        ],
        out_specs=pl.BlockSpec((1, tq, nq), lambda bi, qi: (bi, qi, 0)),
        out_shape=jax.ShapeDtypeStruct((b, s, nq), BF16),
        compiler_params=_cparams("parallel", "arbitrary"),
        name="swa_attn",
    )(sinks, qt3, k3, vt3)


def _layer_norm(z, g, b):
    mu = jnp.mean(z, axis=-1, keepdims=True)
    zc = z - mu
    var = jnp.mean(zc * zc, axis=-1, keepdims=True)
    return zc * lax.rsqrt(var + LN_EPS) * g + b


def _outln_kernel(*refs, n_in, routed):
    o_refs = refs[:n_in]
    w_ref, x_ref, g_ref, b_ref = refs[n_in:n_in + 4]
    rest = refs[n_in + 4:]
    h = None
    row0 = 0
    for o_ref in o_refs:
        n = o_ref.shape[1]
        part = _dot(o_ref[...], w_ref[row0:row0 + n, :])
        h = part if h is None else h + part
        row0 += n
    y = _layer_norm(DEEPNORM_ALPHA * x_ref[...] + h, g_ref[...], b_ref[...])
    if routed:
        rw_ref, rb_ref, y_ref, yb_ref, c_ref = rest
        c_ref[...] = _route(y, rw_ref[...], rb_ref[...])
    else:
        y_ref, yb_ref = rest
    y_ref[...] = y
    yb_ref[...] = y.astype(BF16)


def _out_proj_ln(os_, w, x2d, g, b, router=None, tm=512):
    t = x2d.shape[0]
    n_in = len(os_)
    row = lambda n: pl.BlockSpec((tm, n), lambda i: (i, 0))
    full = lambda a: pl.BlockSpec(a.shape, lambda i: (0, 0))
    in_specs = [row(o.shape[1]) for o in os_] + [full(w), row(D_MODEL), full(g), full(b)]
    out_specs = [row(D_MODEL)] * 2
    out_shape = [jax.ShapeDtypeStruct((t, D_MODEL), F32), jax.ShapeDtypeStruct((t, D_MODEL), BF16)]
    args = [*os_, w, x2d, g, b]
    if router is not None:
        in_specs += [full(a) for a in router]
        out_specs.append(row(LANES))
        out_shape.append(jax.ShapeDtypeStruct((t, LANES), F32))
        args += list(router)
    return pl.pallas_call(
        functools.partial(_outln_kernel, n_in=n_in, routed=router is not None),
        grid=(t // tm,),
        in_specs=in_specs,
        out_specs=out_specs,
        out_shape=out_shape,
        compiler_params=_cparams("parallel"),
        name="out_proj_ln",
    )(*args)


def _route(x, w, b):
    xh = x.astype(BF16)
    xl = (x - xh.astype(F32)).astype(BF16)
    wh = w.astype(BF16)
    wl = (w - wh.astype(F32)).astype(BF16)
    logits = _dot(xh, wh) + (_dot(xl, wh) + _dot(xh, wl)) + b
    lane = lax.broadcasted_iota(jnp.int32, logits.shape, 1)
    logits = jnp.where(lane < N_EXPERTS, logits, NEG_INF)
    m1 = jnp.max(logits, axis=1, keepdims=True)
    i1 = jnp.min(jnp.where(logits == m1, lane, LANES), axis=1, keepdims=True)
    rest = jnp.where(lane == i1, NEG_INF, logits)
    m2 = jnp.max(rest, axis=1, keepdims=True)
    i2 = jnp.min(jnp.where(rest == m2, lane, LANES), axis=1, keepdims=True)
    e2 = jnp.exp(m2 - m1)
    den = 1.0 + e2
    return jnp.where(lane == i1, 1.0 / den, 0.0) + jnp.where(lane == i2, e2 / den, 0.0)


def _moe_ffn_kernel(eid_ref, valid_ref, first_ref, x_ref, wgu_hbm, wd_hbm, o_ref,
                    cg_ref, cu_ref, cd_ref, sg_ref, su_ref, sd_ref, sem_ref, acc_ref, *, fc):
    i = pl.program_id(0)
    e = eid_ref[i]
    f = cd_ref.shape[0]
    n_chunks = f // fc

    def chunk_copies(c, slot):
        return (
            pltpu.make_async_copy(wgu_hbm.at[e, :, pl.ds(c * fc, fc)], sg_ref.at[slot], sem_ref.at[0, slot]),
            pltpu.make_async_copy(wgu_hbm.at[e, :, pl.ds(f + c * fc, fc)], su_ref.at[slot], sem_ref.at[1, slot]),
            pltpu.make_async_copy(wd_hbm.at[e, pl.ds(c * fc, fc), :], sd_ref.at[slot], sem_ref.at[2, slot]),
        )

    def compute(c):
        x = x_ref[...]
        gate = _dot(x, cg_ref[:, c * fc:(c + 1) * fc])
        up = _dot(x, cu_ref[:, c * fc:(c + 1) * fc])
        h = gate * jax.nn.sigmoid(gate) * up
        contrib = _dot(h.astype(BF16), cd_ref[c * fc:(c + 1) * fc, :])
        if c == 0:
            acc_ref[...] = contrib
        else:
            acc_ref[...] += contrib

    valid = valid_ref[i] != 0
    first = first_ref[i] != 0

    @pl.when(valid & first)
    def _():
        for cp in chunk_copies(0, 0):
            cp.start()
        for c in range(n_chunks):
            slot = c % 2
            if c + 1 < n_chunks:
                for cp in chunk_copies(c + 1, 1 - slot):
                    cp.start()
            for cp in chunk_copies(c, slot):
                cp.wait()
            cg_ref[:, c * fc:(c + 1) * fc] = sg_ref[slot].astype(BF16)
            cu_ref[:, c * fc:(c + 1) * fc] = su_ref[slot].astype(BF16)
            cd_ref[c * fc:(c + 1) * fc, :] = sd_ref[slot].astype(BF16)
            compute(c)
        o_ref[...] = acc_ref[...].astype(o_ref.dtype)

    @pl.when(valid & jnp.logical_not(first))
    def _():
        for c in range(n_chunks):
            compute(c)
        o_ref[...] = acc_ref[...].astype(o_ref.dtype)

    @pl.when(jnp.logical_not(valid))
    def _():
        o_ref[...] = jnp.zeros(o_ref.shape, o_ref.dtype)


def _moe_ffn(xs, w_gu, w_d, eids, valid, first, tm=MOE_TM, fc=MOE_FC):
    r = xs.shape[0]
    f = w_d.shape[1]
    return pl.pallas_call(
        functools.partial(_moe_ffn_kernel, fc=fc),
        grid_spec=pltpu.PrefetchScalarGridSpec(
            num_scalar_prefetch=3,
            grid=(r // tm,),
            in_specs=[
                pl.BlockSpec((tm, D_MODEL), lambda i, *_: (i, 0)),
                pl.BlockSpec(memory_space=pl.ANY),
                pl.BlockSpec(memory_space=pl.ANY),
            ],
            out_specs=pl.BlockSpec((tm, D_MODEL), lambda i, *_: (i, 0)),
            scratch_shapes=[
                pltpu.VMEM((D_MODEL, f), BF16),
                pltpu.VMEM((D_MODEL, f), BF16),
                pltpu.VMEM((f, D_MODEL), BF16),
                pltpu.VMEM((2, D_MODEL, fc), F32),
                pltpu.VMEM((2, D_MODEL, fc), F32),
                pltpu.VMEM((2, fc, D_MODEL), F32),
                pltpu.SemaphoreType.DMA((3, 2)),
                pltpu.VMEM((tm, D_MODEL), F32),
            ],
        ),
        out_shape=jax.ShapeDtypeStruct((r, D_MODEL), BF16),
        compiler_params=_cparams("arbitrary"),
        name="ffn_moe",
    )(eids, valid, first, xs, w_gu, w_d)


def _moe_gather_kernel(te_ref, clo_ref, nch_ref, s0_ref, ns_ref, x_ref, pos_ref, o_ref, acc_ref, *, tm, sr, n_chunks):
    r = pl.program_id(0)
    e = te_ref[r]
    sub_row = lax.broadcasted_iota(jnp.int32, (sr, tm), 0)
    acc_ref[...] = jnp.zeros(acc_ref.shape, F32)

    def body(i, carry):
        c = clo_ref[r] + i
        t0 = pl.multiple_of(c * tm, tm)
        p = pos_ref[pl.ds(e, 1), pl.ds(t0, tm)]

        def sub(j, carry2):
            r0 = pl.multiple_of((s0_ref[r * n_chunks + c] + j) * sr, sr)
            onehot = jnp.where(p == r * tm + r0 + sub_row, 1.0, 0.0).astype(BF16)
            acc_ref[pl.ds(r0, sr), :] += _dot(onehot, x_ref[pl.ds(t0, tm), :])
            return carry2

        lax.fori_loop(0, ns_ref[r * n_chunks + c], sub, 0)
        return carry

    lax.fori_loop(0, nch_ref[r], body, 0)
    o_ref[...] = acc_ref[...].astype(BF16)


def _moe_gather(xb, pos_t, tile_e, clo, nch, sub0, nsub, n_rows, tm=MOE_TM, sr=MOE_SR):
    t = xb.shape[0]
    return pl.pallas_call(
        functools.partial(_moe_gather_kernel, tm=tm, sr=sr, n_chunks=t // tm),
        grid_spec=pltpu.PrefetchScalarGridSpec(
            num_scalar_prefetch=5,
            grid=(n_rows // tm,),
            in_specs=[
                pl.BlockSpec((t, D_MODEL), lambda r, *_: (0, 0), pipeline_mode=pl.Buffered(1)),
                pl.BlockSpec(pos_t.shape, lambda r, *_: (0, 0)),
            ],
            out_specs=pl.BlockSpec((tm, D_MODEL), lambda r, *_: (r, 0)),
            scratch_shapes=[pltpu.VMEM((tm, D_MODEL), F32)],
        ),
        out_shape=jax.ShapeDtypeStruct((n_rows, D_MODEL), BF16),
        compiler_params=_cparams("arbitrary"),
        name="moe_gather",
    )(tile_e, clo, nch, sub0, nsub, xb, pos_t)


def _moe_combine_kernel(n_ref, cid_ref, eid_ref, ys_hbm, pos_ref, comb_ref, x_ref, *rest, tm, cs, n_slots, n_buf):
    *ple_refs, o_ref, buf_ref, sem_ref = rest
    i = pl.program_id(0)
    n = n_ref[i]
    base = i * n_slots

    def chunk_copy(k, slot):
        row0 = pl.multiple_of(cid_ref[base + k] * cs, cs)
        return pltpu.make_async_copy(ys_hbm.at[pl.ds(row0, cs), :], buf_ref.at[slot], sem_ref.at[slot])

    o_ref[...] = jnp.zeros(o_ref.shape, F32)

    for d in range(n_buf):

        @pl.when(d < n)
        def _():
            chunk_copy(d, d).start()

    def gathered(k, slot):
        e = eid_ref[base + k]
        lane = lax.broadcasted_iota(jnp.int32, pos_ref.shape, 1)
        pos_e = jnp.sum(jnp.where(lane == e, pos_ref[...], 0), axis=1, keepdims=True)
        lane_c = lax.broadcasted_iota(jnp.int32, comb_ref.shape, 1)
        g_e = jnp.sum(jnp.where(lane_c == e, comb_ref[...], 0.0), axis=1, keepdims=True)
        col = cid_ref[base + k] * cs + lax.broadcasted_iota(jnp.int32, (tm, cs), 1)
        onehot = jnp.where(pos_e == col, 1.0, 0.0).astype(BF16)
        return g_e * _dot(onehot, buf_ref[slot])

    def body(j, carry):
        k = 2 * j
        slot = k % n_buf
        chunk_copy(k, slot).wait()
        chunk_copy(k + 1, slot + 1).wait()
        o_ref[...] += gathered(k, slot) + gathered(k + 1, slot + 1)

        @pl.when(k + n_buf < n)
        def _():
            chunk_copy(k + n_buf, slot).start()
            chunk_copy(k + n_buf + 1, slot + 1).start()

        return carry

    lax.fori_loop(0, n // 2, body, 0)
    o_ref[...] = _ln_ple(x_ref[...], o_ref[...], ple_refs)


def _moe_combine(ys, pos, combine, n_need, cid, eid, x2d, ple_args, tm=MOE_TM, cs=MOE_CS):
    t = pos.shape[0]
    return pl.pallas_call(
        functools.partial(_moe_combine_kernel, tm=tm, cs=cs, n_slots=MOE_SLOTS, n_buf=MOE_RING),
        grid_spec=pltpu.PrefetchScalarGridSpec(
            num_scalar_prefetch=3,
            grid=(t // tm,),
            in_specs=[
                pl.BlockSpec(memory_space=pl.ANY),
                pl.BlockSpec((tm, N_EXPERTS), lambda i, *_: (i, 0)),
                pl.BlockSpec((tm, LANES), lambda i, *_: (i, 0)),
                pl.BlockSpec((tm, D_MODEL), lambda i, *_: (i, 0)),
            ] + _ple_specs(tm, ple_args),
            out_specs=pl.BlockSpec((tm, D_MODEL), lambda i, *_: (i, 0)),
            scratch_shapes=[pltpu.VMEM((MOE_RING, cs, D_MODEL), BF16), pltpu.SemaphoreType.DMA((MOE_RING,))],
        ),
        out_shape=jax.ShapeDtypeStruct((t, D_MODEL), F32),
        compiler_params=_cparams("arbitrary"),
        name="moe_combine",
    )(n_need, cid, eid, ys, pos, combine, x2d, *ple_args)


def _moe_plan(combine, tm=MOE_TM):
    t = combine.shape[0]
    n_tok_tiles = t // tm
    n_row_tiles = (2 * t) // tm + N_EXPERTS
    sel = (combine[:, :N_EXPERTS] != 0.0).astype(jnp.int32)
    csum = jnp.cumsum(sel, axis=0)
    count = csum[-1]
    ntile_e = (count + tm - 1) // tm
    tile_end = jnp.cumsum(ntile_e)
    gstart = (tile_end - ntile_e) * tm
    pos = jnp.where(sel != 0, gstart[None, :] + csum - sel, -1)
    rt = jnp.arange(n_row_tiles, dtype=jnp.int32)
    tile_e = jnp.minimum(jnp.sum(rt[:, None] >= tile_end[None, :], axis=1), N_EXPERTS - 1).astype(jnp.int32)
    valid = (rt < tile_end[-1]).astype(jnp.int32)
    prev_e = jnp.concatenate([jnp.full((1,), -1, jnp.int32), tile_e[:-1]])
    group_head = ((valid != 0) & (tile_e != prev_e)).astype(jnp.int32)
    cb = jnp.concatenate([jnp.zeros((1, N_EXPERTS), jnp.int32), csum[tm - 1::tm]], axis=0)
    off = rt * tm - gstart[tile_e]
    cb_t = cb[:, tile_e].T
    clo = jnp.sum(cb_t[:, 1:] <= off[:, None], axis=1).astype(jnp.int32)
    cend = jnp.sum(cb_t[:, :-1] < (off + tm)[:, None], axis=1).astype(jnp.int32)
    nch = jnp.where(valid != 0, jnp.maximum(cend - clo, 0), 0).astype(jnp.int32)
    clo = jnp.minimum(clo, n_tok_tiles - 1)
    r_lo = jnp.clip(cb_t[:, :-1] - off[:, None], 0, tm)
    r_hi = jnp.clip(cb_t[:, 1:] - off[:, None], 0, tm)
    sub0 = (r_lo // MOE_SR).astype(jnp.int32)
    nsub = jnp.where(r_hi > r_lo, (r_hi + MOE_SR - 1) // MOE_SR - sub0, 0).astype(jnp.int32)
    first = gstart[None, :] + cb[:-1]
    n_ie = cb[1:] - cb[:-1]
    per_e = MOE_SLOTS // N_EXPERTS
    k = jnp.arange(per_e, dtype=jnp.int32)
    c_lo = first // MOE_CS
    c_hi = (first + n_ie - 1) // MOE_CS
    cand = c_lo[..., None] + k
    need = ((n_ie[..., None] > 0) & (cand <= c_hi[..., None])).reshape(n_tok_tiles, MOE_SLOTS)
    cand = cand.reshape(n_tok_tiles, MOE_SLOTS)
    eids = jnp.broadcast_to(jnp.arange(N_EXPERTS, dtype=jnp.int32)[:, None], (N_EXPERTS, per_e)).reshape(-1)
    order = jnp.argsort(jnp.logical_not(need), axis=1, stable=True)
    cid = jnp.where(need, cand, 0)
    cid = jnp.take_along_axis(cid, order, axis=1).reshape(-1).astype(jnp.int32)
    need_sorted = jnp.take_along_axis(need, order, axis=1)
    eid = jnp.where(need_sorted, eids[order], -1).reshape(-1).astype(jnp.int32)
    n_need = jnp.sum(need, axis=1).astype(jnp.int32)
    n_need = n_need + n_need % 2
    return dict(pos=pos.astype(jnp.int32), tile_e=tile_e, valid=valid, first=group_head, clo=clo, nch=nch,
                sub0=sub0.reshape(-1), nsub=nsub.reshape(-1),
                n_need=n_need, cid=cid, eid=eid, n_rows=n_row_tiles * tm)


def _moe(xb, x2d, combine, w_gu, w_d, first_expert, ple_args):
    plan = _moe_plan(combine)
    xs = _moe_gather(xb, plan["pos"].T, plan["tile_e"], plan["clo"], plan["nch"], plan["sub0"], plan["nsub"],
                     plan["n_rows"])
    ys = _moe_ffn(xs, w_gu, w_d, plan["tile_e"] + first_expert, plan["valid"], plan["first"])
    return _moe_combine(ys, plan["pos"], combine, plan["n_need"], plan["cid"], plan["eid"], x2d, ple_args)


def _ln_ple(x, h, ple_refs):
    g_ref, b_ref, p_ref, wg_ref, wp_ref = ple_refs
    y = _layer_norm(DEEPNORM_ALPHA * x + h, g_ref[...], b_ref[...])
    gate = jax.nn.sigmoid(_dot(y.astype(BF16), wg_ref[...]))
    proj = _dot(p_ref[...].astype(BF16), wp_ref[...])
    return y + gate * proj


def _ple_specs(tm, ple_args):
    g, b, p2d, wg, wp = ple_args
    ign = lambda f: (lambda *idx: f(idx[0]))
    full = lambda a: pl.BlockSpec(a.shape, ign(lambda i: (0, 0)))
    return [full(g), full(b), pl.BlockSpec((tm, D_PLE), ign(lambda i: (i, 0))), full(wg), full(wp)]


def _ffn_ple_kernel(xb_ref, wgu_ref, wd_ref, x_ref, *rest, fc):
    *ple_refs, o_ref, acc_ref = rest
    f = wd_ref.shape[0]
    xb = xb_ref[...]
    for c in range(f // fc):
        gate = _dot(xb, wgu_ref[:, c * fc:(c + 1) * fc])
        up = _dot(xb, wgu_ref[:, f + c * fc:f + (c + 1) * fc])
        h = gate * jax.nn.sigmoid(gate) * up
        contrib = _dot(h.astype(BF16), wd_ref[c * fc:(c + 1) * fc, :])
        if c == 0:
            acc_ref[...] = contrib
        else:
            acc_ref[...] += contrib
    o_ref[...] = _ln_ple(x_ref[...], acc_ref[...], ple_refs)


def _ffn_ple(xb, x2d, w_gu, w_d, ple_args, tm=512, fc=1408):
    t = xb.shape[0]
    row = lambda n: pl.BlockSpec((tm, n), lambda i: (i, 0))
    resident = lambda a: pl.BlockSpec(a.shape, lambda i: (0, 0), pipeline_mode=pl.Buffered(1))
    return pl.pallas_call(
        functools.partial(_ffn_ple_kernel, fc=fc),
        grid=(t // tm,),
        in_specs=[row(D_MODEL), resident(w_gu), resident(w_d), row(D_MODEL)] + _ple_specs(tm, ple_args),
        out_specs=row(D_MODEL),
        out_shape=jax.ShapeDtypeStruct((t, D_MODEL), F32),
        scratch_shapes=[pltpu.VMEM((tm, D_MODEL), F32)],
        compiler_params=_cparams("parallel"),
        name="ffn_dense",
    )(xb, w_gu, w_d, x2d, *ple_args)


def _even_mixer(x2d, b, s, w_in, b_f, w_out):
    nf, ns = FOX_W, SB_W
    o_f = 3 * nf
    o_s = o_f + N_HEADS_FOX
    qa, ka, va = w_in[:, :nf], w_in[:, nf:2 * nf], w_in[:, 2 * nf:o_f]
    qs, ks, vs = w_in[:, o_s:o_s + ns], w_in[:, o_s + ns:o_s + 2 * ns], w_in[:, o_s + 2 * ns:]
    w_k = jnp.concatenate([ka, ks], axis=1).astype(BF16)
    w_t = jnp.concatenate([qa * Q_SCALE, va, qs * Q_SCALE, vs], axis=1).T.astype(BF16)
    w_f = jnp.pad(w_in[:, o_f:o_s], ((0, 0), (0, LANES - N_HEADS_FOX))).astype(BF16)
    bias_f = jnp.pad(b_f, (0, LANES - N_HEADS_FOX)).reshape(1, LANES)
    k2d, ht, lf = _ab_proj(x2d, w_k, w_t, w_f, bias_f, b, s)
    lf_t = lf.reshape(b, s, LANES)[:, :, :N_HEADS_FOX].transpose(0, 2, 1)
    crow, caug = _cumsum_seq(lf_t)
    k3 = k2d.reshape(b, s, -1)
    oa, ob = _even_attention(k3, ht, caug, crow)
    return [oa.reshape(b * s, nf), ob.reshape(b * s, ns)], w_out.astype(BF16)


def _odd_mixer(x2d, b, s, w_qkv, sinks, w_out, tables):
    nq = N_HEADS_SWA * HEAD_DIM
    wq = w_qkv[:, :nq]
    wk = [w_qkv[:, nq + i * HEAD_DIM: nq + (i + 1) * HEAD_DIM] for i in range(N_KV_SWA)]
    wv = [w_qkv[:, nq + (N_KV_SWA + i) * HEAD_DIM: nq + (N_KV_SWA + i + 1) * HEAD_DIM] for i in range(N_KV_SWA)]
    dup = lambda ws: [w for w in ws for _ in range(2)]
    w_k = jnp.concatenate(dup(wk), axis=1).astype(BF16)
    w_t = jnp.concatenate([wq * Q_SCALE] + dup(wv), axis=1).T.astype(BF16)
    k2d, qt, vt = _swa_proj(x2d, w_k, w_t, tables, b, s)
    o = _swa_attention(qt, k2d.reshape(b, s, -1), vt, sinks)
    return [o.reshape(b * s, nq)], w_out.astype(BF16)


def kernel(x, p, ln_mix_g, ln_mix_b, ln_ffn_g, ln_ffn_b, ab_w_in, ab_b_f, ab_w_out, c_w_qkv, c_sinks, c_w_out,
           ffn_w_gate_up, ffn_w_down, router_w, router_b, moe_w_gate_up, moe_w_down, ple_w_gate, ple_w_proj):
    b, s, d = x.shape
    t = b * s
    x2d = x.reshape(t, d)
    tables = _rope_tables(s)
    row = lambda v: v.reshape(1, -1)
    moe_gu = moe_w_gate_up.reshape((-1,) + moe_w_gate_up.shape[2:])
    moe_d = moe_w_down.reshape((-1,) + moe_w_down.shape[2:])
    for i in range(DEPTH):
        j = i // 2
        if i % 2 == 0:
            os_, ws = _even_mixer(x2d, b, s, ab_w_in[j], ab_b_f[j], ab_w_out[j])
        else:
            os_, ws = _odd_mixer(x2d, b, s, c_w_qkv[j], c_sinks[j], c_w_out[j], tables)
        ple_args = (row(ln_ffn_g[i]), row(ln_ffn_b[i]), p[i].reshape(t, D_PLE),
                    ple_w_gate[i].astype(BF16), ple_w_proj[i].astype(BF16))
        if i % 2 == 0:
            x2d, xb = _out_proj_ln(os_, ws, x2d, row(ln_mix_g[i]), row(ln_mix_b[i]))
            x2d = _ffn_ple(xb, x2d, ffn_w_gate_up[j].astype(BF16), ffn_w_down[j].astype(BF16), ple_args)
        else:
            rw = jnp.pad(router_w[j], ((0, 0), (0, LANES - N_EXPERTS)))
            rb = jnp.pad(router_b[j], (0, LANES - N_EXPERTS)).reshape(1, LANES)
            x2d, xb, combine = _out_proj_ln(os_, ws, x2d, row(ln_mix_g[i]), row(ln_mix_b[i]), router=(rw, rb))
            x2d = _moe(xb, x2d, combine, moe_gu, moe_d, j * N_EXPERTS, ple_args)
    return x2d.reshape(b, s, d)
```

```python
import functools
import math

import jax
import jax.numpy as jnp
from jax import lax
from jax.experimental import pallas as pl
from jax.experimental.pallas import tpu as pltpu

F32 = jnp.float32
BF16 = jnp.bfloat16

D_MODEL = 1024
HEAD_DIM = 64
LANES = 128
N_HEADS_FOX = 8
N_HEADS_SB = 8
N_HEADS_SWA = 16
N_KV_SWA = 2
WINDOW = 128
ROPE_THETA = 500000.0
ROPE_DIM = HEAD_DIM // 4
N_EXPERTS = 8
D_PLE = 256
LN_EPS = 1e-5
DEPTH = 4
DEEPNORM_ALPHA = (2.0 * DEPTH) ** 0.25
ATTN_SCALE = HEAD_DIM ** -0.5
FOX_W = N_HEADS_FOX * HEAD_DIM
SB_W = N_HEADS_SB * HEAD_DIM
VMEM_LIMIT = 56 * 1024 * 1024

NEG_INF = float("-inf")
LOG2E = math.log2(math.e)
Q_SCALE = ATTN_SCALE * LOG2E
SUM_ROWS = 16
SWA_HEAD_BATCH = 4
MOE_TM = 512
MOE_SR = 128
MOE_FC = 512
MOE_CS = 256
MOE_SLOTS = N_EXPERTS * (MOE_TM // MOE_CS + 1)
MOE_RING = 8


def _cparams(*sem):
    return pltpu.CompilerParams(dimension_semantics=sem, vmem_limit_bytes=VMEM_LIMIT)


def _dot(a, b):
    return jnp.dot(a, b, preferred_element_type=F32)


def _dot_nt(a, b):
    return lax.dot_general(a, b, (((1,), (1,)), ((), ())), preferred_element_type=F32)


def _log_sigmoid(x):
    return jnp.minimum(x, 0.0) - jnp.log1p(jnp.exp(-jnp.abs(x)))


def _ab_proj_kernel(x_ref, wk_ref, wt_ref, wf_ref, bf_ref, k_ref, ht_ref, lf_ref):
    xb = x_ref[...].astype(BF16)
    k_ref[...] = _dot(xb, wk_ref[...]).astype(BF16)
    ht_ref[0] = _dot_nt(wt_ref[...], xb).astype(BF16)
    lf_ref[...] = _log_sigmoid(_dot(xb, wf_ref[...]) + bf_ref[...])


def _ab_proj(x2d, wk, wt, wf, bf, b, s, tm=512):
    t = x2d.shape[0]
    nk = wk.shape[1]
    nt = wt.shape[0]
    spt = s // tm
    return pl.pallas_call(
        _ab_proj_kernel,
        grid=(t // tm,),
        in_specs=[
            pl.BlockSpec((tm, D_MODEL), lambda i: (i, 0)),
            pl.BlockSpec((D_MODEL, nk), lambda i: (0, 0)),
            pl.BlockSpec((nt, D_MODEL), lambda i: (0, 0)),
            pl.BlockSpec((D_MODEL, LANES), lambda i: (0, 0)),
            pl.BlockSpec((1, LANES), lambda i: (0, 0)),
        ],
        out_specs=[
            pl.BlockSpec((tm, nk), lambda i: (i, 0)),
            pl.BlockSpec((1, nt, tm), lambda i: (i // spt, 0, i % spt)),
            pl.BlockSpec((tm, LANES), lambda i: (i, 0)),
        ],
        out_shape=[
            jax.ShapeDtypeStruct((t, nk), BF16),
            jax.ShapeDtypeStruct((b, nt, s), BF16),
            jax.ShapeDtypeStruct((t, LANES), F32),
        ],
        compiler_params=_cparams("parallel"),
        name="ab_proj",
    )(x2d, wk, wt, wf, bf)


def _split3(v):
    hi = v.astype(BF16).astype(F32)
    r1 = v - hi
    mid = r1.astype(BF16).astype(F32)
    return hi, mid, r1 - mid


def _cumsum_kernel(x_ref, c_ref, aug_ref, st_ref):
    x = x_ref[0]
    s = x.shape[1]
    lane = lax.broadcasted_iota(jnp.int32, x.shape, 1) & (LANES - 1)
    sh = 1
    while sh < LANES:
        x = x + jnp.where(lane >= sh, pltpu.roll(x, sh, axis=1), 0.0)
        sh *= 2
    carry = jnp.zeros((x.shape[0], 1), F32)
    for c in range(s // LANES):
        blk = x[:, c * LANES:(c + 1) * LANES] + carry
        c_ref[0, :, c * LANES:(c + 1) * LANES] = blk
        carry = blk[:, LANES - 1:LANES]

    ones = jnp.ones((3, s), F32)
    st_ref[...] = jnp.zeros(st_ref.shape, F32)
    for p in range(x.shape[0] // 2):
        for which, base in ((0, HEAD_DIM), (1, 0)):
            negc = c_ref[0, 2 * p + which:2 * p + which + 1, :] * -LOG2E
            hi, mid, lo = _split3(negc)
            st_ref[base:base + 1, :] = hi
            st_ref[base + 1:base + 2, :] = mid
            st_ref[base + 2:base + 3, :] = lo
            st_ref[base + 3:base + 6, :] = ones
        for c in range(s // LANES):
            blk = st_ref[:, c * LANES:(c + 1) * LANES]
            aug_ref[0, p, c * LANES:(c + 1) * LANES, :] = blk.T.astype(BF16)


def _cumsum_seq(lf_t):
    b, h, s = lf_t.shape
    return pl.pallas_call(
        _cumsum_kernel,
        grid=(b,),
        in_specs=[pl.BlockSpec((1, h, s), lambda i: (i, 0, 0))],
        out_specs=[
            pl.BlockSpec((1, h, s), lambda i: (i, 0, 0)),
            pl.BlockSpec((1, h // 2, s, LANES), lambda i: (i, 0, 0, 0)),
        ],
        out_shape=[
            jax.ShapeDtypeStruct((b, h, s), F32),
            jax.ShapeDtypeStruct((b, h // 2, s, LANES), BF16),
        ],
        scratch_shapes=[pltpu.VMEM((LANES, s), F32)],
        compiler_params=_cparams("parallel"),
        name="fox_cumsum",
    )(lf_t)


def _fox_stages(qt_ref, k_ref, aug_ref, vt_ref, c_ref, o_ref, m_ref, acc_ref, sa_ref, sb_ref, *, tq, tk):
    hp = pl.program_id(1)
    qi = pl.program_id(2)
    q0 = qi * tq
    qt = qt_ref[0].astype(F32)
    sub = lax.broadcasted_iota(jnp.int32, (LANES, tq), 0)
    lane = lax.broadcasted_iota(jnp.int32, (tk, LANES), 1)
    vsub = lax.broadcasted_iota(jnp.int32, (LANES, tk), 0)
    own_rows = [sub < HEAD_DIM, sub >= HEAD_DIM]
    own_lanes = [lane < HEAD_DIM, lane >= HEAD_DIM]
    own_vrows = [vsub < HEAD_DIM, vsub >= HEAD_DIM]

    qaug = []
    for r in range(2):
        base = HEAD_DIM * (1 - r)
        ci = c_ref[0, pl.ds(2 * hp + r, 1), :]
        hi, mid, lo = _split3(ci * LOG2E)
        t = jnp.where(own_rows[r], qt, 0.0)
        t = jnp.where((sub >= base) & (sub < base + 3), 1.0, t)
        t = jnp.where(sub == base + 3, hi, t)
        t = jnp.where(sub == base + 4, mid, t)
        t = jnp.where(sub == base + 5, lo, t)
        qaug.append(t.astype(BF16))

    m_ref[...] = jnp.full(m_ref.shape, NEG_INF, F32)
    acc_ref[...] = jnp.zeros(acc_ref.shape, F32)

    def scores(k0, s_ref, lo=0):
        kt = k_ref[0, pl.ds(k0, tk), :]
        ca = aug_ref[0, 0, pl.ds(k0, tk), :]
        for r in range(2):
            ka = jnp.where(own_lanes[r], kt, ca)
            s_ref[r, :, lo:] = _dot(ka, qaug[r][:, lo:])

    def softmax_pv(k0, s_ref, masked, lo=0):
        vt = vt_ref[0, :, pl.ds(k0, tk)]
        if masked:
            key = k0 + lax.broadcasted_iota(jnp.int32, (tk, tq - lo), 0)
            qry = q0 + lo + lax.broadcasted_iota(jnp.int32, (tk, tq - lo), 1)
            keep = key <= qry
        for r in range(2):
            st = s_ref[r, :, lo:]
            if masked:
                st = jnp.where(keep, st, NEG_INF)
            m_prev = m_ref[r, :, lo:]
            m_new = jnp.maximum(m_prev, jnp.max(st, axis=0, keepdims=True))
            alpha = jnp.exp2(m_prev - m_new)
            pt = jnp.exp2(st - m_new).astype(BF16)
            va = jnp.where(own_vrows[r], vt, 1.0)
            acc_ref[r, :, lo:] = alpha * acc_ref[r, :, lo:] + _dot(va, pt)
            m_ref[r, :, lo:] = m_new

    tile = lambda j: pl.multiple_of(j * tk, tk)

    def start():
        scores(tile(0), sa_ref)

    def half_a(i):
        scores(tile(2 * i + 1), sb_ref)
        softmax_pv(tile(2 * i), sa_ref, False)

    def half_b(i):
        scores(tile(2 * i + 2), sa_ref)
        softmax_pv(tile(2 * i + 1), sb_ref, False)

    def finish():
        scores(tile(2 * qi + 1), sb_ref, lo=tk)
        softmax_pv(tile(2 * qi), sa_ref, True)
        softmax_pv(tile(2 * qi + 1), sb_ref, True, lo=tk)
        ot = jnp.zeros((LANES, tq), F32)
        for r in range(2):
            a = acc_ref[r]
            den = a[HEAD_DIM * (1 - r):HEAD_DIM * (1 - r) + 1, :]
            ot = ot + jnp.where(own_rows[r], a / den, 0.0)
        o_ref[0] = ot.T.astype(BF16)

    return start, half_a, half_b, finish


def _sb_stages(qt_ref, k_ref, vt_ref, o_ref, run_ref, acc_ref, za_ref, zb_ref, *, tq, cw):
    qi = pl.program_id(2)
    q0 = qi * tq
    qt = qt_ref[0]
    sub = lax.broadcasted_iota(jnp.int32, (LANES, tq), 0)
    own_rows = [sub < HEAD_DIM, sub >= HEAD_DIM]
    qm = [jnp.where(mk, qt, 0.0) for mk in own_rows]

    ss = lax.broadcasted_iota(jnp.int32, (cw, cw), 0)
    jj = lax.broadcasted_iota(jnp.int32, (cw, cw), 1)
    tri = jnp.where(jj > ss, 1.0, 0.0).astype(BF16)
    tri = jnp.concatenate([tri, jnp.ones((SUM_ROWS, cw), BF16)], axis=0)

    run_ref[...] = jnp.zeros(run_ref.shape, F32)
    acc_ref[...] = jnp.zeros(acc_ref.shape, F32)

    def start_of(c):
        return pl.multiple_of(jnp.maximum(q0 + tq - (c + 1) * cw, 0), cw)

    def logits(c, z_ref, lo=0):
        kt = k_ref[0, pl.ds(start_of(c), cw), :]
        for r in range(2):
            z_ref[r, :, lo:] = _dot(kt, qm[r][:, lo:])

    def weights_pv(c, z_ref, masked, lo=0):
        k0 = start_of(c)
        vt = vt_ref[0, :, pl.ds(k0, cw)]
        if masked:
            key = k0 + lax.broadcasted_iota(jnp.int32, (cw, tq - lo), 0)
            qry = q0 + lo + lax.broadcasted_iota(jnp.int32, (cw, tq - lo), 1)
            keep = key < qry
        zs, sps, laters, ws = [], [], [], []
        for r in range(2):
            z = z_ref[r, :, lo:]
            e = jnp.exp2(-jnp.abs(z))
            sp = jnp.maximum(z, 0.0) + jnp.log(1.0 + e) * LOG2E
            if masked:
                sp = jnp.where(keep, sp, 0.0)
            zs.append(z)
            sps.append(sp)
        for r in range(2):
            laters.append(_dot(tri, sps[r].astype(BF16)))
        for r in range(2):
            run = run_ref[r, :, lo:]
            a = jnp.exp2((zs[r] - sps[r]) - (laters[r][:cw] + run))
            if masked:
                a = jnp.where(keep, a, 0.0)
            ws.append(a.astype(BF16))
            run_ref[r, :, lo:] = run + laters[r][cw:cw + 1]
        for r in range(2):
            acc_ref[r, :, lo:] = acc_ref[r, :, lo:] + _dot(vt, ws[r])

    def start():
        logits(0, za_ref, lo=cw)
        logits(1, zb_ref)
        weights_pv(0, za_ref, True, lo=cw)
        logits(2, za_ref)
        weights_pv(1, zb_ref, True)

    def half_a(i):
        logits(2 * i + 1, zb_ref)
        weights_pv(2 * i, za_ref, False)

    def half_b(i):
        logits(2 * i + 2, za_ref)
        weights_pv(2 * i + 1, zb_ref, False)

    def finish():
        ot = jnp.where(own_rows[0], acc_ref[0], acc_ref[1])
        o_ref[0] = ot.T.astype(BF16)

    return start, half_a, half_b, finish


def _even_attn_kernel(fq_ref, fk_ref, aug_ref, fv_ref, c_ref, sq_ref, sk_ref, sv_ref, of_ref, os_ref,
                      m_ref, facc_ref, sa_ref, sb_ref, run_ref, sacc_ref, za_ref, zb_ref, *, tq, tk):
    qi = pl.program_id(2)
    f_start, f_a, f_b, f_finish = _fox_stages(fq_ref, fk_ref, aug_ref, fv_ref, c_ref, of_ref,
                                              m_ref, facc_ref, sa_ref, sb_ref, tq=tq, tk=tk)
    s_start, s_a, s_b, s_finish = _sb_stages(sq_ref, sk_ref, sv_ref, os_ref,
                                             run_ref, sacc_ref, za_ref, zb_ref, tq=tq, cw=tk)
    f_start()
    s_start()

    def body(i, carry):
        f_a(i)
        s_a(i + 1)
        f_b(i)
        s_b(i + 1)
        return carry

    lax.fori_loop(0, qi, body, 0)
    f_finish()
    s_finish()


def _even_attention(k3, ht, caug, crow, tq=512):
    b, s, _ = k3.shape
    nhp = N_HEADS_FOX // 2
    tk = tq // 2
    qt_spec = lambda blk: pl.BlockSpec((1, LANES, tq), lambda bi, hp, qi: (bi, blk * nhp + hp, qi))
    k_spec = lambda blk: pl.BlockSpec((1, s, LANES), lambda bi, hp, qi: (bi, 0, blk * nhp + hp))
    vt_spec = lambda blk: pl.BlockSpec((1, LANES, s), lambda bi, hp, qi: (bi, blk * nhp + hp, 0))
    out_spec = pl.BlockSpec((1, tq, LANES), lambda bi, hp, qi: (bi, qi, hp))
    return pl.pallas_call(
        functools.partial(_even_attn_kernel, tq=tq, tk=tk),
        grid=(b, nhp, s // tq),
        in_specs=[
            qt_spec(0), k_spec(0),
            pl.BlockSpec((1, 1, s, LANES), lambda bi, hp, qi: (bi, hp, 0, 0)),
            vt_spec(1),
            pl.BlockSpec((1, N_HEADS_FOX, tq), lambda bi, hp, qi: (bi, 0, qi)),
            qt_spec(2), k_spec(1), vt_spec(3),
        ],
        out_specs=[out_spec, out_spec],
        out_shape=[jax.ShapeDtypeStruct((b, s, FOX_W), BF16), jax.ShapeDtypeStruct((b, s, SB_W), BF16)],
        scratch_shapes=[
            pltpu.VMEM((2, 1, tq), F32), pltpu.VMEM((2, LANES, tq), F32),
            pltpu.VMEM((2, tk, tq), F32), pltpu.VMEM((2, tk, tq), F32),
            pltpu.VMEM((2, 1, tq), F32), pltpu.VMEM((2, LANES, tq), F32),
            pltpu.VMEM((2, tk, tq), F32), pltpu.VMEM((2, tk, tq), F32),
        ],
        compiler_params=_cparams("parallel", "parallel", "arbitrary"),
        name="even_attn",
    )(ht, k3, caug, ht, crow, ht, k3, ht)


def _swa_proj_kernel(x_ref, wk_ref, wt_ref, cos_ref, sa_ref, sb_ref, cost_ref, sint_ref, k_ref, qt_ref, vt_ref):
    xb = x_ref[...].astype(BF16)
    half = ROPE_DIM // 2
    yk = _dot(xb, wk_ref[...])
    for c in range(yk.shape[1] // LANES):
        blk = yk[:, c * LANES:(c + 1) * LANES]
        blk = (blk * cos_ref[...] + pltpu.roll(blk, half, axis=1) * sa_ref[...]
               + pltpu.roll(blk, LANES - half, axis=1) * sb_ref[...])
        k_ref[:, c * LANES:(c + 1) * LANES] = blk.astype(BF16)
    yt = _dot_nt(wt_ref[...], xb)
    cos = cost_ref[...]
    sin = sint_ref[...]
    nq = N_HEADS_SWA * HEAD_DIM
    for h in range(N_HEADS_SWA):
        base = h * HEAD_DIM
        x1 = yt[base:base + half]
        x2 = yt[base + half:base + ROPE_DIM]
        rot = jnp.concatenate([x1 * cos - x2 * sin, x2 * cos + x1 * sin], axis=0)
        qt_ref[0, base:base + ROPE_DIM, :] = rot.astype(BF16)
        qt_ref[0, base + ROPE_DIM:base + HEAD_DIM, :] = yt[base + ROPE_DIM:base + HEAD_DIM].astype(BF16)
    vt_ref[0] = yt[nq:].astype(BF16)


def _rope_tables(s):
    half = ROPE_DIM // 2
    inv = ROPE_THETA ** (-jnp.arange(half, dtype=F32) * 2.0 / ROPE_DIM)
    ang = jnp.arange(s, dtype=F32)[:, None] * inv[None, :]
    cos, sin = jnp.cos(ang), jnp.sin(ang)
    d = jnp.arange(LANES) % HEAD_DIM
    idx = d % half
    cos_t = jnp.where(d[None, :] < ROPE_DIM, cos[:, idx], 1.0)
    sin_a = jnp.where((d[None, :] >= half) & (d[None, :] < ROPE_DIM), sin[:, idx], 0.0)
    sin_b = jnp.where(d[None, :] < half, -sin[:, idx], 0.0)
    return cos_t.astype(F32), sin_a.astype(F32), sin_b.astype(F32), cos.T.astype(F32), sin.T.astype(F32)


def _swa_proj(x2d, wk, wt, tables, b, s, tm=512):
    t = x2d.shape[0]
    nk = wk.shape[1]
    nq = N_HEADS_SWA * HEAD_DIM
    nv = wt.shape[0] - nq
    spt = s // tm
    half = ROPE_DIM // 2
    lane_tab = pl.BlockSpec((tm, LANES), lambda i: (i % spt, 0))
    row_tab = pl.BlockSpec((half, tm), lambda i: (0, i % spt))
    return pl.pallas_call(
        _swa_proj_kernel,
        grid=(t // tm,),
        in_specs=[
            pl.BlockSpec((tm, D_MODEL), lambda i: (i, 0)),
            pl.BlockSpec(wk.shape, lambda i: (0, 0)),
            pl.BlockSpec(wt.shape, lambda i: (0, 0)),
            lane_tab, lane_tab, lane_tab, row_tab, row_tab,
        ],
        out_specs=[
            pl.BlockSpec((tm, nk), lambda i: (i, 0)),
            pl.BlockSpec((1, nq, tm), lambda i: (i // spt, 0, i % spt)),
            pl.BlockSpec((1, nv, tm), lambda i: (i // spt, 0, i % spt)),
        ],
        out_shape=[
            jax.ShapeDtypeStruct((t, nk), BF16),
            jax.ShapeDtypeStruct((b, nq, s), BF16),
            jax.ShapeDtypeStruct((b, nv, s), BF16),
        ],
        compiler_params=_cparams("parallel"),
        name="swa_proj",
    )(x2d, wk, wt, *tables)


def _swa_kernel(sink_ref, qt_ref, k_ref, vt_ref, o_ref, *, tq):
    qi = pl.program_id(1)
    q0 = qi * tq
    kw = tq + WINDOW
    kstart = pl.multiple_of(jnp.maximum(q0 - WINDOW, 0), WINDOW)
    kwin = k_ref[0, pl.ds(kstart, kw), :]
    vwin = vt_ref[0, :, pl.ds(kstart, kw)]
    key = kstart + lax.broadcasted_iota(jnp.int32, (kw, tq), 0)
    qry = q0 + lax.broadcasted_iota(jnp.int32, (kw, tq), 1)
    diff = qry - key
    keep = (diff >= 0) & (diff < WINDOW)
    sub = lax.broadcasted_iota(jnp.int32, (LANES, tq), 0)
    vsub = lax.broadcasted_iota(jnp.int32, (LANES, kw), 0)
    own_rows = [sub < HEAD_DIM, sub >= HEAD_DIM]
    own_vrows = [vsub < HEAD_DIM, vsub >= HEAD_DIM]
    group = N_HEADS_SWA // N_KV_SWA

    def scores(h):
        c, r = divmod(h, 2)
        g = h // group
        qblk = qt_ref[0, c * LANES:(c + 1) * LANES, :]
        return _dot(kwin[:, g * LANES:(g + 1) * LANES], jnp.where(own_rows[r], qblk, 0.0))

    def weights(h, st):
        st = jnp.where(keep, st, NEG_INF)
        sink = sink_ref[h] * LOG2E
        mx = jnp.maximum(jnp.max(st, axis=0, keepdims=True), sink)
        return jnp.exp2(st - mx).astype(BF16), jnp.exp2(sink - mx)

    def values(h, pt):
        r = h % 2
        g = h // group
        va = jnp.where(own_vrows[r], vwin[g * LANES:(g + 1) * LANES, :], 1.0)
        return _dot(va, pt)

    batches = [list(range(i, i + SWA_HEAD_BATCH)) for i in range(0, N_HEADS_SWA, SWA_HEAD_BATCH)]
    sts = [scores(h) for h in batches[0]]
    for bi, heads in enumerate(batches):
        nxt = [scores(h) for h in batches[bi + 1]] if bi + 1 < len(batches) else None
        pts = [weights(h, st) for h, st in zip(heads, sts)]
        accs = [values(h, pt) for h, (pt, _) in zip(heads, pts)]
        outs = []
        for h, acc, (_, esink) in zip(heads, accs, pts):
            r = h % 2
            den = acc[HEAD_DIM * (1 - r):HEAD_DIM * (1 - r) + 1, :] + esink
            outs.append(acc / den)
            if r == 1:
                c = h // 2
                ot = jnp.where(own_rows[0], outs[-2], outs[-1])
                o_ref[0, :, c * LANES:(c + 1) * LANES] = ot.T.astype(BF16)
        sts = nxt


def _swa_attention(qt3, k3, vt3, sinks, tq=256):
    b, nq, s = qt3.shape
    kern = functools.partial(_swa_kernel, tq=tq)
    return pl.pallas_call(
        kern,
        grid=(b, s // tq),
        in_specs=[
            pl.BlockSpec(memory_space=pltpu.SMEM),
            pl.BlockSpec((1, nq, tq), lambda bi, qi: (bi, 0, qi)),
            pl.BlockSpec((1, s, k3.shape[2]), lambda bi, qi: (bi, 0, 0)),
            pl.BlockSpec((1, vt3.shape[1], s), lambda bi, qi: (bi, 0, 0)),
        ],
        out_specs=pl.BlockSpec((1, tq, nq), lambda bi, qi: (bi, qi, 0)),
        out_shape=jax.ShapeDtypeStruct((b, s, nq), BF16),
        compiler_params=_cparams("parallel", "arbitrary"),
        name="swa_attn",
    )(sinks, qt3, k3, vt3)


def _layer_norm(z, g, b):
    mu = jnp.mean(z, axis=-1, keepdims=True)
    zc = z - mu
    var = jnp.mean(zc * zc, axis=-1, keepdims=True)
    return zc * lax.rsqrt(var + LN_EPS) * g + b


def _outln_kernel(*refs, n_in, routed):
    o_refs = refs[:n_in]
    w_ref, x_ref, g_ref, b_ref = refs[n_in:n_in + 4]
    rest = refs[n_in + 4:]
    h = None
    row0 = 0
    for o_ref in o_refs:
        n = o_ref.shape[1]
        part = _dot(o_ref[...], w_ref[row0:row0 + n, :])
        h = part if h is None else h + part
        row0 += n
    y = _layer_norm(DEEPNORM_ALPHA * x_ref[...] + h, g_ref[...], b_ref[...])
    if routed:
        rw_ref, rb_ref, y_ref, yb_ref, c_ref = rest
        c_ref[...] = _route(y, rw_ref[...], rb_ref[...])
    else:
        y_ref, yb_ref = rest
    y_ref[...] = y
    yb_ref[...] = y.astype(BF16)


def _out_proj_ln(os_, w, x2d, g, b, router=None, tm=512):
    t = x2d.shape[0]
    n_in = len(os_)
    row = lambda n: pl.BlockSpec((tm, n), lambda i: (i, 0))
    full = lambda a: pl.BlockSpec(a.shape, lambda i: (0, 0))
    in_specs = [row(o.shape[1]) for o in os_] + [full(w), row(D_MODEL), full(g), full(b)]
    out_specs = [row(D_MODEL)] * 2
    out_shape = [jax.ShapeDtypeStruct((t, D_MODEL), F32), jax.ShapeDtypeStruct((t, D_MODEL), BF16)]
    args = [*os_, w, x2d, g, b]
    if router is not None:
        in_specs += [full(a) for a in router]
        out_specs.append(row(LANES))
        out_shape.append(jax.ShapeDtypeStruct((t, LANES), F32))
        args += list(router)
    return pl.pallas_call(
        functools.partial(_outln_kernel, n_in=n_in, routed=router is not None),
        grid=(t // tm,),
        in_specs=in_specs,
        out_specs=out_specs,
        out_shape=out_shape,
        compiler_params=_cparams("parallel"),
        name="out_proj_ln",
    )(*args)


def _route(x, w, b):
    xh = x.astype(BF16)
    xl = (x - xh.astype(F32)).astype(BF16)
    wh = w.astype(BF16)
    wl = (w - wh.astype(F32)).astype(BF16)
    logits = _dot(xh, wh) + (_dot(xl, wh) + _dot(xh, wl)) + b
    lane = lax.broadcasted_iota(jnp.int32, logits.shape, 1)
    logits = jnp.where(lane < N_EXPERTS, logits, NEG_INF)
    m1 = jnp.max(logits, axis=1, keepdims=True)
    i1 = jnp.min(jnp.where(logits == m1, lane, LANES), axis=1, keepdims=True)
    rest = jnp.where(lane == i1, NEG_INF, logits)
    m2 = jnp.max(rest, axis=1, keepdims=True)
    i2 = jnp.min(jnp.where(rest == m2, lane, LANES), axis=1, keepdims=True)
    e2 = jnp.exp(m2 - m1)
    den = 1.0 + e2
    return jnp.where(lane == i1, 1.0 / den, 0.0) + jnp.where(lane == i2, e2 / den, 0.0)


def _moe_ffn_kernel(eid_ref, valid_ref, first_ref, x_ref, wgu_hbm, wd_hbm, o_ref,
                    cg_ref, cu_ref, cd_ref, sg_ref, su_ref, sd_ref, sem_ref, acc_ref, *, fc):
    i = pl.program_id(0)
    e = eid_ref[i]
    f = cd_ref.shape[0]
    n_chunks = f // fc

    def chunk_copies(c, slot):
        return (
            pltpu.make_async_copy(wgu_hbm.at[e, :, pl.ds(c * fc, fc)], sg_ref.at[slot], sem_ref.at[0, slot]),
            pltpu.make_async_copy(wgu_hbm.at[e, :, pl.ds(f + c * fc, fc)], su_ref.at[slot], sem_ref.at[1, slot]),
            pltpu.make_async_copy(wd_hbm.at[e, pl.ds(c * fc, fc), :], sd_ref.at[slot], sem_ref.at[2, slot]),
        )

    def compute(c):
        x = x_ref[...]
        gate = _dot(x, cg_ref[:, c * fc:(c + 1) * fc])
        up = _dot(x, cu_ref[:, c * fc:(c + 1) * fc])
        h = gate * jax.nn.sigmoid(gate) * up
        contrib = _dot(h.astype(BF16), cd_ref[c * fc:(c + 1) * fc, :])
        if c == 0:
            acc_ref[...] = contrib
        else:
            acc_ref[...] += contrib

    valid = valid_ref[i] != 0
    first = first_ref[i] != 0

    @pl.when(valid & first)
    def _():
        for c in range(min(2, n_chunks)):
            for cp in chunk_copies(c, c):
                cp.start()
        for c in range(n_chunks):
            slot = c % 2
            for cp in chunk_copies(c, slot):
                cp.wait()
            cg_ref[:, c * fc:(c + 1) * fc] = sg_ref[slot].astype(BF16)
            cu_ref[:, c * fc:(c + 1) * fc] = su_ref[slot].astype(BF16)
            cd_ref[c * fc:(c + 1) * fc, :] = sd_ref[slot].astype(BF16)
            if c + 2 < n_chunks:
                for cp in chunk_copies(c + 2, slot):
                    cp.start()
            compute(c)
        o_ref[...] = acc_ref[...].astype(o_ref.dtype)

    @pl.when(valid & jnp.logical_not(first))
    def _():
        for c in range(n_chunks):
            compute(c)
        o_ref[...] = acc_ref[...].astype(o_ref.dtype)

    @pl.when(jnp.logical_not(valid))
    def _():
        o_ref[...] = jnp.zeros(o_ref.shape, o_ref.dtype)


def _moe_ffn(xs, w_gu, w_d, eids, valid, first, tm=MOE_TM, fc=MOE_FC):
    r = xs.shape[0]
    f = w_d.shape[1]
    return pl.pallas_call(
        functools.partial(_moe_ffn_kernel, fc=fc),
        grid_spec=pltpu.PrefetchScalarGridSpec(
            num_scalar_prefetch=3,
            grid=(r // tm,),
            in_specs=[
                pl.BlockSpec((tm, D_MODEL), lambda i, *_: (i, 0)),
                pl.BlockSpec(memory_space=pl.ANY),
                pl.BlockSpec(memory_space=pl.ANY),
            ],
            out_specs=pl.BlockSpec((tm, D_MODEL), lambda i, *_: (i, 0)),
            scratch_shapes=[
                pltpu.VMEM((D_MODEL, f), BF16),
                pltpu.VMEM((D_MODEL, f), BF16),
                pltpu.VMEM((f, D_MODEL), BF16),
                pltpu.VMEM((2, D_MODEL, fc), F32),
                pltpu.VMEM((2, D_MODEL, fc), F32),
                pltpu.VMEM((2, fc, D_MODEL), F32),
                pltpu.SemaphoreType.DMA((3, 2)),
                pltpu.VMEM((tm, D_MODEL), F32),
            ],
        ),
        out_shape=jax.ShapeDtypeStruct((r, D_MODEL), BF16),
        compiler_params=_cparams("arbitrary"),
        name="ffn_moe",
    )(eids, valid, first, xs, w_gu, w_d)


def _moe_gather_kernel(te_ref, clo_ref, nch_ref, s0_ref, ns_ref, x_ref, pos_ref, o_ref, acc_ref, *, tm, sr, n_chunks):
    r = pl.program_id(0)
    e = te_ref[r]
    sub_row = lax.broadcasted_iota(jnp.int32, (sr, tm), 0)
    acc_ref[...] = jnp.zeros(acc_ref.shape, F32)

    def body(i, carry):
        c = clo_ref[r] + i
        t0 = pl.multiple_of(c * tm, tm)
        p = pos_ref[pl.ds(e, 1), pl.ds(t0, tm)]

        def sub(j, carry2):
            r0 = pl.multiple_of((s0_ref[r * n_chunks + c] + j) * sr, sr)
            onehot = jnp.where(p == r * tm + r0 + sub_row, 1.0, 0.0).astype(BF16)
            acc_ref[pl.ds(r0, sr), :] += _dot(onehot, x_ref[pl.ds(t0, tm), :])
            return carry2

        lax.fori_loop(0, ns_ref[r * n_chunks + c], sub, 0)
        return carry

    lax.fori_loop(0, nch_ref[r], body, 0)
    o_ref[...] = acc_ref[...].astype(BF16)


def _moe_gather(xb, pos_t, tile_e, clo, nch, sub0, nsub, n_rows, tm=MOE_TM, sr=MOE_SR):
    t = xb.shape[0]
    return pl.pallas_call(
        functools.partial(_moe_gather_kernel, tm=tm, sr=sr, n_chunks=t // tm),
        grid_spec=pltpu.PrefetchScalarGridSpec(
            num_scalar_prefetch=5,
            grid=(n_rows // tm,),
            in_specs=[
                pl.BlockSpec((t, D_MODEL), lambda r, *_: (0, 0), pipeline_mode=pl.Buffered(1)),
                pl.BlockSpec(pos_t.shape, lambda r, *_: (0, 0)),
            ],
            out_specs=pl.BlockSpec((tm, D_MODEL), lambda r, *_: (r, 0)),
            scratch_shapes=[pltpu.VMEM((tm, D_MODEL), F32)],
        ),
        out_shape=jax.ShapeDtypeStruct((n_rows, D_MODEL), BF16),
        compiler_params=_cparams("arbitrary"),
        name="moe_gather",
    )(tile_e, clo, nch, sub0, nsub, xb, pos_t)


def _moe_combine_kernel(n_ref, cid_ref, eid_ref, ys_hbm, pos_ref, comb_ref, x_ref, *rest, tm, cs, n_slots, n_buf):
    *ple_refs, o_ref, buf_ref, sem_ref, proj_ref = rest
    i = pl.program_id(0)
    n = n_ref[i]
    base = i * n_slots

    def chunk_copy(k, slot):
        row0 = pl.multiple_of(cid_ref[base + k] * cs, cs)
        return pltpu.make_async_copy(ys_hbm.at[pl.ds(row0, cs), :], buf_ref.at[slot], sem_ref.at[slot])

    proj_ref[...] = _dot(ple_refs[2][...].astype(BF16), ple_refs[4][...])
    o_ref[...] = jnp.zeros(o_ref.shape, F32)

    for d in range(n_buf):

        @pl.when(d < n)
        def _():
            chunk_copy(d, d).start()

    def gathered(k, slot):
        e = eid_ref[base + k]
        lane = lax.broadcasted_iota(jnp.int32, pos_ref.shape, 1)
        pos_e = jnp.sum(jnp.where(lane == e, pos_ref[...], 0), axis=1, keepdims=True)
        lane_c = lax.broadcasted_iota(jnp.int32, comb_ref.shape, 1)
        g_e = jnp.sum(jnp.where(lane_c == e, comb_ref[...], 0.0), axis=1, keepdims=True)
        col = cid_ref[base + k] * cs + lax.broadcasted_iota(jnp.int32, (tm, cs), 1)
        onehot = jnp.where(pos_e == col, 1.0, 0.0).astype(BF16)
        return g_e * _dot(onehot, buf_ref[slot])

    def body(j, carry):
        k = 2 * j
        slot = k % n_buf
        chunk_copy(k, slot).wait()
        chunk_copy(k + 1, slot + 1).wait()
        o_ref[...] += gathered(k, slot) + gathered(k + 1, slot + 1)

        @pl.when(k + n_buf < n)
        def _():
            chunk_copy(k + n_buf, slot).start()
            chunk_copy(k + n_buf + 1, slot + 1).start()

        return carry

    lax.fori_loop(0, n // 2, body, 0)
    o_ref[...] = _ln_ple(x_ref[...], o_ref[...], ple_refs, proj=proj_ref[...])


def _moe_combine(ys, pos, combine, n_need, cid, eid, x2d, ple_args, tm=MOE_TM, cs=MOE_CS):
    t = pos.shape[0]
    return pl.pallas_call(
        functools.partial(_moe_combine_kernel, tm=tm, cs=cs, n_slots=MOE_SLOTS, n_buf=MOE_RING),
        grid_spec=pltpu.PrefetchScalarGridSpec(
            num_scalar_prefetch=3,
            grid=(t // tm,),
            in_specs=[
                pl.BlockSpec(memory_space=pl.ANY),
                pl.BlockSpec((tm, N_EXPERTS), lambda i, *_: (i, 0)),
                pl.BlockSpec((tm, LANES), lambda i, *_: (i, 0)),
                pl.BlockSpec((tm, D_MODEL), lambda i, *_: (i, 0)),
            ] + _ple_specs(tm, ple_args),
            out_specs=pl.BlockSpec((tm, D_MODEL), lambda i, *_: (i, 0)),
            scratch_shapes=[pltpu.VMEM((MOE_RING, cs, D_MODEL), BF16), pltpu.SemaphoreType.DMA((MOE_RING,)),
                            pltpu.VMEM((tm, D_MODEL), F32)],
        ),
        out_shape=jax.ShapeDtypeStruct((t, D_MODEL), F32),
        compiler_params=_cparams("arbitrary"),
        name="moe_combine",
    )(n_need, cid, eid, ys, pos, combine, x2d, *ple_args)


def _moe_plan(combine, tm=MOE_TM):
    t = combine.shape[0]
    n_tok_tiles = t // tm
    n_row_tiles = (2 * t) // tm + N_EXPERTS
    sel = (combine[:, :N_EXPERTS] != 0.0).astype(jnp.int32)
    csum = jnp.cumsum(sel, axis=0)
    count = csum[-1]
    ntile_e = (count + tm - 1) // tm
    tile_end = jnp.cumsum(ntile_e)
    gstart = (tile_end - ntile_e) * tm
    pos = jnp.where(sel != 0, gstart[None, :] + csum - sel, -1)
    rt = jnp.arange(n_row_tiles, dtype=jnp.int32)
    tile_e = jnp.minimum(jnp.sum(rt[:, None] >= tile_end[None, :], axis=1), N_EXPERTS - 1).astype(jnp.int32)
    valid = (rt < tile_end[-1]).astype(jnp.int32)
    prev_e = jnp.concatenate([jnp.full((1,), -1, jnp.int32), tile_e[:-1]])
    group_head = ((valid != 0) & (tile_e != prev_e)).astype(jnp.int32)
    cb = jnp.concatenate([jnp.zeros((1, N_EXPERTS), jnp.int32), csum[tm - 1::tm]], axis=0)
    off = rt * tm - gstart[tile_e]
    cb_t = cb[:, tile_e].T
    clo = jnp.sum(cb_t[:, 1:] <= off[:, None], axis=1).astype(jnp.int32)
    cend = jnp.sum(cb_t[:, :-1] < (off + tm)[:, None], axis=1).astype(jnp.int32)
    nch = jnp.where(valid != 0, jnp.maximum(cend - clo, 0), 0).astype(jnp.int32)
    clo = jnp.minimum(clo, n_tok_tiles - 1)
    r_lo = jnp.clip(cb_t[:, :-1] - off[:, None], 0, tm)
    r_hi = jnp.clip(cb_t[:, 1:] - off[:, None], 0, tm)
    sub0 = (r_lo // MOE_SR).astype(jnp.int32)
    nsub = jnp.where(r_hi > r_lo, (r_hi + MOE_SR - 1) // MOE_SR - sub0, 0).astype(jnp.int32)
    first = gstart[None, :] + cb[:-1]
    n_ie = cb[1:] - cb[:-1]
    per_e = MOE_SLOTS // N_EXPERTS
    k = jnp.arange(per_e, dtype=jnp.int32)
    c_lo = first // MOE_CS
    c_hi = (first + n_ie - 1) // MOE_CS
    cand = c_lo[..., None] + k
    need = ((n_ie[..., None] > 0) & (cand <= c_hi[..., None])).reshape(n_tok_tiles, MOE_SLOTS)
    cand = cand.reshape(n_tok_tiles, MOE_SLOTS)
    eids = jnp.broadcast_to(jnp.arange(N_EXPERTS, dtype=jnp.int32)[:, None], (N_EXPERTS, per_e)).reshape(-1)
    order = jnp.argsort(jnp.logical_not(need), axis=1, stable=True)
    cid = jnp.where(need, cand, 0)
    cid = jnp.take_along_axis(cid, order, axis=1).reshape(-1).astype(jnp.int32)
    need_sorted = jnp.take_along_axis(need, order, axis=1)
    eid = jnp.where(need_sorted, eids[order], -1).reshape(-1).astype(jnp.int32)
    n_need = jnp.sum(need, axis=1).astype(jnp.int32)
    n_need = n_need + n_need % 2
    return dict(pos=pos.astype(jnp.int32), tile_e=tile_e, valid=valid, first=group_head, clo=clo, nch=nch,
                sub0=sub0.reshape(-1), nsub=nsub.reshape(-1),
                n_need=n_need, cid=cid, eid=eid, n_rows=n_row_tiles * tm)


def _moe(xb, x2d, combine, w_gu, w_d, first_expert, ple_args):
    plan = _moe_plan(combine)
    xs = _moe_gather(xb, plan["pos"].T, plan["tile_e"], plan["clo"], plan["nch"], plan["sub0"], plan["nsub"],
                     plan["n_rows"])
    ys = _moe_ffn(xs, w_gu, w_d, plan["tile_e"] + first_expert, plan["valid"], plan["first"])
    return _moe_combine(ys, plan["pos"], combine, plan["n_need"], plan["cid"], plan["eid"], x2d, ple_args)


def _ln_ple(x, h, ple_refs, proj=None):
    g_ref, b_ref, p_ref, wg_ref, wp_ref = ple_refs
    y = _layer_norm(DEEPNORM_ALPHA * x + h, g_ref[...], b_ref[...])
    gate = jax.nn.sigmoid(_dot(y.astype(BF16), wg_ref[...]))
    if proj is None:
        proj = _dot(p_ref[...].astype(BF16), wp_ref[...])
    return y + gate * proj


def _ple_specs(tm, ple_args):
    g, b, p2d, wg, wp = ple_args
    ign = lambda f: (lambda *idx: f(idx[0]))
    full = lambda a: pl.BlockSpec(a.shape, ign(lambda i: (0, 0)))
    return [full(g), full(b), pl.BlockSpec((tm, D_PLE), ign(lambda i: (i, 0))), full(wg), full(wp)]


def _ffn_ple_kernel(xb_ref, wgu_ref, wd_ref, x_ref, *rest, fc):
    *ple_refs, o_ref, acc_ref = rest
    f = wd_ref.shape[0]
    xb = xb_ref[...]
    for c in range(f // fc):
        gate = _dot(xb, wgu_ref[:, c * fc:(c + 1) * fc])
        up = _dot(xb, wgu_ref[:, f + c * fc:f + (c + 1) * fc])
        h = gate * jax.nn.sigmoid(gate) * up
        contrib = _dot(h.astype(BF16), wd_ref[c * fc:(c + 1) * fc, :])
        if c == 0:
            acc_ref[...] = contrib
        else:
            acc_ref[...] += contrib
    o_ref[...] = _ln_ple(x_ref[...], acc_ref[...], ple_refs)


def _ffn_ple(xb, x2d, w_gu, w_d, ple_args, tm=512, fc=1408):
    t = xb.shape[0]
    row = lambda n: pl.BlockSpec((tm, n), lambda i: (i, 0))
    resident = lambda a: pl.BlockSpec(a.shape, lambda i: (0, 0), pipeline_mode=pl.Buffered(1))
    return pl.pallas_call(
        functools.partial(_ffn_ple_kernel, fc=fc),
        grid=(t // tm,),
        in_specs=[row(D_MODEL), resident(w_gu), resident(w_d), row(D_MODEL)] + _ple_specs(tm, ple_args),
        out_specs=row(D_MODEL),
        out_shape=jax.ShapeDtypeStruct((t, D_MODEL), F32),
        scratch_shapes=[pltpu.VMEM((tm, D_MODEL), F32)],
        compiler_params=_cparams("parallel"),
        name="ffn_dense",
    )(xb, w_gu, w_d, x2d, *ple_args)


def _even_mixer(x2d, b, s, w_in, b_f, w_out):
    nf, ns = FOX_W, SB_W
    o_f = 3 * nf
    o_s = o_f + N_HEADS_FOX
    qa, ka, va = w_in[:, :nf], w_in[:, nf:2 * nf], w_in[:, 2 * nf:o_f]
    qs, ks, vs = w_in[:, o_s:o_s + ns], w_in[:, o_s + ns:o_s + 2 * ns], w_in[:, o_s + 2 * ns:]
    w_k = jnp.concatenate([ka, ks], axis=1).astype(BF16)
    w_t = jnp.concatenate([qa * Q_SCALE, va, qs * Q_SCALE, vs], axis=1).T.astype(BF16)
    w_f = jnp.pad(w_in[:, o_f:o_s], ((0, 0), (0, LANES - N_HEADS_FOX))).astype(BF16)
    bias_f = jnp.pad(b_f, (0, LANES - N_HEADS_FOX)).reshape(1, LANES)
    k2d, ht, lf = _ab_proj(x2d, w_k, w_t, w_f, bias_f, b, s)
    lf_t = lf.reshape(b, s, LANES)[:, :, :N_HEADS_FOX].transpose(0, 2, 1)
    crow, caug = _cumsum_seq(lf_t)
    k3 = k2d.reshape(b, s, -1)
    oa, ob = _even_attention(k3, ht, caug, crow)
    return [oa.reshape(b * s, nf), ob.reshape(b * s, ns)], w_out.astype(BF16)


def _odd_mixer(x2d, b, s, w_qkv, sinks, w_out, tables):
    nq = N_HEADS_SWA * HEAD_DIM
    wq = w_qkv[:, :nq]
    wk = [w_qkv[:, nq + i * HEAD_DIM: nq + (i + 1) * HEAD_DIM] for i in range(N_KV_SWA)]
    wv = [w_qkv[:, nq + (N_KV_SWA + i) * HEAD_DIM: nq + (N_KV_SWA + i + 1) * HEAD_DIM] for i in range(N_KV_SWA)]
    dup = lambda ws: [w for w in ws for _ in range(2)]
    w_k = jnp.concatenate(dup(wk), axis=1).astype(BF16)
    w_t = jnp.concatenate([wq * Q_SCALE] + dup(wv), axis=1).T.astype(BF16)
    k2d, qt, vt = _swa_proj(x2d, w_k, w_t, tables, b, s)
    o = _swa_attention(qt, k2d.reshape(b, s, -1), vt, sinks)
    return [o.reshape(b * s, nq)], w_out.astype(BF16)


def kernel(x, p, ln_mix_g, ln_mix_b, ln_ffn_g, ln_ffn_b, ab_w_in, ab_b_f, ab_w_out, c_w_qkv, c_sinks, c_w_out,
           ffn_w_gate_up, ffn_w_down, router_w, router_b, moe_w_gate_up, moe_w_down, ple_w_gate, ple_w_proj):
    b, s, d = x.shape
    t = b * s
    x2d = x.reshape(t, d)
    tables = _rope_tables(s)
    row = lambda v: v.reshape(1, -1)
    moe_gu = moe_w_gate_up.reshape((-1,) + moe_w_gate_up.shape[2:])
    moe_d = moe_w_down.reshape((-1,) + moe_w_down.shape[2:])
    for i in range(DEPTH):
        j = i // 2
        if i % 2 == 0:
            os_, ws = _even_mixer(x2d, b, s, ab_w_in[j], ab_b_f[j], ab_w_out[j])
        else:
            os_, ws = _odd_mixer(x2d, b, s, c_w_qkv[j], c_sinks[j], c_w_out[j], tables)
        ple_args = (row(ln_ffn_g[i]), row(ln_ffn_b[i]), p[i].reshape(t, D_PLE),
                    ple_w_gate[i].astype(BF16), ple_w_proj[i].astype(BF16))
        if i % 2 == 0:
            x2d, xb = _out_proj_ln(os_, ws, x2d, row(ln_mix_g[i]), row(ln_mix_b[i]))
            x2d = _ffn_ple(xb, x2d, ffn_w_gate_up[j].astype(BF16), ffn_w_down[j].astype(BF16), ple_args)
        else:
            rw = jnp.pad(router_w[j], ((0, 0), (0, LANES - N_EXPERTS)))
            rb = jnp.pad(router_b[j], (0, LANES - N_EXPERTS)).reshape(1, LANES)
            x2d, xb, combine = _out_proj_ln(os_, ws, x2d, row(ln_mix_g[i]), row(ln_mix_b[i]), router=(rw, rb))
            x2d = _moe(xb, x2d, combine, moe_gu, moe_d, j * N_EXPERTS, ple_args)
    return x2d.reshape(b, s, d)
```
